```python
import math
import jax, jax.numpy as jnp
from jax import lax
import numpy as np

D_MODEL = 2048
BATCH = 8
SEQ = 2048
DEPTH = 1

CHUNK = 64
N_META = 16
Q_BLOCK = 128
PAD_FRONT = Q_BLOCK - N_META
HEAD_DIM = 64
ROPE_THETA = 10000.0
SWA_Q_HEADS = 16
SWA_KV_HEADS = 2
SWA_GROUP = SWA_Q_HEADS // SWA_KV_HEADS
SWA_WINDOW = 128
SWA_WIN_CHUNKS = SWA_WINDOW // CHUNK
DIFF_HEADS = 8
SWA_Q_W = SWA_Q_HEADS * HEAD_DIM
SWA_KV_W = SWA_KV_HEADS * HEAD_DIM
DIFF_QK_W = DIFF_HEADS * 2 * HEAD_DIM
DIFF_V_W = DIFF_HEADS * 2 * HEAD_DIM
MIX_WIDTH = SWA_Q_W + DIFF_V_W
IN_SPLITS = (SWA_Q_W, SWA_Q_W + SWA_KV_W, SWA_Q_W + 2 * SWA_KV_W,
             SWA_Q_W + 2 * SWA_KV_W + DIFF_QK_W, SWA_Q_W + 2 * SWA_KV_W + 2 * DIFF_QK_W)
IN_WIDTH = SWA_Q_W + 2 * SWA_KV_W + 2 * DIFF_QK_W + DIFF_V_W
N_GROUPS = 8
EXPERTS_PER_GROUP = 8
N_EXPERTS = N_GROUPS * EXPERTS_PER_GROUP
TOP_K = 2
D_EXPERT = 1408
MOE_BLOCK = 128
RMS_EPS = 1e-6
NEG_INF = -1e30

kernel_name = 'hymba_swa_sink_diffattn_hier_moe'


def rmsnorm(x, g):
    x32 = x.astype(jnp.float32)
    y = x32 * lax.rsqrt(jnp.mean(x32 * x32, axis=-1, keepdims=True) + RMS_EPS)
    return (y * g.astype(jnp.float32)).astype(x.dtype)


def rope_tables(n_pos, dim):
    inv = 1.0 / (ROPE_THETA ** (jnp.arange(0, dim, 2, dtype=jnp.float32) / dim))
    ang = jnp.arange(n_pos, dtype=jnp.float32)[:, None] * inv[None, :]
    return jnp.cos(ang), jnp.sin(ang)


def apply_rope(x, cos, sin):
    x1, x2 = jnp.split(x.astype(jnp.float32), 2, axis=-1)
    c = cos[None, :, None, :]
    s = sin[None, :, None, :]
    return jnp.concatenate([x1 * c - x2 * s, x2 * c + x1 * s], axis=-1).astype(x.dtype)


def pad_front(t):
    return jnp.pad(t, [(0, 0), (PAD_FRONT, 0)] + [(0, 0)] * (t.ndim - 2))


def sliding_window_sink_attention(q, k, v, sinks):
    B, L = q.shape[0], q.shape[1]
    Lp = L + PAD_FRONT
    n_c = Lp // CHUNK
    n_band = (SWA_WIN_CHUNKS + 1) * CHUNK
    qc = pad_front(q).reshape(B, n_c, CHUNK, SWA_KV_HEADS, SWA_GROUP, HEAD_DIM)

    def band(t):
        t = pad_front(t).reshape(B, n_c, CHUNK, SWA_KV_HEADS, HEAD_DIM)
        t = jnp.pad(t, ((0, 0), (SWA_WIN_CHUNKS, 0), (0, 0), (0, 0), (0, 0)))
        return jnp.concatenate([t[:, i:i + n_c] for i in range(SWA_WIN_CHUNKS + 1)], axis=2)

    def with_meta(t):
        meta = jnp.broadcast_to(t[:, None, :N_META], (B, n_c, N_META, SWA_KV_HEADS, HEAD_DIM))
        return jnp.concatenate([band(t), meta], axis=2)

    kb, vb = with_meta(k), with_meta(v)
    n = jnp.arange(n_c)[:, None]
    i = jnp.arange(n_band)[None, :]
    key_chunk = n - SWA_WIN_CHUNKS + i // CHUNK
    key_pos = key_chunk * CHUNK + i % CHUNK
    band_ok = (key_chunk >= 0) & (key_pos >= PAD_FRONT)
    meta_chunk = PAD_FRONT // CHUNK
    meta_ok = jnp.broadcast_to(n - SWA_WIN_CHUNKS > meta_chunk, (n_c, N_META))
    mask = jnp.concatenate([band_ok, meta_ok], axis=1)

    s = jnp.einsum('bnqhgd,bnkhd->bnhgqk', qc, kb,
                   preferred_element_type=jnp.float32) * (HEAD_DIM ** -0.5)
    s = jnp.where(mask[None, :, None, None, None, :], s, NEG_INF)
    sink = sinks.astype(jnp.float32).reshape(SWA_KV_HEADS, SWA_GROUP)[None, None, :, :, None, None]
    m = jnp.maximum(jnp.max(s, axis=-1, keepdims=True), sink)
    p = jnp.exp(s - m)
    probs = p / (jnp.sum(p, axis=-1, keepdims=True) + jnp.exp(sink - m))
    o = jnp.einsum('bnhgqk,bnkhd->bnqhgd', probs.astype(v.dtype), vb)
    return o.reshape(B, Lp, SWA_Q_W)[:, PAD_FRONT:]


def differential_attention(q, k, v, lam, lam_init, subln_g):
    B, L = q.shape[0], q.shape[1]
    Lp = L + PAD_FRONT
    n_q = Lp // Q_BLOCK
    kp, vp = pad_front(k), pad_front(v)
    key_pos = jnp.arange(Lp)
    key_chunk = key_pos // CHUNK
    key_ok = key_pos >= PAD_FRONT
    qb = jnp.moveaxis(pad_front(q).reshape(B, n_q, Q_BLOCK, DIFF_HEADS, 2, HEAD_DIM), 1, 0)

    def block(args):
        q_blk, b = args
        q_chunk = (b * Q_BLOCK + jnp.arange(Q_BLOCK)) // CHUNK
        mask = key_ok[None, :] & (key_chunk[None, :] <= q_chunk[:, None])
        s = jnp.einsum('bqhcd,bkhcd->bhcqk', q_blk, kp,
                       preferred_element_type=jnp.float32) * (HEAD_DIM ** -0.5)
        a = jax.nn.softmax(jnp.where(mask, s, NEG_INF), axis=-1)
        a = a[:, :, 0] - lam * a[:, :, 1]
        return jnp.einsum('bhqk,bkhe->bqhe', a.astype(vp.dtype), vp)

    o = lax.map(block, (qb, jnp.arange(n_q)))
    o = jnp.moveaxis(o, 0, 1).reshape(B, Lp, DIFF_HEADS, 2 * HEAD_DIM)[:, PAD_FRONT:]
    o = rmsnorm(o, subln_g) * (1.0 - lam_init)
    return o.reshape(B, L, DIFF_V_W)


def hierarchical_moe(f, w_rg, w_re, w_gate, w_up, w_down):
    B, L, D = f.shape
    T = B * L
    xt = f.reshape(T, D)
    pg = jax.nn.softmax((xt @ w_rg).astype(jnp.float32), axis=-1)
    g_sel = jnp.argmax(pg, axis=-1)
    gate_g = jnp.take_along_axis(pg, g_sel[:, None], axis=1)
    le = (xt @ w_re).astype(jnp.float32).reshape(T, N_GROUPS, EXPERTS_PER_GROUP)
    le = jnp.take_along_axis(le, g_sel[:, None, None], axis=1)[:, 0]
    top_p, top_i = lax.top_k(jax.nn.softmax(le, axis=-1), TOP_K)
    wts = gate_g * top_p / jnp.sum(top_p, axis=-1, keepdims=True)
    eid = g_sel[:, None] * EXPERTS_PER_GROUP + top_i

    A = T * TOP_K
    e_flat = eid.reshape(A)
    tok = jnp.repeat(jnp.arange(T, dtype=jnp.int32), TOP_K)
    w_flat = wts.reshape(A)
    order = jnp.argsort(e_flat)
    se, st, sw = e_flat[order], tok[order], w_flat[order]
    counts = jnp.bincount(e_flat, length=N_EXPERTS)
    starts = jnp.cumsum(counts) - counts
    padded = (counts + MOE_BLOCK - 1) // MOE_BLOCK * MOE_BLOCK
    pends = jnp.cumsum(padded)
    pstarts = pends - padded
    dest = pstarts[se] + jnp.arange(A) - starts[se]
    n_blocks = -(-A // MOE_BLOCK) + N_EXPERTS
    cap = n_blocks * MOE_BLOCK
    slot_tok = jnp.full((cap,), T, dtype=jnp.int32).at[dest].set(st)
    slot_w = jnp.zeros((cap,), jnp.float32).at[dest].set(sw)
    block_expert = jnp.minimum(
        jnp.searchsorted(pends, jnp.arange(n_blocks) * MOE_BLOCK, side='right'), N_EXPERTS - 1)
    x_ext = jnp.concatenate([xt, jnp.zeros((1, D), xt.dtype)], axis=0)
    xb = x_ext[slot_tok].reshape(n_blocks, MOE_BLOCK, D)

    def expert_block(args):
        xe, e = args
        hid = jax.nn.silu(xe @ w_gate[e]) * (xe @ w_up[e])
        return hid @ w_down[e]

    y = lax.map(expert_block, (xb, block_expert)).reshape(cap, D)
    y = y * slot_w[:, None].astype(y.dtype)
    out = jnp.zeros((T + 1, D), y.dtype).at[slot_tok].add(y)[:T]
    return out.reshape(B, L, D)


def setup_inputs(seed: int = 0) -> dict:
    key = jax.random.key(seed)
    ks = jax.random.split(key, 15)
    nrm = lambda k, shape, scale: jax.random.normal(k, shape, jnp.float32) * scale
    return {
        'x': nrm(ks[0], (BATCH, SEQ, D_MODEL), 1.0),
        'meta_tokens': nrm(ks[1], (N_META, D_MODEL), 1.0),
        'attn_norm_g': 1.0 + nrm(ks[2], (DEPTH, D_MODEL), 0.01),
        'w_in': nrm(ks[3], (DEPTH, D_MODEL, IN_WIDTH), D_MODEL ** -0.5),
        'swa_sinks': nrm(ks[4], (DEPTH, SWA_Q_HEADS), 0.5),
        'diff_lambda': nrm(ks[5], (DEPTH, 4, HEAD_DIM), 0.1),
        'diff_subln_g': 1.0 + nrm(ks[6], (DEPTH, 2 * HEAD_DIM), 0.01),
        'w_out': nrm(ks[7], (DEPTH, MIX_WIDTH, D_MODEL), MIX_WIDTH ** -0.5),
        'ffn_norm_g': 1.0 + nrm(ks[8], (DEPTH, D_MODEL), 0.01),
        'w_router_group': nrm(ks[9], (DEPTH, D_MODEL, N_GROUPS), D_MODEL ** -0.5),
        'w_router_expert': nrm(ks[10], (DEPTH, D_MODEL, N_EXPERTS), D_MODEL ** -0.5),
        'w_gate': nrm(ks[11], (DEPTH, N_EXPERTS, D_MODEL, D_EXPERT), D_MODEL ** -0.5),
        'w_up': nrm(ks[12], (DEPTH, N_EXPERTS, D_MODEL, D_EXPERT), D_MODEL ** -0.5),
        'w_down': nrm(ks[13], (DEPTH, N_EXPERTS, D_EXPERT, D_MODEL), D_EXPERT ** -0.5),
        'final_norm_g': 1.0 + nrm(ks[14], (D_MODEL,), 0.01),
    }


def reference(x, meta_tokens, attn_norm_g, w_in, swa_sinks, diff_lambda, diff_subln_g, w_out,
              ffn_norm_g, w_router_group, w_router_expert, w_gate, w_up, w_down, final_norm_g):
    B = x.shape[0]
    meta = jnp.broadcast_to(meta_tokens.astype(x.dtype)[None], (B, N_META, D_MODEL))
    h = jnp.concatenate([meta, x], axis=1)
    L = h.shape[1]
    cos, sin = rope_tables(L, HEAD_DIM)
    for layer in range(DEPTH):
        a = rmsnorm(h, attn_norm_g[layer])
        proj = a @ w_in[layer]
        q_s, k_s, v_s, q_d, k_d, v_d = jnp.split(proj, IN_SPLITS, axis=-1)
        q_s = apply_rope(q_s.reshape(B, L, SWA_Q_HEADS, HEAD_DIM), cos, sin)
        k_s = apply_rope(k_s.reshape(B, L, SWA_KV_HEADS, HEAD_DIM), cos, sin)
        v_s = v_s.reshape(B, L, SWA_KV_HEADS, HEAD_DIM)
        o_s = sliding_window_sink_attention(q_s, k_s, v_s, swa_sinks[layer])

        q_d = apply_rope(q_d.reshape(B, L, 2 * DIFF_HEADS, HEAD_DIM), cos, sin)
        k_d = apply_rope(k_d.reshape(B, L, 2 * DIFF_HEADS, HEAD_DIM), cos, sin)
        q_d = q_d.reshape(B, L, DIFF_HEADS, 2, HEAD_DIM)
        k_d = k_d.reshape(B, L, DIFF_HEADS, 2, HEAD_DIM)
        v_d = v_d.reshape(B, L, DIFF_HEADS, 2 * HEAD_DIM)
        lam_init = 0.8 - 0.6 * math.exp(-0.3 * layer)
        dl = diff_lambda[layer].astype(jnp.float32)
        lam = jnp.exp(jnp.sum(dl[0] * dl[1])) - jnp.exp(jnp.sum(dl[2] * dl[3])) + lam_init
        o_d = differential_attention(q_d, k_d, v_d, lam, lam_init, diff_subln_g[layer])

        mix = jnp.concatenate([o_s, o_d.astype(o_s.dtype)], axis=-1)
        h = h + mix @ w_out[layer]
        f = rmsnorm(h, ffn_norm_g[layer])
        h = h + hierarchical_moe(f, w_router_group[layer], w_router_expert[layer],
                                 w_gate[layer], w_up[layer], w_down[layer])
    h = rmsnorm(h, final_norm_g)
    return h[:, N_META:]
```

```python
import functools
import math

import jax
import jax.numpy as jnp
from jax import lax
from jax.experimental import pallas as pl
from jax.experimental.pallas import tpu as pltpu

D_MODEL = 2048
N_META = 16
CHUNK = 64
Q_BLOCK = 128
PAD_FRONT = Q_BLOCK - N_META
HEAD_DIM = 64
ROPE_THETA = 10000.0
SWA_Q_HEADS = 16
SWA_KV_HEADS = 2
DIFF_HEADS = 8
SWA_Q_W = SWA_Q_HEADS * HEAD_DIM
SWA_KV_W = SWA_KV_HEADS * HEAD_DIM
DIFF_W = DIFF_HEADS * 2 * HEAD_DIM
IN_WIDTH = SWA_Q_W + 2 * SWA_KV_W + 3 * DIFF_W
N_GROUPS = 8
EXPERTS_PER_GROUP = 8
N_EXPERTS = N_GROUPS * EXPERTS_PER_GROUP
TOP_K = 2
D_EXPERT = 1408
RMS_EPS = 1e-6
NEG_INF = -1e30
LAM_INIT = 0.8 - 0.6 * math.exp(-0.3 * 0)

LANES = 128
ROW_BLK = 128
VMEM_LIMIT = 56 * 1024 * 1024

BF16 = jnp.bfloat16
F32 = jnp.float32


def _nt_dot(a, b):
    return lax.dot_general(a, b, (((1,), (1,)), ((), ())), preferred_element_type=F32)


_SEGS = (
    ("qs", 0, 512, True, HEAD_DIM ** -0.5), ("qs", 512, 512, True, HEAD_DIM ** -0.5),
    ("ks", 1024, 128, True, 1.0), ("vs", 1152, 128, False, 1.0),
    ("qd", 1280, 512, True, HEAD_DIM ** -0.5), ("qd", 1792, 512, True, HEAD_DIM ** -0.5),
    ("kd", 2304, 512, True, 1.0), ("kd", 2816, 512, True, 1.0),
    ("vd", 3328, 512, False, 1.0), ("vd", 3840, 512, False, 1.0),
)
_SEG_BASE = {"qs": 0, "ks": 1024, "vs": 1152, "qd": 1280, "kd": 2304, "vd": 3328}


def _inproj_kernel(x_ref, g_ref, w_ref, cos_ref, sin_ref,
                   qs_ref, ks_ref, vs_ref, qd_ref, kd_ref, vd_ref):
    x = x_ref[...]
    ms = jnp.mean(x * x, axis=-1, keepdims=True)
    a = (x * lax.rsqrt(ms + RMS_EPS) * g_ref[...]).astype(BF16)
    cos = cos_ref[...]
    sin = sin_ref[...]
    lane = lax.broadcasted_iota(jnp.int32, cos.shape, 1)
    first_half = (lane % HEAD_DIM) < (HEAD_DIM // 2)
    low_head = lane < HEAD_DIM
    outs = {"qs": qs_ref, "ks": ks_ref, "vs": vs_ref, "qd": qd_ref, "kd": kd_ref, "vd": vd_ref}
    for name, c0, width, rotary, scale in _SEGS:
        y = jnp.dot(a, w_ref[:, c0:c0 + width], preferred_element_type=F32)
        o0 = c0 - _SEG_BASE[name]
        for t in range(width // LANES):
            yt = y[:, t * LANES:(t + 1) * LANES]
            if rotary:
                partner = jnp.where(first_half, pltpu.roll(yt, LANES - HEAD_DIM // 2, 1),
                                    pltpu.roll(yt, HEAD_DIM // 2, 1))
                yt = yt * cos + partner * sin
            if scale != 1.0:
                yt = yt * scale
            if name in ("ks", "vs"):
                other = pltpu.roll(yt, HEAD_DIM, 1)
                outs[name][:, 0:LANES] = jnp.where(low_head, yt, other).astype(BF16)
                outs[name][:, LANES:2 * LANES] = jnp.where(low_head, other, yt).astype(BF16)
            else:
                outs[name][:, o0 + t * LANES:o0 + (t + 1) * LANES] = yt.astype(BF16)


def _inproj(x2, g, w_bf, cos_t, sin_t, tm, rows_per_seq):
    rows = x2.shape[0]
    nseq = rows_per_seq // tm
    row_spec = lambda width: pl.BlockSpec((tm, width), lambda i: (i, 0))
    out_shapes = (
        jax.ShapeDtypeStruct((rows, SWA_Q_W), BF16), jax.ShapeDtypeStruct((rows, 2 * SWA_KV_W), BF16),
        jax.ShapeDtypeStruct((rows, 2 * SWA_KV_W), BF16), jax.ShapeDtypeStruct((rows, DIFF_W), BF16),
        jax.ShapeDtypeStruct((rows, DIFF_W), BF16), jax.ShapeDtypeStruct((rows, DIFF_W), BF16),
    )
    return pl.pallas_call(
        _inproj_kernel,
        grid=(rows // tm,),
        in_specs=[
            row_spec(D_MODEL),
            pl.BlockSpec((1, D_MODEL), lambda i: (0, 0)),
            pl.BlockSpec((D_MODEL, IN_WIDTH), lambda i: (0, 0), pipeline_mode=pl.Buffered(1)),
            pl.BlockSpec((tm, LANES), lambda i: (i % nseq, 0)),
            pl.BlockSpec((tm, LANES), lambda i: (i % nseq, 0)),
        ],
        out_specs=(row_spec(SWA_Q_W), row_spec(2 * SWA_KV_W), row_spec(2 * SWA_KV_W),
                   row_spec(DIFF_W), row_spec(DIFF_W), row_spec(DIFF_W)),
        out_shape=out_shapes,
        compiler_params=pltpu.CompilerParams(dimension_semantics=("arbitrary",),
                                             vmem_limit_bytes=VMEM_LIMIT),
        name="inproj",
    )(x2, g, w_bf, cos_t, sin_t)


def _rope_tables(positions):
    inv = 1.0 / (ROPE_THETA ** (jnp.arange(0, HEAD_DIM, 2, dtype=F32) / HEAD_DIM))
    ang = positions.astype(F32)[:, None] * inv[None, :]
    cos, sin = jnp.cos(ang), jnp.sin(ang)
    cos_t = jnp.tile(cos, (1, 2 * LANES // HEAD_DIM))
    sin_t = jnp.tile(jnp.concatenate([-sin, sin], axis=1), (1, LANES // HEAD_DIM))
    return cos_t, sin_t


def _swa_kernel(sink_ref, q_ref, kp_ref, kc_ref, km_ref, vp_ref, vc_ref, vm_ref, o_ref):
    i = pl.program_id(1)
    nk = 3 * Q_BLOCK
    row = lax.broadcasted_iota(jnp.int32, (2 * Q_BLOCK, nk), 0)
    col = lax.broadcasted_iota(jnp.int32, (2 * Q_BLOCK, nk), 1)
    second_half = (row % Q_BLOCK) >= CHUNK
    part = col // Q_BLOCK
    idx = col % Q_BLOCK
    ok_prev = (part == 0) & (i > 0) & ((idx >= CHUNK) | jnp.logical_not(second_half))
    ok_cur = (part == 1) & ((idx < CHUNK) | second_half)
    ok_meta = (part == 2) & (idx >= PAD_FRONT)
    mask = ok_prev | ok_cur | ok_meta
    upper_rows = lax.broadcasted_iota(jnp.int32, (2 * Q_BLOCK, 1), 0) >= Q_BLOCK
    lane = lax.broadcasted_iota(jnp.int32, (Q_BLOCK, LANES), 1)
    low_head = lane < HEAD_DIM
    zero = jnp.zeros((Q_BLOCK, LANES), BF16)
    for g in range(SWA_KV_HEADS):
        gl = slice(g * LANES, (g + 1) * LANES)
        k_all = jnp.concatenate([kp_ref[:, gl], kc_ref[:, gl], km_ref[:, gl]], axis=0)
        v_all = jnp.concatenate([vp_ref[:, gl], vc_ref[:, gl], vm_ref[:, gl]], axis=0)
        pairs = SWA_Q_HEADS // SWA_KV_HEADS // 2
        for c in range(g * pairs, (g + 1) * pairs):
            qp = q_ref[:, c * LANES:(c + 1) * LANES]
            q2 = jnp.concatenate([jnp.where(low_head, qp, zero), jnp.where(low_head, zero, qp)], axis=0)
            s = jnp.where(mask, _nt_dot(q2, k_all), NEG_INF)
            sink = jnp.where(upper_rows, sink_ref[2 * c + 1], sink_ref[2 * c])
            m = jnp.maximum(jnp.max(s, axis=-1, keepdims=True), sink)
            p = jnp.exp(s - m)
            denom = jnp.sum(p, axis=-1, keepdims=True) + jnp.exp(sink - m)
            o2 = jnp.dot(p.astype(BF16), v_all, preferred_element_type=F32) / denom
            o_ref[:, c * LANES:(c + 1) * LANES] = jnp.where(
                low_head, o2[:Q_BLOCK], o2[Q_BLOCK:]).astype(BF16)


def _swa(sinks, qs, ks, vs, ks_meta, vs_meta, batch, nblk):
    rows = qs.shape[0]
    cur = lambda b, i: (b * nblk + i, 0)
    prev = lambda b, i: (b * nblk + jnp.maximum(i - 1, 0), 0)
    kv_w = 2 * SWA_KV_W
    return pl.pallas_call(
        _swa_kernel,
        grid=(batch, nblk),
        in_specs=[
            pl.BlockSpec(memory_space=pltpu.SMEM),
            pl.BlockSpec((Q_BLOCK, SWA_Q_W), cur),
            pl.BlockSpec((Q_BLOCK, kv_w), prev), pl.BlockSpec((Q_BLOCK, kv_w), cur),
            pl.BlockSpec((Q_BLOCK, kv_w), lambda b, i: (0, 0)),
            pl.BlockSpec((Q_BLOCK, kv_w), prev), pl.BlockSpec((Q_BLOCK, kv_w), cur),
            pl.BlockSpec((Q_BLOCK, kv_w), lambda b, i: (0, 0)),
        ],
        out_specs=pl.BlockSpec((Q_BLOCK, SWA_Q_W), cur),
        out_shape=jax.ShapeDtypeStruct((rows, SWA_Q_W), BF16),
        compiler_params=pltpu.CompilerParams(dimension_semantics=("arbitrary", "arbitrary"),
                                             vmem_limit_bytes=VMEM_LIMIT),
        name="swa",
    )(sinks, qs, ks, ks, ks_meta, vs, vs, vs_meta)


DIFF_TILE = 256


def _diff_kernel(lam_ref, g_ref, q_ref, k_ref, v_ref, km_ref, vm_ref, o_ref, *, seq):
    dl = lam_ref[...]
    lam = (jnp.exp(jnp.sum(dl[0:1] * dl[1:2], axis=-1, keepdims=True))
           - jnp.exp(jnp.sum(dl[2:3] * dl[3:4], axis=-1, keepdims=True)) + LAM_INIT)
    lane = lax.broadcasted_iota(jnp.int32, (DIFF_TILE, LANES), 1)
    low_map = lane < HEAD_DIM
    zero = jnp.zeros((DIFF_TILE, LANES), BF16)
    meta_ok = lax.broadcasted_iota(jnp.int32, (DIFF_TILE, Q_BLOCK), 1) >= PAD_FRONT
    rq = lax.broadcasted_iota(jnp.int32, (DIFF_TILE, DIFF_TILE), 0) // CHUNK
    ck = lax.broadcasted_iota(jnp.int32, (DIFF_TILE, DIFF_TILE), 1) // CHUNK
    diag_ok = ck <= rq

    def one_map(qm, k_t, v_t, mask, state):
        s = _nt_dot(qm, k_t)
        if mask is not None:
            s = jnp.where(mask, s, NEG_INF)
        smax = jnp.max(s, axis=-1, keepdims=True)
        if state is None:
            m_new = smax
            p = jnp.exp(s - m_new)
            return m_new, jnp.sum(p, axis=-1, keepdims=True), jnp.dot(
                p.astype(BF16), v_t, preferred_element_type=F32)
        m_old, l_old, acc = state
        m_new = jnp.maximum(m_old, smax)
        alpha = jnp.exp(m_old - m_new)
        p = jnp.exp(s - m_new)
        return (m_new, alpha * l_old + jnp.sum(p, axis=-1, keepdims=True),
                alpha * acc + jnp.dot(p.astype(BF16), v_t, preferred_element_type=F32))

    def q_tile(qi, carry):
        q0 = pl.multiple_of(qi * DIFF_TILE, DIFF_TILE)
        q = q_ref[pl.ds(q0, DIFF_TILE), :]
        qa = jnp.where(low_map, q, zero)
        qb = jnp.where(low_map, zero, q)
        km, vm = km_ref[...], vm_ref[...]
        st = one_map(qa, km, vm, meta_ok, None) + one_map(qb, km, vm, meta_ok, None)

        def k_tile(kj, st):
            k0 = pl.multiple_of(kj * DIFF_TILE, DIFF_TILE)
            k_t = k_ref[pl.ds(k0, DIFF_TILE), :]
            v_t = v_ref[pl.ds(k0, DIFF_TILE), :]
            return one_map(qa, k_t, v_t, None, st[:3]) + one_map(qb, k_t, v_t, None, st[3:])

        st = lax.fori_loop(0, qi, k_tile, st)
        k_t = k_ref[pl.ds(q0, DIFF_TILE), :]
        v_t = v_ref[pl.ds(q0, DIFF_TILE), :]
        st = one_map(qa, k_t, v_t, diag_ok, st[:3]) + one_map(qb, k_t, v_t, diag_ok, st[3:])
        o = st[2] / st[1] - lam * (st[5] / st[4])
        o = o * lax.rsqrt(jnp.mean(o * o, axis=-1, keepdims=True) + RMS_EPS)
        o_ref[pl.ds(q0, DIFF_TILE), :] = (o * g_ref[...] * (1.0 - LAM_INIT)).astype(BF16)
        return carry

    lax.fori_loop(0, seq // DIFF_TILE, q_tile, 0)


def _diff(diff_lambda, subln_g, qd, kd, vd, kd_meta, vd_meta, batch, seq):
    rows = qd.shape[0]
    head = lambda b, h: (b, h)
    meta = lambda b, h: (0, h)
    return pl.pallas_call(
        functools.partial(_diff_kernel, seq=seq),
        grid=(batch, DIFF_HEADS),
        in_specs=[
            pl.BlockSpec((4, HEAD_DIM), lambda b, h: (0, 0)),
            pl.BlockSpec((1, LANES), lambda b, h: (0, 0)),
            pl.BlockSpec((seq, LANES), head), pl.BlockSpec((seq, LANES), head),
            pl.BlockSpec((seq, LANES), head),
            pl.BlockSpec((Q_BLOCK, LANES), meta), pl.BlockSpec((Q_BLOCK, LANES), meta),
        ],
        out_specs=pl.BlockSpec((seq, LANES), head),
        out_shape=jax.ShapeDtypeStruct((rows, DIFF_W), BF16),
        compiler_params=pltpu.CompilerParams(dimension_semantics=("arbitrary", "arbitrary"),
                                             vmem_limit_bytes=VMEM_LIMIT),
        name="diffattn",
    )(diff_lambda, subln_g, qd, kd, vd, kd_meta, vd_meta)


HALF = D_MODEL // 2


def _pack_bf16_pair(a, b):
    hi = pltpu.bitcast(a.astype(BF16).astype(F32), jnp.uint32)
    lo = pltpu.bitcast(b.astype(BF16).astype(F32), jnp.uint32)
    return hi | (lo >> 16)


def _unpack_bf16_pair(w):
    a = pltpu.bitcast(w & jnp.uint32(0xFFFF0000), F32)
    b = pltpu.bitcast(w << 16, F32)
    return a, b


def _outproj_kernel(os_ref, od_ref, x_ref, w_ref, g_ref, wr_ref, h_ref, fp_ref, lg_ref):
    y = jnp.dot(os_ref[...], w_ref[0:SWA_Q_W, :], preferred_element_type=F32)
    y = y + jnp.dot(od_ref[...], w_ref[SWA_Q_W:, :], preferred_element_type=F32)
    h = x_ref[...] + y
    h_ref[...] = h
    f = h * lax.rsqrt(jnp.mean(h * h, axis=-1, keepdims=True) + RMS_EPS) * g_ref[...]
    fp_ref[...] = _pack_bf16_pair(f[:, :HALF], f[:, HALF:])
    lg_ref[...] = jnp.dot(f, wr_ref[...], preferred_element_type=F32,
                          precision=lax.Precision.HIGHEST)


def _outproj(o_s, o_d, x2, w_bf, g, w_router, tm):
    rows = x2.shape[0]
    row_spec = lambda width: pl.BlockSpec((tm, width), lambda i: (i, 0))
    return pl.pallas_call(
        _outproj_kernel,
        grid=(rows // tm,),
        in_specs=[
            row_spec(SWA_Q_W), row_spec(DIFF_W), row_spec(D_MODEL),
            pl.BlockSpec((D_MODEL, D_MODEL), lambda i: (0, 0), pipeline_mode=pl.Buffered(1)),
            pl.BlockSpec((1, D_MODEL), lambda i: (0, 0)),
            pl.BlockSpec((D_MODEL, LANES), lambda i: (0, 0)),
        ],
        out_specs=(row_spec(D_MODEL), row_spec(HALF), row_spec(LANES)),
        out_shape=(jax.ShapeDtypeStruct((rows, D_MODEL), F32),
                   jax.ShapeDtypeStruct((rows, HALF), jnp.uint32),
                   jax.ShapeDtypeStruct((rows, LANES), F32)),
        compiler_params=pltpu.CompilerParams(dimension_semantics=("arbitrary",),
                                             vmem_limit_bytes=VMEM_LIMIT),
        name="outproj",
    )(o_s, o_d, x2, w_bf, g, w_router)


def _gather_rows_kernel(nblk_ref, tok_ref, src_ref, dst_ref, zeros, sem):
    i = pl.program_id(0)
    last = pl.num_programs(0) - 1

    def block_copy(blk):
        rows = pl.ds(blk * ROW_BLK, ROW_BLK)
        return pltpu.make_async_copy(zeros, dst_ref.at[rows], sem.at[blk % 2])

    @pl.when(i == 0)
    def _():
        zeros[...] = jnp.zeros_like(zeros)

    @pl.when(i < nblk_ref[0])
    def _():
        def issue(r, c):
            pltpu.make_async_copy(src_ref.at[pl.ds(tok_ref[r], 1)],
                                  dst_ref.at[pl.ds(i * ROW_BLK + r, 1)], sem.at[i % 2]).start()
            return c
        lax.fori_loop(0, ROW_BLK, issue, 0)

    @pl.when(i >= nblk_ref[0])
    def _():
        block_copy(i).start()

    @pl.when(i > 0)
    def _():
        block_copy(i - 1).wait()

    @pl.when(i == last)
    def _():
        block_copy(i).wait()


def _gather_rows(n_used_blocks, slot_tok, src, n_blocks):
    width = src.shape[1]
    return pl.pallas_call(
        _gather_rows_kernel,
        grid_spec=pltpu.PrefetchScalarGridSpec(
            num_scalar_prefetch=1,
            grid=(n_blocks,),
            in_specs=[
                pl.BlockSpec((ROW_BLK,), lambda i, n: (i,), memory_space=pltpu.SMEM),
                pl.BlockSpec(memory_space=pl.ANY),
            ],
            out_specs=pl.BlockSpec(memory_space=pl.ANY),
            scratch_shapes=[pltpu.VMEM((ROW_BLK, width), src.dtype),
                            pltpu.SemaphoreType.DMA((2,))],
        ),
        out_shape=jax.ShapeDtypeStruct((n_blocks * ROW_BLK, width), src.dtype),
        compiler_params=pltpu.CompilerParams(dimension_semantics=("arbitrary",)),
        name="gather_rows",
    )(n_used_blocks, slot_tok, src)


ITEM_BLKS = 8
ITEM_ROWS = ITEM_BLKS * ROW_BLK
F_TILE = 128
N_FT = D_EXPERT // F_TILE


def _moe_kernel(e_ref, r0_ref, nb_ref, used_ref, xs_ref, wg_ref, wu_ref, wd_ref, y_ref,
                xraw, xb, yacc, ystage, wgu, wdb, xsem, ysem):
    i = pl.program_id(0)
    k = pl.program_id(1)
    n_items = pl.num_programs(0)
    slot = i % 2
    nb = nb_ref[i]

    def x_copy(item, buf, j):
        src = xs_ref.at[pl.ds((r0_ref[item] + j) * ROW_BLK, ROW_BLK)]
        return pltpu.make_async_copy(src, xraw.at[buf, pl.ds(j * ROW_BLK, ROW_BLK)],
                                     xsem.at[buf, j])

    def y_copy(j):
        dst = y_ref.at[pl.ds((r0_ref[i] + j) * ROW_BLK, ROW_BLK)]
        return pltpu.make_async_copy(ystage.at[pl.ds(j * ROW_BLK, ROW_BLK)], dst, ysem.at[0])

    def for_blocks(count, fn):
        def body(j, c):
            fn(j)
            return c
        lax.fori_loop(0, count, body, 0)

    @pl.when(k == 0)
    def _():
        @pl.when(i == 0)
        def _():
            def zero_copy(j):
                dst = y_ref.at[pl.ds((used_ref[0] + j) * ROW_BLK, ROW_BLK)]
                return pltpu.make_async_copy(ystage.at[pl.ds(0, ROW_BLK)], dst, ysem.at[0])
            n_unused = y_ref.shape[0] // ROW_BLK - used_ref[0]
            ystage[0:ROW_BLK, :] = jnp.zeros((ROW_BLK, HALF), jnp.uint32)
            for_blocks(n_unused, lambda j: zero_copy(j).start())
            for_blocks(n_unused, lambda j: zero_copy(j).wait())
            for_blocks(nb, lambda j: x_copy(0, 0, j).start())

        @pl.when(i + 1 < n_items)
        def _():
            for_blocks(nb_ref[i + 1], lambda j: x_copy(i + 1, 1 - slot, j).start())

        def arrive(j):
            x_copy(i, slot, j).wait()
            rows = pl.ds(pl.multiple_of(j * ROW_BLK, ROW_BLK), ROW_BLK)
            a, b = _unpack_bf16_pair(xraw[slot, rows, :])
            xb[rows, 0:HALF] = a.astype(BF16)
            xb[rows, HALF:] = b.astype(BF16)
        for_blocks(nb, arrive)

    @pl.when(nb > 0)
    def _():
        wgu[:, 0:F_TILE] = wg_ref[0].astype(BF16)
        wgu[:, F_TILE:] = wu_ref[0].astype(BF16)
        wdb[...] = wd_ref[0].astype(BF16)

        def rows_tile(row0, nrows):
            rows = pl.ds(pl.multiple_of(row0, ROW_BLK), nrows)
            gu = jnp.dot(xb[rows, :], wgu[...], preferred_element_type=F32)
            gate, up = gu[:, 0:F_TILE], gu[:, F_TILE:]
            hid = (gate * (1.0 / (1.0 + jnp.exp(-gate))) * up).astype(BF16)
            yc = jnp.dot(hid, wdb[...], preferred_element_type=F32)

            @pl.when(k == 0)
            def _():
                yacc[rows, :] = yc

            @pl.when(k > 0)
            def _():
                yacc[rows, :] += yc

        big = 4 * ROW_BLK
        for_blocks(nb // 4, lambda t: rows_tile(t * big, big))
        done = (nb // 4) * big

        @pl.when((nb & 2) != 0)
        def _():
            rows_tile(done, 2 * ROW_BLK)

        @pl.when((nb & 1) != 0)
        def _():
            rows_tile(done + (nb & 2) * ROW_BLK, ROW_BLK)

        @pl.when(k == N_FT - 1)
        def _():
            def emit(j):
                rows = pl.ds(pl.multiple_of(j * ROW_BLK, ROW_BLK), ROW_BLK)
                ystage[rows, :] = _pack_bf16_pair(yacc[rows, 0:HALF], yacc[rows, HALF:])
                y_copy(j).start()
            for_blocks(nb, emit)
            for_blocks(nb, lambda j: y_copy(j).wait())


def _moe(item_e, item_r0, item_nb, n_used, xs, w_gate, w_up, w_down, n_items, n_blocks):
    wcol = lambda i, k, e, r, n, u: (e[i], 0, jnp.where(n[i] > 0, k, N_FT - 1))
    wrow = lambda i, k, e, r, n, u: (e[i], jnp.where(n[i] > 0, k, N_FT - 1), 0)
    return pl.pallas_call(
        _moe_kernel,
        grid_spec=pltpu.PrefetchScalarGridSpec(
            num_scalar_prefetch=4,
            grid=(n_items, N_FT),
            in_specs=[
                pl.BlockSpec(memory_space=pl.ANY),
                pl.BlockSpec((1, D_MODEL, F_TILE), wcol),
                pl.BlockSpec((1, D_MODEL, F_TILE), wcol),
                pl.BlockSpec((1, F_TILE, D_MODEL), wrow),
            ],
            out_specs=pl.BlockSpec(memory_space=pl.ANY),
            scratch_shapes=[
                pltpu.VMEM((2, ITEM_ROWS, HALF), jnp.uint32),
                pltpu.VMEM((ITEM_ROWS, D_MODEL), BF16),
                pltpu.VMEM((ITEM_ROWS, D_MODEL), F32),
                pltpu.VMEM((ITEM_ROWS, HALF), jnp.uint32),
                pltpu.VMEM((D_MODEL, 2 * F_TILE), BF16),
                pltpu.VMEM((F_TILE, D_MODEL), BF16),
                pltpu.SemaphoreType.DMA((2, ITEM_BLKS)),
                pltpu.SemaphoreType.DMA((1,)),
            ],
        ),
        out_shape=jax.ShapeDtypeStruct((n_blocks * ROW_BLK, HALF), jnp.uint32),
        compiler_params=pltpu.CompilerParams(dimension_semantics=("arbitrary", "arbitrary"),
                                             vmem_limit_bytes=VMEM_LIMIT),
        name="moe",
    )(item_e, item_r0, item_nb, n_used, xs, w_gate, w_up, w_down)


def _combine_kernel(cur_ref, nxt_ref, h_ref, w_ref, g_ref, y_ref, o_ref, ya, yb, sem, *, tm):
    i = pl.program_id(0)
    n = pl.num_programs(0)
    slot = i % 2

    def fetch(idx_ref, buf):
        def issue(r, c):
            pltpu.make_async_copy(y_ref.at[pl.ds(idx_ref[TOP_K * r], 1)], ya.at[buf, pl.ds(r, 1)],
                                  sem.at[buf]).start()
            pltpu.make_async_copy(y_ref.at[pl.ds(idx_ref[TOP_K * r + 1], 1)], yb.at[buf, pl.ds(r, 1)],
                                  sem.at[buf]).start()
            return c
        lax.fori_loop(0, tm, issue, 0)

    @pl.when(i == 0)
    def _():
        fetch(cur_ref, 0)

    @pl.when(i + 1 < n)
    def _():
        fetch(nxt_ref, 1 - slot)

    pltpu.make_async_copy(ya.at[slot], ya.at[slot], sem.at[slot]).wait()
    pltpu.make_async_copy(yb.at[slot], yb.at[slot], sem.at[slot]).wait()

    wa = w_ref[:, 0:1]
    wb = w_ref[:, 1:2]
    a_lo, a_hi = _unpack_bf16_pair(ya[slot])
    b_lo, b_hi = _unpack_bf16_pair(yb[slot])
    lo = h_ref[:, 0:HALF] + wa * a_lo + wb * b_lo
    hi = h_ref[:, HALF:] + wa * a_hi + wb * b_hi
    ms = (jnp.sum(lo * lo, axis=-1, keepdims=True)
          + jnp.sum(hi * hi, axis=-1, keepdims=True)) * (1.0 / D_MODEL)
    inv = lax.rsqrt(ms + RMS_EPS)
    o_ref[:, 0:HALF] = lo * inv * g_ref[:, 0:HALF]
    o_ref[:, HALF:] = hi * inv * g_ref[:, HALF:]


def _combine(dest_flat, h2, wts, g, y, tm):
    rows = h2.shape[0]
    steps = rows // tm
    idx_spec = lambda fn: pl.BlockSpec((TOP_K * tm,), fn, memory_space=pltpu.SMEM)
    return pl.pallas_call(
        functools.partial(_combine_kernel, tm=tm),
        grid=(steps,),
        in_specs=[
            idx_spec(lambda i: (i,)),
            idx_spec(lambda i: (jnp.minimum(i + 1, steps - 1),)),
            pl.BlockSpec((tm, D_MODEL), lambda i: (i, 0)),
            pl.BlockSpec((tm, TOP_K), lambda i: (i, 0)),
            pl.BlockSpec((1, D_MODEL), lambda i: (0, 0)),
            pl.BlockSpec(memory_space=pl.ANY),
        ],
        out_specs=pl.BlockSpec((tm, D_MODEL), lambda i: (i, 0)),
        scratch_shapes=[
            pltpu.VMEM((2, tm, HALF), jnp.uint32),
            pltpu.VMEM((2, tm, HALF), jnp.uint32),
            pltpu.SemaphoreType.DMA((2,)),
        ],
        out_shape=jax.ShapeDtypeStruct((rows, D_MODEL), F32),
        compiler_params=pltpu.CompilerParams(dimension_semantics=("arbitrary",),
                                             vmem_limit_bytes=VMEM_LIMIT),
        name="combine",
    )(dest_flat, dest_flat, h2, wts, g, y)


def _route(logits, n_tok):
    pg = jax.nn.softmax(logits[:, :N_GROUPS], axis=-1)
    g_sel = jnp.argmax(pg, axis=-1)
    gate_g = jnp.take_along_axis(pg, g_sel[:, None], axis=1)
    le = logits[:, N_GROUPS:N_GROUPS + N_EXPERTS].reshape(n_tok, N_GROUPS, EXPERTS_PER_GROUP)
    le = jnp.take_along_axis(le, g_sel[:, None, None], axis=1)[:, 0]
    top_p, top_i = lax.top_k(jax.nn.softmax(le, axis=-1), TOP_K)
    wts = gate_g * top_p / jnp.sum(top_p, axis=-1, keepdims=True)
    eid = (g_sel[:, None] * EXPERTS_PER_GROUP + top_i).astype(jnp.int32)

    n_asg = n_tok * TOP_K
    e_flat = eid.reshape(n_asg)
    onehot = (e_flat[:, None] == jnp.arange(N_EXPERTS, dtype=jnp.int32)[None, :]).astype(jnp.int32)
    csum = jnp.cumsum(onehot, axis=0)
    rank = jnp.take_along_axis(csum, e_flat[:, None], axis=1)[:, 0] - 1
    counts = csum[-1]
    blocks_e = (counts + ROW_BLK - 1) // ROW_BLK
    blk_end = jnp.cumsum(blocks_e)
    blk_start = blk_end - blocks_e
    dest = (blk_start[e_flat] * ROW_BLK + rank).astype(jnp.int32)
    return wts, dest.reshape(n_tok, TOP_K), blocks_e, blk_start, blk_end[-1]


def _work_items(blocks_e, blk_start, n_items):
    items_e = (blocks_e + ITEM_BLKS - 1) // ITEM_BLKS
    item_end = jnp.cumsum(items_e)
    item_start = item_end - items_e
    total = item_end[-1]
    idx = jnp.arange(n_items, dtype=jnp.int32)
    live = idx < total
    ref = jnp.minimum(idx, total - 1)
    e = jnp.searchsorted(item_end, ref, side='right').astype(jnp.int32)
    j = ref - item_start[e]
    r0 = blk_start[e] + ITEM_BLKS * j
    nb = jnp.where(live, jnp.clip(blocks_e[e] - ITEM_BLKS * j, 0, ITEM_BLKS), 0)
    return e, r0.astype(jnp.int32), nb.astype(jnp.int32)


def kernel(x, meta_tokens, attn_norm_g, w_in, swa_sinks, diff_lambda, diff_subln_g, w_out,
           ffn_norm_g, w_router_group, w_router_expert, w_gate, w_up, w_down, final_norm_g):
    batch, seq, _ = x.shape
    n_tok = batch * seq
    nblk = seq // Q_BLOCK
    x2 = x.reshape(n_tok, D_MODEL)

    w_in_bf = w_in[0].astype(BF16)
    g_attn = attn_norm_g[0].reshape(1, D_MODEL)
    cos_x, sin_x = _rope_tables(N_META + jnp.arange(seq))
    cos_m, sin_m = _rope_tables(jnp.maximum(jnp.arange(Q_BLOCK) - PAD_FRONT, 0))
    qs, ks, vs, qd, kd, vd = _inproj(x2, g_attn, w_in_bf, cos_x, sin_x, 512, seq)
    meta_blk = jnp.concatenate(
        [jnp.zeros((PAD_FRONT, D_MODEL), x.dtype), meta_tokens.astype(x.dtype)], axis=0)
    _, ks_m, vs_m, _, kd_m, vd_m = _inproj(meta_blk, g_attn, w_in_bf, cos_m, sin_m, Q_BLOCK, Q_BLOCK)

    o_s = _swa(swa_sinks[0].astype(F32), qs, ks, vs, ks_m, vs_m, batch, nblk)
    o_d = _diff(diff_lambda[0].astype(F32), diff_subln_g[0].astype(F32).reshape(1, LANES),
                qd, kd, vd, kd_m, vd_m, batch, seq)

    w_router = jnp.concatenate(
        [w_router_group[0], w_router_expert[0],
         jnp.zeros((D_MODEL, LANES - N_GROUPS - N_EXPERTS), F32)], axis=1)
    h2, f_packed, logits = _outproj(o_s, o_d, x2, w_out[0].astype(BF16),
                                    ffn_norm_g[0].reshape(1, D_MODEL), w_router, 512)

    wts, dest, blocks_e, blk_start, n_used = _route(logits, n_tok)
    n_blocks = n_tok * TOP_K // ROW_BLK + N_EXPERTS
    n_items = n_blocks // ITEM_BLKS + N_EXPERTS
    tok = jnp.repeat(jnp.arange(n_tok, dtype=jnp.int32), TOP_K)
    slot_tok = jnp.zeros((n_blocks * ROW_BLK,), jnp.int32).at[dest.reshape(-1)].set(tok)
    n_used = n_used.astype(jnp.int32).reshape(1)
    xs = _gather_rows(n_used, slot_tok, f_packed, n_blocks)
    item_e, item_r0, item_nb = _work_items(blocks_e, blk_start, n_items)
    y = _moe(item_e, item_r0, item_nb, n_used, xs, w_gate[0], w_up[0], w_down[0],
             n_items, n_blocks)

    out = _combine(dest.reshape(-1), h2, wts.astype(F32),
                   final_norm_g.reshape(1, D_MODEL), y, 256)
    return out.reshape(batch, seq, D_MODEL)
```

```python
import functools
import math

import jax
import jax.numpy as jnp
from jax import lax
from jax.experimental import pallas as pl
from jax.experimental.pallas import tpu as pltpu

D_MODEL = 2048
N_META = 16
CHUNK = 64
Q_BLOCK = 128
PAD_FRONT = Q_BLOCK - N_META
HEAD_DIM = 64
ROPE_THETA = 10000.0
SWA_Q_HEADS = 16
SWA_KV_HEADS = 2
DIFF_HEADS = 8
SWA_Q_W = SWA_Q_HEADS * HEAD_DIM
SWA_KV_W = SWA_KV_HEADS * HEAD_DIM
DIFF_W = DIFF_HEADS * 2 * HEAD_DIM
IN_WIDTH = SWA_Q_W + 2 * SWA_KV_W + 3 * DIFF_W
N_GROUPS = 8
EXPERTS_PER_GROUP = 8
N_EXPERTS = N_GROUPS * EXPERTS_PER_GROUP
TOP_K = 2
D_EXPERT = 1408
RMS_EPS = 1e-6
NEG_INF = -1e30
LAM_INIT = 0.8 - 0.6 * math.exp(-0.3 * 0)

LANES = 128
ROW_BLK = 128
VMEM_LIMIT = 56 * 1024 * 1024

BF16 = jnp.bfloat16
F32 = jnp.float32


def _nt_dot(a, b):
    return lax.dot_general(a, b, (((1,), (1,)), ((), ())), preferred_element_type=F32)


_SEGS = (
    ("qs", 0, 512, True, HEAD_DIM ** -0.5), ("qs", 512, 512, True, HEAD_DIM ** -0.5),
    ("ks", 1024, 128, True, 1.0), ("vs", 1152, 128, False, 1.0),
    ("qd", 1280, 512, True, HEAD_DIM ** -0.5), ("qd", 1792, 512, True, HEAD_DIM ** -0.5),
    ("kd", 2304, 512, True, 1.0), ("kd", 2816, 512, True, 1.0),
    ("vd", 3328, 512, False, 1.0), ("vd", 3840, 512, False, 1.0),
)
_SEG_BASE = {"qs": 0, "ks": 1024, "vs": 1152, "qd": 1280, "kd": 2304, "vd": 3328}


def _inproj_kernel(x_ref, g_ref, w_ref, cos_ref, sin_ref,
                   qs_ref, ks_ref, vs_ref, qd_ref, kd_ref, vd_ref):
    x = x_ref[...]
    ms = jnp.mean(x * x, axis=-1, keepdims=True)
    a = (x * lax.rsqrt(ms + RMS_EPS) * g_ref[...]).astype(BF16)
    cos = cos_ref[...]
    sin = sin_ref[...]
    lane = lax.broadcasted_iota(jnp.int32, cos.shape, 1)
    first_half = (lane % HEAD_DIM) < (HEAD_DIM // 2)
    low_head = lane < HEAD_DIM
    outs = {"qs": qs_ref, "ks": ks_ref, "vs": vs_ref, "qd": qd_ref, "kd": kd_ref, "vd": vd_ref}
    for name, c0, width, rotary, scale in _SEGS:
        y = jnp.dot(a, w_ref[:, c0:c0 + width], preferred_element_type=F32)
        o0 = c0 - _SEG_BASE[name]
        for t in range(width // LANES):
            yt = y[:, t * LANES:(t + 1) * LANES]
            if rotary:
                partner = jnp.where(first_half, pltpu.roll(yt, LANES - HEAD_DIM // 2, 1),
                                    pltpu.roll(yt, HEAD_DIM // 2, 1))
                yt = yt * cos + partner * sin
            if scale != 1.0:
                yt = yt * scale
            if name in ("ks", "vs"):
                other = pltpu.roll(yt, HEAD_DIM, 1)
                outs[name][:, 0:LANES] = jnp.where(low_head, yt, other).astype(BF16)
                outs[name][:, LANES:2 * LANES] = jnp.where(low_head, other, yt).astype(BF16)
            else:
                outs[name][:, o0 + t * LANES:o0 + (t + 1) * LANES] = yt.astype(BF16)


def _inproj(x2, g, w_bf, cos_t, sin_t, tm, rows_per_seq):
    rows = x2.shape[0]
    nseq = rows_per_seq // tm
    row_spec = lambda width: pl.BlockSpec((tm, width), lambda i: (i, 0))
    out_shapes = (
        jax.ShapeDtypeStruct((rows, SWA_Q_W), BF16), jax.ShapeDtypeStruct((rows, 2 * SWA_KV_W), BF16),
        jax.ShapeDtypeStruct((rows, 2 * SWA_KV_W), BF16), jax.ShapeDtypeStruct((rows, DIFF_W), BF16),
        jax.ShapeDtypeStruct((rows, DIFF_W), BF16), jax.ShapeDtypeStruct((rows, DIFF_W), BF16),
    )
    return pl.pallas_call(
        _inproj_kernel,
        grid=(rows // tm,),
        in_specs=[
            row_spec(D_MODEL),
            pl.BlockSpec((1, D_MODEL), lambda i: (0, 0)),
            pl.BlockSpec((D_MODEL, IN_WIDTH), lambda i: (0, 0), pipeline_mode=pl.Buffered(1)),
            pl.BlockSpec((tm, LANES), lambda i: (i % nseq, 0)),
            pl.BlockSpec((tm, LANES), lambda i: (i % nseq, 0)),
        ],
        out_specs=(row_spec(SWA_Q_W), row_spec(2 * SWA_KV_W), row_spec(2 * SWA_KV_W),
                   row_spec(DIFF_W), row_spec(DIFF_W), row_spec(DIFF_W)),
        out_shape=out_shapes,
        compiler_params=pltpu.CompilerParams(dimension_semantics=("arbitrary",),
                                             vmem_limit_bytes=VMEM_LIMIT),
        name="inproj",
    )(x2, g, w_bf, cos_t, sin_t)


def _rope_tables(positions):
    inv = 1.0 / (ROPE_THETA ** (jnp.arange(0, HEAD_DIM, 2, dtype=F32) / HEAD_DIM))
    ang = positions.astype(F32)[:, None] * inv[None, :]
    cos, sin = jnp.cos(ang), jnp.sin(ang)
    cos_t = jnp.tile(cos, (1, 2 * LANES // HEAD_DIM))
    sin_t = jnp.tile(jnp.concatenate([-sin, sin], axis=1), (1, LANES // HEAD_DIM))
    return cos_t, sin_t


def _swa_kernel(sink_ref, q_ref, kp_ref, kc_ref, km_ref, vp_ref, vc_ref, vm_ref, o_ref):
    i = pl.program_id(1)
    nk = 3 * Q_BLOCK
    row = lax.broadcasted_iota(jnp.int32, (2 * Q_BLOCK, nk), 0)
    col = lax.broadcasted_iota(jnp.int32, (2 * Q_BLOCK, nk), 1)
    second_half = (row % Q_BLOCK) >= CHUNK
    part = col // Q_BLOCK
    idx = col % Q_BLOCK
    ok_prev = (part == 0) & (i > 0) & ((idx >= CHUNK) | jnp.logical_not(second_half))
    ok_cur = (part == 1) & ((idx < CHUNK) | second_half)
    ok_meta = (part == 2) & (idx >= PAD_FRONT)
    mask = ok_prev | ok_cur | ok_meta
    upper_rows = lax.broadcasted_iota(jnp.int32, (2 * Q_BLOCK, 1), 0) >= Q_BLOCK
    lane = lax.broadcasted_iota(jnp.int32, (Q_BLOCK, LANES), 1)
    low_head = lane < HEAD_DIM
    zero = jnp.zeros((Q_BLOCK, LANES), BF16)
    for g in range(SWA_KV_HEADS):
        gl = slice(g * LANES, (g + 1) * LANES)
        k_all = jnp.concatenate([kp_ref[:, gl], kc_ref[:, gl], km_ref[:, gl]], axis=0)
        v_all = jnp.concatenate([vp_ref[:, gl], vc_ref[:, gl], vm_ref[:, gl]], axis=0)
        pairs = SWA_Q_HEADS // SWA_KV_HEADS // 2
        for c in range(g * pairs, (g + 1) * pairs):
            qp = q_ref[:, c * LANES:(c + 1) * LANES]
            q2 = jnp.concatenate([jnp.where(low_head, qp, zero), jnp.where(low_head, zero, qp)], axis=0)
            s = jnp.where(mask, _nt_dot(q2, k_all), NEG_INF)
            sink = jnp.where(upper_rows, sink_ref[2 * c + 1], sink_ref[2 * c])
            m = jnp.maximum(jnp.max(s, axis=-1, keepdims=True), sink)
            p = jnp.exp(s - m)
            denom = jnp.sum(p, axis=-1, keepdims=True) + jnp.exp(sink - m)
            o2 = jnp.dot(p.astype(BF16), v_all, preferred_element_type=F32) / denom
            o_ref[:, c * LANES:(c + 1) * LANES] = jnp.where(
                low_head, o2[:Q_BLOCK], o2[Q_BLOCK:]).astype(BF16)


def _swa(sinks, qs, ks, vs, ks_meta, vs_meta, batch, nblk):
    rows = qs.shape[0]
    cur = lambda b, i: (b * nblk + i, 0)
    prev = lambda b, i: (b * nblk + jnp.maximum(i - 1, 0), 0)
    kv_w = 2 * SWA_KV_W
    return pl.pallas_call(
        _swa_kernel,
        grid=(batch, nblk),
        in_specs=[
            pl.BlockSpec(memory_space=pltpu.SMEM),
            pl.BlockSpec((Q_BLOCK, SWA_Q_W), cur),
            pl.BlockSpec((Q_BLOCK, kv_w), prev), pl.BlockSpec((Q_BLOCK, kv_w), cur),
            pl.BlockSpec((Q_BLOCK, kv_w), lambda b, i: (0, 0)),
            pl.BlockSpec((Q_BLOCK, kv_w), prev), pl.BlockSpec((Q_BLOCK, kv_w), cur),
            pl.BlockSpec((Q_BLOCK, kv_w), lambda b, i: (0, 0)),
        ],
        out_specs=pl.BlockSpec((Q_BLOCK, SWA_Q_W), cur),
        out_shape=jax.ShapeDtypeStruct((rows, SWA_Q_W), BF16),
        compiler_params=pltpu.CompilerParams(dimension_semantics=("arbitrary", "arbitrary"),
                                             vmem_limit_bytes=VMEM_LIMIT),
        name="swa",
    )(sinks, qs, ks, ks, ks_meta, vs, vs, vs_meta)


DIFF_TQ = 256
DIFF_TK = 512


def _diff_kernel(lam_ref, g_ref, q_ref, k_ref, v_ref, km_ref, vm_ref, o_ref, *, seq):
    dl = lam_ref[...]
    lam = (jnp.exp(jnp.sum(dl[0:1] * dl[1:2], axis=-1, keepdims=True))
           - jnp.exp(jnp.sum(dl[2:3] * dl[3:4], axis=-1, keepdims=True)) + LAM_INIT)
    lane = lax.broadcasted_iota(jnp.int32, (DIFF_TQ, LANES), 1)
    low_map = lane < HEAD_DIM
    zero = jnp.zeros((DIFF_TQ, LANES), BF16)
    meta_ok = lax.broadcasted_iota(jnp.int32, (2 * DIFF_TQ, Q_BLOCK), 1) >= PAD_FRONT
    chunk_gap = (lax.broadcasted_iota(jnp.int32, (2 * DIFF_TQ, DIFF_TK), 1) // CHUNK
                 - (lax.broadcasted_iota(jnp.int32, (2 * DIFF_TQ, DIFF_TK), 0) % DIFF_TQ) // CHUNK)

    def step(q2, k_t, v_t, mask, state):
        s = _nt_dot(q2, k_t)
        if mask is not None:
            s = jnp.where(mask, s, NEG_INF)
        smax = jnp.max(s, axis=-1, keepdims=True)
        if state is None:
            p = jnp.exp(s - smax)
            return smax, jnp.sum(p, axis=-1, keepdims=True), jnp.dot(
                p.astype(BF16), v_t, preferred_element_type=F32)
        m_old, l_old, acc = state
        m_new = jnp.maximum(m_old, smax)
        alpha = jnp.exp(m_old - m_new)
        p = jnp.exp(s - m_new)
        return (m_new, alpha * l_old + jnp.sum(p, axis=-1, keepdims=True),
                alpha * acc + jnp.dot(p.astype(BF16), v_t, preferred_element_type=F32))

    def q_tile(qi, carry):
        q0 = pl.multiple_of(qi * DIFF_TQ, DIFF_TQ)
        q = q_ref[pl.ds(q0, DIFF_TQ), :]
        q2 = jnp.concatenate([jnp.where(low_map, q, zero), jnp.where(low_map, zero, q)], axis=0)
        st = step(q2, km_ref[...], vm_ref[...], meta_ok, None)

        def k_tile(kj, st):
            k0 = pl.multiple_of(kj * DIFF_TK, DIFF_TK)
            return step(q2, k_ref[pl.ds(k0, DIFF_TK), :], v_ref[pl.ds(k0, DIFF_TK), :], None, st)

        n_full = q0 // DIFF_TK
        st = lax.fori_loop(0, n_full, k_tile, st)
        k0 = pl.multiple_of(n_full * DIFF_TK, DIFF_TK)
        diag_ok = chunk_gap <= (q0 - k0) // CHUNK
        _, l, acc = step(q2, k_ref[pl.ds(k0, DIFF_TK), :], v_ref[pl.ds(k0, DIFF_TK), :], diag_ok, st)
        o = acc / l
        o = o[:DIFF_TQ] - lam * o[DIFF_TQ:]
        o = o * lax.rsqrt(jnp.mean(o * o, axis=-1, keepdims=True) + RMS_EPS)
        o_ref[pl.ds(q0, DIFF_TQ), :] = (o * g_ref[...] * (1.0 - LAM_INIT)).astype(BF16)
        return carry

    lax.fori_loop(0, seq // DIFF_TQ, q_tile, 0)


def _diff(diff_lambda, subln_g, qd, kd, vd, kd_meta, vd_meta, batch, seq):
    rows = qd.shape[0]
    head = lambda b, h: (b, h)
    meta = lambda b, h: (0, h)
    return pl.pallas_call(
        functools.partial(_diff_kernel, seq=seq),
        grid=(batch, DIFF_HEADS),
        in_specs=[
            pl.BlockSpec((4, HEAD_DIM), lambda b, h: (0, 0)),
            pl.BlockSpec((1, LANES), lambda b, h: (0, 0)),
            pl.BlockSpec((seq, LANES), head), pl.BlockSpec((seq, LANES), head),
            pl.BlockSpec((seq, LANES), head),
            pl.BlockSpec((Q_BLOCK, LANES), meta), pl.BlockSpec((Q_BLOCK, LANES), meta),
        ],
        out_specs=pl.BlockSpec((seq, LANES), head),
        out_shape=jax.ShapeDtypeStruct((rows, DIFF_W), BF16),
        compiler_params=pltpu.CompilerParams(dimension_semantics=("arbitrary", "arbitrary"),
                                             vmem_limit_bytes=VMEM_LIMIT),
        name="diffattn",
    )(diff_lambda, subln_g, qd, kd, vd, kd_meta, vd_meta)


HALF = D_MODEL // 2


def _pack_bf16_pair(a, b):
    hi = pltpu.bitcast(a.astype(BF16).astype(F32), jnp.uint32)
    lo = pltpu.bitcast(b.astype(BF16).astype(F32), jnp.uint32)
    return hi | (lo >> 16)


def _unpack_bf16_pair(w):
    a = pltpu.bitcast(w & jnp.uint32(0xFFFF0000), F32)
    b = pltpu.bitcast(w << 16, F32)
    return a, b


def _outproj_kernel(os_ref, od_ref, x_ref, w_ref, g_ref, wr_ref, h_ref, fp_ref, lg_ref):
    y = jnp.dot(os_ref[...], w_ref[0:SWA_Q_W, :], preferred_element_type=F32)
    y = y + jnp.dot(od_ref[...], w_ref[SWA_Q_W:, :], preferred_element_type=F32)
    h = x_ref[...] + y
    h_ref[...] = h
    f = h * lax.rsqrt(jnp.mean(h * h, axis=-1, keepdims=True) + RMS_EPS) * g_ref[...]
    fp_ref[...] = _pack_bf16_pair(f[:, :HALF], f[:, HALF:])
    lg_ref[...] = jnp.dot(f, wr_ref[...], preferred_element_type=F32,
                          precision=lax.Precision.HIGHEST)


def _outproj(o_s, o_d, x2, w_bf, g, w_router, tm):
    rows = x2.shape[0]
    row_spec = lambda width: pl.BlockSpec((tm, width), lambda i: (i, 0))
    return pl.pallas_call(
        _outproj_kernel,
        grid=(rows // tm,),
        in_specs=[
            row_spec(SWA_Q_W), row_spec(DIFF_W), row_spec(D_MODEL),
            pl.BlockSpec((D_MODEL, D_MODEL), lambda i: (0, 0), pipeline_mode=pl.Buffered(1)),
            pl.BlockSpec((1, D_MODEL), lambda i: (0, 0)),
            pl.BlockSpec((D_MODEL, LANES), lambda i: (0, 0)),
        ],
        out_specs=(row_spec(D_MODEL), row_spec(HALF), row_spec(LANES)),
        out_shape=(jax.ShapeDtypeStruct((rows, D_MODEL), F32),
                   jax.ShapeDtypeStruct((rows, HALF), jnp.uint32),
                   jax.ShapeDtypeStruct((rows, LANES), F32)),
        compiler_params=pltpu.CompilerParams(dimension_semantics=("arbitrary",),
                                             vmem_limit_bytes=VMEM_LIMIT),
        name="outproj",
    )(o_s, o_d, x2, w_bf, g, w_router)


GATHER_BLK = 2 * ROW_BLK


def _gather_rows_kernel(nblk_ref, cur_ref, nxt_ref, src_ref, o_ref, raw, sem):
    i = pl.program_id(0)
    slot = i % 2
    n_live = (nblk_ref[0] * ROW_BLK + GATHER_BLK - 1) // GATHER_BLK

    def fetch(idx_ref, buf):
        def issue(r, c):
            pltpu.make_async_copy(src_ref.at[pl.ds(idx_ref[r], 1)], raw.at[buf, pl.ds(r, 1)],
                                  sem.at[buf]).start()
            return c
        lax.fori_loop(0, GATHER_BLK, issue, 0)

    @pl.when(i == 0)
    def _():
        fetch(cur_ref, 0)

    @pl.when(i + 1 < n_live)
    def _():
        fetch(nxt_ref, 1 - slot)

    @pl.when(i < n_live)
    def _():
        pltpu.make_async_copy(raw.at[slot], raw.at[slot], sem.at[slot]).wait()
        a, b = _unpack_bf16_pair(raw[slot])
        o_ref[:, 0:HALF] = a.astype(BF16)
        o_ref[:, HALF:] = b.astype(BF16)

    @pl.when(i >= n_live)
    def _():
        o_ref[...] = jnp.zeros_like(o_ref)


def _gather_rows(n_used_blocks, slot_tok, src, n_blocks):
    steps = n_blocks * ROW_BLK // GATHER_BLK
    idx_spec = lambda fn: pl.BlockSpec((GATHER_BLK,), fn, memory_space=pltpu.SMEM)
    return pl.pallas_call(
        _gather_rows_kernel,
        grid_spec=pltpu.PrefetchScalarGridSpec(
            num_scalar_prefetch=1,
            grid=(steps,),
            in_specs=[
                idx_spec(lambda i, n: (i,)),
                idx_spec(lambda i, n: (jnp.minimum(i + 1, steps - 1),)),
                pl.BlockSpec(memory_space=pl.ANY),
            ],
            out_specs=pl.BlockSpec((GATHER_BLK, D_MODEL), lambda i, n: (i, 0)),
            scratch_shapes=[pltpu.VMEM((2, GATHER_BLK, HALF), jnp.uint32),
                            pltpu.SemaphoreType.DMA((2,))],
        ),
        out_shape=jax.ShapeDtypeStruct((n_blocks * ROW_BLK, D_MODEL), BF16),
        compiler_params=pltpu.CompilerParams(dimension_semantics=("arbitrary",),
                                             vmem_limit_bytes=VMEM_LIMIT),
        name="gather_rows",
    )(n_used_blocks, slot_tok, slot_tok, src)


ITEM_BLKS = 8
ITEM_ROWS = ITEM_BLKS * ROW_BLK
F_TILE = 128
N_FT = D_EXPERT // F_TILE


def _moe_kernel(e_ref, r0_ref, nb_ref, used_ref, xs_ref, wg_ref, wu_ref, wd_ref, y_ref,
                xb, yacc, ystage, wgu, wdb, xsem, ysem):
    i = pl.program_id(0)
    k = pl.program_id(1)
    n_items = pl.num_programs(0)
    slot = i % 2
    nb = nb_ref[i]

    def x_copy(item, buf, j):
        src = xs_ref.at[pl.ds((r0_ref[item] + j) * ROW_BLK, ROW_BLK)]
        return pltpu.make_async_copy(src, xb.at[buf, pl.ds(j * ROW_BLK, ROW_BLK)],
                                     xsem.at[buf, j])

    def y_copy(j):
        dst = y_ref.at[pl.ds((r0_ref[i] + j) * ROW_BLK, ROW_BLK)]
        return pltpu.make_async_copy(ystage.at[pl.ds(j * ROW_BLK, ROW_BLK)], dst, ysem.at[0])

    def for_blocks(count, fn):
        def body(j, c):
            fn(j)
            return c
        lax.fori_loop(0, count, body, 0)

    @pl.when(k == 0)
    def _():
        @pl.when(i == 0)
        def _():
            def zero_copy(j):
                dst = y_ref.at[pl.ds((used_ref[0] + j) * ROW_BLK, ROW_BLK)]
                return pltpu.make_async_copy(ystage.at[pl.ds(0, ROW_BLK)], dst, ysem.at[0])
            n_unused = y_ref.shape[0] // ROW_BLK - used_ref[0]
            ystage[0:ROW_BLK, :] = jnp.zeros((ROW_BLK, HALF), jnp.uint32)
            for_blocks(n_unused, lambda j: zero_copy(j).start())
            for_blocks(n_unused, lambda j: zero_copy(j).wait())
            for_blocks(nb, lambda j: x_copy(0, 0, j).start())

        @pl.when(i + 1 < n_items)
        def _():
            for_blocks(nb_ref[i + 1], lambda j: x_copy(i + 1, 1 - slot, j).start())

        for_blocks(nb, lambda j: x_copy(i, slot, j).wait())

    @pl.when(nb > 0)
    def _():
        wgu[:, 0:F_TILE] = wg_ref[0].astype(BF16)
        wgu[:, F_TILE:] = wu_ref[0].astype(BF16)
        wdb[...] = wd_ref[0].astype(BF16)

        def rows_tile(row0, nrows):
            rows = pl.ds(pl.multiple_of(row0, ROW_BLK), nrows)
            gu = jnp.dot(xb[slot, rows, :], wgu[...], preferred_element_type=F32)
            gate, up = gu[:, 0:F_TILE], gu[:, F_TILE:]
            hid = (gate * (1.0 / (1.0 + jnp.exp(-gate))) * up).astype(BF16)
            yc = jnp.dot(hid, wdb[...], preferred_element_type=F32)

            @pl.when(k == 0)
            def _():
                yacc[rows, :] = yc

            @pl.when(k > 0)
            def _():
                yacc[rows, :] += yc

        big = 4 * ROW_BLK
        for_blocks(nb // 4, lambda t: rows_tile(t * big, big))
        done = (nb // 4) * big

        @pl.when((nb & 2) != 0)
        def _():
            rows_tile(done, 2 * ROW_BLK)

        @pl.when((nb & 1) != 0)
        def _():
            rows_tile(done + (nb & 2) * ROW_BLK, ROW_BLK)

        @pl.when(k == N_FT - 1)
        def _():
            def emit(j):
                rows = pl.ds(pl.multiple_of(j * ROW_BLK, ROW_BLK), ROW_BLK)
                ystage[rows, :] = _pack_bf16_pair(yacc[rows, 0:HALF], yacc[rows, HALF:])
                y_copy(j).start()
            for_blocks(nb, emit)
            for_blocks(nb, lambda j: y_copy(j).wait())


def _moe(item_e, item_r0, item_nb, n_used, xs, w_gate, w_up, w_down, n_items, n_blocks):
    wcol = lambda i, k, e, r, n, u: (e[i], 0, jnp.where(n[i] > 0, k, N_FT - 1))
    wrow = lambda i, k, e, r, n, u: (e[i], jnp.where(n[i] > 0, k, N_FT - 1), 0)
    return pl.pallas_call(
        _moe_kernel,
        grid_spec=pltpu.PrefetchScalarGridSpec(
            num_scalar_prefetch=4,
            grid=(n_items, N_FT),
            in_specs=[
                pl.BlockSpec(memory_space=pl.ANY),
                pl.BlockSpec((1, D_MODEL, F_TILE), wcol),
                pl.BlockSpec((1, D_MODEL, F_TILE), wcol),
                pl.BlockSpec((1, F_TILE, D_MODEL), wrow),
            ],
            out_specs=pl.BlockSpec(memory_space=pl.ANY),
            scratch_shapes=[
                pltpu.VMEM((2, ITEM_ROWS, D_MODEL), BF16),
                pltpu.VMEM((ITEM_ROWS, D_MODEL), F32),
                pltpu.VMEM((ITEM_ROWS, HALF), jnp.uint32),
                pltpu.VMEM((D_MODEL, 2 * F_TILE), BF16),
                pltpu.VMEM((F_TILE, D_MODEL), BF16),
                pltpu.SemaphoreType.DMA((2, ITEM_BLKS)),
                pltpu.SemaphoreType.DMA((1,)),
            ],
        ),
        out_shape=jax.ShapeDtypeStruct((n_blocks * ROW_BLK, HALF), jnp.uint32),
        compiler_params=pltpu.CompilerParams(dimension_semantics=("arbitrary", "arbitrary"),
                                             vmem_limit_bytes=VMEM_LIMIT),
        name="moe",
    )(item_e, item_r0, item_nb, n_used, xs, w_gate, w_up, w_down)


def _combine_kernel(cur_ref, nxt_ref, h_ref, w_ref, g_ref, y_ref, o_ref, ya, yb, sem, *, tm):
    i = pl.program_id(0)
    n = pl.num_programs(0)
    slot = i % 2

    def fetch(idx_ref, buf):
        def issue(r, c):
            pltpu.make_async_copy(y_ref.at[pl.ds(idx_ref[TOP_K * r], 1)], ya.at[buf, pl.ds(r, 1)],
                                  sem.at[buf]).start()
            pltpu.make_async_copy(y_ref.at[pl.ds(idx_ref[TOP_K * r + 1], 1)], yb.at[buf, pl.ds(r, 1)],
                                  sem.at[buf]).start()
            return c
        lax.fori_loop(0, tm, issue, 0)

    @pl.when(i == 0)
    def _():
        fetch(cur_ref, 0)

    @pl.when(i + 1 < n)
    def _():
        fetch(nxt_ref, 1 - slot)

    pltpu.make_async_copy(ya.at[slot], ya.at[slot], sem.at[slot]).wait()
    pltpu.make_async_copy(yb.at[slot], yb.at[slot], sem.at[slot]).wait()

    wa = w_ref[:, 0:1]
    wb = w_ref[:, 1:2]
    a_lo, a_hi = _unpack_bf16_pair(ya[slot])
    b_lo, b_hi = _unpack_bf16_pair(yb[slot])
    lo = h_ref[:, 0:HALF] + wa * a_lo + wb * b_lo
    hi = h_ref[:, HALF:] + wa * a_hi + wb * b_hi
    ms = (jnp.sum(lo * lo, axis=-1, keepdims=True)
          + jnp.sum(hi * hi, axis=-1, keepdims=True)) * (1.0 / D_MODEL)
    inv = lax.rsqrt(ms + RMS_EPS)
    o_ref[:, 0:HALF] = lo * inv * g_ref[:, 0:HALF]
    o_ref[:, HALF:] = hi * inv * g_ref[:, HALF:]


def _combine(dest_flat, h2, wts, g, y, tm):
    rows = h2.shape[0]
    steps = rows // tm
    idx_spec = lambda fn: pl.BlockSpec((TOP_K * tm,), fn, memory_space=pltpu.SMEM)
    return pl.pallas_call(
        functools.partial(_combine_kernel, tm=tm),
        grid=(steps,),
        in_specs=[
            idx_spec(lambda i: (i,)),
            idx_spec(lambda i: (jnp.minimum(i + 1, steps - 1),)),
            pl.BlockSpec((tm, D_MODEL), lambda i: (i, 0)),
            pl.BlockSpec((tm, TOP_K), lambda i: (i, 0)),
            pl.BlockSpec((1, D_MODEL), lambda i: (0, 0)),
            pl.BlockSpec(memory_space=pl.ANY),
        ],
        out_specs=pl.BlockSpec((tm, D_MODEL), lambda i: (i, 0)),
        scratch_shapes=[
            pltpu.VMEM((2, tm, HALF), jnp.uint32),
            pltpu.VMEM((2, tm, HALF), jnp.uint32),
            pltpu.SemaphoreType.DMA((2,)),
        ],
        out_shape=jax.ShapeDtypeStruct((rows, D_MODEL), F32),
        compiler_params=pltpu.CompilerParams(dimension_semantics=("arbitrary",),
                                             vmem_limit_bytes=VMEM_LIMIT),
        name="combine",
    )(dest_flat, dest_flat, h2, wts, g, y)


def _route(logits, n_tok):
    pg = jax.nn.softmax(logits[:, :N_GROUPS], axis=-1)
    g_sel = jnp.argmax(pg, axis=-1)
    gate_g = jnp.take_along_axis(pg, g_sel[:, None], axis=1)
    le = logits[:, N_GROUPS:N_GROUPS + N_EXPERTS].reshape(n_tok, N_GROUPS, EXPERTS_PER_GROUP)
    le = jnp.take_along_axis(le, g_sel[:, None, None], axis=1)[:, 0]
    top_p, top_i = lax.top_k(jax.nn.softmax(le, axis=-1), TOP_K)
    wts = gate_g * top_p / jnp.sum(top_p, axis=-1, keepdims=True)
    eid = (g_sel[:, None] * EXPERTS_PER_GROUP + top_i).astype(jnp.int32)

    n_asg = n_tok * TOP_K
    e_flat = eid.reshape(n_asg)
    onehot = (e_flat[:, None] == jnp.arange(N_EXPERTS, dtype=jnp.int32)[None, :]).astype(jnp.int32)
    csum = jnp.cumsum(onehot, axis=0)
    rank = jnp.take_along_axis(csum, e_flat[:, None], axis=1)[:, 0] - 1
    counts = csum[-1]
    blocks_e = (counts + ROW_BLK - 1) // ROW_BLK
    blk_end = jnp.cumsum(blocks_e)
    blk_start = blk_end - blocks_e
    dest = (blk_start[e_flat] * ROW_BLK + rank).astype(jnp.int32)
    return wts, dest.reshape(n_tok, TOP_K), blocks_e, blk_start, blk_end[-1]


def _work_items(blocks_e, blk_start, n_items):
    items_e = (blocks_e + ITEM_BLKS - 1) // ITEM_BLKS
    item_end = jnp.cumsum(items_e)
    item_start = item_end - items_e
    total = item_end[-1]
    idx = jnp.arange(n_items, dtype=jnp.int32)
    live = idx < total
    ref = jnp.minimum(idx, total - 1)
    e = jnp.searchsorted(item_end, ref, side='right').astype(jnp.int32)
    j = ref - item_start[e]
    r0 = blk_start[e] + ITEM_BLKS * j
    nb = jnp.where(live, jnp.clip(blocks_e[e] - ITEM_BLKS * j, 0, ITEM_BLKS), 0)
    return e, r0.astype(jnp.int32), nb.astype(jnp.int32)


def kernel(x, meta_tokens, attn_norm_g, w_in, swa_sinks, diff_lambda, diff_subln_g, w_out,
           ffn_norm_g, w_router_group, w_router_expert, w_gate, w_up, w_down, final_norm_g):
    batch, seq, _ = x.shape
    n_tok = batch * seq
    nblk = seq // Q_BLOCK
    x2 = x.reshape(n_tok, D_MODEL)

    w_in_bf = w_in[0].astype(BF16)
    g_attn = attn_norm_g[0].reshape(1, D_MODEL)
    cos_x, sin_x = _rope_tables(N_META + jnp.arange(seq))
    cos_m, sin_m = _rope_tables(jnp.maximum(jnp.arange(Q_BLOCK) - PAD_FRONT, 0))
    qs, ks, vs, qd, kd, vd = _inproj(x2, g_attn, w_in_bf, cos_x, sin_x, 512, seq)
    meta_blk = jnp.concatenate(
        [jnp.zeros((PAD_FRONT, D_MODEL), x.dtype), meta_tokens.astype(x.dtype)], axis=0)
    _, ks_m, vs_m, _, kd_m, vd_m = _inproj(meta_blk, g_attn, w_in_bf, cos_m, sin_m, Q_BLOCK, Q_BLOCK)

    o_s = _swa(swa_sinks[0].astype(F32), qs, ks, vs, ks_m, vs_m, batch, nblk)
    o_d = _diff(diff_lambda[0].astype(F32), diff_subln_g[0].astype(F32).reshape(1, LANES),
                qd, kd, vd, kd_m, vd_m, batch, seq)

    w_router = jnp.concatenate(
        [w_router_group[0], w_router_expert[0],
         jnp.zeros((D_MODEL, LANES - N_GROUPS - N_EXPERTS), F32)], axis=1)
    h2, f_packed, logits = _outproj(o_s, o_d, x2, w_out[0].astype(BF16),
                                    ffn_norm_g[0].reshape(1, D_MODEL), w_router, 512)

    wts, dest, blocks_e, blk_start, n_used = _route(logits, n_tok)
    n_blocks = n_tok * TOP_K // ROW_BLK + N_EXPERTS
    n_items = n_blocks // ITEM_BLKS + N_EXPERTS
    tok = jnp.repeat(jnp.arange(n_tok, dtype=jnp.int32), TOP_K)
    slot_tok = jnp.zeros((n_blocks * ROW_BLK,), jnp.int32).at[dest.reshape(-1)].set(tok)
    n_used = n_used.astype(jnp.int32).reshape(1)
    xs = _gather_rows(n_used, slot_tok, f_packed, n_blocks)
    item_e, item_r0, item_nb = _work_items(blocks_e, blk_start, n_items)
    y = _moe(item_e, item_r0, item_nb, n_used, xs, w_gate[0], w_up[0], w_down[0],
             n_items, n_blocks)

    out = _combine(dest.reshape(-1), h2, wts.astype(F32),
                   final_norm_g.reshape(1, D_MODEL), y, 256)
    return out.reshape(batch, seq, D_MODEL)
```

```python
import functools
import math

import jax
import jax.numpy as jnp
from jax import lax
from jax.experimental import pallas as pl
from jax.experimental.pallas import tpu as pltpu

D_MODEL = 2048
N_META = 16
CHUNK = 64
Q_BLOCK = 128
PAD_FRONT = Q_BLOCK - N_META
HEAD_DIM = 64
ROPE_THETA = 10000.0
SWA_Q_HEADS = 16
SWA_KV_HEADS = 2
DIFF_HEADS = 8
SWA_Q_W = SWA_Q_HEADS * HEAD_DIM
SWA_KV_W = SWA_KV_HEADS * HEAD_DIM
DIFF_W = DIFF_HEADS * 2 * HEAD_DIM
IN_WIDTH = SWA_Q_W + 2 * SWA_KV_W + 3 * DIFF_W
N_GROUPS = 8
EXPERTS_PER_GROUP = 8
N_EXPERTS = N_GROUPS * EXPERTS_PER_GROUP
TOP_K = 2
D_EXPERT = 1408
RMS_EPS = 1e-6
NEG_INF = -1e30
LAM_INIT = 0.8 - 0.6 * math.exp(-0.3 * 0)

LANES = 128
ROW_BLK = 128
VMEM_LIMIT = 56 * 1024 * 1024

BF16 = jnp.bfloat16
F32 = jnp.float32


def _nt_dot(a, b):
    return lax.dot_general(a, b, (((1,), (1,)), ((), ())), preferred_element_type=F32)


_SEGS = (
    ("qs", 0, 512, True, HEAD_DIM ** -0.5), ("qs", 512, 512, True, HEAD_DIM ** -0.5),
    ("ks", 1024, 128, True, 1.0), ("vs", 1152, 128, False, 1.0),
    ("qd", 1280, 512, True, HEAD_DIM ** -0.5), ("qd", 1792, 512, True, HEAD_DIM ** -0.5),
    ("kd", 2304, 512, True, 1.0), ("kd", 2816, 512, True, 1.0),
    ("vd", 3328, 512, False, 1.0), ("vd", 3840, 512, False, 1.0),
)
_SEG_BASE = {"qs": 0, "ks": 1024, "vs": 1152, "qd": 1280, "kd": 2304, "vd": 3328}


def _inproj_kernel(x_ref, g_ref, w_ref, cos_ref, sin_ref,
                   qs_ref, ks_ref, vs_ref, qd_ref, kd_ref, vd_ref):
    x = x_ref[...]
    ms = jnp.mean(x * x, axis=-1, keepdims=True)
    a = (x * lax.rsqrt(ms + RMS_EPS) * g_ref[...]).astype(BF16)
    cos = cos_ref[...]
    sin = sin_ref[...]
    lane = lax.broadcasted_iota(jnp.int32, cos.shape, 1)
    first_half = (lane % HEAD_DIM) < (HEAD_DIM // 2)
    low_head = lane < HEAD_DIM
    outs = {"qs": qs_ref, "ks": ks_ref, "vs": vs_ref, "qd": qd_ref, "kd": kd_ref, "vd": vd_ref}
    for name, c0, width, rotary, scale in _SEGS:
        y = jnp.dot(a, w_ref[:, c0:c0 + width], preferred_element_type=F32)
        o0 = c0 - _SEG_BASE[name]
        for t in range(width // LANES):
            yt = y[:, t * LANES:(t + 1) * LANES]
            if rotary:
                partner = jnp.where(first_half, pltpu.roll(yt, LANES - HEAD_DIM // 2, 1),
                                    pltpu.roll(yt, HEAD_DIM // 2, 1))
                yt = yt * cos + partner * sin
            if scale != 1.0:
                yt = yt * scale
            if name in ("ks", "vs"):
                other = pltpu.roll(yt, HEAD_DIM, 1)
                outs[name][:, 0:LANES] = jnp.where(low_head, yt, other).astype(BF16)
                outs[name][:, LANES:2 * LANES] = jnp.where(low_head, other, yt).astype(BF16)
            else:
                outs[name][:, o0 + t * LANES:o0 + (t + 1) * LANES] = yt.astype(BF16)


def _inproj(x2, g, w_bf, cos_t, sin_t, tm, rows_per_seq):
    rows = x2.shape[0]
    nseq = rows_per_seq // tm
    row_spec = lambda width: pl.BlockSpec((tm, width), lambda i: (i, 0))
    out_shapes = (
        jax.ShapeDtypeStruct((rows, SWA_Q_W), BF16), jax.ShapeDtypeStruct((rows, 2 * SWA_KV_W), BF16),
        jax.ShapeDtypeStruct((rows, 2 * SWA_KV_W), BF16), jax.ShapeDtypeStruct((rows, DIFF_W), BF16),
        jax.ShapeDtypeStruct((rows, DIFF_W), BF16), jax.ShapeDtypeStruct((rows, DIFF_W), BF16),
    )
    return pl.pallas_call(
        _inproj_kernel,
        grid=(rows // tm,),
        in_specs=[
            row_spec(D_MODEL),
            pl.BlockSpec((1, D_MODEL), lambda i: (0, 0)),
            pl.BlockSpec((D_MODEL, IN_WIDTH), lambda i: (0, 0), pipeline_mode=pl.Buffered(1)),
            pl.BlockSpec((tm, LANES), lambda i: (i % nseq, 0)),
            pl.BlockSpec((tm, LANES), lambda i: (i % nseq, 0)),
        ],
        out_specs=(row_spec(SWA_Q_W), row_spec(2 * SWA_KV_W), row_spec(2 * SWA_KV_W),
                   row_spec(DIFF_W), row_spec(DIFF_W), row_spec(DIFF_W)),
        out_shape=out_shapes,
        compiler_params=pltpu.CompilerParams(dimension_semantics=("arbitrary",),
                                             vmem_limit_bytes=VMEM_LIMIT),
        name="inproj",
    )(x2, g, w_bf, cos_t, sin_t)


def _rope_tables(positions):
    inv = 1.0 / (ROPE_THETA ** (jnp.arange(0, HEAD_DIM, 2, dtype=F32) / HEAD_DIM))
    ang = positions.astype(F32)[:, None] * inv[None, :]
    cos, sin = jnp.cos(ang), jnp.sin(ang)
    cos_t = jnp.tile(cos, (1, 2 * LANES // HEAD_DIM))
    sin_t = jnp.tile(jnp.concatenate([-sin, sin], axis=1), (1, LANES // HEAD_DIM))
    return cos_t, sin_t


def _swa_kernel(sink_ref, q_ref, kp_ref, kc_ref, km_ref, vp_ref, vc_ref, vm_ref, o_ref):
    i = pl.program_id(1)
    nk = 3 * Q_BLOCK
    row = lax.broadcasted_iota(jnp.int32, (2 * Q_BLOCK, nk), 0)
    col = lax.broadcasted_iota(jnp.int32, (2 * Q_BLOCK, nk), 1)
    second_half = (row % Q_BLOCK) >= CHUNK
    part = col // Q_BLOCK
    idx = col % Q_BLOCK
    ok_prev = (part == 0) & (i > 0) & ((idx >= CHUNK) | jnp.logical_not(second_half))
    ok_cur = (part == 1) & ((idx < CHUNK) | second_half)
    ok_meta = (part == 2) & (idx >= PAD_FRONT)
    mask = ok_prev | ok_cur | ok_meta
    upper_rows = lax.broadcasted_iota(jnp.int32, (2 * Q_BLOCK, 1), 0) >= Q_BLOCK
    lane = lax.broadcasted_iota(jnp.int32, (Q_BLOCK, LANES), 1)
    low_head = lane < HEAD_DIM
    zero = jnp.zeros((Q_BLOCK, LANES), BF16)
    for g in range(SWA_KV_HEADS):
        gl = slice(g * LANES, (g + 1) * LANES)
        k_all = jnp.concatenate([kp_ref[:, gl], kc_ref[:, gl], km_ref[:, gl]], axis=0)
        v_all = jnp.concatenate([vp_ref[:, gl], vc_ref[:, gl], vm_ref[:, gl]], axis=0)
        pairs = SWA_Q_HEADS // SWA_KV_HEADS // 2
        for c in range(g * pairs, (g + 1) * pairs):
            qp = q_ref[:, c * LANES:(c + 1) * LANES]
            q2 = jnp.concatenate([jnp.where(low_head, qp, zero), jnp.where(low_head, zero, qp)], axis=0)
            s = jnp.where(mask, _nt_dot(q2, k_all), NEG_INF)
            sink = jnp.where(upper_rows, sink_ref[2 * c + 1], sink_ref[2 * c])
            m = jnp.maximum(jnp.max(s, axis=-1, keepdims=True), sink)
            p = jnp.exp(s - m)
            denom = jnp.sum(p, axis=-1, keepdims=True) + jnp.exp(sink - m)
            o2 = jnp.dot(p.astype(BF16), v_all, preferred_element_type=F32) / denom
            o_ref[:, c * LANES:(c + 1) * LANES] = jnp.where(
                low_head, o2[:Q_BLOCK], o2[Q_BLOCK:]).astype(BF16)


def _swa(sinks, qs, ks, vs, ks_meta, vs_meta, batch, nblk):
    rows = qs.shape[0]
    cur = lambda b, i: (b * nblk + i, 0)
    prev = lambda b, i: (b * nblk + jnp.maximum(i - 1, 0), 0)
    kv_w = 2 * SWA_KV_W
    return pl.pallas_call(
        _swa_kernel,
        grid=(batch, nblk),
        in_specs=[
            pl.BlockSpec(memory_space=pltpu.SMEM),
            pl.BlockSpec((Q_BLOCK, SWA_Q_W), cur),
            pl.BlockSpec((Q_BLOCK, kv_w), prev), pl.BlockSpec((Q_BLOCK, kv_w), cur),
            pl.BlockSpec((Q_BLOCK, kv_w), lambda b, i: (0, 0)),
            pl.BlockSpec((Q_BLOCK, kv_w), prev), pl.BlockSpec((Q_BLOCK, kv_w), cur),
            pl.BlockSpec((Q_BLOCK, kv_w), lambda b, i: (0, 0)),
        ],
        out_specs=pl.BlockSpec((Q_BLOCK, SWA_Q_W), cur),
        out_shape=jax.ShapeDtypeStruct((rows, SWA_Q_W), BF16),
        compiler_params=pltpu.CompilerParams(dimension_semantics=("arbitrary", "arbitrary"),
                                             vmem_limit_bytes=VMEM_LIMIT),
        name="swa",
    )(sinks, qs, ks, ks, ks_meta, vs, vs, vs_meta)


DIFF_TQ = 256
DIFF_TK = 512


def _diff_kernel(lam_ref, g_ref, q_ref, k_ref, v_ref, km_ref, vm_ref, o_ref, *, seq):
    dl = lam_ref[...]
    lam = (jnp.exp(jnp.sum(dl[0:1] * dl[1:2], axis=-1, keepdims=True))
           - jnp.exp(jnp.sum(dl[2:3] * dl[3:4], axis=-1, keepdims=True)) + LAM_INIT)
    lane = lax.broadcasted_iota(jnp.int32, (DIFF_TQ, LANES), 1)
    low_map = lane < HEAD_DIM
    zero = jnp.zeros((DIFF_TQ, LANES), BF16)
    meta_ok = lax.broadcasted_iota(jnp.int32, (2 * DIFF_TQ, Q_BLOCK), 1) >= PAD_FRONT
    chunk_gap = (lax.broadcasted_iota(jnp.int32, (2 * DIFF_TQ, DIFF_TK), 1) // CHUNK
                 - (lax.broadcasted_iota(jnp.int32, (2 * DIFF_TQ, DIFF_TK), 0) % DIFF_TQ) // CHUNK)

    def step(q2, k_t, v_t, mask, state):
        s = _nt_dot(q2, k_t)
        if mask is not None:
            s = jnp.where(mask, s, NEG_INF)
        smax = jnp.max(s, axis=-1, keepdims=True)
        if state is None:
            p = jnp.exp(s - smax)
            return smax, jnp.sum(p, axis=-1, keepdims=True), jnp.dot(
                p.astype(BF16), v_t, preferred_element_type=F32)
        m_old, l_old, acc = state
        m_new = jnp.maximum(m_old, smax)
        alpha = jnp.exp(m_old - m_new)
        p = jnp.exp(s - m_new)
        return (m_new, alpha * l_old + jnp.sum(p, axis=-1, keepdims=True),
                alpha * acc + jnp.dot(p.astype(BF16), v_t, preferred_element_type=F32))

    def q_tile(qi, carry):
        q0 = pl.multiple_of(qi * DIFF_TQ, DIFF_TQ)
        q = q_ref[pl.ds(q0, DIFF_TQ), :]
        q2 = jnp.concatenate([jnp.where(low_map, q, zero), jnp.where(low_map, zero, q)], axis=0)
        st = step(q2, km_ref[...], vm_ref[...], meta_ok, None)

        def k_tile(kj, st):
            k0 = pl.multiple_of(kj * DIFF_TK, DIFF_TK)
            return step(q2, k_ref[pl.ds(k0, DIFF_TK), :], v_ref[pl.ds(k0, DIFF_TK), :], None, st)

        n_full = q0 // DIFF_TK
        st = lax.fori_loop(0, n_full, k_tile, st)
        k0 = pl.multiple_of(n_full * DIFF_TK, DIFF_TK)
        diag_ok = chunk_gap <= (q0 - k0) // CHUNK
        _, l, acc = step(q2, k_ref[pl.ds(k0, DIFF_TK), :], v_ref[pl.ds(k0, DIFF_TK), :], diag_ok, st)
        o = acc / l
        o = o[:DIFF_TQ] - lam * o[DIFF_TQ:]
        o = o * lax.rsqrt(jnp.mean(o * o, axis=-1, keepdims=True) + RMS_EPS)
        o_ref[pl.ds(q0, DIFF_TQ), :] = (o * g_ref[...] * (1.0 - LAM_INIT)).astype(BF16)
        return carry

    lax.fori_loop(0, seq // DIFF_TQ, q_tile, 0)


def _diff(diff_lambda, subln_g, qd, kd, vd, kd_meta, vd_meta, batch, seq):
    rows = qd.shape[0]
    head = lambda b, h: (b, h)
    meta = lambda b, h: (0, h)
    return pl.pallas_call(
        functools.partial(_diff_kernel, seq=seq),
        grid=(batch, DIFF_HEADS),
        in_specs=[
            pl.BlockSpec((4, HEAD_DIM), lambda b, h: (0, 0)),
            pl.BlockSpec((1, LANES), lambda b, h: (0, 0)),
            pl.BlockSpec((seq, LANES), head), pl.BlockSpec((seq, LANES), head),
            pl.BlockSpec((seq, LANES), head),
            pl.BlockSpec((Q_BLOCK, LANES), meta), pl.BlockSpec((Q_BLOCK, LANES), meta),
        ],
        out_specs=pl.BlockSpec((seq, LANES), head),
        out_shape=jax.ShapeDtypeStruct((rows, DIFF_W), BF16),
        compiler_params=pltpu.CompilerParams(dimension_semantics=("arbitrary", "arbitrary"),
                                             vmem_limit_bytes=VMEM_LIMIT),
        name="diffattn",
    )(diff_lambda, subln_g, qd, kd, vd, kd_meta, vd_meta)


HALF = D_MODEL // 2


def _pack_bf16_pair(a, b):
    hi = pltpu.bitcast(a.astype(BF16).astype(F32), jnp.uint32)
    lo = pltpu.bitcast(b.astype(BF16).astype(F32), jnp.uint32)
    return hi | (lo >> 16)


def _unpack_bf16_pair(w):
    a = pltpu.bitcast(w & jnp.uint32(0xFFFF0000), F32)
    b = pltpu.bitcast(w << 16, F32)
    return a, b


OUT_CHUNK = 512


def _outproj_kernel(os_ref, od_ref, x_ref, w_ref, g_ref, wr_ref, h_ref, fp_ref, lg_ref):
    tm = x_ref.shape[0]
    ssq = jnp.zeros((tm, 1), F32)
    for c in range(D_MODEL // OUT_CHUNK):
        cols = slice(c * OUT_CHUNK, (c + 1) * OUT_CHUNK)
        y = jnp.dot(os_ref[...], w_ref[0:SWA_Q_W, cols], preferred_element_type=F32)
        y = y + jnp.dot(od_ref[...], w_ref[SWA_Q_W:, cols], preferred_element_type=F32)
        h = x_ref[:, cols] + y
        h_ref[:, cols] = h
        ssq = ssq + jnp.sum(h * h, axis=-1, keepdims=True)
    inv = lax.rsqrt(ssq * (1.0 / D_MODEL) + RMS_EPS)
    lg = jnp.zeros((tm, LANES), F32)
    for c in range(HALF // OUT_CHUNK):
        ca = slice(c * OUT_CHUNK, (c + 1) * OUT_CHUNK)
        cb = slice(HALF + c * OUT_CHUNK, HALF + (c + 1) * OUT_CHUNK)
        fa = h_ref[:, ca] * inv * g_ref[:, ca]
        fb = h_ref[:, cb] * inv * g_ref[:, cb]
        fp_ref[:, ca] = _pack_bf16_pair(fa, fb)
        for f, cc in ((fa, ca), (fb, cb)):
            f_hi = f.astype(BF16)
            f_lo = (f - f_hi.astype(F32)).astype(BF16)
            w = wr_ref[cc, :]
            w_hi = w.astype(BF16)
            w_lo = (w - w_hi.astype(F32)).astype(BF16)
            p = jnp.dot(f_hi, jnp.concatenate([w_hi, w_lo], axis=1), preferred_element_type=F32)
            lg = lg + p[:, 0:LANES] + p[:, LANES:] + jnp.dot(f_lo, w_hi, preferred_element_type=F32)
    lg_ref[...] = lg


def _outproj(o_s, o_d, x2, w_bf, g, w_router, tm):
    rows = x2.shape[0]
    row_spec = lambda width: pl.BlockSpec((tm, width), lambda i: (i, 0))
    return pl.pallas_call(
        _outproj_kernel,
        grid=(rows // tm,),
        in_specs=[
            row_spec(SWA_Q_W), row_spec(DIFF_W), row_spec(D_MODEL),
            pl.BlockSpec((D_MODEL, D_MODEL), lambda i: (0, 0), pipeline_mode=pl.Buffered(1)),
            pl.BlockSpec((1, D_MODEL), lambda i: (0, 0)),
            pl.BlockSpec((D_MODEL, LANES), lambda i: (0, 0)),
        ],
        out_specs=(row_spec(D_MODEL), row_spec(HALF), row_spec(LANES)),
        out_shape=(jax.ShapeDtypeStruct((rows, D_MODEL), F32),
                   jax.ShapeDtypeStruct((rows, HALF), jnp.uint32),
                   jax.ShapeDtypeStruct((rows, LANES), F32)),
        compiler_params=pltpu.CompilerParams(dimension_semantics=("arbitrary",),
                                             vmem_limit_bytes=VMEM_LIMIT),
        name="outproj",
    )(o_s, o_d, x2, w_bf, g, w_router)


GATHER_BLK = 2 * ROW_BLK


def _gather_rows_kernel(nblk_ref, cur_ref, nxt_ref, src_ref, o_ref, raw, sem):
    i = pl.program_id(0)
    slot = i % 2
    n_live = (nblk_ref[0] * ROW_BLK + GATHER_BLK - 1) // GATHER_BLK

    def fetch(idx_ref, buf):
        def issue(r, c):
            pltpu.make_async_copy(src_ref.at[pl.ds(idx_ref[r], 1)], raw.at[buf, pl.ds(r, 1)],
                                  sem.at[buf]).start()
            return c
        lax.fori_loop(0, GATHER_BLK, issue, 0, unroll=8)

    @pl.when(i == 0)
    def _():
        fetch(cur_ref, 0)

    @pl.when(i + 1 < n_live)
    def _():
        fetch(nxt_ref, 1 - slot)

    @pl.when(i < n_live)
    def _():
        pltpu.make_async_copy(raw.at[slot], raw.at[slot], sem.at[slot]).wait()
        a, b = _unpack_bf16_pair(raw[slot])
        o_ref[:, 0:HALF] = a.astype(BF16)
        o_ref[:, HALF:] = b.astype(BF16)

    @pl.when(i >= n_live)
    def _():
        o_ref[...] = jnp.zeros_like(o_ref)


def _gather_rows(n_used_blocks, slot_tok, src, n_blocks):
    steps = n_blocks * ROW_BLK // GATHER_BLK
    idx_spec = lambda fn: pl.BlockSpec((GATHER_BLK,), fn, memory_space=pltpu.SMEM)
    return pl.pallas_call(
        _gather_rows_kernel,
        grid_spec=pltpu.PrefetchScalarGridSpec(
            num_scalar_prefetch=1,
            grid=(steps,),
            in_specs=[
                idx_spec(lambda i, n: (i,)),
                idx_spec(lambda i, n: (jnp.minimum(i + 1, steps - 1),)),
                pl.BlockSpec(memory_space=pl.ANY),
            ],
            out_specs=pl.BlockSpec((GATHER_BLK, D_MODEL), lambda i, n: (i, 0)),
            scratch_shapes=[pltpu.VMEM((2, GATHER_BLK, HALF), jnp.uint32),
                            pltpu.SemaphoreType.DMA((2,))],
        ),
        out_shape=jax.ShapeDtypeStruct((n_blocks * ROW_BLK, D_MODEL), BF16),
        compiler_params=pltpu.CompilerParams(dimension_semantics=("arbitrary",),
                                             vmem_limit_bytes=VMEM_LIMIT),
        name="gather_rows",
    )(n_used_blocks, slot_tok, slot_tok, src)


ITEM_BLKS = 8
ITEM_ROWS = ITEM_BLKS * ROW_BLK
F_TILE = 128
N_FT = D_EXPERT // F_TILE
Y_TILE = 256
N_YT = D_MODEL // Y_TILE


def _moe_kernel(e_ref, r0_ref, nb_ref, used_ref, xs_ref, wg_ref, wu_ref, wd_ref, y_ref,
                xb, hid, ystage, wgu, wdr, xsem, ysem):
    i = pl.program_id(0)
    k = pl.program_id(1)
    n_items = pl.num_programs(0)
    slot = i % 2
    nb = nb_ref[i]
    prev = jnp.maximum(i - 1, 0)
    nb_prev = jnp.where(i > 0, nb_ref[prev], 0)

    def x_copy(item, buf, j):
        src = xs_ref.at[pl.ds((r0_ref[item] + j) * ROW_BLK, ROW_BLK)]
        return pltpu.make_async_copy(src, xb.at[buf, pl.ds(j * ROW_BLK, ROW_BLK)],
                                     xsem.at[buf, j])

    def y_copy(j):
        dst = y_ref.at[pl.ds((r0_ref[prev] + j) * ROW_BLK, ROW_BLK)]
        return pltpu.make_async_copy(ystage.at[pl.ds(j * ROW_BLK, ROW_BLK)], dst, ysem.at[0])

    def for_row_tiles(n_blk, fn):
        big = 4 * ROW_BLK

        def body(t, c):
            fn(t * big, big)
            return c
        lax.fori_loop(0, n_blk // 4, body, 0)
        done = (n_blk // 4) * big

        @pl.when((n_blk & 2) != 0)
        def _():
            fn(done, 2 * ROW_BLK)

        @pl.when((n_blk & 1) != 0)
        def _():
            fn(done + (n_blk & 2) * ROW_BLK, ROW_BLK)

    def for_blocks(count, fn):
        def body(j, c):
            fn(j)
            return c
        lax.fori_loop(0, count, body, 0)

    @pl.when(k == 0)
    def _():
        @pl.when(i == 0)
        def _():
            def zero_copy(j):
                dst = y_ref.at[pl.ds((used_ref[0] + j) * ROW_BLK, ROW_BLK)]
                return pltpu.make_async_copy(ystage.at[pl.ds(0, ROW_BLK)], dst, ysem.at[0])
            n_unused = y_ref.shape[0] // ROW_BLK - used_ref[0]
            ystage[0:ROW_BLK, :] = jnp.zeros((ROW_BLK, HALF), jnp.uint32)
            for_blocks(n_unused, lambda j: zero_copy(j).start())
            for_blocks(n_unused, lambda j: zero_copy(j).wait())
            for_blocks(nb, lambda j: x_copy(0, 0, j).start())

        @pl.when(i + 1 < n_items)
        def _():
            for_blocks(nb_ref[i + 1], lambda j: x_copy(i + 1, 1 - slot, j).start())

        for_blocks(nb, lambda j: x_copy(i, slot, j).wait())

    @pl.when(nb > 0)
    def _():
        wgu[:, 0:F_TILE] = wg_ref[0].astype(BF16)
        wgu[:, F_TILE:] = wu_ref[0].astype(BF16)
        f_rows = pl.ds(pl.multiple_of(k * F_TILE, F_TILE), F_TILE)
        for n in range(N_YT):
            wdr[slot, n, f_rows, :] = wd_ref[0, :, n * Y_TILE:(n + 1) * Y_TILE].astype(BF16)

        def gate_up(row0, nrows):
            rows = pl.ds(pl.multiple_of(row0, ROW_BLK), nrows)
            gu = jnp.dot(xb[slot, rows, :], wgu[...], preferred_element_type=F32)
            gate, up = gu[:, 0:F_TILE], gu[:, F_TILE:]
            h = (gate * (1.0 / (1.0 + jnp.exp(-gate))) * up).astype(BF16)
            for kk in range(N_FT):
                @pl.when(k == kk)
                def _():
                    hid[slot, rows, kk * F_TILE:(kk + 1) * F_TILE] = h

        for_row_tiles(nb, gate_up)

    @pl.when((nb_prev > 0) & (k < N_YT))
    def _():
        def down(row0, nrows):
            rows = pl.ds(pl.multiple_of(row0, ROW_BLK), nrows)
            yt = jnp.dot(hid[1 - slot, rows, :], wdr[1 - slot, k], preferred_element_type=F32)
            packed = _pack_bf16_pair(yt[:, 0:Y_TILE // 2], yt[:, Y_TILE // 2:])
            for n in range(N_YT):
                @pl.when(k == n)
                def _():
                    ystage[rows, n * (Y_TILE // 2):(n + 1) * (Y_TILE // 2)] = packed

        for_row_tiles(nb_prev, down)

    @pl.when((nb_prev > 0) & (k == N_YT))
    def _():
        for_blocks(nb_prev, lambda j: y_copy(j).start())

    @pl.when((nb_prev > 0) & (k == N_FT - 1))
    def _():
        for_blocks(nb_prev, lambda j: y_copy(j).wait())


def _moe(item_e, item_r0, item_nb, n_used, xs, w_gate, w_up, w_down, n_items, n_blocks):
    wcol = lambda i, k, e, r, n, u: (e[i], 0, jnp.where(n[i] > 0, k, N_FT - 1))
    wrow = lambda i, k, e, r, n, u: (e[i], jnp.where(n[i] > 0, k, N_FT - 1), 0)
    return pl.pallas_call(
        _moe_kernel,
        grid_spec=pltpu.PrefetchScalarGridSpec(
            num_scalar_prefetch=4,
            grid=(n_items, N_FT),
            in_specs=[
                pl.BlockSpec(memory_space=pl.ANY),
                pl.BlockSpec((1, D_MODEL, F_TILE), wcol),
                pl.BlockSpec((1, D_MODEL, F_TILE), wcol),
                pl.BlockSpec((1, F_TILE, D_MODEL), wrow),
            ],
            out_specs=pl.BlockSpec(memory_space=pl.ANY),
            scratch_shapes=[
                pltpu.VMEM((2, ITEM_ROWS, D_MODEL), BF16),
                pltpu.VMEM((2, ITEM_ROWS, D_EXPERT), BF16),
                pltpu.VMEM((ITEM_ROWS, HALF), jnp.uint32),
                pltpu.VMEM((D_MODEL, 2 * F_TILE), BF16),
                pltpu.VMEM((2, N_YT, D_EXPERT, Y_TILE), BF16),
                pltpu.SemaphoreType.DMA((2, ITEM_BLKS)),
                pltpu.SemaphoreType.DMA((1,)),
            ],
        ),
        out_shape=jax.ShapeDtypeStruct((n_blocks * ROW_BLK, HALF), jnp.uint32),
        compiler_params=pltpu.CompilerParams(dimension_semantics=("arbitrary", "arbitrary"),
                                             vmem_limit_bytes=VMEM_LIMIT),
        name="moe",
    )(item_e, item_r0, item_nb, n_used, xs, w_gate, w_up, w_down)


def _combine_kernel(cur_ref, nxt_ref, h_ref, w_ref, g_ref, y_ref, o_ref, ya, yb, sem, *, tm):
    i = pl.program_id(0)
    n = pl.num_programs(0)
    slot = i % 2

    def fetch(idx_ref, buf):
        def issue(r, c):
            pltpu.make_async_copy(y_ref.at[pl.ds(idx_ref[TOP_K * r], 1)], ya.at[buf, pl.ds(r, 1)],
                                  sem.at[buf]).start()
            pltpu.make_async_copy(y_ref.at[pl.ds(idx_ref[TOP_K * r + 1], 1)], yb.at[buf, pl.ds(r, 1)],
                                  sem.at[buf]).start()
            return c
        lax.fori_loop(0, tm, issue, 0, unroll=8)

    @pl.when(i == 0)
    def _():
        fetch(cur_ref, 0)

    @pl.when(i + 1 < n)
    def _():
        fetch(nxt_ref, 1 - slot)

    pltpu.make_async_copy(ya.at[slot], ya.at[slot], sem.at[slot]).wait()
    pltpu.make_async_copy(yb.at[slot], yb.at[slot], sem.at[slot]).wait()

    wa = w_ref[:, 0:1]
    wb = w_ref[:, 1:2]
    half_t = Y_TILE // 2
    ssq = jnp.zeros((tm, 1), F32)
    for n in range(N_YT):
        words = slice(n * half_t, (n + 1) * half_t)
        a_lo, a_hi = _unpack_bf16_pair(ya[slot, :, words])
        b_lo, b_hi = _unpack_bf16_pair(yb[slot, :, words])
        lo_cols = slice(n * Y_TILE, n * Y_TILE + half_t)
        hi_cols = slice(n * Y_TILE + half_t, (n + 1) * Y_TILE)
        lo = h_ref[:, lo_cols] + wa * a_lo + wb * b_lo
        hi = h_ref[:, hi_cols] + wa * a_hi + wb * b_hi
        ssq = ssq + jnp.sum(lo * lo, axis=-1, keepdims=True) + jnp.sum(hi * hi, axis=-1, keepdims=True)
        o_ref[:, lo_cols] = lo
        o_ref[:, hi_cols] = hi
    inv = lax.rsqrt(ssq * (1.0 / D_MODEL) + RMS_EPS)
    o_ref[...] = o_ref[...] * inv * g_ref[...]


def _combine(dest_flat, h2, wts, g, y, tm):
    rows = h2.shape[0]
    steps = rows // tm
    idx_spec = lambda fn: pl.BlockSpec((TOP_K * tm,), fn, memory_space=pltpu.SMEM)
    return pl.pallas_call(
        functools.partial(_combine_kernel, tm=tm),
        grid=(steps,),
        in_specs=[
            idx_spec(lambda i: (i,)),
            idx_spec(lambda i: (jnp.minimum(i + 1, steps - 1),)),
            pl.BlockSpec((tm, D_MODEL), lambda i: (i, 0)),
            pl.BlockSpec((tm, TOP_K), lambda i: (i, 0)),
            pl.BlockSpec((1, D_MODEL), lambda i: (0, 0)),
            pl.BlockSpec(memory_space=pl.ANY),
        ],
        out_specs=pl.BlockSpec((tm, D_MODEL), lambda i: (i, 0)),
        scratch_shapes=[
            pltpu.VMEM((2, tm, HALF), jnp.uint32),
            pltpu.VMEM((2, tm, HALF), jnp.uint32),
            pltpu.SemaphoreType.DMA((2,)),
        ],
        out_shape=jax.ShapeDtypeStruct((rows, D_MODEL), F32),
        compiler_params=pltpu.CompilerParams(dimension_semantics=("arbitrary",),
                                             vmem_limit_bytes=VMEM_LIMIT),
        name="combine",
    )(dest_flat, dest_flat, h2, wts, g, y)


def _route(logits, n_tok):
    pg = jax.nn.softmax(logits[:, :N_GROUPS], axis=-1)
    g_sel = jnp.argmax(pg, axis=-1)
    gate_g = jnp.take_along_axis(pg, g_sel[:, None], axis=1)
    le = logits[:, N_GROUPS:N_GROUPS + N_EXPERTS].reshape(n_tok, N_GROUPS, EXPERTS_PER_GROUP)
    le = jnp.take_along_axis(le, g_sel[:, None, None], axis=1)[:, 0]
    top_p, top_i = lax.top_k(jax.nn.softmax(le, axis=-1), TOP_K)
    wts = gate_g * top_p / jnp.sum(top_p, axis=-1, keepdims=True)
    eid = (g_sel[:, None] * EXPERTS_PER_GROUP + top_i).astype(jnp.int32)

    n_asg = n_tok * TOP_K
    e_flat = eid.reshape(n_asg)
    onehot = (e_flat[:, None] == jnp.arange(N_EXPERTS, dtype=jnp.int32)[None, :]).astype(jnp.int32)
    csum = jnp.cumsum(onehot, axis=0)
    rank = jnp.take_along_axis(csum, e_flat[:, None], axis=1)[:, 0] - 1
    counts = csum[-1]
    blocks_e = (counts + ROW_BLK - 1) // ROW_BLK
    blk_end = jnp.cumsum(blocks_e)
    blk_start = blk_end - blocks_e
    dest = (blk_start[e_flat] * ROW_BLK + rank).astype(jnp.int32)
    return wts, dest.reshape(n_tok, TOP_K), blocks_e, blk_start, blk_end[-1]


def _work_items(blocks_e, blk_start, n_items):
    items_e = (blocks_e + ITEM_BLKS - 1) // ITEM_BLKS
    item_end = jnp.cumsum(items_e)
    item_start = item_end - items_e
    total = item_end[-1]
    idx = jnp.arange(n_items, dtype=jnp.int32)
    live = idx < total
    ref = jnp.minimum(idx, total - 1)
    e = jnp.searchsorted(item_end, ref, side='right').astype(jnp.int32)
    j = ref - item_start[e]
    r0 = blk_start[e] + ITEM_BLKS * j
    nb = jnp.where(live, jnp.clip(blocks_e[e] - ITEM_BLKS * j, 0, ITEM_BLKS), 0)
    return e, r0.astype(jnp.int32), nb.astype(jnp.int32)


def kernel(x, meta_tokens, attn_norm_g, w_in, swa_sinks, diff_lambda, diff_subln_g, w_out,
           ffn_norm_g, w_router_group, w_router_expert, w_gate, w_up, w_down, final_norm_g):
    batch, seq, _ = x.shape
    n_tok = batch * seq
    nblk = seq // Q_BLOCK
    x2 = x.reshape(n_tok, D_MODEL)

    w_in_bf = w_in[0].astype(BF16)
    g_attn = attn_norm_g[0].reshape(1, D_MODEL)
    cos_x, sin_x = _rope_tables(N_META + jnp.arange(seq))
    cos_m, sin_m = _rope_tables(jnp.maximum(jnp.arange(Q_BLOCK) - PAD_FRONT, 0))
    qs, ks, vs, qd, kd, vd = _inproj(x2, g_attn, w_in_bf, cos_x, sin_x, 512, seq)
    meta_blk = jnp.concatenate(
        [jnp.zeros((PAD_FRONT, D_MODEL), x.dtype), meta_tokens.astype(x.dtype)], axis=0)
    _, ks_m, vs_m, _, kd_m, vd_m = _inproj(meta_blk, g_attn, w_in_bf, cos_m, sin_m, Q_BLOCK, Q_BLOCK)

    o_s = _swa(swa_sinks[0].astype(F32), qs, ks, vs, ks_m, vs_m, batch, nblk)
    o_d = _diff(diff_lambda[0].astype(F32), diff_subln_g[0].astype(F32).reshape(1, LANES),
                qd, kd, vd, kd_m, vd_m, batch, seq)

    w_router = jnp.concatenate(
        [w_router_group[0], w_router_expert[0],
         jnp.zeros((D_MODEL, LANES - N_GROUPS - N_EXPERTS), F32)], axis=1)
    h2, f_packed, logits = _outproj(o_s, o_d, x2, w_out[0].astype(BF16),
                                    ffn_norm_g[0].reshape(1, D_MODEL), w_router, 512)

    wts, dest, blocks_e, blk_start, n_used = _route(logits, n_tok)
    n_blocks = n_tok * TOP_K // ROW_BLK + N_EXPERTS
    n_items = n_blocks // ITEM_BLKS + N_EXPERTS
    tok = jnp.repeat(jnp.arange(n_tok, dtype=jnp.int32), TOP_K)
    slot_tok = jnp.zeros((n_blocks * ROW_BLK,), jnp.int32).at[dest.reshape(-1)].set(tok)
    n_used = n_used.astype(jnp.int32).reshape(1)
    xs = _gather_rows(n_used, slot_tok, f_packed, n_blocks)
    item_e, item_r0, item_nb = _work_items(blocks_e, blk_start, n_items)
    y = _moe(item_e, item_r0, item_nb, n_used, xs, w_gate[0], w_up[0], w_down[0],
             n_items, n_blocks)

    out = _combine(dest.reshape(-1), h2, wts.astype(F32),
                   final_norm_g.reshape(1, D_MODEL), y, 256)
    return out.reshape(batch, seq, D_MODEL)
```

```python
import functools
import math

import jax
import jax.numpy as jnp
from jax import lax
from jax.experimental import pallas as pl
from jax.experimental.pallas import tpu as pltpu

D_MODEL = 2048
N_META = 16
CHUNK = 64
Q_BLOCK = 128
PAD_FRONT = Q_BLOCK - N_META
HEAD_DIM = 64
ROPE_THETA = 10000.0
SWA_Q_HEADS = 16
SWA_KV_HEADS = 2
DIFF_HEADS = 8
SWA_Q_W = SWA_Q_HEADS * HEAD_DIM
SWA_KV_W = SWA_KV_HEADS * HEAD_DIM
DIFF_W = DIFF_HEADS * 2 * HEAD_DIM
IN_WIDTH = SWA_Q_W + 2 * SWA_KV_W + 3 * DIFF_W
N_GROUPS = 8
EXPERTS_PER_GROUP = 8
N_EXPERTS = N_GROUPS * EXPERTS_PER_GROUP
TOP_K = 2
D_EXPERT = 1408
RMS_EPS = 1e-6
NEG_INF = -1e30
LAM_INIT = 0.8 - 0.6 * math.exp(-0.3 * 0)

LANES = 128
ROW_BLK = 128
VMEM_LIMIT = 56 * 1024 * 1024

BF16 = jnp.bfloat16
F32 = jnp.float32


def _nt_dot(a, b):
    return lax.dot_general(a, b, (((1,), (1,)), ((), ())), preferred_element_type=F32)


_SEGS = (
    ("qs", 0, 512, True, HEAD_DIM ** -0.5), ("qs", 512, 512, True, HEAD_DIM ** -0.5),
    ("ks", 1024, 128, True, 1.0), ("vs", 1152, 128, False, 1.0),
    ("qd", 1280, 512, True, HEAD_DIM ** -0.5), ("qd", 1792, 512, True, HEAD_DIM ** -0.5),
    ("kd", 2304, 512, True, 1.0), ("kd", 2816, 512, True, 1.0),
    ("vd", 3328, 512, False, 1.0), ("vd", 3840, 512, False, 1.0),
)
_SEG_BASE = {"qs": 0, "ks": 1024, "vs": 1152, "qd": 1280, "kd": 2304, "vd": 3328}


def _inproj_kernel(x_ref, g_ref, w_ref, cos_ref, sin_ref,
                   qs_ref, ks_ref, vs_ref, qd_ref, kd_ref, vd_ref):
    x = x_ref[...]
    ms = jnp.mean(x * x, axis=-1, keepdims=True)
    a = (x * lax.rsqrt(ms + RMS_EPS) * g_ref[...]).astype(BF16)
    cos = cos_ref[...]
    sin = sin_ref[...]
    lane = lax.broadcasted_iota(jnp.int32, cos.shape, 1)
    first_half = (lane % HEAD_DIM) < (HEAD_DIM // 2)
    low_head = lane < HEAD_DIM
    outs = {"qs": qs_ref, "ks": ks_ref, "vs": vs_ref, "qd": qd_ref, "kd": kd_ref, "vd": vd_ref}
    for name, c0, width, rotary, scale in _SEGS:
        y = jnp.dot(a, w_ref[:, c0:c0 + width], preferred_element_type=F32)
        o0 = c0 - _SEG_BASE[name]
        for t in range(width // LANES):
            yt = y[:, t * LANES:(t + 1) * LANES]
            if rotary:
                partner = jnp.where(first_half, pltpu.roll(yt, LANES - HEAD_DIM // 2, 1),
                                    pltpu.roll(yt, HEAD_DIM // 2, 1))
                yt = yt * cos + partner * sin
            if scale != 1.0:
                yt = yt * scale
            if name in ("ks", "vs"):
                other = pltpu.roll(yt, HEAD_DIM, 1)
                outs[name][:, 0:LANES] = jnp.where(low_head, yt, other).astype(BF16)
                outs[name][:, LANES:2 * LANES] = jnp.where(low_head, other, yt).astype(BF16)
            else:
                outs[name][:, o0 + t * LANES:o0 + (t + 1) * LANES] = yt.astype(BF16)


def _inproj(x2, g, w_bf, cos_t, sin_t, tm, rows_per_seq):
    rows = x2.shape[0]
    nseq = rows_per_seq // tm
    row_spec = lambda width: pl.BlockSpec((tm, width), lambda i: (i, 0))
    out_shapes = (
        jax.ShapeDtypeStruct((rows, SWA_Q_W), BF16), jax.ShapeDtypeStruct((rows, 2 * SWA_KV_W), BF16),
        jax.ShapeDtypeStruct((rows, 2 * SWA_KV_W), BF16), jax.ShapeDtypeStruct((rows, DIFF_W), BF16),
        jax.ShapeDtypeStruct((rows, DIFF_W), BF16), jax.ShapeDtypeStruct((rows, DIFF_W), BF16),
    )
    return pl.pallas_call(
        _inproj_kernel,
        grid=(rows // tm,),
        in_specs=[
            row_spec(D_MODEL),
            pl.BlockSpec((1, D_MODEL), lambda i: (0, 0)),
            pl.BlockSpec((D_MODEL, IN_WIDTH), lambda i: (0, 0), pipeline_mode=pl.Buffered(1)),
            pl.BlockSpec((tm, LANES), lambda i: (i % nseq, 0)),
            pl.BlockSpec((tm, LANES), lambda i: (i % nseq, 0)),
        ],
        out_specs=(row_spec(SWA_Q_W), row_spec(2 * SWA_KV_W), row_spec(2 * SWA_KV_W),
                   row_spec(DIFF_W), row_spec(DIFF_W), row_spec(DIFF_W)),
        out_shape=out_shapes,
        compiler_params=pltpu.CompilerParams(dimension_semantics=("arbitrary",),
                                             vmem_limit_bytes=VMEM_LIMIT),
        name="inproj",
    )(x2, g, w_bf, cos_t, sin_t)


def _rope_tables(positions):
    inv = 1.0 / (ROPE_THETA ** (jnp.arange(0, HEAD_DIM, 2, dtype=F32) / HEAD_DIM))
    ang = positions.astype(F32)[:, None] * inv[None, :]
    cos, sin = jnp.cos(ang), jnp.sin(ang)
    cos_t = jnp.tile(cos, (1, 2 * LANES // HEAD_DIM))
    sin_t = jnp.tile(jnp.concatenate([-sin, sin], axis=1), (1, LANES // HEAD_DIM))
    return cos_t, sin_t


def _swa_kernel(sink_ref, q_ref, kp_ref, kc_ref, km_ref, vp_ref, vc_ref, vm_ref, o_ref):
    i = pl.program_id(1)
    nk = 3 * Q_BLOCK
    row = lax.broadcasted_iota(jnp.int32, (2 * Q_BLOCK, nk), 0)
    col = lax.broadcasted_iota(jnp.int32, (2 * Q_BLOCK, nk), 1)
    second_half = (row % Q_BLOCK) >= CHUNK
    part = col // Q_BLOCK
    idx = col % Q_BLOCK
    ok_prev = (part == 0) & (i > 0) & ((idx >= CHUNK) | jnp.logical_not(second_half))
    ok_cur = (part == 1) & ((idx < CHUNK) | second_half)
    ok_meta = (part == 2) & (idx >= PAD_FRONT)
    mask = ok_prev | ok_cur | ok_meta
    upper_rows = lax.broadcasted_iota(jnp.int32, (2 * Q_BLOCK, 1), 0) >= Q_BLOCK
    lane = lax.broadcasted_iota(jnp.int32, (Q_BLOCK, LANES), 1)
    low_head = lane < HEAD_DIM
    zero = jnp.zeros((Q_BLOCK, LANES), BF16)
    for g in range(SWA_KV_HEADS):
        gl = slice(g * LANES, (g + 1) * LANES)
        k_all = jnp.concatenate([kp_ref[:, gl], kc_ref[:, gl], km_ref[:, gl]], axis=0)
        v_all = jnp.concatenate([vp_ref[:, gl], vc_ref[:, gl], vm_ref[:, gl]], axis=0)
        pairs = SWA_Q_HEADS // SWA_KV_HEADS // 2
        for c in range(g * pairs, (g + 1) * pairs):
            qp = q_ref[:, c * LANES:(c + 1) * LANES]
            q2 = jnp.concatenate([jnp.where(low_head, qp, zero), jnp.where(low_head, zero, qp)], axis=0)
            s = jnp.where(mask, _nt_dot(q2, k_all), NEG_INF)
            sink = jnp.where(upper_rows, sink_ref[2 * c + 1], sink_ref[2 * c])
            m = jnp.maximum(jnp.max(s, axis=-1, keepdims=True), sink)
            p = jnp.exp(s - m)
            denom = jnp.sum(p, axis=-1, keepdims=True) + jnp.exp(sink - m)
            o2 = jnp.dot(p.astype(BF16), v_all, preferred_element_type=F32) / denom
            o_ref[:, c * LANES:(c + 1) * LANES] = jnp.where(
                low_head, o2[:Q_BLOCK], o2[Q_BLOCK:]).astype(BF16)


def _swa(sinks, qs, ks, vs, ks_meta, vs_meta, batch, nblk):
    rows = qs.shape[0]
    cur = lambda b, i: (b * nblk + i, 0)
    prev = lambda b, i: (b * nblk + jnp.maximum(i - 1, 0), 0)
    kv_w = 2 * SWA_KV_W
    return pl.pallas_call(
        _swa_kernel,
        grid=(batch, nblk),
        in_specs=[
            pl.BlockSpec(memory_space=pltpu.SMEM),
            pl.BlockSpec((Q_BLOCK, SWA_Q_W), cur),
            pl.BlockSpec((Q_BLOCK, kv_w), prev), pl.BlockSpec((Q_BLOCK, kv_w), cur),
            pl.BlockSpec((Q_BLOCK, kv_w), lambda b, i: (0, 0)),
            pl.BlockSpec((Q_BLOCK, kv_w), prev), pl.BlockSpec((Q_BLOCK, kv_w), cur),
            pl.BlockSpec((Q_BLOCK, kv_w), lambda b, i: (0, 0)),
        ],
        out_specs=pl.BlockSpec((Q_BLOCK, SWA_Q_W), cur),
        out_shape=jax.ShapeDtypeStruct((rows, SWA_Q_W), BF16),
        compiler_params=pltpu.CompilerParams(dimension_semantics=("arbitrary", "arbitrary"),
                                             vmem_limit_bytes=VMEM_LIMIT),
        name="swa",
    )(sinks, qs, ks, ks, ks_meta, vs, vs, vs_meta)


DIFF_TQ = 256
DIFF_TK = 512


def _diff_kernel(lam_ref, g_ref, q_ref, k_ref, v_ref, km_ref, vm_ref, o_ref, *, seq):
    dl = lam_ref[...]
    lam = (jnp.exp(jnp.sum(dl[0:1] * dl[1:2], axis=-1, keepdims=True))
           - jnp.exp(jnp.sum(dl[2:3] * dl[3:4], axis=-1, keepdims=True)) + LAM_INIT)
    lane = lax.broadcasted_iota(jnp.int32, (DIFF_TQ, LANES), 1)
    low_map = lane < HEAD_DIM
    zero = jnp.zeros((DIFF_TQ, LANES), BF16)
    meta_ok = lax.broadcasted_iota(jnp.int32, (2 * DIFF_TQ, Q_BLOCK), 1) >= PAD_FRONT
    chunk_gap = (lax.broadcasted_iota(jnp.int32, (2 * DIFF_TQ, DIFF_TK), 1) // CHUNK
                 - (lax.broadcasted_iota(jnp.int32, (2 * DIFF_TQ, DIFF_TK), 0) % DIFF_TQ) // CHUNK)

    def step(q2, k_t, v_t, mask, state):
        s = _nt_dot(q2, k_t)
        if mask is not None:
            s = jnp.where(mask, s, NEG_INF)
        smax = jnp.max(s, axis=-1, keepdims=True)
        if state is None:
            p = jnp.exp(s - smax)
            return smax, jnp.sum(p, axis=-1, keepdims=True), jnp.dot(
                p.astype(BF16), v_t, preferred_element_type=F32)
        m_old, l_old, acc = state
        m_new = jnp.maximum(m_old, smax)
        alpha = jnp.exp(m_old - m_new)
        p = jnp.exp(s - m_new)
        return (m_new, alpha * l_old + jnp.sum(p, axis=-1, keepdims=True),
                alpha * acc + jnp.dot(p.astype(BF16), v_t, preferred_element_type=F32))

    def q_tile(qi, carry):
        q0 = pl.multiple_of(qi * DIFF_TQ, DIFF_TQ)
        q = q_ref[pl.ds(q0, DIFF_TQ), :]
        q2 = jnp.concatenate([jnp.where(low_map, q, zero), jnp.where(low_map, zero, q)], axis=0)
        st = step(q2, km_ref[...], vm_ref[...], meta_ok, None)

        def k_tile(kj, st):
            k0 = pl.multiple_of(kj * DIFF_TK, DIFF_TK)
            return step(q2, k_ref[pl.ds(k0, DIFF_TK), :], v_ref[pl.ds(k0, DIFF_TK), :], None, st)

        n_full = q0 // DIFF_TK
        st = lax.fori_loop(0, n_full, k_tile, st)
        k0 = pl.multiple_of(n_full * DIFF_TK, DIFF_TK)
        diag_ok = chunk_gap <= (q0 - k0) // CHUNK
        _, l, acc = step(q2, k_ref[pl.ds(k0, DIFF_TK), :], v_ref[pl.ds(k0, DIFF_TK), :], diag_ok, st)
        o = acc / l
        o = o[:DIFF_TQ] - lam * o[DIFF_TQ:]
        o = o * lax.rsqrt(jnp.mean(o * o, axis=-1, keepdims=True) + RMS_EPS)
        o_ref[pl.ds(q0, DIFF_TQ), :] = (o * g_ref[...] * (1.0 - LAM_INIT)).astype(BF16)
        return carry

    lax.fori_loop(0, seq // DIFF_TQ, q_tile, 0)


def _diff(diff_lambda, subln_g, qd, kd, vd, kd_meta, vd_meta, batch, seq):
    rows = qd.shape[0]
    head = lambda b, h: (b, h)
    meta = lambda b, h: (0, h)
    return pl.pallas_call(
        functools.partial(_diff_kernel, seq=seq),
        grid=(batch, DIFF_HEADS),
        in_specs=[
            pl.BlockSpec((4, HEAD_DIM), lambda b, h: (0, 0)),
            pl.BlockSpec((1, LANES), lambda b, h: (0, 0)),
            pl.BlockSpec((seq, LANES), head), pl.BlockSpec((seq, LANES), head),
            pl.BlockSpec((seq, LANES), head),
            pl.BlockSpec((Q_BLOCK, LANES), meta), pl.BlockSpec((Q_BLOCK, LANES), meta),
        ],
        out_specs=pl.BlockSpec((seq, LANES), head),
        out_shape=jax.ShapeDtypeStruct((rows, DIFF_W), BF16),
        compiler_params=pltpu.CompilerParams(dimension_semantics=("arbitrary", "arbitrary"),
                                             vmem_limit_bytes=VMEM_LIMIT),
        name="diffattn",
    )(diff_lambda, subln_g, qd, kd, vd, kd_meta, vd_meta)


HALF = D_MODEL // 2


def _pack_bf16_pair(a, b):
    hi = pltpu.bitcast(a.astype(BF16).astype(F32), jnp.uint32)
    lo = pltpu.bitcast(b.astype(BF16).astype(F32), jnp.uint32)
    return hi | (lo >> 16)


def _unpack_bf16_pair(w):
    a = pltpu.bitcast(w & jnp.uint32(0xFFFF0000), F32)
    b = pltpu.bitcast(w << 16, F32)
    return a, b


OUT_CHUNK = 512


def _outproj_kernel(os_ref, od_ref, x_ref, w_ref, g_ref, wr_ref, h_ref, fp_ref, lg_ref):
    tm = x_ref.shape[0]
    ssq = jnp.zeros((tm, 1), F32)
    for c in range(D_MODEL // OUT_CHUNK):
        cols = slice(c * OUT_CHUNK, (c + 1) * OUT_CHUNK)
        y = jnp.dot(os_ref[...], w_ref[0:SWA_Q_W, cols], preferred_element_type=F32)
        y = y + jnp.dot(od_ref[...], w_ref[SWA_Q_W:, cols], preferred_element_type=F32)
        h = x_ref[:, cols] + y
        h_ref[:, cols] = h
        ssq = ssq + jnp.sum(h * h, axis=-1, keepdims=True)
    inv = lax.rsqrt(ssq * (1.0 / D_MODEL) + RMS_EPS)
    lg = jnp.zeros((tm, LANES), F32)
    for c in range(HALF // OUT_CHUNK):
        ca = slice(c * OUT_CHUNK, (c + 1) * OUT_CHUNK)
        cb = slice(HALF + c * OUT_CHUNK, HALF + (c + 1) * OUT_CHUNK)
        fa = h_ref[:, ca] * inv * g_ref[:, ca]
        fb = h_ref[:, cb] * inv * g_ref[:, cb]
        fp_ref[:, ca] = _pack_bf16_pair(fa, fb)
        for f, cc in ((fa, ca), (fb, cb)):
            f_hi = f.astype(BF16)
            f_lo = (f - f_hi.astype(F32)).astype(BF16)
            w = wr_ref[cc, :]
            w_hi = w.astype(BF16)
            w_lo = (w - w_hi.astype(F32)).astype(BF16)
            p = jnp.dot(f_hi, jnp.concatenate([w_hi, w_lo], axis=1), preferred_element_type=F32)
            lg = lg + p[:, 0:LANES] + p[:, LANES:] + jnp.dot(f_lo, w_hi, preferred_element_type=F32)
    lg_ref[...] = lg


def _outproj(o_s, o_d, x2, w_bf, g, w_router, tm):
    rows = x2.shape[0]
    row_spec = lambda width: pl.BlockSpec((tm, width), lambda i: (i, 0))
    return pl.pallas_call(
        _outproj_kernel,
        grid=(rows // tm,),
        in_specs=[
            row_spec(SWA_Q_W), row_spec(DIFF_W), row_spec(D_MODEL),
            pl.BlockSpec((D_MODEL, D_MODEL), lambda i: (0, 0), pipeline_mode=pl.Buffered(1)),
            pl.BlockSpec((1, D_MODEL), lambda i: (0, 0)),
            pl.BlockSpec((D_MODEL, LANES), lambda i: (0, 0)),
        ],
        out_specs=(row_spec(D_MODEL), row_spec(HALF), row_spec(LANES)),
        out_shape=(jax.ShapeDtypeStruct((rows, D_MODEL), F32),
                   jax.ShapeDtypeStruct((rows, HALF), jnp.uint32),
                   jax.ShapeDtypeStruct((rows, LANES), F32)),
        compiler_params=pltpu.CompilerParams(dimension_semantics=("arbitrary",),
                                             vmem_limit_bytes=VMEM_LIMIT),
        name="outproj",
    )(o_s, o_d, x2, w_bf, g, w_router)


GATHER_BLK = 2 * ROW_BLK


def _gather_rows_kernel(nblk_ref, cur_ref, nxt_ref, src_ref, o_ref, raw, sem):
    i = pl.program_id(0)
    slot = i % 2
    n_live = (nblk_ref[0] * ROW_BLK + GATHER_BLK - 1) // GATHER_BLK

    def fetch(idx_ref, buf):
        def issue(r, c):
            pltpu.make_async_copy(src_ref.at[pl.ds(idx_ref[r], 1)], raw.at[buf, pl.ds(r, 1)],
                                  sem.at[buf]).start()
            return c
        lax.fori_loop(0, GATHER_BLK, issue, 0, unroll=8)

    @pl.when(i == 0)
    def _():
        fetch(cur_ref, 0)

    @pl.when(i + 1 < n_live)
    def _():
        fetch(nxt_ref, 1 - slot)

    @pl.when(i < n_live)
    def _():
        pltpu.make_async_copy(raw.at[slot], raw.at[slot], sem.at[slot]).wait()
        a, b = _unpack_bf16_pair(raw[slot])
        o_ref[:, 0:HALF] = a.astype(BF16)
        o_ref[:, HALF:] = b.astype(BF16)

    @pl.when(i >= n_live)
    def _():
        o_ref[...] = jnp.zeros_like(o_ref)


def _gather_rows(n_used_blocks, slot_tok, src, n_blocks):
    steps = n_blocks * ROW_BLK // GATHER_BLK
    idx_spec = lambda fn: pl.BlockSpec((GATHER_BLK,), fn, memory_space=pltpu.SMEM)
    return pl.pallas_call(
        _gather_rows_kernel,
        grid_spec=pltpu.PrefetchScalarGridSpec(
            num_scalar_prefetch=1,
            grid=(steps,),
            in_specs=[
                idx_spec(lambda i, n: (i,)),
                idx_spec(lambda i, n: (jnp.minimum(i + 1, steps - 1),)),
                pl.BlockSpec(memory_space=pl.ANY),
            ],
            out_specs=pl.BlockSpec((GATHER_BLK, D_MODEL), lambda i, n: (i, 0)),
            scratch_shapes=[pltpu.VMEM((2, GATHER_BLK, HALF), jnp.uint32),
                            pltpu.SemaphoreType.DMA((2,))],
        ),
        out_shape=jax.ShapeDtypeStruct((n_blocks * ROW_BLK, D_MODEL), BF16),
        compiler_params=pltpu.CompilerParams(dimension_semantics=("arbitrary",),
                                             vmem_limit_bytes=VMEM_LIMIT),
        name="gather_rows",
    )(n_used_blocks, slot_tok, slot_tok, src)


ITEM_BLKS = 8
ITEM_ROWS = ITEM_BLKS * ROW_BLK
F_TILE = 128
N_FT = D_EXPERT // F_TILE
Y_TILE = 256
N_YT = D_MODEL // Y_TILE


def _moe_kernel(e_ref, r0_ref, nb_ref, meta_ref, xs_ref, wg_hbm, wu_hbm, wd_hbm, y_ref,
                xb, hid, ystage, wgf, wuf, wdf, wgu, wdr, xsem, ysem, wsem):
    n_used = meta_ref[0]
    n_live = meta_ref[1]

    def for_blocks(count, fn):
        def body(j, c):
            fn(j)
            return c
        lax.fori_loop(0, count, body, 0)

    def x_copy(item, buf, j):
        src = xs_ref.at[pl.ds((r0_ref[item] + j) * ROW_BLK, ROW_BLK)]
        return pltpu.make_async_copy(src, xb.at[buf, pl.ds(j * ROW_BLK, ROW_BLK)],
                                     xsem.at[buf, j])

    def w_copies(item, k, buf):
        e = e_ref[item]
        cols = pl.ds(pl.multiple_of(k * F_TILE, F_TILE), F_TILE)
        return (pltpu.make_async_copy(wg_hbm.at[e, :, cols], wgf.at[buf], wsem.at[0, buf]),
                pltpu.make_async_copy(wu_hbm.at[e, :, cols], wuf.at[buf], wsem.at[1, buf]),
                pltpu.make_async_copy(wd_hbm.at[e, cols, :], wdf.at[buf], wsem.at[2, buf]))

    def zero_copy(j):
        dst = y_ref.at[pl.ds((n_used + j) * ROW_BLK, ROW_BLK)]
        return pltpu.make_async_copy(ystage.at[pl.ds(0, ROW_BLK)], dst, ysem.at[0])
    n_unused = y_ref.shape[0] // ROW_BLK - n_used
    ystage[0:ROW_BLK, :] = jnp.zeros((ROW_BLK, HALF), jnp.uint32)
    for_blocks(n_unused, lambda j: zero_copy(j).start())
    for_blocks(n_unused, lambda j: zero_copy(j).wait())

    for_blocks(nb_ref[0], lambda j: x_copy(0, 0, j).start())

    @pl.when(nb_ref[0] > 0)
    def _():
        for c in w_copies(0, 0, 0):
            c.start()

    def item_body(i, carry):
        slot = i % 2
        nb = nb_ref[i]
        prev = jnp.maximum(i - 1, 0)
        nb_prev = jnp.where(i > 0, nb_ref[prev], 0)

        def y_copy(j):
            dst = y_ref.at[pl.ds((r0_ref[prev] + j) * ROW_BLK, ROW_BLK)]
            return pltpu.make_async_copy(ystage.at[pl.ds(j * ROW_BLK, ROW_BLK)], dst, ysem.at[0])

        for_blocks(nb_ref[i + 1], lambda j: x_copy(i + 1, 1 - slot, j).start())
        for_blocks(nb, lambda j: x_copy(i, slot, j).wait())

        def tile_body(k, carry):
            wbuf = (i + k) % 2
            last = k == N_FT - 1
            nxt_i = jnp.where(last, i + 1, i)
            nxt_k = jnp.where(last, 0, k + 1)

            @pl.when(nb_ref[nxt_i] > 0)
            def _():
                for c in w_copies(nxt_i, nxt_k, 1 - wbuf):
                    c.start()

            @pl.when(nb > 0)
            def _():
                for c in w_copies(i, k, wbuf):
                    c.wait()
                wgu[:, 0:F_TILE] = wgf[wbuf].astype(BF16)
                wgu[:, F_TILE:] = wuf[wbuf].astype(BF16)
                f_rows = pl.ds(pl.multiple_of(k * F_TILE, F_TILE), F_TILE)
                for n in range(N_YT):
                    wdr[slot, n, f_rows, :] = wdf[wbuf, :, n * Y_TILE:(n + 1) * Y_TILE].astype(BF16)

            def gate_up(row0, nrows):
                rows = pl.ds(pl.multiple_of(row0, ROW_BLK), nrows)
                gu = jnp.dot(xb[slot, rows, :], wgu[...], preferred_element_type=F32)
                gate, up = gu[:, 0:F_TILE], gu[:, F_TILE:]
                h = (gate * (1.0 / (1.0 + jnp.exp(-gate))) * up).astype(BF16)
                hid[slot, rows, pl.ds(pl.multiple_of(k * F_TILE, F_TILE), F_TILE)] = h

            k_y = jnp.minimum(k, N_YT - 1)

            def down(row0, nrows):
                rows = pl.ds(pl.multiple_of(row0, ROW_BLK), nrows)
                yt = jnp.dot(hid[1 - slot, rows, :], wdr[1 - slot, k_y], preferred_element_type=F32)
                packed = _pack_bf16_pair(yt[:, 0:Y_TILE // 2], yt[:, Y_TILE // 2:])
                words = pl.ds(pl.multiple_of(k_y * (Y_TILE // 2), Y_TILE // 2), Y_TILE // 2)
                ystage[rows, words] = packed

            def for_row_tiles(n_blk, fn):
                big = 4 * ROW_BLK
                for_blocks(n_blk // 4, lambda t: fn(t * big, big))
                for bit in (2, 1):
                    @pl.when((n_blk & bit) != 0)
                    def _():
                        fn((n_blk // (2 * bit)) * (2 * bit) * ROW_BLK, bit * ROW_BLK)

            for_row_tiles(nb, gate_up)
            for_row_tiles(jnp.where(k < N_YT, nb_prev, 0), down)

            @pl.when(k == N_YT)
            def _():
                for_blocks(nb_prev, lambda j: y_copy(j).start())

            @pl.when(last)
            def _():
                for_blocks(nb_prev, lambda j: y_copy(j).wait())
            return carry

        lax.fori_loop(0, N_FT, tile_body, 0)
        return carry

    lax.fori_loop(0, n_live + 1, item_body, 0)


def _moe(item_e, item_r0, item_nb, meta, xs, w_gate, w_up, w_down, n_blocks):
    return pl.pallas_call(
        _moe_kernel,
        grid_spec=pltpu.PrefetchScalarGridSpec(
            num_scalar_prefetch=4,
            grid=(1,),
            in_specs=[pl.BlockSpec(memory_space=pl.ANY)] * 4,
            out_specs=pl.BlockSpec(memory_space=pl.ANY),
            scratch_shapes=[
                pltpu.VMEM((2, ITEM_ROWS, D_MODEL), BF16),
                pltpu.VMEM((2, ITEM_ROWS, D_EXPERT), BF16),
                pltpu.VMEM((ITEM_ROWS, HALF), jnp.uint32),
                pltpu.VMEM((2, D_MODEL, F_TILE), F32),
                pltpu.VMEM((2, D_MODEL, F_TILE), F32),
                pltpu.VMEM((2, F_TILE, D_MODEL), F32),
                pltpu.VMEM((D_MODEL, 2 * F_TILE), BF16),
                pltpu.VMEM((2, N_YT, D_EXPERT, Y_TILE), BF16),
                pltpu.SemaphoreType.DMA((2, ITEM_BLKS)),
                pltpu.SemaphoreType.DMA((1,)),
                pltpu.SemaphoreType.DMA((3, 2)),
            ],
        ),
        out_shape=jax.ShapeDtypeStruct((n_blocks * ROW_BLK, HALF), jnp.uint32),
        compiler_params=pltpu.CompilerParams(dimension_semantics=("arbitrary",),
                                             vmem_limit_bytes=VMEM_LIMIT),
        name="moe",
    )(item_e, item_r0, item_nb, meta, xs, w_gate, w_up, w_down)


def _combine_kernel(cur_ref, nxt_ref, h_ref, w_ref, g_ref, y_ref, o_ref, ya, yb, sem, *, tm):
    i = pl.program_id(0)
    n = pl.num_programs(0)
    slot = i % 2

    def fetch(idx_ref, buf):
        def issue(r, c):
            pltpu.make_async_copy(y_ref.at[pl.ds(idx_ref[TOP_K * r], 1)], ya.at[buf, pl.ds(r, 1)],
                                  sem.at[buf]).start()
            pltpu.make_async_copy(y_ref.at[pl.ds(idx_ref[TOP_K * r + 1], 1)], yb.at[buf, pl.ds(r, 1)],
                                  sem.at[buf]).start()
            return c
        lax.fori_loop(0, tm, issue, 0, unroll=8)

    @pl.when(i == 0)
    def _():
        fetch(cur_ref, 0)

    @pl.when(i + 1 < n)
    def _():
        fetch(nxt_ref, 1 - slot)

    pltpu.make_async_copy(ya.at[slot], ya.at[slot], sem.at[slot]).wait()
    pltpu.make_async_copy(yb.at[slot], yb.at[slot], sem.at[slot]).wait()

    wa = w_ref[:, 0:1]
    wb = w_ref[:, 1:2]
    half_t = Y_TILE // 2
    ssq = jnp.zeros((tm, 1), F32)
    for n in range(N_YT):
        words = slice(n * half_t, (n + 1) * half_t)
        a_lo, a_hi = _unpack_bf16_pair(ya[slot, :, words])
        b_lo, b_hi = _unpack_bf16_pair(yb[slot, :, words])
        lo_cols = slice(n * Y_TILE, n * Y_TILE + half_t)
        hi_cols = slice(n * Y_TILE + half_t, (n + 1) * Y_TILE)
        lo = h_ref[:, lo_cols] + wa * a_lo + wb * b_lo
        hi = h_ref[:, hi_cols] + wa * a_hi + wb * b_hi
        ssq = ssq + jnp.sum(lo * lo, axis=-1, keepdims=True) + jnp.sum(hi * hi, axis=-1, keepdims=True)
        o_ref[:, lo_cols] = lo
        o_ref[:, hi_cols] = hi
    inv = lax.rsqrt(ssq * (1.0 / D_MODEL) + RMS_EPS)
    o_ref[...] = o_ref[...] * inv * g_ref[...]


def _combine(dest_flat, h2, wts, g, y, tm):
    rows = h2.shape[0]
    steps = rows // tm
    idx_spec = lambda fn: pl.BlockSpec((TOP_K * tm,), fn, memory_space=pltpu.SMEM)
    return pl.pallas_call(
        functools.partial(_combine_kernel, tm=tm),
        grid=(steps,),
        in_specs=[
            idx_spec(lambda i: (i,)),
            idx_spec(lambda i: (jnp.minimum(i + 1, steps - 1),)),
            pl.BlockSpec((tm, D_MODEL), lambda i: (i, 0)),
            pl.BlockSpec((tm, TOP_K), lambda i: (i, 0)),
            pl.BlockSpec((1, D_MODEL), lambda i: (0, 0)),
            pl.BlockSpec(memory_space=pl.ANY),
        ],
        out_specs=pl.BlockSpec((tm, D_MODEL), lambda i: (i, 0)),
        scratch_shapes=[
            pltpu.VMEM((2, tm, HALF), jnp.uint32),
            pltpu.VMEM((2, tm, HALF), jnp.uint32),
            pltpu.SemaphoreType.DMA((2,)),
        ],
        out_shape=jax.ShapeDtypeStruct((rows, D_MODEL), F32),
        compiler_params=pltpu.CompilerParams(dimension_semantics=("arbitrary",),
                                             vmem_limit_bytes=VMEM_LIMIT),
        name="combine",
    )(dest_flat, dest_flat, h2, wts, g, y)


def _route(logits, n_tok):
    pg = jax.nn.softmax(logits[:, :N_GROUPS], axis=-1)
    g_sel = jnp.argmax(pg, axis=-1)
    gate_g = jnp.take_along_axis(pg, g_sel[:, None], axis=1)
    le = logits[:, N_GROUPS:N_GROUPS + N_EXPERTS].reshape(n_tok, N_GROUPS, EXPERTS_PER_GROUP)
    le = jnp.take_along_axis(le, g_sel[:, None, None], axis=1)[:, 0]
    top_p, top_i = lax.top_k(jax.nn.softmax(le, axis=-1), TOP_K)
    wts = gate_g * top_p / jnp.sum(top_p, axis=-1, keepdims=True)
    eid = (g_sel[:, None] * EXPERTS_PER_GROUP + top_i).astype(jnp.int32)

    n_asg = n_tok * TOP_K
    e_flat = eid.reshape(n_asg)
    onehot = (e_flat[:, None] == jnp.arange(N_EXPERTS, dtype=jnp.int32)[None, :]).astype(jnp.int32)
    csum = jnp.cumsum(onehot, axis=0)
    rank = jnp.take_along_axis(csum, e_flat[:, None], axis=1)[:, 0] - 1
    counts = csum[-1]
    blocks_e = (counts + ROW_BLK - 1) // ROW_BLK
    blk_end = jnp.cumsum(blocks_e)
    blk_start = blk_end - blocks_e
    dest = (blk_start[e_flat] * ROW_BLK + rank).astype(jnp.int32)
    return wts, dest.reshape(n_tok, TOP_K), blocks_e, blk_start, blk_end[-1]


def _work_items(blocks_e, blk_start, n_items):
    items_e = (blocks_e + ITEM_BLKS - 1) // ITEM_BLKS
    item_end = jnp.cumsum(items_e)
    item_start = item_end - items_e
    total = item_end[-1]
    idx = jnp.arange(n_items, dtype=jnp.int32)
    live = idx < total
    ref = jnp.minimum(idx, total - 1)
    e = jnp.searchsorted(item_end, ref, side='right').astype(jnp.int32)
    j = ref - item_start[e]
    r0 = blk_start[e] + ITEM_BLKS * j
    nb = jnp.where(live, jnp.clip(blocks_e[e] - ITEM_BLKS * j, 0, ITEM_BLKS), 0)
    return e, r0.astype(jnp.int32), nb.astype(jnp.int32), total.astype(jnp.int32)


def kernel(x, meta_tokens, attn_norm_g, w_in, swa_sinks, diff_lambda, diff_subln_g, w_out,
           ffn_norm_g, w_router_group, w_router_expert, w_gate, w_up, w_down, final_norm_g):
    batch, seq, _ = x.shape
    n_tok = batch * seq
    nblk = seq // Q_BLOCK
    x2 = x.reshape(n_tok, D_MODEL)

    w_in_bf = w_in[0].astype(BF16)
    g_attn = attn_norm_g[0].reshape(1, D_MODEL)
    cos_x, sin_x = _rope_tables(N_META + jnp.arange(seq))
    cos_m, sin_m = _rope_tables(jnp.maximum(jnp.arange(Q_BLOCK) - PAD_FRONT, 0))
    qs, ks, vs, qd, kd, vd = _inproj(x2, g_attn, w_in_bf, cos_x, sin_x, 512, seq)
    meta_blk = jnp.concatenate(
        [jnp.zeros((PAD_FRONT, D_MODEL), x.dtype), meta_tokens.astype(x.dtype)], axis=0)
    _, ks_m, vs_m, _, kd_m, vd_m = _inproj(meta_blk, g_attn, w_in_bf, cos_m, sin_m, Q_BLOCK, Q_BLOCK)

    o_s = _swa(swa_sinks[0].astype(F32), qs, ks, vs, ks_m, vs_m, batch, nblk)
    o_d = _diff(diff_lambda[0].astype(F32), diff_subln_g[0].astype(F32).reshape(1, LANES),
                qd, kd, vd, kd_m, vd_m, batch, seq)

    w_router = jnp.concatenate(
        [w_router_group[0], w_router_expert[0],
         jnp.zeros((D_MODEL, LANES - N_GROUPS - N_EXPERTS), F32)], axis=1)
    h2, f_packed, logits = _outproj(o_s, o_d, x2, w_out[0].astype(BF16),
                                    ffn_norm_g[0].reshape(1, D_MODEL), w_router, 512)

    wts, dest, blocks_e, blk_start, n_used = _route(logits, n_tok)
    n_blocks = n_tok * TOP_K // ROW_BLK + N_EXPERTS
    n_items = n_blocks // ITEM_BLKS + N_EXPERTS
    tok = jnp.repeat(jnp.arange(n_tok, dtype=jnp.int32), TOP_K)
    slot_tok = jnp.zeros((n_blocks * ROW_BLK,), jnp.int32).at[dest.reshape(-1)].set(tok)
    n_used = n_used.astype(jnp.int32).reshape(1)
    xs = _gather_rows(n_used, slot_tok, f_packed, n_blocks)
    item_e, item_r0, item_nb, n_live = _work_items(blocks_e, blk_start, n_items)
    y = _moe(item_e, item_r0, item_nb, jnp.concatenate([n_used, n_live.reshape(1)]), xs,
             w_gate[0], w_up[0], w_down[0], n_blocks)

    out = _combine(dest.reshape(-1), h2, wts.astype(F32),
                   final_norm_g.reshape(1, D_MODEL), y, 256)
    return out.reshape(batch, seq, D_MODEL)
```

```python
import functools
import math

import jax
import jax.numpy as jnp
from jax import lax
from jax.experimental import pallas as pl
from jax.experimental.pallas import tpu as pltpu

D_MODEL = 2048
N_META = 16
CHUNK = 64
Q_BLOCK = 128
PAD_FRONT = Q_BLOCK - N_META
HEAD_DIM = 64
ROPE_THETA = 10000.0
SWA_Q_HEADS = 16
SWA_KV_HEADS = 2
DIFF_HEADS = 8
SWA_Q_W = SWA_Q_HEADS * HEAD_DIM
SWA_KV_W = SWA_KV_HEADS * HEAD_DIM
DIFF_W = DIFF_HEADS * 2 * HEAD_DIM
IN_WIDTH = SWA_Q_W + 2 * SWA_KV_W + 3 * DIFF_W
N_GROUPS = 8
EXPERTS_PER_GROUP = 8
N_EXPERTS = N_GROUPS * EXPERTS_PER_GROUP
TOP_K = 2
D_EXPERT = 1408
RMS_EPS = 1e-6
NEG_INF = -1e30
LAM_INIT = 0.8 - 0.6 * math.exp(-0.3 * 0)

LANES = 128
ROW_BLK = 128
VMEM_LIMIT = 56 * 1024 * 1024

BF16 = jnp.bfloat16
F32 = jnp.float32


def _nt_dot(a, b):
    return lax.dot_general(a, b, (((1,), (1,)), ((), ())), preferred_element_type=F32)


_SEGS = (
    ("qs", 0, 512, True, HEAD_DIM ** -0.5), ("qs", 512, 512, True, HEAD_DIM ** -0.5),
    ("ks", 1024, 128, True, 1.0), ("vs", 1152, 128, False, 1.0),
    ("qd", 1280, 512, True, HEAD_DIM ** -0.5), ("qd", 1792, 512, True, HEAD_DIM ** -0.5),
    ("kd", 2304, 512, True, 1.0), ("kd", 2816, 512, True, 1.0),
    ("vd", 3328, 512, False, 1.0), ("vd", 3840, 512, False, 1.0),
)
_SEG_BASE = {"qs": 0, "ks": 1024, "vs": 1152, "qd": 1280, "kd": 2304, "vd": 3328}


def _inproj_kernel(x_ref, g_ref, w_ref, cos_ref, sin_ref,
                   qs_ref, ks_ref, vs_ref, qd_ref, kd_ref, vd_ref):
    x = x_ref[...]
    ms = jnp.mean(x * x, axis=-1, keepdims=True)
    a = (x * lax.rsqrt(ms + RMS_EPS) * g_ref[...]).astype(BF16)
    cos = cos_ref[...]
    sin = sin_ref[...]
    lane = lax.broadcasted_iota(jnp.int32, cos.shape, 1)
    first_half = (lane % HEAD_DIM) < (HEAD_DIM // 2)
    low_head = lane < HEAD_DIM
    outs = {"qs": qs_ref, "ks": ks_ref, "vs": vs_ref, "qd": qd_ref, "kd": kd_ref, "vd": vd_ref}
    for name, c0, width, rotary, scale in _SEGS:
        y = jnp.dot(a, w_ref[:, c0:c0 + width], preferred_element_type=F32)
        o0 = c0 - _SEG_BASE[name]
        for t in range(width // LANES):
            yt = y[:, t * LANES:(t + 1) * LANES]
            if rotary:
                partner = jnp.where(first_half, pltpu.roll(yt, LANES - HEAD_DIM // 2, 1),
                                    pltpu.roll(yt, HEAD_DIM // 2, 1))
                yt = yt * cos + partner * sin
            if scale != 1.0:
                yt = yt * scale
            if name in ("ks", "vs"):
                other = pltpu.roll(yt, HEAD_DIM, 1)
                outs[name][:, 0:LANES] = jnp.where(low_head, yt, other).astype(BF16)
                outs[name][:, LANES:2 * LANES] = jnp.where(low_head, other, yt).astype(BF16)
            else:
                outs[name][:, o0 + t * LANES:o0 + (t + 1) * LANES] = yt.astype(BF16)


def _inproj(x2, g, w_bf, cos_t, sin_t, tm, rows_per_seq):
    rows = x2.shape[0]
    nseq = rows_per_seq // tm
    row_spec = lambda width: pl.BlockSpec((tm, width), lambda i: (i, 0))
    out_shapes = (
        jax.ShapeDtypeStruct((rows, SWA_Q_W), BF16), jax.ShapeDtypeStruct((rows, 2 * SWA_KV_W), BF16),
        jax.ShapeDtypeStruct((rows, 2 * SWA_KV_W), BF16), jax.ShapeDtypeStruct((rows, DIFF_W), BF16),
        jax.ShapeDtypeStruct((rows, DIFF_W), BF16), jax.ShapeDtypeStruct((rows, DIFF_W), BF16),
    )
    return pl.pallas_call(
        _inproj_kernel,
        grid=(rows // tm,),
        in_specs=[
            row_spec(D_MODEL),
            pl.BlockSpec((1, D_MODEL), lambda i: (0, 0)),
            pl.BlockSpec((D_MODEL, IN_WIDTH), lambda i: (0, 0), pipeline_mode=pl.Buffered(1)),
            pl.BlockSpec((tm, LANES), lambda i: (i % nseq, 0)),
            pl.BlockSpec((tm, LANES), lambda i: (i % nseq, 0)),
        ],
        out_specs=(row_spec(SWA_Q_W), row_spec(2 * SWA_KV_W), row_spec(2 * SWA_KV_W),
                   row_spec(DIFF_W), row_spec(DIFF_W), row_spec(DIFF_W)),
        out_shape=out_shapes,
        compiler_params=pltpu.CompilerParams(dimension_semantics=("arbitrary",),
                                             vmem_limit_bytes=VMEM_LIMIT),
        name="inproj",
    )(x2, g, w_bf, cos_t, sin_t)


def _rope_tables(positions):
    inv = 1.0 / (ROPE_THETA ** (jnp.arange(0, HEAD_DIM, 2, dtype=F32) / HEAD_DIM))
    ang = positions.astype(F32)[:, None] * inv[None, :]
    cos, sin = jnp.cos(ang), jnp.sin(ang)
    cos_t = jnp.tile(cos, (1, 2 * LANES // HEAD_DIM))
    sin_t = jnp.tile(jnp.concatenate([-sin, sin], axis=1), (1, LANES // HEAD_DIM))
    return cos_t, sin_t


def _swa_kernel(sink_ref, q_ref, kp_ref, kc_ref, km_ref, vp_ref, vc_ref, vm_ref, o_ref):
    i = pl.program_id(1)
    nk = 3 * Q_BLOCK
    row = lax.broadcasted_iota(jnp.int32, (2 * Q_BLOCK, nk), 0)
    col = lax.broadcasted_iota(jnp.int32, (2 * Q_BLOCK, nk), 1)
    second_half = (row % Q_BLOCK) >= CHUNK
    part = col // Q_BLOCK
    idx = col % Q_BLOCK
    ok_prev = (part == 0) & (i > 0) & ((idx >= CHUNK) | jnp.logical_not(second_half))
    ok_cur = (part == 1) & ((idx < CHUNK) | second_half)
    ok_meta = (part == 2) & (idx >= PAD_FRONT)
    mask = ok_prev | ok_cur | ok_meta
    upper_rows = lax.broadcasted_iota(jnp.int32, (2 * Q_BLOCK, 1), 0) >= Q_BLOCK
    lane = lax.broadcasted_iota(jnp.int32, (Q_BLOCK, LANES), 1)
    low_head = lane < HEAD_DIM
    zero = jnp.zeros((Q_BLOCK, LANES), BF16)
    for g in range(SWA_KV_HEADS):
        gl = slice(g * LANES, (g + 1) * LANES)
        k_all = jnp.concatenate([kp_ref[:, gl], kc_ref[:, gl], km_ref[:, gl]], axis=0)
        v_all = jnp.concatenate([vp_ref[:, gl], vc_ref[:, gl], vm_ref[:, gl]], axis=0)
        pairs = SWA_Q_HEADS // SWA_KV_HEADS // 2
        for c in range(g * pairs, (g + 1) * pairs):
            qp = q_ref[:, c * LANES:(c + 1) * LANES]
            q2 = jnp.concatenate([jnp.where(low_head, qp, zero), jnp.where(low_head, zero, qp)], axis=0)
            s = jnp.where(mask, _nt_dot(q2, k_all), NEG_INF)
            sink = jnp.where(upper_rows, sink_ref[2 * c + 1], sink_ref[2 * c])
            m = jnp.maximum(jnp.max(s, axis=-1, keepdims=True), sink)
            p = jnp.exp(s - m)
            denom = jnp.sum(p, axis=-1, keepdims=True) + jnp.exp(sink - m)
            o2 = jnp.dot(p.astype(BF16), v_all, preferred_element_type=F32) / denom
            o_ref[:, c * LANES:(c + 1) * LANES] = jnp.where(
                low_head, o2[:Q_BLOCK], o2[Q_BLOCK:]).astype(BF16)


def _swa(sinks, qs, ks, vs, ks_meta, vs_meta, batch, nblk):
    rows = qs.shape[0]
    cur = lambda b, i: (b * nblk + i, 0)
    prev = lambda b, i: (b * nblk + jnp.maximum(i - 1, 0), 0)
    kv_w = 2 * SWA_KV_W
    return pl.pallas_call(
        _swa_kernel,
        grid=(batch, nblk),
        in_specs=[
            pl.BlockSpec(memory_space=pltpu.SMEM),
            pl.BlockSpec((Q_BLOCK, SWA_Q_W), cur),
            pl.BlockSpec((Q_BLOCK, kv_w), prev), pl.BlockSpec((Q_BLOCK, kv_w), cur),
            pl.BlockSpec((Q_BLOCK, kv_w), lambda b, i: (0, 0)),
            pl.BlockSpec((Q_BLOCK, kv_w), prev), pl.BlockSpec((Q_BLOCK, kv_w), cur),
            pl.BlockSpec((Q_BLOCK, kv_w), lambda b, i: (0, 0)),
        ],
        out_specs=pl.BlockSpec((Q_BLOCK, SWA_Q_W), cur),
        out_shape=jax.ShapeDtypeStruct((rows, SWA_Q_W), BF16),
        compiler_params=pltpu.CompilerParams(dimension_semantics=("arbitrary", "arbitrary"),
                                             vmem_limit_bytes=VMEM_LIMIT),
        name="swa",
    )(sinks, qs, ks, ks, ks_meta, vs, vs, vs_meta)


DIFF_TQ = 256
DIFF_TK = 512
DIFF_HEADS_PER_STEP = 2


def _diff_kernel(lam_ref, g_ref, q_ref, k_ref, v_ref, km_ref, vm_ref, o_ref, *, seq):
    dl = lam_ref[...]
    lam = (jnp.exp(jnp.sum(dl[0:1] * dl[1:2], axis=-1, keepdims=True))
           - jnp.exp(jnp.sum(dl[2:3] * dl[3:4], axis=-1, keepdims=True)) + LAM_INIT)
    lane = lax.broadcasted_iota(jnp.int32, (DIFF_TQ, LANES), 1)
    low_map = lane < HEAD_DIM
    zero = jnp.zeros((DIFF_TQ, LANES), BF16)
    meta_ok = lax.broadcasted_iota(jnp.int32, (2 * DIFF_TQ, Q_BLOCK), 1) >= PAD_FRONT
    chunk_gap = (lax.broadcasted_iota(jnp.int32, (2 * DIFF_TQ, DIFF_TK), 1) // CHUNK
                 - (lax.broadcasted_iota(jnp.int32, (2 * DIFF_TQ, DIFF_TK), 0) % DIFF_TQ) // CHUNK)

    def step(q2, k_t, v_t, mask, state):
        s = _nt_dot(q2, k_t)
        if mask is not None:
            s = jnp.where(mask, s, NEG_INF)
        smax = jnp.max(s, axis=-1, keepdims=True)
        if state is None:
            p = jnp.exp(s - smax)
            return smax, jnp.sum(p, axis=-1, keepdims=True), jnp.dot(
                p.astype(BF16), v_t, preferred_element_type=F32)
        m_old, l_old, acc = state
        m_new = jnp.maximum(m_old, smax)
        alpha = jnp.exp(m_old - m_new)
        p = jnp.exp(s - m_new)
        return (m_new, alpha * l_old + jnp.sum(p, axis=-1, keepdims=True),
                alpha * acc + jnp.dot(p.astype(BF16), v_t, preferred_element_type=F32))

    heads = [slice(h * LANES, (h + 1) * LANES) for h in range(DIFF_HEADS_PER_STEP)]

    def q_tile(qi, carry):
        q0 = pl.multiple_of(qi * DIFF_TQ, DIFF_TQ)
        q2s = []
        for hs in heads:
            q = q_ref[pl.ds(q0, DIFF_TQ), hs]
            q2s.append(jnp.concatenate(
                [jnp.where(low_map, q, zero), jnp.where(low_map, zero, q)], axis=0))

        def all_heads(rows, mask, states):
            out = ()
            for h, hs in enumerate(heads):
                st = None if states is None else states[3 * h:3 * h + 3]
                k_t = km_ref[:, hs] if rows is None else k_ref[rows, hs]
                v_t = vm_ref[:, hs] if rows is None else v_ref[rows, hs]
                out = out + step(q2s[h], k_t, v_t, mask, st)
            return out

        st = all_heads(None, meta_ok, None)

        def k_tile(kj, st):
            k0 = pl.multiple_of(kj * DIFF_TK, DIFF_TK)
            return all_heads(pl.ds(k0, DIFF_TK), None, st)

        n_full = q0 // DIFF_TK
        st = lax.fori_loop(0, n_full, k_tile, st)
        k0 = pl.multiple_of(n_full * DIFF_TK, DIFF_TK)
        diag_ok = chunk_gap <= (q0 - k0) // CHUNK
        st = all_heads(pl.ds(k0, DIFF_TK), diag_ok, st)
        for h, hs in enumerate(heads):
            o = st[3 * h + 2] / st[3 * h + 1]
            o = o[:DIFF_TQ] - lam * o[DIFF_TQ:]
            o = o * lax.rsqrt(jnp.mean(o * o, axis=-1, keepdims=True) + RMS_EPS)
            o_ref[pl.ds(q0, DIFF_TQ), hs] = (o * g_ref[...] * (1.0 - LAM_INIT)).astype(BF16)
        return carry

    lax.fori_loop(0, seq // DIFF_TQ, q_tile, 0)


def _diff(diff_lambda, subln_g, qd, kd, vd, kd_meta, vd_meta, batch, seq):
    rows = qd.shape[0]
    head = lambda b, h: (b, h)
    meta = lambda b, h: (0, h)
    width = DIFF_HEADS_PER_STEP * LANES
    return pl.pallas_call(
        functools.partial(_diff_kernel, seq=seq),
        grid=(batch, DIFF_HEADS // DIFF_HEADS_PER_STEP),
        in_specs=[
            pl.BlockSpec((4, HEAD_DIM), lambda b, h: (0, 0)),
            pl.BlockSpec((1, LANES), lambda b, h: (0, 0)),
            pl.BlockSpec((seq, width), head), pl.BlockSpec((seq, width), head),
            pl.BlockSpec((seq, width), head),
            pl.BlockSpec((Q_BLOCK, width), meta), pl.BlockSpec((Q_BLOCK, width), meta),
        ],
        out_specs=pl.BlockSpec((seq, width), head),
        out_shape=jax.ShapeDtypeStruct((rows, DIFF_W), BF16),
        compiler_params=pltpu.CompilerParams(dimension_semantics=("arbitrary", "arbitrary"),
                                             vmem_limit_bytes=VMEM_LIMIT),
        name="diffattn",
    )(diff_lambda, subln_g, qd, kd, vd, kd_meta, vd_meta)


HALF = D_MODEL // 2


def _pack_bf16_pair(a, b):
    hi = pltpu.bitcast(a.astype(BF16).astype(F32), jnp.uint32)
    lo = pltpu.bitcast(b.astype(BF16).astype(F32), jnp.uint32)
    return hi | (lo >> 16)


def _unpack_bf16_pair(w):
    a = pltpu.bitcast(w & jnp.uint32(0xFFFF0000), F32)
    b = pltpu.bitcast(w << 16, F32)
    return a, b


OUT_CHUNK = 512


def _outproj_kernel(os_ref, od_ref, x_ref, w_ref, g_ref, wr_ref, h_ref, fp_ref, lg_ref):
    tm = x_ref.shape[0]
    ssq = jnp.zeros((tm, 1), F32)
    for c in range(D_MODEL // OUT_CHUNK):
        cols = slice(c * OUT_CHUNK, (c + 1) * OUT_CHUNK)
        y = jnp.dot(os_ref[...], w_ref[0:SWA_Q_W, cols], preferred_element_type=F32)
        y = y + jnp.dot(od_ref[...], w_ref[SWA_Q_W:, cols], preferred_element_type=F32)
        h = x_ref[:, cols] + y
        h_ref[:, cols] = h
        ssq = ssq + jnp.sum(h * h, axis=-1, keepdims=True)
    inv = lax.rsqrt(ssq * (1.0 / D_MODEL) + RMS_EPS)
    lg = jnp.zeros((tm, LANES), F32)
    for c in range(HALF // OUT_CHUNK):
        ca = slice(c * OUT_CHUNK, (c + 1) * OUT_CHUNK)
        cb = slice(HALF + c * OUT_CHUNK, HALF + (c + 1) * OUT_CHUNK)
        fa = h_ref[:, ca] * inv * g_ref[:, ca]
        fb = h_ref[:, cb] * inv * g_ref[:, cb]
        fp_ref[:, ca] = _pack_bf16_pair(fa, fb)
        for f, cc in ((fa, ca), (fb, cb)):
            f_hi = f.astype(BF16)
            f_lo = (f - f_hi.astype(F32)).astype(BF16)
            w = wr_ref[cc, :]
            w_hi = w.astype(BF16)
            w_lo = (w - w_hi.astype(F32)).astype(BF16)
            p = jnp.dot(f_hi, jnp.concatenate([w_hi, w_lo], axis=1), preferred_element_type=F32)
            lg = lg + p[:, 0:LANES] + p[:, LANES:] + jnp.dot(f_lo, w_hi, preferred_element_type=F32)
    lg_ref[...] = lg


def _outproj(o_s, o_d, x2, w_bf, g, w_router, tm):
    rows = x2.shape[0]
    row_spec = lambda width: pl.BlockSpec((tm, width), lambda i: (i, 0))
    return pl.pallas_call(
        _outproj_kernel,
        grid=(rows // tm,),
        in_specs=[
            row_spec(SWA_Q_W), row_spec(DIFF_W), row_spec(D_MODEL),
            pl.BlockSpec((D_MODEL, D_MODEL), lambda i: (0, 0), pipeline_mode=pl.Buffered(1)),
            pl.BlockSpec((1, D_MODEL), lambda i: (0, 0)),
            pl.BlockSpec((D_MODEL, LANES), lambda i: (0, 0)),
        ],
        out_specs=(row_spec(D_MODEL), row_spec(HALF), row_spec(LANES)),
        out_shape=(jax.ShapeDtypeStruct((rows, D_MODEL), F32),
                   jax.ShapeDtypeStruct((rows, HALF), jnp.uint32),
                   jax.ShapeDtypeStruct((rows, LANES), F32)),
        compiler_params=pltpu.CompilerParams(dimension_semantics=("arbitrary",),
                                             vmem_limit_bytes=VMEM_LIMIT),
        name="outproj",
    )(o_s, o_d, x2, w_bf, g, w_router)


GATHER_BLK = 2 * ROW_BLK


def _gather_rows_kernel(nblk_ref, cur_ref, nxt_ref, src_ref, o_ref, raw, sem):
    i = pl.program_id(0)
    slot = i % 2
    n_live = (nblk_ref[0] * ROW_BLK + GATHER_BLK - 1) // GATHER_BLK

    def fetch(idx_ref, buf):
        def issue(r, c):
            pltpu.make_async_copy(src_ref.at[pl.ds(idx_ref[r], 1)], raw.at[buf, pl.ds(r, 1)],
                                  sem.at[buf]).start()
            return c
        lax.fori_loop(0, GATHER_BLK, issue, 0, unroll=8)

    @pl.when(i == 0)
    def _():
        fetch(cur_ref, 0)

    @pl.when(i + 1 < n_live)
    def _():
        fetch(nxt_ref, 1 - slot)

    @pl.when(i < n_live)
    def _():
        pltpu.make_async_copy(raw.at[slot], raw.at[slot], sem.at[slot]).wait()
        a, b = _unpack_bf16_pair(raw[slot])
        o_ref[:, 0:HALF] = a.astype(BF16)
        o_ref[:, HALF:] = b.astype(BF16)

    @pl.when(i >= n_live)
    def _():
        o_ref[...] = jnp.zeros_like(o_ref)


def _gather_rows(n_used_blocks, slot_tok, src, n_blocks):
    steps = n_blocks * ROW_BLK // GATHER_BLK
    idx_spec = lambda fn: pl.BlockSpec((GATHER_BLK,), fn, memory_space=pltpu.SMEM)
    return pl.pallas_call(
        _gather_rows_kernel,
        grid_spec=pltpu.PrefetchScalarGridSpec(
            num_scalar_prefetch=1,
            grid=(steps,),
            in_specs=[
                idx_spec(lambda i, n: (i,)),
                idx_spec(lambda i, n: (jnp.minimum(i + 1, steps - 1),)),
                pl.BlockSpec(memory_space=pl.ANY),
            ],
            out_specs=pl.BlockSpec((GATHER_BLK, D_MODEL), lambda i, n: (i, 0)),
            scratch_shapes=[pltpu.VMEM((2, GATHER_BLK, HALF), jnp.uint32),
                            pltpu.SemaphoreType.DMA((2,))],
        ),
        out_shape=jax.ShapeDtypeStruct((n_blocks * ROW_BLK, D_MODEL), BF16),
        compiler_params=pltpu.CompilerParams(dimension_semantics=("arbitrary",),
                                             vmem_limit_bytes=VMEM_LIMIT),
        name="gather_rows",
    )(n_used_blocks, slot_tok, slot_tok, src)


ITEM_BLKS = 8
ITEM_ROWS = ITEM_BLKS * ROW_BLK
F_TILE = 256
N_FT = -(-D_EXPERT // F_TILE)
Y_TILE = 512
N_YT = D_MODEL // Y_TILE


def _f_tile_start(k):
    return pl.multiple_of(jnp.minimum(k * F_TILE, D_EXPERT - F_TILE), LANES)


def _moe_kernel(e_ref, r0_ref, nb_ref, meta_ref, xs_ref, wg_hbm, wu_hbm, wd_hbm, y_ref,
                xb, hid, ystage, wgf, wuf, wdf, wgu, wdr, xsem, ysem, wsem):
    n_used = meta_ref[0]
    n_live = meta_ref[1]

    def for_blocks(count, fn):
        def body(j, c):
            fn(j)
            return c
        lax.fori_loop(0, count, body, 0)

    def x_copy(item, buf, j):
        src = xs_ref.at[pl.ds((r0_ref[item] + j) * ROW_BLK, ROW_BLK)]
        return pltpu.make_async_copy(src, xb.at[buf, pl.ds(j * ROW_BLK, ROW_BLK)],
                                     xsem.at[buf, j])

    def w_copies(item, k, buf):
        e = e_ref[item]
        cols = pl.ds(_f_tile_start(k), F_TILE)
        return (pltpu.make_async_copy(wg_hbm.at[e, :, cols], wgf.at[buf], wsem.at[0, buf]),
                pltpu.make_async_copy(wu_hbm.at[e, :, cols], wuf.at[buf], wsem.at[1, buf]),
                pltpu.make_async_copy(wd_hbm.at[e, cols, :], wdf.at[buf], wsem.at[2, buf]))

    def zero_copy(j):
        dst = y_ref.at[pl.ds((n_used + j) * ROW_BLK, ROW_BLK)]
        return pltpu.make_async_copy(ystage.at[pl.ds(0, ROW_BLK)], dst, ysem.at[0])
    n_unused = y_ref.shape[0] // ROW_BLK - n_used
    ystage[0:ROW_BLK, :] = jnp.zeros((ROW_BLK, HALF), jnp.uint32)
    for_blocks(n_unused, lambda j: zero_copy(j).start())
    for_blocks(n_unused, lambda j: zero_copy(j).wait())

    for_blocks(nb_ref[0], lambda j: x_copy(0, 0, j).start())

    @pl.when(nb_ref[0] > 0)
    def _():
        for c in w_copies(0, 0, 0):
            c.start()

    def item_body(i, carry):
        slot = i % 2
        nb = nb_ref[i]
        prev = jnp.maximum(i - 1, 0)
        nb_prev = jnp.where(i > 0, nb_ref[prev], 0)

        def y_copy(j):
            dst = y_ref.at[pl.ds((r0_ref[prev] + j) * ROW_BLK, ROW_BLK)]
            return pltpu.make_async_copy(ystage.at[pl.ds(j * ROW_BLK, ROW_BLK)], dst, ysem.at[0])

        for_blocks(nb_ref[i + 1], lambda j: x_copy(i + 1, 1 - slot, j).start())
        for_blocks(nb, lambda j: x_copy(i, slot, j).wait())

        def tile_body(k, carry):
            wbuf = (i * N_FT + k) % 2
            last = k == N_FT - 1
            nxt_i = jnp.where(last, i + 1, i)
            nxt_k = jnp.where(last, 0, k + 1)

            @pl.when(nb_ref[nxt_i] > 0)
            def _():
                for c in w_copies(nxt_i, nxt_k, 1 - wbuf):
                    c.start()

            @pl.when(nb > 0)
            def _():
                for c in w_copies(i, k, wbuf):
                    c.wait()
                wgu[:, 0:F_TILE] = wgf[wbuf].astype(BF16)
                wgu[:, F_TILE:] = wuf[wbuf].astype(BF16)
                f_rows = pl.ds(_f_tile_start(k), F_TILE)
                for n in range(N_YT):
                    wdr[slot, n, f_rows, :] = wdf[wbuf, :, n * Y_TILE:(n + 1) * Y_TILE].astype(BF16)

            def gate_up(row0, nrows):
                rows = pl.ds(pl.multiple_of(row0, ROW_BLK), nrows)
                gu = jnp.dot(xb[slot, rows, :], wgu[...], preferred_element_type=F32)
                gate, up = gu[:, 0:F_TILE], gu[:, F_TILE:]
                h = (gate * (1.0 / (1.0 + jnp.exp(-gate))) * up).astype(BF16)
                hid[slot, rows, pl.ds(_f_tile_start(k), F_TILE)] = h

            k_y = jnp.minimum(k, N_YT - 1)

            def down(row0, nrows):
                rows = pl.ds(pl.multiple_of(row0, ROW_BLK), nrows)
                yt = jnp.dot(hid[1 - slot, rows, :], wdr[1 - slot, k_y], preferred_element_type=F32)
                packed = _pack_bf16_pair(yt[:, 0:Y_TILE // 2], yt[:, Y_TILE // 2:])
                words = pl.ds(pl.multiple_of(k_y * (Y_TILE // 2), Y_TILE // 2), Y_TILE // 2)
                ystage[rows, words] = packed

            def for_row_tiles(n_blk, fn):
                big = 4 * ROW_BLK
                for_blocks(n_blk // 4, lambda t: fn(t * big, big))
                for bit in (2, 1):
                    @pl.when((n_blk & bit) != 0)
                    def _():
                        fn((n_blk // (2 * bit)) * (2 * bit) * ROW_BLK, bit * ROW_BLK)

            for_row_tiles(nb, gate_up)
            for_row_tiles(jnp.where(k < N_YT, nb_prev, 0), down)

            @pl.when(k == N_YT)
            def _():
                for_blocks(nb_prev, lambda j: y_copy(j).start())

            @pl.when(last)
            def _():
                for_blocks(nb_prev, lambda j: y_copy(j).wait())
            return carry

        lax.fori_loop(0, N_FT, tile_body, 0)
        return carry

    lax.fori_loop(0, n_live + 1, item_body, 0)


def _moe(item_e, item_r0, item_nb, meta, xs, w_gate, w_up, w_down, n_blocks):
    return pl.pallas_call(
        _moe_kernel,
        grid_spec=pltpu.PrefetchScalarGridSpec(
            num_scalar_prefetch=4,
            grid=(1,),
            in_specs=[pl.BlockSpec(memory_space=pl.ANY)] * 4,
            out_specs=pl.BlockSpec(memory_space=pl.ANY),
            scratch_shapes=[
                pltpu.VMEM((2, ITEM_ROWS, D_MODEL), BF16),
                pltpu.VMEM((2, ITEM_ROWS, D_EXPERT), BF16),
                pltpu.VMEM((ITEM_ROWS, HALF), jnp.uint32),
                pltpu.VMEM((2, D_MODEL, F_TILE), F32),
                pltpu.VMEM((2, D_MODEL, F_TILE), F32),
                pltpu.VMEM((2, F_TILE, D_MODEL), F32),
                pltpu.VMEM((D_MODEL, 2 * F_TILE), BF16),
                pltpu.VMEM((2, N_YT, D_EXPERT, Y_TILE), BF16),
                pltpu.SemaphoreType.DMA((2, ITEM_BLKS)),
                pltpu.SemaphoreType.DMA((1,)),
                pltpu.SemaphoreType.DMA((3, 2)),
            ],
        ),
        out_shape=jax.ShapeDtypeStruct((n_blocks * ROW_BLK, HALF), jnp.uint32),
        compiler_params=pltpu.CompilerParams(dimension_semantics=("arbitrary",),
                                             vmem_limit_bytes=VMEM_LIMIT),
        name="moe",
    )(item_e, item_r0, item_nb, meta, xs, w_gate, w_up, w_down)


def _combine_kernel(cur_ref, nxt_ref, h_ref, w_ref, g_ref, y_ref, o_ref, ya, yb, sem, *, tm):
    i = pl.program_id(0)
    n = pl.num_programs(0)
    slot = i % 2

    def fetch(idx_ref, buf):
        def issue(r, c):
            pltpu.make_async_copy(y_ref.at[pl.ds(idx_ref[TOP_K * r], 1)], ya.at[buf, pl.ds(r, 1)],
                                  sem.at[buf]).start()
            pltpu.make_async_copy(y_ref.at[pl.ds(idx_ref[TOP_K * r + 1], 1)], yb.at[buf, pl.ds(r, 1)],
                                  sem.at[buf]).start()
            return c
        lax.fori_loop(0, tm, issue, 0, unroll=8)

    @pl.when(i == 0)
    def _():
        fetch(cur_ref, 0)

    @pl.when(i + 1 < n)
    def _():
        fetch(nxt_ref, 1 - slot)

    pltpu.make_async_copy(ya.at[slot], ya.at[slot], sem.at[slot]).wait()
    pltpu.make_async_copy(yb.at[slot], yb.at[slot], sem.at[slot]).wait()

    wa = w_ref[:, 0:1]
    wb = w_ref[:, 1:2]
    half_t = Y_TILE // 2
    ssq = jnp.zeros((tm, 1), F32)
    for n in range(N_YT):
        words = slice(n * half_t, (n + 1) * half_t)
        a_lo, a_hi = _unpack_bf16_pair(ya[slot, :, words])
        b_lo, b_hi = _unpack_bf16_pair(yb[slot, :, words])
        lo_cols = slice(n * Y_TILE, n * Y_TILE + half_t)
        hi_cols = slice(n * Y_TILE + half_t, (n + 1) * Y_TILE)
        lo = h_ref[:, lo_cols] + wa * a_lo + wb * b_lo
        hi = h_ref[:, hi_cols] + wa * a_hi + wb * b_hi
        ssq = ssq + jnp.sum(lo * lo, axis=-1, keepdims=True) + jnp.sum(hi * hi, axis=-1, keepdims=True)
        o_ref[:, lo_cols] = lo
        o_ref[:, hi_cols] = hi
    inv = lax.rsqrt(ssq * (1.0 / D_MODEL) + RMS_EPS)
    o_ref[...] = o_ref[...] * inv * g_ref[...]


def _combine(dest_flat, h2, wts, g, y, tm):
    rows = h2.shape[0]
    steps = rows // tm
    idx_spec = lambda fn: pl.BlockSpec((TOP_K * tm,), fn, memory_space=pltpu.SMEM)
    return pl.pallas_call(
        functools.partial(_combine_kernel, tm=tm),
        grid=(steps,),
        in_specs=[
            idx_spec(lambda i: (i,)),
            idx_spec(lambda i: (jnp.minimum(i + 1, steps - 1),)),
            pl.BlockSpec((tm, D_MODEL), lambda i: (i, 0)),
            pl.BlockSpec((tm, TOP_K), lambda i: (i, 0)),
            pl.BlockSpec((1, D_MODEL), lambda i: (0, 0)),
            pl.BlockSpec(memory_space=pl.ANY),
        ],
        out_specs=pl.BlockSpec((tm, D_MODEL), lambda i: (i, 0)),
        scratch_shapes=[
            pltpu.VMEM((2, tm, HALF), jnp.uint32),
            pltpu.VMEM((2, tm, HALF), jnp.uint32),
            pltpu.SemaphoreType.DMA((2,)),
        ],
        out_shape=jax.ShapeDtypeStruct((rows, D_MODEL), F32),
        compiler_params=pltpu.CompilerParams(dimension_semantics=("arbitrary",),
                                             vmem_limit_bytes=VMEM_LIMIT),
        name="combine",
    )(dest_flat, dest_flat, h2, wts, g, y)


def _route(logits, n_tok):
    pg = jax.nn.softmax(logits[:, :N_GROUPS], axis=-1)
    g_sel = jnp.argmax(pg, axis=-1)
    gate_g = jnp.take_along_axis(pg, g_sel[:, None], axis=1)
    le = logits[:, N_GROUPS:N_GROUPS + N_EXPERTS].reshape(n_tok, N_GROUPS, EXPERTS_PER_GROUP)
    le = jnp.take_along_axis(le, g_sel[:, None, None], axis=1)[:, 0]
    top_p, top_i = lax.top_k(jax.nn.softmax(le, axis=-1), TOP_K)
    wts = gate_g * top_p / jnp.sum(top_p, axis=-1, keepdims=True)
    eid = (g_sel[:, None] * EXPERTS_PER_GROUP + top_i).astype(jnp.int32)

    n_asg = n_tok * TOP_K
    e_flat = eid.reshape(n_asg)
    onehot = (e_flat[:, None] == jnp.arange(N_EXPERTS, dtype=jnp.int32)[None, :]).astype(jnp.int32)
    csum = jnp.cumsum(onehot, axis=0)
    rank = jnp.take_along_axis(csum, e_flat[:, None], axis=1)[:, 0] - 1
    counts = csum[-1]
    blocks_e = (counts + ROW_BLK - 1) // ROW_BLK
    blk_end = jnp.cumsum(blocks_e)
    blk_start = blk_end - blocks_e
    dest = (blk_start[e_flat] * ROW_BLK + rank).astype(jnp.int32)
    return wts, dest.reshape(n_tok, TOP_K), blocks_e, blk_start, blk_end[-1]


def _work_items(blocks_e, blk_start, n_items):
    items_e = (blocks_e + ITEM_BLKS - 1) // ITEM_BLKS
    item_end = jnp.cumsum(items_e)
    item_start = item_end - items_e
    total = item_end[-1]
    idx = jnp.arange(n_items, dtype=jnp.int32)
    live = idx < total
    ref = jnp.minimum(idx, total - 1)
    e = jnp.searchsorted(item_end, ref, side='right').astype(jnp.int32)
    j = ref - item_start[e]
    r0 = blk_start[e] + ITEM_BLKS * j
    nb = jnp.where(live, jnp.clip(blocks_e[e] - ITEM_BLKS * j, 0, ITEM_BLKS), 0)
    return e, r0.astype(jnp.int32), nb.astype(jnp.int32), total.astype(jnp.int32)


def kernel(x, meta_tokens, attn_norm_g, w_in, swa_sinks, diff_lambda, diff_subln_g, w_out,
           ffn_norm_g, w_router_group, w_router_expert, w_gate, w_up, w_down, final_norm_g):
    batch, seq, _ = x.shape
    n_tok = batch * seq
    nblk = seq // Q_BLOCK
    x2 = x.reshape(n_tok, D_MODEL)

    w_in_bf = w_in[0].astype(BF16)
    g_attn = attn_norm_g[0].reshape(1, D_MODEL)
    cos_x, sin_x = _rope_tables(N_META + jnp.arange(seq))
    cos_m, sin_m = _rope_tables(jnp.maximum(jnp.arange(Q_BLOCK) - PAD_FRONT, 0))
    qs, ks, vs, qd, kd, vd = _inproj(x2, g_attn, w_in_bf, cos_x, sin_x, 512, seq)
    meta_blk = jnp.concatenate(
        [jnp.zeros((PAD_FRONT, D_MODEL), x.dtype), meta_tokens.astype(x.dtype)], axis=0)
    _, ks_m, vs_m, _, kd_m, vd_m = _inproj(meta_blk, g_attn, w_in_bf, cos_m, sin_m, Q_BLOCK, Q_BLOCK)

    o_s = _swa(swa_sinks[0].astype(F32), qs, ks, vs, ks_m, vs_m, batch, nblk)
    o_d = _diff(diff_lambda[0].astype(F32), diff_subln_g[0].astype(F32).reshape(1, LANES),
                qd, kd, vd, kd_m, vd_m, batch, seq)

    w_router = jnp.concatenate(
        [w_router_group[0], w_router_expert[0],
         jnp.zeros((D_MODEL, LANES - N_GROUPS - N_EXPERTS), F32)], axis=1)
    h2, f_packed, logits = _outproj(o_s, o_d, x2, w_out[0].astype(BF16),
                                    ffn_norm_g[0].reshape(1, D_MODEL), w_router, 512)

    wts, dest, blocks_e, blk_start, n_used = _route(logits, n_tok)
    n_blocks = n_tok * TOP_K // ROW_BLK + N_EXPERTS
    n_items = n_blocks // ITEM_BLKS + N_EXPERTS
    tok = jnp.repeat(jnp.arange(n_tok, dtype=jnp.int32), TOP_K)
    slot_tok = jnp.zeros((n_blocks * ROW_BLK,), jnp.int32).at[dest.reshape(-1)].set(tok)
    n_used = n_used.astype(jnp.int32).reshape(1)
    xs = _gather_rows(n_used, slot_tok, f_packed, n_blocks)
    item_e, item_r0, item_nb, n_live = _work_items(blocks_e, blk_start, n_items)
    y = _moe(item_e, item_r0, item_nb, jnp.concatenate([n_used, n_live.reshape(1)]), xs,
             w_gate[0], w_up[0], w_down[0], n_blocks)

    out = _combine(dest.reshape(-1), h2, wts.astype(F32),
                   final_norm_g.reshape(1, D_MODEL), y, 256)
    return out.reshape(batch, seq, D_MODEL)
```

```python
import functools
import math

import jax
import jax.numpy as jnp
from jax import lax
from jax.experimental import pallas as pl
from jax.experimental.pallas import tpu as pltpu

D_MODEL = 2048
N_META = 16
CHUNK = 64
Q_BLOCK = 128
PAD_FRONT = Q_BLOCK - N_META
HEAD_DIM = 64
ROPE_THETA = 10000.0
SWA_Q_HEADS = 16
SWA_KV_HEADS = 2
DIFF_HEADS = 8
SWA_Q_W = SWA_Q_HEADS * HEAD_DIM
SWA_KV_W = SWA_KV_HEADS * HEAD_DIM
DIFF_W = DIFF_HEADS * 2 * HEAD_DIM
IN_WIDTH = SWA_Q_W + 2 * SWA_KV_W + 3 * DIFF_W
N_GROUPS = 8
EXPERTS_PER_GROUP = 8
N_EXPERTS = N_GROUPS * EXPERTS_PER_GROUP
TOP_K = 2
D_EXPERT = 1408
RMS_EPS = 1e-6
NEG_INF = -1e30
LAM_INIT = 0.8 - 0.6 * math.exp(-0.3 * 0)

LANES = 128
ROW_BLK = 128
VMEM_LIMIT = 56 * 1024 * 1024

BF16 = jnp.bfloat16
F32 = jnp.float32


def _nt_dot(a, b):
    return lax.dot_general(a, b, (((1,), (1,)), ((), ())), preferred_element_type=F32)


_SEGS = (
    ("qs", 0, 512, True, HEAD_DIM ** -0.5), ("qs", 512, 512, True, HEAD_DIM ** -0.5),
    ("ks", 1024, 128, True, 1.0), ("vs", 1152, 128, False, 1.0),
    ("qd", 1280, 512, True, HEAD_DIM ** -0.5), ("qd", 1792, 512, True, HEAD_DIM ** -0.5),
    ("kd", 2304, 512, True, 1.0), ("kd", 2816, 512, True, 1.0),
    ("vd", 3328, 512, False, 1.0), ("vd", 3840, 512, False, 1.0),
)
_SEG_BASE = {"qs": 0, "ks": 1024, "vs": 1152, "qd": 1280, "kd": 2304, "vd": 3328}


def _inproj_kernel(x_ref, g_ref, w_ref, cos_ref, sin_ref,
                   qs_ref, ks_ref, vs_ref, qd_ref, kd_ref, vd_ref):
    x = x_ref[...]
    ms = jnp.mean(x * x, axis=-1, keepdims=True)
    a = (x * lax.rsqrt(ms + RMS_EPS) * g_ref[...]).astype(BF16)
    cos = cos_ref[...]
    sin = sin_ref[...]
    lane = lax.broadcasted_iota(jnp.int32, cos.shape, 1)
    first_half = (lane % HEAD_DIM) < (HEAD_DIM // 2)
    low_head = lane < HEAD_DIM
    outs = {"qs": qs_ref, "ks": ks_ref, "vs": vs_ref, "qd": qd_ref, "kd": kd_ref, "vd": vd_ref}
    for name, c0, width, rotary, scale in _SEGS:
        y = jnp.dot(a, w_ref[:, c0:c0 + width], preferred_element_type=F32)
        o0 = c0 - _SEG_BASE[name]
        for t in range(width // LANES):
            yt = y[:, t * LANES:(t + 1) * LANES]
            if rotary:
                partner = jnp.where(first_half, pltpu.roll(yt, LANES - HEAD_DIM // 2, 1),
                                    pltpu.roll(yt, HEAD_DIM // 2, 1))
                yt = yt * cos + partner * sin
            if scale != 1.0:
                yt = yt * scale
            if name in ("ks", "vs"):
                other = pltpu.roll(yt, HEAD_DIM, 1)
                outs[name][:, 0:LANES] = jnp.where(low_head, yt, other).astype(BF16)
                outs[name][:, LANES:2 * LANES] = jnp.where(low_head, other, yt).astype(BF16)
            else:
                outs[name][:, o0 + t * LANES:o0 + (t + 1) * LANES] = yt.astype(BF16)


def _inproj(x2, g, w_bf, cos_t, sin_t, tm, rows_per_seq):
    rows = x2.shape[0]
    nseq = rows_per_seq // tm
    row_spec = lambda width: pl.BlockSpec((tm, width), lambda i: (i, 0))
    out_shapes = (
        jax.ShapeDtypeStruct((rows, SWA_Q_W), BF16), jax.ShapeDtypeStruct((rows, 2 * SWA_KV_W), BF16),
        jax.ShapeDtypeStruct((rows, 2 * SWA_KV_W), BF16), jax.ShapeDtypeStruct((rows, DIFF_W), BF16),
        jax.ShapeDtypeStruct((rows, DIFF_W), BF16), jax.ShapeDtypeStruct((rows, DIFF_W), BF16),
    )
    return pl.pallas_call(
        _inproj_kernel,
        grid=(rows // tm,),
        in_specs=[
            row_spec(D_MODEL),
            pl.BlockSpec((1, D_MODEL), lambda i: (0, 0)),
            pl.BlockSpec((D_MODEL, IN_WIDTH), lambda i: (0, 0), pipeline_mode=pl.Buffered(1)),
            pl.BlockSpec((tm, LANES), lambda i: (i % nseq, 0)),
            pl.BlockSpec((tm, LANES), lambda i: (i % nseq, 0)),
        ],
        out_specs=(row_spec(SWA_Q_W), row_spec(2 * SWA_KV_W), row_spec(2 * SWA_KV_W),
                   row_spec(DIFF_W), row_spec(DIFF_W), row_spec(DIFF_W)),
        out_shape=out_shapes,
        compiler_params=pltpu.CompilerParams(dimension_semantics=("arbitrary",),
                                             vmem_limit_bytes=VMEM_LIMIT),
        name="inproj",
    )(x2, g, w_bf, cos_t, sin_t)


def _rope_tables(positions):
    inv = 1.0 / (ROPE_THETA ** (jnp.arange(0, HEAD_DIM, 2, dtype=F32) / HEAD_DIM))
    ang = positions.astype(F32)[:, None] * inv[None, :]
    cos, sin = jnp.cos(ang), jnp.sin(ang)
    cos_t = jnp.tile(cos, (1, 2 * LANES // HEAD_DIM))
    sin_t = jnp.tile(jnp.concatenate([-sin, sin], axis=1), (1, LANES // HEAD_DIM))
    return cos_t, sin_t


def _swa_kernel(sink_ref, q_ref, kp_ref, kc_ref, km_ref, vp_ref, vc_ref, vm_ref, o_ref):
    i = pl.program_id(1)
    nk = 3 * Q_BLOCK
    row = lax.broadcasted_iota(jnp.int32, (2 * Q_BLOCK, nk), 0)
    col = lax.broadcasted_iota(jnp.int32, (2 * Q_BLOCK, nk), 1)
    second_half = (row % Q_BLOCK) >= CHUNK
    part = col // Q_BLOCK
    idx = col % Q_BLOCK
    ok_prev = (part == 0) & (i > 0) & ((idx >= CHUNK) | jnp.logical_not(second_half))
    ok_cur = (part == 1) & ((idx < CHUNK) | second_half)
    ok_meta = (part == 2) & (idx >= PAD_FRONT)
    mask = ok_prev | ok_cur | ok_meta
    upper_rows = lax.broadcasted_iota(jnp.int32, (2 * Q_BLOCK, 1), 0) >= Q_BLOCK
    lane = lax.broadcasted_iota(jnp.int32, (Q_BLOCK, LANES), 1)
    low_head = lane < HEAD_DIM
    zero = jnp.zeros((Q_BLOCK, LANES), BF16)
    for g in range(SWA_KV_HEADS):
        gl = slice(g * LANES, (g + 1) * LANES)
        k_all = jnp.concatenate([kp_ref[:, gl], kc_ref[:, gl], km_ref[:, gl]], axis=0)
        v_all = jnp.concatenate([vp_ref[:, gl], vc_ref[:, gl], vm_ref[:, gl]], axis=0)
        pairs = SWA_Q_HEADS // SWA_KV_HEADS // 2
        for c in range(g * pairs, (g + 1) * pairs):
            qp = q_ref[:, c * LANES:(c + 1) * LANES]
            q2 = jnp.concatenate([jnp.where(low_head, qp, zero), jnp.where(low_head, zero, qp)], axis=0)
            s = jnp.where(mask, _nt_dot(q2, k_all), NEG_INF)
            sink = jnp.where(upper_rows, sink_ref[2 * c + 1], sink_ref[2 * c])
            m = jnp.maximum(jnp.max(s, axis=-1, keepdims=True), sink)
            p = jnp.exp(s - m)
            denom = jnp.sum(p, axis=-1, keepdims=True) + jnp.exp(sink - m)
            o2 = jnp.dot(p.astype(BF16), v_all, preferred_element_type=F32) / denom
            o_ref[:, c * LANES:(c + 1) * LANES] = jnp.where(
                low_head, o2[:Q_BLOCK], o2[Q_BLOCK:]).astype(BF16)


def _swa(sinks, qs, ks, vs, ks_meta, vs_meta, batch, nblk):
    rows = qs.shape[0]
    cur = lambda b, i: (b * nblk + i, 0)
    prev = lambda b, i: (b * nblk + jnp.maximum(i - 1, 0), 0)
    kv_w = 2 * SWA_KV_W
    return pl.pallas_call(
        _swa_kernel,
        grid=(batch, nblk),
        in_specs=[
            pl.BlockSpec(memory_space=pltpu.SMEM),
            pl.BlockSpec((Q_BLOCK, SWA_Q_W), cur),
            pl.BlockSpec((Q_BLOCK, kv_w), prev), pl.BlockSpec((Q_BLOCK, kv_w), cur),
            pl.BlockSpec((Q_BLOCK, kv_w), lambda b, i: (0, 0)),
            pl.BlockSpec((Q_BLOCK, kv_w), prev), pl.BlockSpec((Q_BLOCK, kv_w), cur),
            pl.BlockSpec((Q_BLOCK, kv_w), lambda b, i: (0, 0)),
        ],
        out_specs=pl.BlockSpec((Q_BLOCK, SWA_Q_W), cur),
        out_shape=jax.ShapeDtypeStruct((rows, SWA_Q_W), BF16),
        compiler_params=pltpu.CompilerParams(dimension_semantics=("arbitrary", "arbitrary"),
                                             vmem_limit_bytes=VMEM_LIMIT),
        name="swa",
    )(sinks, qs, ks, ks, ks_meta, vs, vs, vs_meta)


DIFF_TQ = 256
DIFF_TK = 512
DIFF_HEADS_PER_STEP = 2


def _diff_kernel(lam_ref, g_ref, q_ref, k_ref, v_ref, km_ref, vm_ref, o_ref, *, seq):
    dl = lam_ref[...]
    lam = (jnp.exp(jnp.sum(dl[0:1] * dl[1:2], axis=-1, keepdims=True))
           - jnp.exp(jnp.sum(dl[2:3] * dl[3:4], axis=-1, keepdims=True)) + LAM_INIT)
    lane = lax.broadcasted_iota(jnp.int32, (DIFF_TQ, LANES), 1)
    low_map = lane < HEAD_DIM
    zero = jnp.zeros((DIFF_TQ, LANES), BF16)
    meta_ok = lax.broadcasted_iota(jnp.int32, (2 * DIFF_TQ, Q_BLOCK), 1) >= PAD_FRONT
    chunk_gap = (lax.broadcasted_iota(jnp.int32, (2 * DIFF_TQ, DIFF_TK), 1) // CHUNK
                 - (lax.broadcasted_iota(jnp.int32, (2 * DIFF_TQ, DIFF_TK), 0) % DIFF_TQ) // CHUNK)

    def step(q2, k_t, v_t, mask, state):
        s = _nt_dot(q2, k_t)
        if mask is not None:
            s = jnp.where(mask, s, NEG_INF)
        smax = jnp.max(s, axis=-1, keepdims=True)
        if state is None:
            p = jnp.exp(s - smax)
            return smax, jnp.sum(p, axis=-1, keepdims=True), jnp.dot(
                p.astype(BF16), v_t, preferred_element_type=F32)
        m_old, l_old, acc = state
        m_new = jnp.maximum(m_old, smax)
        alpha = jnp.exp(m_old - m_new)
        p = jnp.exp(s - m_new)
        return (m_new, alpha * l_old + jnp.sum(p, axis=-1, keepdims=True),
                alpha * acc + jnp.dot(p.astype(BF16), v_t, preferred_element_type=F32))

    heads = [slice(h * LANES, (h + 1) * LANES) for h in range(DIFF_HEADS_PER_STEP)]

    def q_tile(qi, carry):
        q0 = pl.multiple_of(qi * DIFF_TQ, DIFF_TQ)
        q2s = []
        for hs in heads:
            q = q_ref[pl.ds(q0, DIFF_TQ), hs]
            q2s.append(jnp.concatenate(
                [jnp.where(low_map, q, zero), jnp.where(low_map, zero, q)], axis=0))

        def all_heads(rows, mask, states):
            out = ()
            for h, hs in enumerate(heads):
                st = None if states is None else states[3 * h:3 * h + 3]
                k_t = km_ref[:, hs] if rows is None else k_ref[rows, hs]
                v_t = vm_ref[:, hs] if rows is None else v_ref[rows, hs]
                out = out + step(q2s[h], k_t, v_t, mask, st)
            return out

        st = all_heads(None, meta_ok, None)

        def k_tile(kj, st):
            k0 = pl.multiple_of(kj * DIFF_TK, DIFF_TK)
            return all_heads(pl.ds(k0, DIFF_TK), None, st)

        n_full = q0 // DIFF_TK
        st = lax.fori_loop(0, n_full, k_tile, st)
        k0 = pl.multiple_of(n_full * DIFF_TK, DIFF_TK)
        diag_ok = chunk_gap <= (q0 - k0) // CHUNK
        st = all_heads(pl.ds(k0, DIFF_TK), diag_ok, st)
        for h, hs in enumerate(heads):
            o = st[3 * h + 2] / st[3 * h + 1]
            o = o[:DIFF_TQ] - lam * o[DIFF_TQ:]
            o = o * lax.rsqrt(jnp.mean(o * o, axis=-1, keepdims=True) + RMS_EPS)
            o_ref[pl.ds(q0, DIFF_TQ), hs] = (o * g_ref[...] * (1.0 - LAM_INIT)).astype(BF16)
        return carry

    lax.fori_loop(0, seq // DIFF_TQ, q_tile, 0)


def _diff(diff_lambda, subln_g, qd, kd, vd, kd_meta, vd_meta, batch, seq):
    rows = qd.shape[0]
    head = lambda b, h: (b, h)
    meta = lambda b, h: (0, h)
    width = DIFF_HEADS_PER_STEP * LANES
    return pl.pallas_call(
        functools.partial(_diff_kernel, seq=seq),
        grid=(batch, DIFF_HEADS // DIFF_HEADS_PER_STEP),
        in_specs=[
            pl.BlockSpec((4, HEAD_DIM), lambda b, h: (0, 0)),
            pl.BlockSpec((1, LANES), lambda b, h: (0, 0)),
            pl.BlockSpec((seq, width), head), pl.BlockSpec((seq, width), head),
            pl.BlockSpec((seq, width), head),
            pl.BlockSpec((Q_BLOCK, width), meta), pl.BlockSpec((Q_BLOCK, width), meta),
        ],
        out_specs=pl.BlockSpec((seq, width), head),
        out_shape=jax.ShapeDtypeStruct((rows, DIFF_W), BF16),
        compiler_params=pltpu.CompilerParams(dimension_semantics=("arbitrary", "arbitrary"),
                                             vmem_limit_bytes=VMEM_LIMIT),
        name="diffattn",
    )(diff_lambda, subln_g, qd, kd, vd, kd_meta, vd_meta)


HALF = D_MODEL // 2


def _pack_bf16_pair(a, b):
    hi = pltpu.bitcast(a.astype(BF16).astype(F32), jnp.uint32)
    lo = pltpu.bitcast(b.astype(BF16).astype(F32), jnp.uint32)
    return hi | (lo >> 16)


def _unpack_bf16_pair(w):
    a = pltpu.bitcast(w & jnp.uint32(0xFFFF0000), F32)
    b = pltpu.bitcast(w << 16, F32)
    return a, b


OUT_CHUNK = 512


def _outproj_kernel(os_ref, od_ref, x_ref, w_ref, g_ref, wr_ref, h_ref, fp_ref, lg_ref):
    tm = x_ref.shape[0]
    ssq = jnp.zeros((tm, 1), F32)
    for c in range(D_MODEL // OUT_CHUNK):
        cols = slice(c * OUT_CHUNK, (c + 1) * OUT_CHUNK)
        y = jnp.dot(os_ref[...], w_ref[0:SWA_Q_W, cols], preferred_element_type=F32)
        y = y + jnp.dot(od_ref[...], w_ref[SWA_Q_W:, cols], preferred_element_type=F32)
        h = x_ref[:, cols] + y
        h_ref[:, cols] = h
        ssq = ssq + jnp.sum(h * h, axis=-1, keepdims=True)
    inv = lax.rsqrt(ssq * (1.0 / D_MODEL) + RMS_EPS)
    lg = jnp.zeros((tm, LANES), F32)
    for c in range(HALF // OUT_CHUNK):
        ca = slice(c * OUT_CHUNK, (c + 1) * OUT_CHUNK)
        cb = slice(HALF + c * OUT_CHUNK, HALF + (c + 1) * OUT_CHUNK)
        fa = h_ref[:, ca] * inv * g_ref[:, ca]
        fb = h_ref[:, cb] * inv * g_ref[:, cb]
        fp_ref[:, ca] = _pack_bf16_pair(fa, fb)
        for f, cc in ((fa, ca), (fb, cb)):
            f_hi = f.astype(BF16)
            f_lo = (f - f_hi.astype(F32)).astype(BF16)
            w = wr_ref[cc, :]
            w_hi = w.astype(BF16)
            w_lo = (w - w_hi.astype(F32)).astype(BF16)
            p = jnp.dot(f_hi, jnp.concatenate([w_hi, w_lo], axis=1), preferred_element_type=F32)
            lg = lg + p[:, 0:LANES] + p[:, LANES:] + jnp.dot(f_lo, w_hi, preferred_element_type=F32)
    lg_ref[...] = lg


def _outproj(o_s, o_d, x2, w_bf, g, w_router, tm):
    rows = x2.shape[0]
    row_spec = lambda width: pl.BlockSpec((tm, width), lambda i: (i, 0))
    return pl.pallas_call(
        _outproj_kernel,
        grid=(rows // tm,),
        in_specs=[
            row_spec(SWA_Q_W), row_spec(DIFF_W), row_spec(D_MODEL),
            pl.BlockSpec((D_MODEL, D_MODEL), lambda i: (0, 0), pipeline_mode=pl.Buffered(1)),
            pl.BlockSpec((1, D_MODEL), lambda i: (0, 0)),
            pl.BlockSpec((D_MODEL, LANES), lambda i: (0, 0)),
        ],
        out_specs=(row_spec(D_MODEL), row_spec(HALF), row_spec(LANES)),
        out_shape=(jax.ShapeDtypeStruct((rows, D_MODEL), F32),
                   jax.ShapeDtypeStruct((rows, HALF), jnp.uint32),
                   jax.ShapeDtypeStruct((rows, LANES), F32)),
        compiler_params=pltpu.CompilerParams(dimension_semantics=("arbitrary",),
                                             vmem_limit_bytes=VMEM_LIMIT),
        name="outproj",
    )(o_s, o_d, x2, w_bf, g, w_router)


DISPATCH_TM = 1024


def _dispatch_kernel(zero_blk_ref, n_zero_ref, dest_ref, f_ref, xs_ref, zeros, sem, zsem):
    i = pl.program_id(0)

    @pl.when(i == 0)
    def _():
        zeros[...] = jnp.zeros_like(zeros)

        def zero_copy(j):
            rows = pl.ds(zero_blk_ref[j] * ROW_BLK, ROW_BLK)
            return pltpu.make_async_copy(zeros, xs_ref.at[rows], zsem.at[0])

        def start(j, c):
            zero_copy(j).start()
            return c

        def wait(j, c):
            zero_copy(j).wait()
            return c
        lax.fori_loop(0, n_zero_ref[0], start, 0)
        lax.fori_loop(0, n_zero_ref[0], wait, 0)

    def issue(r, c):
        for kk in range(TOP_K):
            pltpu.make_async_copy(f_ref.at[pl.ds(r, 1)],
                                  xs_ref.at[pl.ds(dest_ref[TOP_K * r + kk], 1)], sem.at[0]).start()
        return c
    lax.fori_loop(0, DISPATCH_TM, issue, 0, unroll=8)
    n_rows = TOP_K * DISPATCH_TM
    pltpu.make_async_copy(xs_ref.at[pl.ds(0, n_rows)], xs_ref.at[pl.ds(0, n_rows)], sem.at[0]).wait()


def _dispatch(zero_blk, n_zero, dest_flat, f_packed, n_blocks):
    rows = f_packed.shape[0]
    return pl.pallas_call(
        _dispatch_kernel,
        grid_spec=pltpu.PrefetchScalarGridSpec(
            num_scalar_prefetch=2,
            grid=(rows // DISPATCH_TM,),
            in_specs=[
                pl.BlockSpec((TOP_K * DISPATCH_TM,), lambda i, z, n: (i,), memory_space=pltpu.SMEM),
                pl.BlockSpec((DISPATCH_TM, HALF), lambda i, z, n: (i, 0)),
            ],
            out_specs=pl.BlockSpec(memory_space=pl.ANY),
            scratch_shapes=[pltpu.VMEM((ROW_BLK, HALF), jnp.uint32),
                            pltpu.SemaphoreType.DMA((1,)),
                            pltpu.SemaphoreType.DMA((1,))],
        ),
        out_shape=jax.ShapeDtypeStruct((n_blocks * ROW_BLK, HALF), jnp.uint32),
        compiler_params=pltpu.CompilerParams(dimension_semantics=("arbitrary",),
                                             vmem_limit_bytes=VMEM_LIMIT),
        name="dispatch",
    )(zero_blk, n_zero, dest_flat, f_packed)


ITEM_BLKS = 8
ITEM_ROWS = ITEM_BLKS * ROW_BLK
F_TILE = 256
N_FT = -(-D_EXPERT // F_TILE)
Y_TILE = 512
N_YT = D_MODEL // Y_TILE


def _f_tile_start(k):
    return pl.multiple_of(jnp.minimum(k * F_TILE, D_EXPERT - F_TILE), LANES)


def _moe_kernel(e_ref, r0_ref, nb_ref, meta_ref, xs_ref, wg_hbm, wu_hbm, wd_hbm, y_ref,
                xraw, xb, hid, ystage, wgf, wuf, wdf, wgu, wdr, xsem, ysem, wsem):
    n_used = meta_ref[0]
    n_live = meta_ref[1]

    def for_blocks(count, fn):
        def body(j, c):
            fn(j)
            return c
        lax.fori_loop(0, count, body, 0)

    def x_copy(item, buf, j):
        src = xs_ref.at[pl.ds((r0_ref[item] + j) * ROW_BLK, ROW_BLK)]
        return pltpu.make_async_copy(src, xraw.at[buf, pl.ds(j * ROW_BLK, ROW_BLK)],
                                     xsem.at[buf, j])

    def w_copies(item, k, buf):
        e = e_ref[item]
        cols = pl.ds(_f_tile_start(k), F_TILE)
        return (pltpu.make_async_copy(wg_hbm.at[e, :, cols], wgf.at[buf], wsem.at[0, buf]),
                pltpu.make_async_copy(wu_hbm.at[e, :, cols], wuf.at[buf], wsem.at[1, buf]),
                pltpu.make_async_copy(wd_hbm.at[e, cols, :], wdf.at[buf], wsem.at[2, buf]))

    def zero_copy(j):
        dst = y_ref.at[pl.ds((n_used + j) * ROW_BLK, ROW_BLK)]
        return pltpu.make_async_copy(ystage.at[pl.ds(0, ROW_BLK)], dst, ysem.at[0])
    n_unused = y_ref.shape[0] // ROW_BLK - n_used
    ystage[0:ROW_BLK, :] = jnp.zeros((ROW_BLK, HALF), jnp.uint32)
    for_blocks(n_unused, lambda j: zero_copy(j).start())
    for_blocks(n_unused, lambda j: zero_copy(j).wait())

    for_blocks(nb_ref[0], lambda j: x_copy(0, 0, j).start())

    @pl.when(nb_ref[0] > 0)
    def _():
        for c in w_copies(0, 0, 0):
            c.start()

    def item_body(i, carry):
        slot = i % 2
        nb = nb_ref[i]
        prev = jnp.maximum(i - 1, 0)
        nb_prev = jnp.where(i > 0, nb_ref[prev], 0)

        def y_copy(j):
            dst = y_ref.at[pl.ds((r0_ref[prev] + j) * ROW_BLK, ROW_BLK)]
            return pltpu.make_async_copy(ystage.at[pl.ds(j * ROW_BLK, ROW_BLK)], dst, ysem.at[0])

        for_blocks(nb_ref[i + 1], lambda j: x_copy(i + 1, 1 - slot, j).start())

        def arrive(j):
            x_copy(i, slot, j).wait()
            rows = pl.ds(pl.multiple_of(j * ROW_BLK, ROW_BLK), ROW_BLK)
            lo, hi = _unpack_bf16_pair(xraw[slot, rows, :])
            xb[rows, 0:HALF] = lo.astype(BF16)
            xb[rows, HALF:] = hi.astype(BF16)
        for_blocks(nb, arrive)

        def tile_body(k, carry):
            wbuf = (i * N_FT + k) % 2
            last = k == N_FT - 1
            nxt_i = jnp.where(last, i + 1, i)
            nxt_k = jnp.where(last, 0, k + 1)

            @pl.when(nb_ref[nxt_i] > 0)
            def _():
                for c in w_copies(nxt_i, nxt_k, 1 - wbuf):
                    c.start()

            @pl.when(nb > 0)
            def _():
                for c in w_copies(i, k, wbuf):
                    c.wait()
                wgu[:, 0:F_TILE] = wgf[wbuf].astype(BF16)
                wgu[:, F_TILE:] = wuf[wbuf].astype(BF16)
                f_rows = pl.ds(_f_tile_start(k), F_TILE)
                for n in range(N_YT):
                    wdr[slot, n, f_rows, :] = wdf[wbuf, :, n * Y_TILE:(n + 1) * Y_TILE].astype(BF16)

            def gate_up(row0, nrows):
                rows = pl.ds(pl.multiple_of(row0, ROW_BLK), nrows)
                gu = jnp.dot(xb[rows, :], wgu[...], preferred_element_type=F32)
                gate, up = gu[:, 0:F_TILE], gu[:, F_TILE:]
                h = (gate * (1.0 / (1.0 + jnp.exp(-gate))) * up).astype(BF16)
                hid[slot, rows, pl.ds(_f_tile_start(k), F_TILE)] = h

            k_y = jnp.minimum(k, N_YT - 1)

            def down(row0, nrows):
                rows = pl.ds(pl.multiple_of(row0, ROW_BLK), nrows)
                yt = jnp.dot(hid[1 - slot, rows, :], wdr[1 - slot, k_y], preferred_element_type=F32)
                packed = _pack_bf16_pair(yt[:, 0:Y_TILE // 2], yt[:, Y_TILE // 2:])
                words = pl.ds(pl.multiple_of(k_y * (Y_TILE // 2), Y_TILE // 2), Y_TILE // 2)
                ystage[rows, words] = packed

            def for_row_tiles(n_blk, fn):
                big = 4 * ROW_BLK
                for_blocks(n_blk // 4, lambda t: fn(t * big, big))
                for bit in (2, 1):
                    @pl.when((n_blk & bit) != 0)
                    def _():
                        fn((n_blk // (2 * bit)) * (2 * bit) * ROW_BLK, bit * ROW_BLK)

            for_row_tiles(nb, gate_up)
            for_row_tiles(jnp.where(k < N_YT, nb_prev, 0), down)

            @pl.when(k == N_YT)
            def _():
                for_blocks(nb_prev, lambda j: y_copy(j).start())

            @pl.when(last)
            def _():
                for_blocks(nb_prev, lambda j: y_copy(j).wait())
            return carry

        lax.fori_loop(0, N_FT, tile_body, 0)
        return carry

    lax.fori_loop(0, n_live + 1, item_body, 0)


def _moe(item_e, item_r0, item_nb, meta, xs, w_gate, w_up, w_down, n_blocks):
    return pl.pallas_call(
        _moe_kernel,
        grid_spec=pltpu.PrefetchScalarGridSpec(
            num_scalar_prefetch=4,
            grid=(1,),
            in_specs=[pl.BlockSpec(memory_space=pl.ANY)] * 4,
            out_specs=pl.BlockSpec(memory_space=pl.ANY),
            scratch_shapes=[
                pltpu.VMEM((2, ITEM_ROWS, HALF), jnp.uint32),
                pltpu.VMEM((ITEM_ROWS, D_MODEL), BF16),
                pltpu.VMEM((2, ITEM_ROWS, D_EXPERT), BF16),
                pltpu.VMEM((ITEM_ROWS, HALF), jnp.uint32),
                pltpu.VMEM((2, D_MODEL, F_TILE), F32),
                pltpu.VMEM((2, D_MODEL, F_TILE), F32),
                pltpu.VMEM((2, F_TILE, D_MODEL), F32),
                pltpu.VMEM((D_MODEL, 2 * F_TILE), BF16),
                pltpu.VMEM((2, N_YT, D_EXPERT, Y_TILE), BF16),
                pltpu.SemaphoreType.DMA((2, ITEM_BLKS)),
                pltpu.SemaphoreType.DMA((1,)),
                pltpu.SemaphoreType.DMA((3, 2)),
            ],
        ),
        out_shape=jax.ShapeDtypeStruct((n_blocks * ROW_BLK, HALF), jnp.uint32),
        compiler_params=pltpu.CompilerParams(dimension_semantics=("arbitrary",),
                                             vmem_limit_bytes=VMEM_LIMIT),
        name="moe",
    )(item_e, item_r0, item_nb, meta, xs, w_gate, w_up, w_down)


def _combine_kernel(cur_ref, nxt_ref, h_ref, w_ref, g_ref, y_ref, o_ref, ya, yb, sem, *, tm):
    i = pl.program_id(0)
    n = pl.num_programs(0)
    slot = i % 2

    def fetch(idx_ref, buf):
        def issue(r, c):
            pltpu.make_async_copy(y_ref.at[pl.ds(idx_ref[TOP_K * r], 1)], ya.at[buf, pl.ds(r, 1)],
                                  sem.at[buf]).start()
            pltpu.make_async_copy(y_ref.at[pl.ds(idx_ref[TOP_K * r + 1], 1)], yb.at[buf, pl.ds(r, 1)],
                                  sem.at[buf]).start()
            return c
        lax.fori_loop(0, tm, issue, 0, unroll=8)

    @pl.when(i == 0)
    def _():
        fetch(cur_ref, 0)

    @pl.when(i + 1 < n)
    def _():
        fetch(nxt_ref, 1 - slot)

    pltpu.make_async_copy(ya.at[slot], ya.at[slot], sem.at[slot]).wait()
    pltpu.make_async_copy(yb.at[slot], yb.at[slot], sem.at[slot]).wait()

    wa = w_ref[:, 0:1]
    wb = w_ref[:, 1:2]
    half_t = Y_TILE // 2
    ssq = jnp.zeros((tm, 1), F32)
    for n in range(N_YT):
        words = slice(n * half_t, (n + 1) * half_t)
        a_lo, a_hi = _unpack_bf16_pair(ya[slot, :, words])
        b_lo, b_hi = _unpack_bf16_pair(yb[slot, :, words])
        lo_cols = slice(n * Y_TILE, n * Y_TILE + half_t)
        hi_cols = slice(n * Y_TILE + half_t, (n + 1) * Y_TILE)
        lo = h_ref[:, lo_cols] + wa * a_lo + wb * b_lo
        hi = h_ref[:, hi_cols] + wa * a_hi + wb * b_hi
        ssq = ssq + jnp.sum(lo * lo, axis=-1, keepdims=True) + jnp.sum(hi * hi, axis=-1, keepdims=True)
        o_ref[:, lo_cols] = lo
        o_ref[:, hi_cols] = hi
    inv = lax.rsqrt(ssq * (1.0 / D_MODEL) + RMS_EPS)
    o_ref[...] = o_ref[...] * inv * g_ref[...]


def _combine(dest_flat, h2, wts, g, y, tm):
    rows = h2.shape[0]
    steps = rows // tm
    idx_spec = lambda fn: pl.BlockSpec((TOP_K * tm,), fn, memory_space=pltpu.SMEM)
    return pl.pallas_call(
        functools.partial(_combine_kernel, tm=tm),
        grid=(steps,),
        in_specs=[
            idx_spec(lambda i: (i,)),
            idx_spec(lambda i: (jnp.minimum(i + 1, steps - 1),)),
            pl.BlockSpec((tm, D_MODEL), lambda i: (i, 0)),
            pl.BlockSpec((tm, TOP_K), lambda i: (i, 0)),
            pl.BlockSpec((1, D_MODEL), lambda i: (0, 0)),
            pl.BlockSpec(memory_space=pl.ANY),
        ],
        out_specs=pl.BlockSpec((tm, D_MODEL), lambda i: (i, 0)),
        scratch_shapes=[
            pltpu.VMEM((2, tm, HALF), jnp.uint32),
            pltpu.VMEM((2, tm, HALF), jnp.uint32),
            pltpu.SemaphoreType.DMA((2,)),
        ],
        out_shape=jax.ShapeDtypeStruct((rows, D_MODEL), F32),
        compiler_params=pltpu.CompilerParams(dimension_semantics=("arbitrary",),
                                             vmem_limit_bytes=VMEM_LIMIT),
        name="combine",
    )(dest_flat, dest_flat, h2, wts, g, y)


def _route(logits, n_tok):
    pg = jax.nn.softmax(logits[:, :N_GROUPS], axis=-1)
    g_sel = jnp.argmax(pg, axis=-1)
    gate_g = jnp.take_along_axis(pg, g_sel[:, None], axis=1)
    le = logits[:, N_GROUPS:N_GROUPS + N_EXPERTS].reshape(n_tok, N_GROUPS, EXPERTS_PER_GROUP)
    le = jnp.take_along_axis(le, g_sel[:, None, None], axis=1)[:, 0]
    top_p, top_i = lax.top_k(jax.nn.softmax(le, axis=-1), TOP_K)
    wts = gate_g * top_p / jnp.sum(top_p, axis=-1, keepdims=True)
    eid = (g_sel[:, None] * EXPERTS_PER_GROUP + top_i).astype(jnp.int32)

    n_asg = n_tok * TOP_K
    e_flat = eid.reshape(n_asg)
    onehot = (e_flat[:, None] == jnp.arange(N_EXPERTS, dtype=jnp.int32)[None, :]).astype(jnp.int32)
    csum = jnp.cumsum(onehot, axis=0)
    rank = jnp.take_along_axis(csum, e_flat[:, None], axis=1)[:, 0] - 1
    counts = csum[-1]
    blocks_e = (counts + ROW_BLK - 1) // ROW_BLK
    blk_end = jnp.cumsum(blocks_e)
    blk_start = blk_end - blocks_e
    dest = (blk_start[e_flat] * ROW_BLK + rank).astype(jnp.int32)
    return wts, dest.reshape(n_tok, TOP_K), blocks_e, blk_start, blk_end[-1]


def _work_items(blocks_e, blk_start, n_items):
    items_e = (blocks_e + ITEM_BLKS - 1) // ITEM_BLKS
    item_end = jnp.cumsum(items_e)
    item_start = item_end - items_e
    total = item_end[-1]
    idx = jnp.arange(n_items, dtype=jnp.int32)
    live = idx < total
    ref = jnp.minimum(idx, total - 1)
    e = jnp.searchsorted(item_end, ref, side='right').astype(jnp.int32)
    j = ref - item_start[e]
    r0 = blk_start[e] + ITEM_BLKS * j
    nb = jnp.where(live, jnp.clip(blocks_e[e] - ITEM_BLKS * j, 0, ITEM_BLKS), 0)
    return e, r0.astype(jnp.int32), nb.astype(jnp.int32), total.astype(jnp.int32)


def kernel(x, meta_tokens, attn_norm_g, w_in, swa_sinks, diff_lambda, diff_subln_g, w_out,
           ffn_norm_g, w_router_group, w_router_expert, w_gate, w_up, w_down, final_norm_g):
    batch, seq, _ = x.shape
    n_tok = batch * seq
    nblk = seq // Q_BLOCK
    x2 = x.reshape(n_tok, D_MODEL)

    w_in_bf = w_in[0].astype(BF16)
    g_attn = attn_norm_g[0].reshape(1, D_MODEL)
    cos_x, sin_x = _rope_tables(N_META + jnp.arange(seq))
    cos_m, sin_m = _rope_tables(jnp.maximum(jnp.arange(Q_BLOCK) - PAD_FRONT, 0))
    qs, ks, vs, qd, kd, vd = _inproj(x2, g_attn, w_in_bf, cos_x, sin_x, 512, seq)
    meta_blk = jnp.concatenate(
        [jnp.zeros((PAD_FRONT, D_MODEL), x.dtype), meta_tokens.astype(x.dtype)], axis=0)
    _, ks_m, vs_m, _, kd_m, vd_m = _inproj(meta_blk, g_attn, w_in_bf, cos_m, sin_m, Q_BLOCK, Q_BLOCK)

    o_s = _swa(swa_sinks[0].astype(F32), qs, ks, vs, ks_m, vs_m, batch, nblk)
    o_d = _diff(diff_lambda[0].astype(F32), diff_subln_g[0].astype(F32).reshape(1, LANES),
                qd, kd, vd, kd_m, vd_m, batch, seq)

    w_router = jnp.concatenate(
        [w_router_group[0], w_router_expert[0],
         jnp.zeros((D_MODEL, LANES - N_GROUPS - N_EXPERTS), F32)], axis=1)
    h2, f_packed, logits = _outproj(o_s, o_d, x2, w_out[0].astype(BF16),
                                    ffn_norm_g[0].reshape(1, D_MODEL), w_router, 512)

    wts, dest, blocks_e, blk_start, n_used = _route(logits, n_tok)
    n_blocks = n_tok * TOP_K // ROW_BLK + N_EXPERTS
    n_items = n_blocks // ITEM_BLKS + N_EXPERTS
    n_used = n_used.astype(jnp.int32).reshape(1)
    blk = jnp.arange(n_blocks, dtype=jnp.int32)
    is_last = jnp.any((blk[:, None] + 1 == (blk_start + blocks_e)[None, :])
                      & (blocks_e > 0)[None, :], axis=1)
    need_zero = is_last | (blk >= n_used[0])
    zero_blk = jnp.nonzero(need_zero, size=2 * N_EXPERTS, fill_value=0)[0].astype(jnp.int32)
    n_zero = jnp.sum(need_zero).astype(jnp.int32).reshape(1)
    xs = _dispatch(zero_blk, n_zero, dest.reshape(-1), f_packed, n_blocks)
    item_e, item_r0, item_nb, n_live = _work_items(blocks_e, blk_start, n_items)
    y = _moe(item_e, item_r0, item_nb, jnp.concatenate([n_used, n_live.reshape(1)]), xs,
             w_gate[0], w_up[0], w_down[0], n_blocks)

    out = _combine(dest.reshape(-1), h2, wts.astype(F32),
                   final_norm_g.reshape(1, D_MODEL), y, 256)
    return out.reshape(batch, seq, D_MODEL)
```

```python
import functools
import math

import jax
import jax.numpy as jnp
from jax import lax
from jax.experimental import pallas as pl
from jax.experimental.pallas import tpu as pltpu

D_MODEL = 2048
N_META = 16
CHUNK = 64
Q_BLOCK = 128
PAD_FRONT = Q_BLOCK - N_META
HEAD_DIM = 64
ROPE_THETA = 10000.0
SWA_Q_HEADS = 16
SWA_KV_HEADS = 2
DIFF_HEADS = 8
SWA_Q_W = SWA_Q_HEADS * HEAD_DIM
SWA_KV_W = SWA_KV_HEADS * HEAD_DIM
DIFF_W = DIFF_HEADS * 2 * HEAD_DIM
IN_WIDTH = SWA_Q_W + 2 * SWA_KV_W + 3 * DIFF_W
N_GROUPS = 8
EXPERTS_PER_GROUP = 8
N_EXPERTS = N_GROUPS * EXPERTS_PER_GROUP
TOP_K = 2
D_EXPERT = 1408
RMS_EPS = 1e-6
NEG_INF = -1e30
LAM_INIT = 0.8 - 0.6 * math.exp(-0.3 * 0)

LANES = 128
ROW_BLK = 128
VMEM_LIMIT = 56 * 1024 * 1024

BF16 = jnp.bfloat16
F32 = jnp.float32


def _nt_dot(a, b):
    return lax.dot_general(a, b, (((1,), (1,)), ((), ())), preferred_element_type=F32)


LOG2_E = math.log2(math.e)
Q_SCALE = HEAD_DIM ** -0.5 * LOG2_E
_SEGS = (
    ("qs", 0, 512, True, Q_SCALE), ("qs", 512, 512, True, Q_SCALE),
    ("ks", 1024, 128, True, 1.0), ("vs", 1152, 128, False, 1.0),
    ("qd", 1280, 512, True, Q_SCALE), ("qd", 1792, 512, True, Q_SCALE),
    ("kd", 2304, 512, True, 1.0), ("kd", 2816, 512, True, 1.0),
    ("vd", 3328, 512, False, 1.0), ("vd", 3840, 512, False, 1.0),
)
_SEG_BASE = {"qs": 0, "ks": 1024, "vs": 1152, "qd": 1280, "kd": 2304, "vd": 3328}


def _inproj_kernel(x_ref, g_ref, w_ref, cos_ref, sin_ref,
                   qs_ref, ks_ref, vs_ref, qd_ref, kd_ref, vd_ref):
    x = x_ref[...]
    ms = jnp.mean(x * x, axis=-1, keepdims=True)
    a = (x * lax.rsqrt(ms + RMS_EPS) * g_ref[...]).astype(BF16)
    cos = cos_ref[...]
    sin = sin_ref[...]
    lane = lax.broadcasted_iota(jnp.int32, cos.shape, 1)
    first_half = (lane % HEAD_DIM) < (HEAD_DIM // 2)
    low_head = lane < HEAD_DIM
    outs = {"qs": qs_ref, "ks": ks_ref, "vs": vs_ref, "qd": qd_ref, "kd": kd_ref, "vd": vd_ref}
    for name, c0, width, rotary, scale in _SEGS:
        y = jnp.dot(a, w_ref[:, c0:c0 + width], preferred_element_type=F32)
        o0 = c0 - _SEG_BASE[name]
        for t in range(width // LANES):
            yt = y[:, t * LANES:(t + 1) * LANES]
            if rotary:
                partner = jnp.where(first_half, pltpu.roll(yt, LANES - HEAD_DIM // 2, 1),
                                    pltpu.roll(yt, HEAD_DIM // 2, 1))
                yt = yt * cos + partner * sin
            if scale != 1.0:
                yt = yt * scale
            if name in ("ks", "vs"):
                other = pltpu.roll(yt, HEAD_DIM, 1)
                outs[name][:, 0:LANES] = jnp.where(low_head, yt, other).astype(BF16)
                outs[name][:, LANES:2 * LANES] = jnp.where(low_head, other, yt).astype(BF16)
            else:
                outs[name][:, o0 + t * LANES:o0 + (t + 1) * LANES] = yt.astype(BF16)


def _inproj(x2, g, w_bf, cos_t, sin_t, tm, rows_per_seq):
    rows = x2.shape[0]
    nseq = rows_per_seq // tm
    row_spec = lambda width: pl.BlockSpec((tm, width), lambda i: (i, 0))
    out_shapes = (
        jax.ShapeDtypeStruct((rows, SWA_Q_W), BF16), jax.ShapeDtypeStruct((rows, 2 * SWA_KV_W), BF16),
        jax.ShapeDtypeStruct((rows, 2 * SWA_KV_W), BF16), jax.ShapeDtypeStruct((rows, DIFF_W), BF16),
        jax.ShapeDtypeStruct((rows, DIFF_W), BF16), jax.ShapeDtypeStruct((rows, DIFF_W), BF16),
    )
    return pl.pallas_call(
        _inproj_kernel,
        grid=(rows // tm,),
        in_specs=[
            row_spec(D_MODEL),
            pl.BlockSpec((1, D_MODEL), lambda i: (0, 0)),
            pl.BlockSpec((D_MODEL, IN_WIDTH), lambda i: (0, 0), pipeline_mode=pl.Buffered(1)),
            pl.BlockSpec((tm, LANES), lambda i: (i % nseq, 0)),
            pl.BlockSpec((tm, LANES), lambda i: (i % nseq, 0)),
        ],
        out_specs=(row_spec(SWA_Q_W), row_spec(2 * SWA_KV_W), row_spec(2 * SWA_KV_W),
                   row_spec(DIFF_W), row_spec(DIFF_W), row_spec(DIFF_W)),
        out_shape=out_shapes,
        compiler_params=pltpu.CompilerParams(dimension_semantics=("arbitrary",),
                                             vmem_limit_bytes=VMEM_LIMIT),
        name="inproj",
    )(x2, g, w_bf, cos_t, sin_t)


def _rope_tables(positions):
    inv = 1.0 / (ROPE_THETA ** (jnp.arange(0, HEAD_DIM, 2, dtype=F32) / HEAD_DIM))
    ang = positions.astype(F32)[:, None] * inv[None, :]
    cos, sin = jnp.cos(ang), jnp.sin(ang)
    cos_t = jnp.tile(cos, (1, 2 * LANES // HEAD_DIM))
    sin_t = jnp.tile(jnp.concatenate([-sin, sin], axis=1), (1, LANES // HEAD_DIM))
    return cos_t, sin_t


def _swa_kernel(sink_ref, q_ref, kp_ref, kc_ref, km_ref, vp_ref, vc_ref, vm_ref, o_ref):
    i = pl.program_id(1)
    nk = 3 * Q_BLOCK
    row = lax.broadcasted_iota(jnp.int32, (2 * Q_BLOCK, nk), 0)
    col = lax.broadcasted_iota(jnp.int32, (2 * Q_BLOCK, nk), 1)
    second_half = (row % Q_BLOCK) >= CHUNK
    part = col // Q_BLOCK
    idx = col % Q_BLOCK
    ok_prev = (part == 0) & (i > 0) & ((idx >= CHUNK) | jnp.logical_not(second_half))
    ok_cur = (part == 1) & ((idx < CHUNK) | second_half)
    ok_meta = (part == 2) & (idx >= PAD_FRONT)
    mask = ok_prev | ok_cur | ok_meta
    upper_rows = lax.broadcasted_iota(jnp.int32, (2 * Q_BLOCK, 1), 0) >= Q_BLOCK
    lane = lax.broadcasted_iota(jnp.int32, (Q_BLOCK, LANES), 1)
    low_head = lane < HEAD_DIM
    zero = jnp.zeros((Q_BLOCK, LANES), BF16)
    for g in range(SWA_KV_HEADS):
        gl = slice(g * LANES, (g + 1) * LANES)
        k_all = jnp.concatenate([kp_ref[:, gl], kc_ref[:, gl], km_ref[:, gl]], axis=0)
        v_all = jnp.concatenate([vp_ref[:, gl], vc_ref[:, gl], vm_ref[:, gl]], axis=0)
        pairs = SWA_Q_HEADS // SWA_KV_HEADS // 2
        for c in range(g * pairs, (g + 1) * pairs):
            qp = q_ref[:, c * LANES:(c + 1) * LANES]
            q2 = jnp.concatenate([jnp.where(low_head, qp, zero), jnp.where(low_head, zero, qp)], axis=0)
            s = jnp.where(mask, _nt_dot(q2, k_all), NEG_INF)
            sink = jnp.where(upper_rows, sink_ref[2 * c + 1], sink_ref[2 * c]) * LOG2_E
            m = jnp.maximum(jnp.max(s, axis=-1, keepdims=True), sink)
            p = jnp.exp2(s - m)
            denom = jnp.sum(p, axis=-1, keepdims=True) + jnp.exp2(sink - m)
            o2 = jnp.dot(p.astype(BF16), v_all, preferred_element_type=F32) / denom
            o_ref[:, c * LANES:(c + 1) * LANES] = jnp.where(
                low_head, o2[:Q_BLOCK], o2[Q_BLOCK:]).astype(BF16)


def _swa(sinks, qs, ks, vs, ks_meta, vs_meta, batch, nblk):
    rows = qs.shape[0]
    cur = lambda b, i: (b * nblk + i, 0)
    prev = lambda b, i: (b * nblk + jnp.maximum(i - 1, 0), 0)
    kv_w = 2 * SWA_KV_W
    return pl.pallas_call(
        _swa_kernel,
        grid=(batch, nblk),
        in_specs=[
            pl.BlockSpec(memory_space=pltpu.SMEM),
            pl.BlockSpec((Q_BLOCK, SWA_Q_W), cur),
            pl.BlockSpec((Q_BLOCK, kv_w), prev), pl.BlockSpec((Q_BLOCK, kv_w), cur),
            pl.BlockSpec((Q_BLOCK, kv_w), lambda b, i: (0, 0)),
            pl.BlockSpec((Q_BLOCK, kv_w), prev), pl.BlockSpec((Q_BLOCK, kv_w), cur),
            pl.BlockSpec((Q_BLOCK, kv_w), lambda b, i: (0, 0)),
        ],
        out_specs=pl.BlockSpec((Q_BLOCK, SWA_Q_W), cur),
        out_shape=jax.ShapeDtypeStruct((rows, SWA_Q_W), BF16),
        compiler_params=pltpu.CompilerParams(dimension_semantics=("arbitrary", "arbitrary"),
                                             vmem_limit_bytes=VMEM_LIMIT),
        name="swa",
    )(sinks, qs, ks, ks, ks_meta, vs, vs, vs_meta)


DIFF_TQ = 256
DIFF_TK = 512
DIFF_HEADS_PER_STEP = 2


def _diff_kernel(lam_ref, g_ref, q_ref, k_ref, v_ref, km_ref, vm_ref, o_ref, *, seq):
    dl = lam_ref[...]
    lam = (jnp.exp(jnp.sum(dl[0:1] * dl[1:2], axis=-1, keepdims=True))
           - jnp.exp(jnp.sum(dl[2:3] * dl[3:4], axis=-1, keepdims=True)) + LAM_INIT)
    lane = lax.broadcasted_iota(jnp.int32, (DIFF_TQ, LANES), 1)
    low_map = lane < HEAD_DIM
    zero = jnp.zeros((DIFF_TQ, LANES), BF16)
    meta_ok = lax.broadcasted_iota(jnp.int32, (2 * DIFF_TQ, Q_BLOCK), 1) >= PAD_FRONT
    chunk_gap = (lax.broadcasted_iota(jnp.int32, (2 * DIFF_TQ, DIFF_TK), 1) // CHUNK
                 - (lax.broadcasted_iota(jnp.int32, (2 * DIFF_TQ, DIFF_TK), 0) % DIFF_TQ) // CHUNK)

    def step(q2, k_t, v_t, mask, state):
        s = _nt_dot(q2, k_t)
        if mask is not None:
            s = jnp.where(mask, s, NEG_INF)
        smax = jnp.max(s, axis=-1, keepdims=True)
        if state is None:
            p = jnp.exp2(s - smax)
            return smax, jnp.sum(p, axis=-1, keepdims=True), jnp.dot(
                p.astype(BF16), v_t, preferred_element_type=F32)
        m_old, l_old, acc = state
        m_new = jnp.maximum(m_old, smax)
        alpha = jnp.exp2(m_old - m_new)
        p = jnp.exp2(s - m_new)
        return (m_new, alpha * l_old + jnp.sum(p, axis=-1, keepdims=True),
                alpha * acc + jnp.dot(p.astype(BF16), v_t, preferred_element_type=F32))

    heads = [slice(h * LANES, (h + 1) * LANES) for h in range(DIFF_HEADS_PER_STEP)]

    def q_tile(qi, carry):
        q0 = pl.multiple_of(qi * DIFF_TQ, DIFF_TQ)
        q2s = []
        for hs in heads:
            q = q_ref[pl.ds(q0, DIFF_TQ), hs]
            q2s.append(jnp.concatenate(
                [jnp.where(low_map, q, zero), jnp.where(low_map, zero, q)], axis=0))

        def all_heads(rows, mask, states):
            out = ()
            for h, hs in enumerate(heads):
                st = None if states is None else states[3 * h:3 * h + 3]
                k_t = km_ref[:, hs] if rows is None else k_ref[rows, hs]
                v_t = vm_ref[:, hs] if rows is None else v_ref[rows, hs]
                out = out + step(q2s[h], k_t, v_t, mask, st)
            return out

        st = all_heads(None, meta_ok, None)

        def k_tile(kj, st):
            k0 = pl.multiple_of(kj * DIFF_TK, DIFF_TK)
            return all_heads(pl.ds(k0, DIFF_TK), None, st)

        n_full = q0 // DIFF_TK
        st = lax.fori_loop(0, n_full, k_tile, st)
        k0 = pl.multiple_of(n_full * DIFF_TK, DIFF_TK)
        diag_ok = chunk_gap <= (q0 - k0) // CHUNK
        st = all_heads(pl.ds(k0, DIFF_TK), diag_ok, st)
        for h, hs in enumerate(heads):
            o = st[3 * h + 2] / st[3 * h + 1]
            o = o[:DIFF_TQ] - lam * o[DIFF_TQ:]
            o = o * lax.rsqrt(jnp.mean(o * o, axis=-1, keepdims=True) + RMS_EPS)
            o_ref[pl.ds(q0, DIFF_TQ), hs] = (o * g_ref[...] * (1.0 - LAM_INIT)).astype(BF16)
        return carry

    lax.fori_loop(0, seq // DIFF_TQ, q_tile, 0)


def _diff(diff_lambda, subln_g, qd, kd, vd, kd_meta, vd_meta, batch, seq):
    rows = qd.shape[0]
    head = lambda b, h: (b, h)
    meta = lambda b, h: (0, h)
    width = DIFF_HEADS_PER_STEP * LANES
    return pl.pallas_call(
        functools.partial(_diff_kernel, seq=seq),
        grid=(batch, DIFF_HEADS // DIFF_HEADS_PER_STEP),
        in_specs=[
            pl.BlockSpec((4, HEAD_DIM), lambda b, h: (0, 0)),
            pl.BlockSpec((1, LANES), lambda b, h: (0, 0)),
            pl.BlockSpec((seq, width), head), pl.BlockSpec((seq, width), head),
            pl.BlockSpec((seq, width), head),
            pl.BlockSpec((Q_BLOCK, width), meta), pl.BlockSpec((Q_BLOCK, width), meta),
        ],
        out_specs=pl.BlockSpec((seq, width), head),
        out_shape=jax.ShapeDtypeStruct((rows, DIFF_W), BF16),
        compiler_params=pltpu.CompilerParams(dimension_semantics=("arbitrary", "arbitrary"),
                                             vmem_limit_bytes=VMEM_LIMIT),
        name="diffattn",
    )(diff_lambda, subln_g, qd, kd, vd, kd_meta, vd_meta)


HALF = D_MODEL // 2


def _pack_bf16_pair(a, b):
    hi = pltpu.bitcast(a.astype(BF16).astype(F32), jnp.uint32)
    lo = pltpu.bitcast(b.astype(BF16).astype(F32), jnp.uint32)
    return hi | (lo >> 16)


def _unpack_bf16_pair(w):
    a = pltpu.bitcast(w & jnp.uint32(0xFFFF0000), F32)
    b = pltpu.bitcast(w << 16, F32)
    return a, b


OUT_CHUNK = 512


def _outproj_kernel(os_ref, od_ref, x_ref, w_ref, g_ref, wr_ref, h_ref, fp_ref, lg_ref):
    tm = x_ref.shape[0]
    ssq = jnp.zeros((tm, 1), F32)
    for c in range(D_MODEL // OUT_CHUNK):
        cols = slice(c * OUT_CHUNK, (c + 1) * OUT_CHUNK)
        y = jnp.dot(os_ref[...], w_ref[0:SWA_Q_W, cols], preferred_element_type=F32)
        y = y + jnp.dot(od_ref[...], w_ref[SWA_Q_W:, cols], preferred_element_type=F32)
        h = x_ref[:, cols] + y
        h_ref[:, cols] = h
        ssq = ssq + jnp.sum(h * h, axis=-1, keepdims=True)
    inv = lax.rsqrt(ssq * (1.0 / D_MODEL) + RMS_EPS)
    lg = jnp.zeros((tm, LANES), F32)
    for c in range(HALF // OUT_CHUNK):
        ca = slice(c * OUT_CHUNK, (c + 1) * OUT_CHUNK)
        cb = slice(HALF + c * OUT_CHUNK, HALF + (c + 1) * OUT_CHUNK)
        fa = h_ref[:, ca] * inv * g_ref[:, ca]
        fb = h_ref[:, cb] * inv * g_ref[:, cb]
        fp_ref[:, ca] = _pack_bf16_pair(fa, fb)
        for f, cc in ((fa, ca), (fb, cb)):
            f_hi = f.astype(BF16)
            f_lo = (f - f_hi.astype(F32)).astype(BF16)
            w = wr_ref[cc, :]
            w_hi = w.astype(BF16)
            w_lo = (w - w_hi.astype(F32)).astype(BF16)
            p = jnp.dot(f_hi, jnp.concatenate([w_hi, w_lo], axis=1), preferred_element_type=F32)
            lg = lg + p[:, 0:LANES] + p[:, LANES:] + jnp.dot(f_lo, w_hi, preferred_element_type=F32)
    lg_ref[...] = lg


def _outproj(o_s, o_d, x2, w_bf, g, w_router, tm):
    rows = x2.shape[0]
    row_spec = lambda width: pl.BlockSpec((tm, width), lambda i: (i, 0))
    return pl.pallas_call(
        _outproj_kernel,
        grid=(rows // tm,),
        in_specs=[
            row_spec(SWA_Q_W), row_spec(DIFF_W), row_spec(D_MODEL),
            pl.BlockSpec((D_MODEL, D_MODEL), lambda i: (0, 0), pipeline_mode=pl.Buffered(1)),
            pl.BlockSpec((1, D_MODEL), lambda i: (0, 0)),
            pl.BlockSpec((D_MODEL, LANES), lambda i: (0, 0)),
        ],
        out_specs=(row_spec(D_MODEL), row_spec(HALF), row_spec(LANES)),
        out_shape=(jax.ShapeDtypeStruct((rows, D_MODEL), F32),
                   jax.ShapeDtypeStruct((rows, HALF), jnp.uint32),
                   jax.ShapeDtypeStruct((rows, LANES), F32)),
        compiler_params=pltpu.CompilerParams(dimension_semantics=("arbitrary",),
                                             vmem_limit_bytes=VMEM_LIMIT),
        name="outproj",
    )(o_s, o_d, x2, w_bf, g, w_router)


DISPATCH_TM = 1024


def _dispatch_kernel(zero_blk_ref, n_zero_ref, dest_ref, f_ref, xs_ref, zeros, sem, zsem):
    i = pl.program_id(0)

    @pl.when(i == 0)
    def _():
        zeros[...] = jnp.zeros_like(zeros)

        def zero_copy(j):
            rows = pl.ds(zero_blk_ref[j] * ROW_BLK, ROW_BLK)
            return pltpu.make_async_copy(zeros, xs_ref.at[rows], zsem.at[0])

        def start(j, c):
            zero_copy(j).start()
            return c

        def wait(j, c):
            zero_copy(j).wait()
            return c
        lax.fori_loop(0, n_zero_ref[0], start, 0)
        lax.fori_loop(0, n_zero_ref[0], wait, 0)

    def issue(r, c):
        for kk in range(TOP_K):
            pltpu.make_async_copy(f_ref.at[pl.ds(r, 1)],
                                  xs_ref.at[pl.ds(dest_ref[TOP_K * r + kk], 1)], sem.at[0]).start()
        return c
    lax.fori_loop(0, DISPATCH_TM, issue, 0, unroll=8)
    n_rows = TOP_K * DISPATCH_TM
    pltpu.make_async_copy(xs_ref.at[pl.ds(0, n_rows)], xs_ref.at[pl.ds(0, n_rows)], sem.at[0]).wait()


def _dispatch(zero_blk, n_zero, dest_flat, f_packed, n_blocks):
    rows = f_packed.shape[0]
    return pl.pallas_call(
        _dispatch_kernel,
        grid_spec=pltpu.PrefetchScalarGridSpec(
            num_scalar_prefetch=2,
            grid=(rows // DISPATCH_TM,),
            in_specs=[
                pl.BlockSpec((TOP_K * DISPATCH_TM,), lambda i, z, n: (i,), memory_space=pltpu.SMEM),
                pl.BlockSpec((DISPATCH_TM, HALF), lambda i, z, n: (i, 0)),
            ],
            out_specs=pl.BlockSpec(memory_space=pl.ANY),
            scratch_shapes=[pltpu.VMEM((ROW_BLK, HALF), jnp.uint32),
                            pltpu.SemaphoreType.DMA((1,)),
                            pltpu.SemaphoreType.DMA((1,))],
        ),
        out_shape=jax.ShapeDtypeStruct((n_blocks * ROW_BLK, HALF), jnp.uint32),
        compiler_params=pltpu.CompilerParams(dimension_semantics=("arbitrary",),
                                             vmem_limit_bytes=VMEM_LIMIT),
        name="dispatch",
    )(zero_blk, n_zero, dest_flat, f_packed)


ITEM_BLKS = 8
ITEM_ROWS = ITEM_BLKS * ROW_BLK
F_TILE = 256
N_FT = -(-D_EXPERT // F_TILE)
Y_TILE = 512
N_YT = D_MODEL // Y_TILE


def _f_tile_start(k):
    return pl.multiple_of(jnp.minimum(k * F_TILE, D_EXPERT - F_TILE), LANES)


def _moe_kernel(e_ref, r0_ref, nb_ref, meta_ref, xs_ref, wg_hbm, wu_hbm, wd_hbm, y_ref,
                xraw, xb, hid, ystage, wgf, wuf, wdf, wgu, wdr, xsem, ysem, wsem):
    n_used = meta_ref[0]
    n_live = meta_ref[1]

    def for_blocks(count, fn):
        def body(j, c):
            fn(j)
            return c
        lax.fori_loop(0, count, body, 0)

    def x_copy(item, buf, j):
        src = xs_ref.at[pl.ds((r0_ref[item] + j) * ROW_BLK, ROW_BLK)]
        return pltpu.make_async_copy(src, xraw.at[buf, pl.ds(j * ROW_BLK, ROW_BLK)],
                                     xsem.at[buf, j])

    def w_copies(item, k, buf):
        e = e_ref[item]
        cols = pl.ds(_f_tile_start(k), F_TILE)
        return (pltpu.make_async_copy(wg_hbm.at[e, :, cols], wgf.at[buf], wsem.at[0, buf]),
                pltpu.make_async_copy(wu_hbm.at[e, :, cols], wuf.at[buf], wsem.at[1, buf]),
                pltpu.make_async_copy(wd_hbm.at[e, cols, :], wdf.at[buf], wsem.at[2, buf]))

    def zero_copy(j):
        dst = y_ref.at[pl.ds((n_used + j) * ROW_BLK, ROW_BLK)]
        return pltpu.make_async_copy(ystage.at[pl.ds(0, ROW_BLK)], dst, ysem.at[0])
    n_unused = y_ref.shape[0] // ROW_BLK - n_used
    ystage[0:ROW_BLK, :] = jnp.zeros((ROW_BLK, HALF), jnp.uint32)
    for_blocks(n_unused, lambda j: zero_copy(j).start())
    for_blocks(n_unused, lambda j: zero_copy(j).wait())

    for_blocks(nb_ref[0], lambda j: x_copy(0, 0, j).start())

    @pl.when(nb_ref[0] > 0)
    def _():
        for c in w_copies(0, 0, 0):
            c.start()

    def item_body(i, carry):
        slot = i % 2
        nb = nb_ref[i]
        prev = jnp.maximum(i - 1, 0)
        nb_prev = jnp.where(i > 0, nb_ref[prev], 0)

        def y_copy(j):
            dst = y_ref.at[pl.ds((r0_ref[prev] + j) * ROW_BLK, ROW_BLK)]
            return pltpu.make_async_copy(ystage.at[pl.ds(j * ROW_BLK, ROW_BLK)], dst, ysem.at[0])

        for_blocks(nb_ref[i + 1], lambda j: x_copy(i + 1, 1 - slot, j).start())

        def arrive(j):
            x_copy(i, slot, j).wait()
            rows = pl.ds(pl.multiple_of(j * ROW_BLK, ROW_BLK), ROW_BLK)
            lo, hi = _unpack_bf16_pair(xraw[slot, rows, :])
            xb[rows, 0:HALF] = lo.astype(BF16)
            xb[rows, HALF:] = hi.astype(BF16)
        for_blocks(nb, arrive)

        def tile_body(k, carry):
            wbuf = (i * N_FT + k) % 2
            last = k == N_FT - 1
            nxt_i = jnp.where(last, i + 1, i)
            nxt_k = jnp.where(last, 0, k + 1)

            @pl.when(nb_ref[nxt_i] > 0)
            def _():
                for c in w_copies(nxt_i, nxt_k, 1 - wbuf):
                    c.start()

            @pl.when(nb > 0)
            def _():
                for c in w_copies(i, k, wbuf):
                    c.wait()
                wgu[:, 0:F_TILE] = wgf[wbuf].astype(BF16)
                wgu[:, F_TILE:] = wuf[wbuf].astype(BF16)
                f_rows = pl.ds(_f_tile_start(k), F_TILE)
                for n in range(N_YT):
                    wdr[slot, n, f_rows, :] = wdf[wbuf, :, n * Y_TILE:(n + 1) * Y_TILE].astype(BF16)

            def gate_up(row0, nrows):
                rows = pl.ds(pl.multiple_of(row0, ROW_BLK), nrows)
                gu = jnp.dot(xb[rows, :], wgu[...], preferred_element_type=F32)
                gate, up = gu[:, 0:F_TILE], gu[:, F_TILE:]
                h = (gate * (1.0 / (1.0 + jnp.exp(-gate))) * up).astype(BF16)
                hid[slot, rows, pl.ds(_f_tile_start(k), F_TILE)] = h

            k_y = jnp.minimum(k, N_YT - 1)

            def down(row0, nrows):
                rows = pl.ds(pl.multiple_of(row0, ROW_BLK), nrows)
                yt = jnp.dot(hid[1 - slot, rows, :], wdr[1 - slot, k_y], preferred_element_type=F32)
                packed = _pack_bf16_pair(yt[:, 0:Y_TILE // 2], yt[:, Y_TILE // 2:])
                words = pl.ds(pl.multiple_of(k_y * (Y_TILE // 2), Y_TILE // 2), Y_TILE // 2)
                ystage[rows, words] = packed

            def for_row_tiles(n_blk, fn):
                big = 4 * ROW_BLK
                for_blocks(n_blk // 4, lambda t: fn(t * big, big))
                for bit in (2, 1):
                    @pl.when((n_blk & bit) != 0)
                    def _():
                        fn((n_blk // (2 * bit)) * (2 * bit) * ROW_BLK, bit * ROW_BLK)

            for_row_tiles(nb, gate_up)
            for_row_tiles(jnp.where(k < N_YT, nb_prev, 0), down)

            @pl.when(k == N_YT)
            def _():
                for_blocks(nb_prev, lambda j: y_copy(j).start())

            @pl.when(last)
            def _():
                for_blocks(nb_prev, lambda j: y_copy(j).wait())
            return carry

        lax.fori_loop(0, N_FT, tile_body, 0)
        return carry

    lax.fori_loop(0, n_live + 1, item_body, 0)


def _moe(item_e, item_r0, item_nb, meta, xs, w_gate, w_up, w_down, n_blocks):
    return pl.pallas_call(
        _moe_kernel,
        grid_spec=pltpu.PrefetchScalarGridSpec(
            num_scalar_prefetch=4,
            grid=(1,),
            in_specs=[pl.BlockSpec(memory_space=pl.ANY)] * 4,
            out_specs=pl.BlockSpec(memory_space=pl.ANY),
            scratch_shapes=[
                pltpu.VMEM((2, ITEM_ROWS, HALF), jnp.uint32),
                pltpu.VMEM((ITEM_ROWS, D_MODEL), BF16),
                pltpu.VMEM((2, ITEM_ROWS, D_EXPERT), BF16),
                pltpu.VMEM((ITEM_ROWS, HALF), jnp.uint32),
                pltpu.VMEM((2, D_MODEL, F_TILE), F32),
                pltpu.VMEM((2, D_MODEL, F_TILE), F32),
                pltpu.VMEM((2, F_TILE, D_MODEL), F32),
                pltpu.VMEM((D_MODEL, 2 * F_TILE), BF16),
                pltpu.VMEM((2, N_YT, D_EXPERT, Y_TILE), BF16),
                pltpu.SemaphoreType.DMA((2, ITEM_BLKS)),
                pltpu.SemaphoreType.DMA((1,)),
                pltpu.SemaphoreType.DMA((3, 2)),
            ],
        ),
        out_shape=jax.ShapeDtypeStruct((n_blocks * ROW_BLK, HALF), jnp.uint32),
        compiler_params=pltpu.CompilerParams(dimension_semantics=("arbitrary",),
                                             vmem_limit_bytes=VMEM_LIMIT),
        name="moe",
    )(item_e, item_r0, item_nb, meta, xs, w_gate, w_up, w_down)


def _combine_kernel(cur_ref, nxt_ref, h_ref, w_ref, g_ref, y_ref, o_ref, ya, yb, sem, *, tm):
    i = pl.program_id(0)
    n = pl.num_programs(0)
    slot = i % 2

    def fetch(idx_ref, buf):
        def issue(r, c):
            pltpu.make_async_copy(y_ref.at[pl.ds(idx_ref[TOP_K * r], 1)], ya.at[buf, pl.ds(r, 1)],
                                  sem.at[buf]).start()
            pltpu.make_async_copy(y_ref.at[pl.ds(idx_ref[TOP_K * r + 1], 1)], yb.at[buf, pl.ds(r, 1)],
                                  sem.at[buf]).start()
            return c
        lax.fori_loop(0, tm, issue, 0, unroll=8)

    @pl.when(i == 0)
    def _():
        fetch(cur_ref, 0)

    @pl.when(i + 1 < n)
    def _():
        fetch(nxt_ref, 1 - slot)

    pltpu.make_async_copy(ya.at[slot], ya.at[slot], sem.at[slot]).wait()
    pltpu.make_async_copy(yb.at[slot], yb.at[slot], sem.at[slot]).wait()

    wa = w_ref[:, 0:1]
    wb = w_ref[:, 1:2]
    half_t = Y_TILE // 2
    ssq = jnp.zeros((tm, 1), F32)
    for n in range(N_YT):
        words = slice(n * half_t, (n + 1) * half_t)
        a_lo, a_hi = _unpack_bf16_pair(ya[slot, :, words])
        b_lo, b_hi = _unpack_bf16_pair(yb[slot, :, words])
        lo_cols = slice(n * Y_TILE, n * Y_TILE + half_t)
        hi_cols = slice(n * Y_TILE + half_t, (n + 1) * Y_TILE)
        lo = h_ref[:, lo_cols] + wa * a_lo + wb * b_lo
        hi = h_ref[:, hi_cols] + wa * a_hi + wb * b_hi
        ssq = ssq + jnp.sum(lo * lo, axis=-1, keepdims=True) + jnp.sum(hi * hi, axis=-1, keepdims=True)
        o_ref[:, lo_cols] = lo
        o_ref[:, hi_cols] = hi
    inv = lax.rsqrt(ssq * (1.0 / D_MODEL) + RMS_EPS)
    o_ref[...] = o_ref[...] * inv * g_ref[...]


def _combine(dest_flat, h2, wts, g, y, tm):
    rows = h2.shape[0]
    steps = rows // tm
    idx_spec = lambda fn: pl.BlockSpec((TOP_K * tm,), fn, memory_space=pltpu.SMEM)
    return pl.pallas_call(
        functools.partial(_combine_kernel, tm=tm),
        grid=(steps,),
        in_specs=[
            idx_spec(lambda i: (i,)),
            idx_spec(lambda i: (jnp.minimum(i + 1, steps - 1),)),
            pl.BlockSpec((tm, D_MODEL), lambda i: (i, 0)),
            pl.BlockSpec((tm, LANES), lambda i: (i, 0)),
            pl.BlockSpec((1, D_MODEL), lambda i: (0, 0)),
            pl.BlockSpec(memory_space=pl.ANY),
        ],
        out_specs=pl.BlockSpec((tm, D_MODEL), lambda i: (i, 0)),
        scratch_shapes=[
            pltpu.VMEM((2, tm, HALF), jnp.uint32),
            pltpu.VMEM((2, tm, HALF), jnp.uint32),
            pltpu.SemaphoreType.DMA((2,)),
        ],
        out_shape=jax.ShapeDtypeStruct((rows, D_MODEL), F32),
        compiler_params=pltpu.CompilerParams(dimension_semantics=("arbitrary",),
                                             vmem_limit_bytes=VMEM_LIMIT),
        name="combine",
    )(dest_flat, dest_flat, h2, wts, g, y)


ROUTE_TM = 512


def _route_kernel(lg_ref, info_ref, wts_ref, tab_ref, counts, starts, carry):
    phase = pl.program_id(0)
    i = pl.program_id(1)
    tm = ROUTE_TM
    lane = lax.broadcasted_iota(jnp.int32, (tm, LANES), 1)
    lg = lg_ref[...]

    def first_lane_of_max(v):
        vmax = jnp.max(v, axis=-1, keepdims=True)
        return vmax, jnp.min(jnp.where(v == vmax, lane, LANES), axis=-1, keepdims=True)

    gl = jnp.where(lane < N_GROUPS, lg, NEG_INF)
    gmax, g_sel = first_lane_of_max(gl)
    gate = 1.0 / jnp.sum(jnp.exp(gl - gmax), axis=-1, keepdims=True)
    lo = N_GROUPS + EXPERTS_PER_GROUP * g_sel
    el = jnp.where((lane >= lo) & (lane < lo + EXPERTS_PER_GROUP), lg, NEG_INF)
    m1, l1 = first_lane_of_max(el)
    m2, l2 = first_lane_of_max(jnp.where(lane == l1, NEG_INF, el))
    ratio = jnp.exp(m2 - m1)
    w1 = gate / (1.0 + ratio)
    w2 = w1 * ratio
    e1 = l1 - N_GROUPS
    e2 = l2 - N_GROUPS
    hit1 = lane == e1
    hit2 = lane == e2
    onehot = (hit1 | hit2).astype(F32)
    tile_counts = jnp.sum(onehot, axis=0, keepdims=True)

    @pl.when((phase == 0) & (i == 0))
    def _():
        counts[...] = jnp.zeros_like(counts)

    @pl.when(phase == 0)
    def _():
        counts[...] += tile_counts

    @pl.when((phase == 1) & (i == 0))
    def _():
        blocks = jnp.floor((counts[...] + (ROW_BLK - 1)) * (1.0 / ROW_BLK))
        r = lax.broadcasted_iota(jnp.int32, (LANES, LANES), 0)
        c = lax.broadcasted_iota(jnp.int32, (LANES, LANES), 1)
        before = (r < c).astype(F32)
        first = jnp.dot(jnp.broadcast_to(blocks, (8, LANES)), before,
                        preferred_element_type=F32, precision=lax.Precision.HIGHEST)[0:1]
        starts[...] = first
        carry[...] = jnp.zeros_like(carry)
        used = jnp.sum(blocks, axis=-1, keepdims=True)
        row = lax.broadcasted_iota(jnp.int32, (8, LANES), 0)
        tab = jnp.where(row == 0, blocks, jnp.where(row == 1, first, jnp.where(row == 2, used, 0.0)))
        tab_ref[...] = tab.astype(jnp.int32)

    @pl.when(phase == 1)
    def _():
        r = lax.broadcasted_iota(jnp.int32, (tm, tm), 0)
        c = lax.broadcasted_iota(jnp.int32, (tm, tm), 1)
        earlier = (c < r).astype(BF16)
        before = jnp.dot(earlier, onehot.astype(BF16), preferred_element_type=F32)
        slot = starts[...] * ROW_BLK + carry[...] + before
        d1 = jnp.sum(jnp.where(hit1, slot, 0.0), axis=-1, keepdims=True)
        d2 = jnp.sum(jnp.where(hit2, slot, 0.0), axis=-1, keepdims=True)
        info_ref[...] = jnp.where(lane == 0, d1, jnp.where(lane == 1, d2, 0.0)).astype(jnp.int32)
        wts_ref[...] = jnp.where(lane == 0, w1, jnp.where(lane == 1, w2, 0.0))
        carry[...] += tile_counts


def _route(logits):
    rows = logits.shape[0]
    tile = lambda p, i: (i * p, 0)
    return pl.pallas_call(
        _route_kernel,
        grid=(2, rows // ROUTE_TM),
        in_specs=[pl.BlockSpec((ROUTE_TM, LANES), lambda p, i: (i, 0))],
        out_specs=(pl.BlockSpec((ROUTE_TM, LANES), tile), pl.BlockSpec((ROUTE_TM, LANES), tile),
                   pl.BlockSpec((8, LANES), lambda p, i: (0, 0))),
        out_shape=(jax.ShapeDtypeStruct((rows, LANES), jnp.int32),
                   jax.ShapeDtypeStruct((rows, LANES), F32),
                   jax.ShapeDtypeStruct((8, LANES), jnp.int32)),
        scratch_shapes=[pltpu.VMEM((1, LANES), F32)] * 3,
        compiler_params=pltpu.CompilerParams(dimension_semantics=("arbitrary", "arbitrary"),
                                             vmem_limit_bytes=VMEM_LIMIT),
        name="route",
    )(logits)


def _work_items(blocks_e, blk_start, n_items):
    items_e = (blocks_e + ITEM_BLKS - 1) // ITEM_BLKS
    item_end = jnp.cumsum(items_e)
    item_start = item_end - items_e
    total = item_end[-1]
    idx = jnp.arange(n_items, dtype=jnp.int32)
    live = idx < total
    ref = jnp.minimum(idx, total - 1)
    e = jnp.searchsorted(item_end, ref, side='right').astype(jnp.int32)
    j = ref - item_start[e]
    r0 = blk_start[e] + ITEM_BLKS * j
    nb = jnp.where(live, jnp.clip(blocks_e[e] - ITEM_BLKS * j, 0, ITEM_BLKS), 0)
    return e, r0.astype(jnp.int32), nb.astype(jnp.int32), total.astype(jnp.int32)


def kernel(x, meta_tokens, attn_norm_g, w_in, swa_sinks, diff_lambda, diff_subln_g, w_out,
           ffn_norm_g, w_router_group, w_router_expert, w_gate, w_up, w_down, final_norm_g):
    batch, seq, _ = x.shape
    n_tok = batch * seq
    nblk = seq // Q_BLOCK
    x2 = x.reshape(n_tok, D_MODEL)

    w_in_bf = w_in[0].astype(BF16)
    g_attn = attn_norm_g[0].reshape(1, D_MODEL)
    cos_x, sin_x = _rope_tables(N_META + jnp.arange(seq))
    cos_m, sin_m = _rope_tables(jnp.maximum(jnp.arange(Q_BLOCK) - PAD_FRONT, 0))
    qs, ks, vs, qd, kd, vd = _inproj(x2, g_attn, w_in_bf, cos_x, sin_x, 512, seq)
    meta_blk = jnp.concatenate(
        [jnp.zeros((PAD_FRONT, D_MODEL), x.dtype), meta_tokens.astype(x.dtype)], axis=0)
    _, ks_m, vs_m, _, kd_m, vd_m = _inproj(meta_blk, g_attn, w_in_bf, cos_m, sin_m, Q_BLOCK, Q_BLOCK)

    o_s = _swa(swa_sinks[0].astype(F32), qs, ks, vs, ks_m, vs_m, batch, nblk)
    o_d = _diff(diff_lambda[0].astype(F32), diff_subln_g[0].astype(F32).reshape(1, LANES),
                qd, kd, vd, kd_m, vd_m, batch, seq)

    w_router = jnp.concatenate(
        [w_router_group[0], w_router_expert[0],
         jnp.zeros((D_MODEL, LANES - N_GROUPS - N_EXPERTS), F32)], axis=1)
    h2, f_packed, logits = _outproj(o_s, o_d, x2, w_out[0].astype(BF16),
                                    ffn_norm_g[0].reshape(1, D_MODEL), w_router, 512)

    info, wts, tab = _route(logits)
    dest_flat = info[:, :TOP_K].reshape(-1)
    blocks_e, blk_start, n_used = tab[0, :N_EXPERTS], tab[1, :N_EXPERTS], tab[2, 0:1]
    n_blocks = n_tok * TOP_K // ROW_BLK + N_EXPERTS
    n_items = n_blocks // ITEM_BLKS + N_EXPERTS
    blk = jnp.arange(n_blocks, dtype=jnp.int32)
    is_last = jnp.any((blk[:, None] + 1 == (blk_start + blocks_e)[None, :])
                      & (blocks_e > 0)[None, :], axis=1)
    need_zero = is_last | (blk >= n_used[0])
    zero_blk = jnp.nonzero(need_zero, size=2 * N_EXPERTS, fill_value=0)[0].astype(jnp.int32)
    n_zero = jnp.sum(need_zero).astype(jnp.int32).reshape(1)
    xs = _dispatch(zero_blk, n_zero, dest_flat, f_packed, n_blocks)
    item_e, item_r0, item_nb, n_live = _work_items(blocks_e, blk_start, n_items)
    y = _moe(item_e, item_r0, item_nb, jnp.concatenate([n_used, n_live.reshape(1)]), xs,
             w_gate[0], w_up[0], w_down[0], n_blocks)

    out = _combine(dest_flat, h2, wts, final_norm_g.reshape(1, D_MODEL), y, 256)
    return out.reshape(batch, seq, D_MODEL)
```

```python
import functools
import math

import jax
import jax.numpy as jnp
from jax import lax
from jax.experimental import pallas as pl
from jax.experimental.pallas import tpu as pltpu

D_MODEL = 2048
N_META = 16
CHUNK = 64
Q_BLOCK = 128
PAD_FRONT = Q_BLOCK - N_META
HEAD_DIM = 64
ROPE_THETA = 10000.0
SWA_Q_HEADS = 16
SWA_KV_HEADS = 2
DIFF_HEADS = 8
SWA_Q_W = SWA_Q_HEADS * HEAD_DIM
SWA_KV_W = SWA_KV_HEADS * HEAD_DIM
DIFF_W = DIFF_HEADS * 2 * HEAD_DIM
IN_WIDTH = SWA_Q_W + 2 * SWA_KV_W + 3 * DIFF_W
N_GROUPS = 8
EXPERTS_PER_GROUP = 8
N_EXPERTS = N_GROUPS * EXPERTS_PER_GROUP
TOP_K = 2
D_EXPERT = 1408
RMS_EPS = 1e-6
NEG_INF = -1e30
LAM_INIT = 0.8 - 0.6 * math.exp(-0.3 * 0)

LANES = 128
ROW_BLK = 128
VMEM_LIMIT = 56 * 1024 * 1024

BF16 = jnp.bfloat16
F32 = jnp.float32


def _nt_dot(a, b):
    return lax.dot_general(a, b, (((1,), (1,)), ((), ())), preferred_element_type=F32)


LOG2_E = math.log2(math.e)
Q_SCALE = HEAD_DIM ** -0.5 * LOG2_E
_SEGS = (
    ("qs", 0, 512, True, Q_SCALE), ("qs", 512, 512, True, Q_SCALE),
    ("ks", 1024, 128, True, 1.0), ("vs", 1152, 128, False, 1.0),
    ("qd", 1280, 512, True, Q_SCALE), ("qd", 1792, 512, True, Q_SCALE),
    ("kd", 2304, 512, True, 1.0), ("kd", 2816, 512, True, 1.0),
    ("vd", 3328, 512, False, 1.0), ("vd", 3840, 512, False, 1.0),
)
_SEG_BASE = {"qs": 0, "ks": 1024, "vs": 1152, "qd": 1280, "kd": 2304, "vd": 3328}


def _inproj_kernel(x_ref, g_ref, w_ref, cos_ref, sin_ref,
                   qs_ref, ks_ref, vs_ref, qd_ref, kd_ref, vd_ref):
    x = x_ref[...]
    ms = jnp.mean(x * x, axis=-1, keepdims=True)
    a = (x * lax.rsqrt(ms + RMS_EPS) * g_ref[...]).astype(BF16)
    cos = cos_ref[...]
    sin = sin_ref[...]
    lane = lax.broadcasted_iota(jnp.int32, cos.shape, 1)
    first_half = (lane % HEAD_DIM) < (HEAD_DIM // 2)
    low_head = lane < HEAD_DIM
    outs = {"qs": qs_ref, "ks": ks_ref, "vs": vs_ref, "qd": qd_ref, "kd": kd_ref, "vd": vd_ref}
    for name, c0, width, rotary, scale in _SEGS:
        y = jnp.dot(a, w_ref[:, c0:c0 + width], preferred_element_type=F32)
        o0 = c0 - _SEG_BASE[name]
        for t in range(width // LANES):
            yt = y[:, t * LANES:(t + 1) * LANES]
            if rotary:
                partner = jnp.where(first_half, pltpu.roll(yt, LANES - HEAD_DIM // 2, 1),
                                    pltpu.roll(yt, HEAD_DIM // 2, 1))
                yt = yt * cos + partner * sin
            if scale != 1.0:
                yt = yt * scale
            if name in ("ks", "vs"):
                other = pltpu.roll(yt, HEAD_DIM, 1)
                outs[name][:, 0:LANES] = jnp.where(low_head, yt, other).astype(BF16)
                outs[name][:, LANES:2 * LANES] = jnp.where(low_head, other, yt).astype(BF16)
            else:
                outs[name][:, o0 + t * LANES:o0 + (t + 1) * LANES] = yt.astype(BF16)


def _inproj(x2, g, w_bf, cos_t, sin_t, tm, rows_per_seq):
    rows = x2.shape[0]
    nseq = rows_per_seq // tm
    row_spec = lambda width: pl.BlockSpec((tm, width), lambda i: (i, 0))
    out_shapes = (
        jax.ShapeDtypeStruct((rows, SWA_Q_W), BF16), jax.ShapeDtypeStruct((rows, 2 * SWA_KV_W), BF16),
        jax.ShapeDtypeStruct((rows, 2 * SWA_KV_W), BF16), jax.ShapeDtypeStruct((rows, DIFF_W), BF16),
        jax.ShapeDtypeStruct((rows, DIFF_W), BF16), jax.ShapeDtypeStruct((rows, DIFF_W), BF16),
    )
    return pl.pallas_call(
        _inproj_kernel,
        grid=(rows // tm,),
        in_specs=[
            row_spec(D_MODEL),
            pl.BlockSpec((1, D_MODEL), lambda i: (0, 0)),
            pl.BlockSpec((D_MODEL, IN_WIDTH), lambda i: (0, 0), pipeline_mode=pl.Buffered(1)),
            pl.BlockSpec((tm, LANES), lambda i: (i % nseq, 0)),
            pl.BlockSpec((tm, LANES), lambda i: (i % nseq, 0)),
        ],
        out_specs=(row_spec(SWA_Q_W), row_spec(2 * SWA_KV_W), row_spec(2 * SWA_KV_W),
                   row_spec(DIFF_W), row_spec(DIFF_W), row_spec(DIFF_W)),
        out_shape=out_shapes,
        compiler_params=pltpu.CompilerParams(dimension_semantics=("arbitrary",),
                                             vmem_limit_bytes=VMEM_LIMIT),
        name="inproj",
    )(x2, g, w_bf, cos_t, sin_t)


def _rope_tables(positions):
    inv = 1.0 / (ROPE_THETA ** (jnp.arange(0, HEAD_DIM, 2, dtype=F32) / HEAD_DIM))
    ang = positions.astype(F32)[:, None] * inv[None, :]
    cos, sin = jnp.cos(ang), jnp.sin(ang)
    cos_t = jnp.tile(cos, (1, 2 * LANES // HEAD_DIM))
    sin_t = jnp.tile(jnp.concatenate([-sin, sin], axis=1), (1, LANES // HEAD_DIM))
    return cos_t, sin_t


SWA_SLAB = 256
SWA_BAND = 3 * CHUNK + N_META


def _swa_kernel(sink_ref, q_ref, kp_ref, kc_ref, km_ref, vp_ref, vc_ref, vm_ref, o_ref):
    i = pl.program_id(1)
    pairs = SWA_Q_HEADS // SWA_KV_HEADS // 2
    m_rows = pairs * 2 * CHUNK
    col = lax.broadcasted_iota(jnp.int32, (m_rows, SWA_SLAB), 1)
    row = lax.broadcasted_iota(jnp.int32, (m_rows, 1), 0)
    lane = lax.broadcasted_iota(jnp.int32, (CHUNK, LANES), 1)
    low_head = lane < HEAD_DIM
    zero = jnp.zeros((CHUNK, LANES), BF16)
    pad = jnp.zeros((SWA_SLAB - SWA_BAND, LANES), BF16)
    meta_rows = slice(PAD_FRONT, Q_BLOCK)
    for g in range(SWA_KV_HEADS):
        gl = slice(g * LANES, (g + 1) * LANES)
        sink = jnp.zeros((m_rows, 1), F32)
        for h in range(2 * pairs):
            sink = jnp.where(row // CHUNK == h, sink_ref[2 * g * pairs + h], sink)
        sink = sink * LOG2_E
        for half in range(2):
            q_rows = slice(half * CHUNK, (half + 1) * CHUNK)
            if half == 0:
                pieces = lambda p, c, m: [p[:, gl], c[0:CHUNK, gl], m[meta_rows, gl], pad]
                n_prev = 2 * CHUNK
            else:
                pieces = lambda p, c, m: [p[CHUNK:, gl], c[:, gl], m[meta_rows, gl], pad]
                n_prev = CHUNK
            k_slab = jnp.concatenate(pieces(kp_ref, kc_ref, km_ref), axis=0)
            v_slab = jnp.concatenate(pieces(vp_ref, vc_ref, vm_ref), axis=0)
            valid = (col < SWA_BAND) & ((i > 0) | (col >= n_prev))
            q2 = []
            for c in range(g * pairs, (g + 1) * pairs):
                qp = q_ref[q_rows, c * LANES:(c + 1) * LANES]
                q2 += [jnp.where(low_head, qp, zero), jnp.where(low_head, zero, qp)]
            s = jnp.where(valid, _nt_dot(jnp.concatenate(q2, axis=0), k_slab), NEG_INF)
            m = jnp.maximum(jnp.max(s, axis=-1, keepdims=True), sink)
            p = jnp.exp2(s - m)
            denom = jnp.sum(p, axis=-1, keepdims=True) + jnp.exp2(sink - m)
            o2 = jnp.dot(p.astype(BF16), v_slab, preferred_element_type=F32) / denom
            for j, c in enumerate(range(g * pairs, (g + 1) * pairs)):
                lo = o2[(2 * j) * CHUNK:(2 * j + 1) * CHUNK]
                hi = o2[(2 * j + 1) * CHUNK:(2 * j + 2) * CHUNK]
                o_ref[q_rows, c * LANES:(c + 1) * LANES] = jnp.where(low_head, lo, hi).astype(BF16)


def _swa(sinks, qs, ks, vs, ks_meta, vs_meta, batch, nblk):
    rows = qs.shape[0]
    cur = lambda b, i: (b * nblk + i, 0)
    prev = lambda b, i: (b * nblk + jnp.maximum(i - 1, 0), 0)
    kv_w = 2 * SWA_KV_W
    return pl.pallas_call(
        _swa_kernel,
        grid=(batch, nblk),
        in_specs=[
            pl.BlockSpec(memory_space=pltpu.SMEM),
            pl.BlockSpec((Q_BLOCK, SWA_Q_W), cur),
            pl.BlockSpec((Q_BLOCK, kv_w), prev), pl.BlockSpec((Q_BLOCK, kv_w), cur),
            pl.BlockSpec((Q_BLOCK, kv_w), lambda b, i: (0, 0)),
            pl.BlockSpec((Q_BLOCK, kv_w), prev), pl.BlockSpec((Q_BLOCK, kv_w), cur),
            pl.BlockSpec((Q_BLOCK, kv_w), lambda b, i: (0, 0)),
        ],
        out_specs=pl.BlockSpec((Q_BLOCK, SWA_Q_W), cur),
        out_shape=jax.ShapeDtypeStruct((rows, SWA_Q_W), BF16),
        compiler_params=pltpu.CompilerParams(dimension_semantics=("arbitrary", "arbitrary"),
                                             vmem_limit_bytes=VMEM_LIMIT),
        name="swa",
    )(sinks, qs, ks, ks, ks_meta, vs, vs, vs_meta)


DIFF_TQ = 256
DIFF_TK = 512
DIFF_HEADS_PER_STEP = 2


def _diff_kernel(lam_ref, g_ref, q_ref, k_ref, v_ref, km_ref, vm_ref, o_ref, *, seq):
    dl = lam_ref[...]
    lam = (jnp.exp(jnp.sum(dl[0:1] * dl[1:2], axis=-1, keepdims=True))
           - jnp.exp(jnp.sum(dl[2:3] * dl[3:4], axis=-1, keepdims=True)) + LAM_INIT)
    lane = lax.broadcasted_iota(jnp.int32, (DIFF_TQ, LANES), 1)
    low_map = lane < HEAD_DIM
    zero = jnp.zeros((DIFF_TQ, LANES), BF16)

    def last_mask(n_keys):
        shape = (2 * DIFF_TQ, Q_BLOCK + n_keys)
        col = lax.broadcasted_iota(jnp.int32, shape, 1)
        q_chunk = (lax.broadcasted_iota(jnp.int32, shape, 0) % DIFF_TQ) // CHUNK
        first_chunk = (DIFF_TQ - n_keys) // CHUNK
        k_chunk = (col - Q_BLOCK) // CHUNK + first_chunk
        return ((col >= PAD_FRONT) & (col < Q_BLOCK)) | ((col >= Q_BLOCK) & (k_chunk <= q_chunk))

    def step(q2, k_t, v_t, mask, state):
        s = _nt_dot(q2, k_t)
        if mask is not None:
            s = jnp.where(mask, s, NEG_INF)
        smax = jnp.max(s, axis=-1, keepdims=True)
        if state is None:
            p = jnp.exp2(s - smax)
            return smax, jnp.sum(p, axis=-1, keepdims=True), jnp.dot(
                p.astype(BF16), v_t, preferred_element_type=F32)
        m_old, l_old, acc = state
        m_new = jnp.maximum(m_old, smax)
        alpha = jnp.exp2(m_old - m_new)
        p = jnp.exp2(s - m_new)
        return (m_new, alpha * l_old + jnp.sum(p, axis=-1, keepdims=True),
                alpha * acc + jnp.dot(p.astype(BF16), v_t, preferred_element_type=F32))

    heads = [slice(h * LANES, (h + 1) * LANES) for h in range(DIFF_HEADS_PER_STEP)]

    tiles_per_k = DIFF_TK // DIFF_TQ
    masks = [last_mask((t + 1) * DIFF_TQ) for t in range(tiles_per_k)]

    def q_tile(n_full, t):
        if n_full is None:
            aligned = lambda v: v
            q0 = t * DIFF_TQ
        else:
            aligned = lambda v: pl.multiple_of(v, DIFF_TQ)
            q0 = aligned((n_full * tiles_per_k + t) * DIFF_TQ)
        q2s = []
        for hs in heads:
            q = q_ref[pl.ds(q0, DIFF_TQ), hs]
            q2s.append(jnp.concatenate(
                [jnp.where(low_map, q, zero), jnp.where(low_map, zero, q)], axis=0))

        def full_tiles(kj, states):
            rows = pl.ds(pl.multiple_of(kj * DIFF_TK, DIFF_TK), DIFF_TK)
            out = ()
            for h, hs in enumerate(heads):
                st = None if states is None else states[3 * h:3 * h + 3]
                out = out + step(q2s[h], k_ref[rows, hs], v_ref[rows, hs], None, st)
            return out

        if n_full is None:
            states = None
        else:
            states = lax.fori_loop(1, n_full, full_tiles, full_tiles(0, None))

        n_keys = (t + 1) * DIFF_TQ
        rows = pl.ds(aligned(q0 + DIFF_TQ - n_keys), n_keys)
        for h, hs in enumerate(heads):
            st = None if states is None else states[3 * h:3 * h + 3]
            k_t = jnp.concatenate([km_ref[:, hs], k_ref[rows, hs]], axis=0)
            v_t = jnp.concatenate([vm_ref[:, hs], v_ref[rows, hs]], axis=0)
            _, l, acc = step(q2s[h], k_t, v_t, masks[t], st)
            o = acc / l
            o = o[:DIFF_TQ] - lam * o[DIFF_TQ:]
            o = o * lax.rsqrt(jnp.mean(o * o, axis=-1, keepdims=True) + RMS_EPS)
            o_ref[pl.ds(q0, DIFF_TQ), hs] = (o * g_ref[...] * (1.0 - LAM_INIT)).astype(BF16)

    def q_group(n_full, carry):
        for t in range(tiles_per_k):
            q_tile(n_full, t)
        return carry

    q_group(None, 0)
    lax.fori_loop(1, seq // DIFF_TK, q_group, 0)


def _diff(diff_lambda, subln_g, qd, kd, vd, kd_meta, vd_meta, batch, seq):
    rows = qd.shape[0]
    head = lambda b, h: (b, h)
    meta = lambda b, h: (0, h)
    width = DIFF_HEADS_PER_STEP * LANES
    return pl.pallas_call(
        functools.partial(_diff_kernel, seq=seq),
        grid=(batch, DIFF_HEADS // DIFF_HEADS_PER_STEP),
        in_specs=[
            pl.BlockSpec((4, HEAD_DIM), lambda b, h: (0, 0)),
            pl.BlockSpec((1, LANES), lambda b, h: (0, 0)),
            pl.BlockSpec((seq, width), head), pl.BlockSpec((seq, width), head),
            pl.BlockSpec((seq, width), head),
            pl.BlockSpec((Q_BLOCK, width), meta), pl.BlockSpec((Q_BLOCK, width), meta),
        ],
        out_specs=pl.BlockSpec((seq, width), head),
        out_shape=jax.ShapeDtypeStruct((rows, DIFF_W), BF16),
        compiler_params=pltpu.CompilerParams(dimension_semantics=("arbitrary", "arbitrary"),
                                             vmem_limit_bytes=VMEM_LIMIT),
        name="diffattn",
    )(diff_lambda, subln_g, qd, kd, vd, kd_meta, vd_meta)


HALF = D_MODEL // 2


def _pack_bf16_pair(a, b):
    hi = pltpu.bitcast(a.astype(BF16).astype(F32), jnp.uint32)
    lo = pltpu.bitcast(b.astype(BF16).astype(F32), jnp.uint32)
    return hi | (lo >> 16)


def _unpack_bf16_pair(w):
    a = pltpu.bitcast(w & jnp.uint32(0xFFFF0000), F32)
    b = pltpu.bitcast(w << 16, F32)
    return a, b


OUT_CHUNK = 512


def _outproj_kernel(os_ref, od_ref, x_ref, w_ref, g_ref, wr_ref, h_ref, fp_ref, lg_ref):
    tm = x_ref.shape[0]
    ssq = jnp.zeros((tm, 1), F32)
    for c in range(D_MODEL // OUT_CHUNK):
        cols = slice(c * OUT_CHUNK, (c + 1) * OUT_CHUNK)
        y = jnp.dot(os_ref[...], w_ref[0:SWA_Q_W, cols], preferred_element_type=F32)
        y = y + jnp.dot(od_ref[...], w_ref[SWA_Q_W:, cols], preferred_element_type=F32)
        h = x_ref[:, cols] + y
        h_ref[:, cols] = h
        ssq = ssq + jnp.sum(h * h, axis=-1, keepdims=True)
    inv = lax.rsqrt(ssq * (1.0 / D_MODEL) + RMS_EPS)
    lg = jnp.zeros((tm, LANES), F32)
    for c in range(HALF // OUT_CHUNK):
        ca = slice(c * OUT_CHUNK, (c + 1) * OUT_CHUNK)
        cb = slice(HALF + c * OUT_CHUNK, HALF + (c + 1) * OUT_CHUNK)
        fa = h_ref[:, ca] * inv * g_ref[:, ca]
        fb = h_ref[:, cb] * inv * g_ref[:, cb]
        fp_ref[:, ca] = _pack_bf16_pair(fa, fb)
        for f, cc in ((fa, ca), (fb, cb)):
            f_hi = f.astype(BF16)
            f_lo = (f - f_hi.astype(F32)).astype(BF16)
            w = wr_ref[cc, :]
            w_hi = w.astype(BF16)
            w_lo = (w - w_hi.astype(F32)).astype(BF16)
            p = jnp.dot(f_hi, jnp.concatenate([w_hi, w_lo], axis=1), preferred_element_type=F32)
            lg = lg + p[:, 0:LANES] + p[:, LANES:] + jnp.dot(f_lo, w_hi, preferred_element_type=F32)
    lg_ref[...] = lg


def _outproj(o_s, o_d, x2, w_bf, g, w_router, tm):
    rows = x2.shape[0]
    row_spec = lambda width: pl.BlockSpec((tm, width), lambda i: (i, 0))
    return pl.pallas_call(
        _outproj_kernel,
        grid=(rows // tm,),
        in_specs=[
            row_spec(SWA_Q_W), row_spec(DIFF_W), row_spec(D_MODEL),
            pl.BlockSpec((D_MODEL, D_MODEL), lambda i: (0, 0), pipeline_mode=pl.Buffered(1)),
            pl.BlockSpec((1, D_MODEL), lambda i: (0, 0)),
            pl.BlockSpec((D_MODEL, LANES), lambda i: (0, 0)),
        ],
        out_specs=(row_spec(D_MODEL), row_spec(HALF), row_spec(LANES)),
        out_shape=(jax.ShapeDtypeStruct((rows, D_MODEL), F32),
                   jax.ShapeDtypeStruct((rows, HALF), jnp.uint32),
                   jax.ShapeDtypeStruct((rows, LANES), F32)),
        compiler_params=pltpu.CompilerParams(dimension_semantics=("arbitrary",),
                                             vmem_limit_bytes=VMEM_LIMIT),
        name="outproj",
    )(o_s, o_d, x2, w_bf, g, w_router)


DISPATCH_TM = 1024


def _dispatch_kernel(zero_blk_ref, n_zero_ref, dest_ref, f_ref, xs_ref, zeros, sem, zsem):
    i = pl.program_id(0)

    @pl.when(i == 0)
    def _():
        zeros[...] = jnp.zeros_like(zeros)

        def zero_copy(j):
            rows = pl.ds(zero_blk_ref[j] * ROW_BLK, ROW_BLK)
            return pltpu.make_async_copy(zeros, xs_ref.at[rows], zsem.at[0])

        def start(j, c):
            zero_copy(j).start()
            return c

        def wait(j, c):
            zero_copy(j).wait()
            return c
        lax.fori_loop(0, n_zero_ref[0], start, 0)
        lax.fori_loop(0, n_zero_ref[0], wait, 0)

    def issue(r, c):
        for kk in range(TOP_K):
            pltpu.make_async_copy(f_ref.at[pl.ds(r, 1)],
                                  xs_ref.at[pl.ds(dest_ref[TOP_K * r + kk], 1)], sem.at[0]).start()
        return c
    lax.fori_loop(0, DISPATCH_TM, issue, 0, unroll=8)
    n_rows = TOP_K * DISPATCH_TM
    pltpu.make_async_copy(xs_ref.at[pl.ds(0, n_rows)], xs_ref.at[pl.ds(0, n_rows)], sem.at[0]).wait()


def _dispatch(zero_blk, n_zero, dest_flat, f_packed, n_blocks):
    rows = f_packed.shape[0]
    return pl.pallas_call(
        _dispatch_kernel,
        grid_spec=pltpu.PrefetchScalarGridSpec(
            num_scalar_prefetch=2,
            grid=(rows // DISPATCH_TM,),
            in_specs=[
                pl.BlockSpec((TOP_K * DISPATCH_TM,), lambda i, z, n: (i,), memory_space=pltpu.SMEM),
                pl.BlockSpec((DISPATCH_TM, HALF), lambda i, z, n: (i, 0)),
            ],
            out_specs=pl.BlockSpec(memory_space=pl.ANY),
            scratch_shapes=[pltpu.VMEM((ROW_BLK, HALF), jnp.uint32),
                            pltpu.SemaphoreType.DMA((1,)),
                            pltpu.SemaphoreType.DMA((1,))],
        ),
        out_shape=jax.ShapeDtypeStruct((n_blocks * ROW_BLK, HALF), jnp.uint32),
        compiler_params=pltpu.CompilerParams(dimension_semantics=("arbitrary",),
                                             vmem_limit_bytes=VMEM_LIMIT),
        name="dispatch",
    )(zero_blk, n_zero, dest_flat, f_packed)


ITEM_BLKS = 8
ITEM_ROWS = ITEM_BLKS * ROW_BLK
F_TILE = 256
N_FT = -(-D_EXPERT // F_TILE)
Y_TILE = 512
N_YT = D_MODEL // Y_TILE


def _f_tile_start(k):
    return pl.multiple_of(jnp.minimum(k * F_TILE, D_EXPERT - F_TILE), LANES)


def _moe_kernel(e_ref, r0_ref, nb_ref, meta_ref, xs_ref, wg_hbm, wu_hbm, wd_hbm, y_ref,
                xraw, xb, hid, ystage, wgf, wuf, wdf, wgu, wdr, xsem, ysem, wsem):
    n_used = meta_ref[0]
    n_live = meta_ref[1]

    def for_blocks(count, fn):
        def body(j, c):
            fn(j)
            return c
        lax.fori_loop(0, count, body, 0)

    def x_copy(item, buf, j):
        src = xs_ref.at[pl.ds((r0_ref[item] + j) * ROW_BLK, ROW_BLK)]
        return pltpu.make_async_copy(src, xraw.at[buf, pl.ds(j * ROW_BLK, ROW_BLK)],
                                     xsem.at[buf, j])

    def w_copies(item, k, buf):
        e = e_ref[item]
        cols = pl.ds(_f_tile_start(k), F_TILE)
        return (pltpu.make_async_copy(wg_hbm.at[e, :, cols], wgf.at[buf], wsem.at[0, buf]),
                pltpu.make_async_copy(wu_hbm.at[e, :, cols], wuf.at[buf], wsem.at[1, buf]),
                pltpu.make_async_copy(wd_hbm.at[e, cols, :], wdf.at[buf], wsem.at[2, buf]))

    def zero_copy(j):
        dst = y_ref.at[pl.ds((n_used + j) * ROW_BLK, ROW_BLK)]
        return pltpu.make_async_copy(ystage.at[pl.ds(0, ROW_BLK)], dst, ysem.at[0])
    n_unused = y_ref.shape[0] // ROW_BLK - n_used
    ystage[0:ROW_BLK, :] = jnp.zeros((ROW_BLK, HALF), jnp.uint32)
    for_blocks(n_unused, lambda j: zero_copy(j).start())
    for_blocks(n_unused, lambda j: zero_copy(j).wait())

    for_blocks(nb_ref[0], lambda j: x_copy(0, 0, j).start())

    @pl.when(nb_ref[0] > 0)
    def _():
        for c in w_copies(0, 0, 0):
            c.start()

    def item_body(i, carry):
        slot = i % 2
        nb = nb_ref[i]
        prev = jnp.maximum(i - 1, 0)
        nb_prev = jnp.where(i > 0, nb_ref[prev], 0)

        def y_copy(j):
            dst = y_ref.at[pl.ds((r0_ref[prev] + j) * ROW_BLK, ROW_BLK)]
            return pltpu.make_async_copy(ystage.at[pl.ds(j * ROW_BLK, ROW_BLK)], dst, ysem.at[0])

        for_blocks(nb_ref[i + 1], lambda j: x_copy(i + 1, 1 - slot, j).start())

        def arrive(j):
            x_copy(i, slot, j).wait()
            rows = pl.ds(pl.multiple_of(j * ROW_BLK, ROW_BLK), ROW_BLK)
            lo, hi = _unpack_bf16_pair(xraw[slot, rows, :])
            xb[rows, 0:HALF] = lo.astype(BF16)
            xb[rows, HALF:] = hi.astype(BF16)
        for_blocks(nb, arrive)

        def tile_body(k, carry):
            wbuf = (i * N_FT + k) % 2
            last = k == N_FT - 1
            nxt_i = jnp.where(last, i + 1, i)
            nxt_k = jnp.where(last, 0, k + 1)

            @pl.when(nb_ref[nxt_i] > 0)
            def _():
                for c in w_copies(nxt_i, nxt_k, 1 - wbuf):
                    c.start()

            def cast_tiles():
                wgu[:, 0:F_TILE] = wgf[wbuf].astype(BF16)
                wgu[:, F_TILE:] = wuf[wbuf].astype(BF16)
                f_rows = pl.ds(_f_tile_start(k), F_TILE)
                for n in range(N_YT):
                    wdr[slot, n, f_rows, :] = wdf[wbuf, :, n * Y_TILE:(n + 1) * Y_TILE].astype(BF16)

            def gate_up(row0, nrows):
                rows = pl.ds(pl.multiple_of(row0, ROW_BLK), nrows)
                gu = jnp.dot(xb[rows, :], wgu[...], preferred_element_type=F32)
                gate, up = gu[:, 0:F_TILE], gu[:, F_TILE:]
                h = (gate * (1.0 / (1.0 + jnp.exp(-gate))) * up).astype(BF16)
                hid[slot, rows, pl.ds(_f_tile_start(k), F_TILE)] = h

            k_y = jnp.minimum(k, N_YT - 1)

            def down(row0, nrows):
                rows = pl.ds(pl.multiple_of(row0, ROW_BLK), nrows)
                yt = jnp.dot(hid[1 - slot, rows, :], wdr[1 - slot, k_y], preferred_element_type=F32)
                packed = _pack_bf16_pair(yt[:, 0:Y_TILE // 2], yt[:, Y_TILE // 2:])
                words = pl.ds(pl.multiple_of(k_y * (Y_TILE // 2), Y_TILE // 2), Y_TILE // 2)
                ystage[rows, words] = packed

            def for_row_tiles(n_blk, fn):
                big = 4 * ROW_BLK
                for_blocks(n_blk // 4, lambda t: fn(t * big, big))
                for bit in (2, 1):
                    @pl.when((n_blk & bit) != 0)
                    def _():
                        fn((n_blk // (2 * bit)) * (2 * bit) * ROW_BLK, bit * ROW_BLK)

            @pl.when(nb > 0)
            def _():
                for c in w_copies(i, k, wbuf):
                    c.wait()
                cast_tiles()

            for_row_tiles(nb, gate_up)
            for_row_tiles(jnp.where(k < N_YT, nb_prev, 0), down)

            @pl.when(k == N_YT)
            def _():
                for_blocks(nb_prev, lambda j: y_copy(j).start())

            @pl.when(last)
            def _():
                for_blocks(nb_prev, lambda j: y_copy(j).wait())
            return carry

        lax.fori_loop(0, N_FT, tile_body, 0)
        return carry

    lax.fori_loop(0, n_live + 1, item_body, 0)


def _moe(item_e, item_r0, item_nb, meta, xs, w_gate, w_up, w_down, n_blocks):
    return pl.pallas_call(
        _moe_kernel,
        grid_spec=pltpu.PrefetchScalarGridSpec(
            num_scalar_prefetch=4,
            grid=(1,),
            in_specs=[pl.BlockSpec(memory_space=pl.ANY)] * 4,
            out_specs=pl.BlockSpec(memory_space=pl.ANY),
            scratch_shapes=[
                pltpu.VMEM((2, ITEM_ROWS, HALF), jnp.uint32),
                pltpu.VMEM((ITEM_ROWS, D_MODEL), BF16),
                pltpu.VMEM((2, ITEM_ROWS, D_EXPERT), BF16),
                pltpu.VMEM((ITEM_ROWS, HALF), jnp.uint32),
                pltpu.VMEM((2, D_MODEL, F_TILE), F32),
                pltpu.VMEM((2, D_MODEL, F_TILE), F32),
                pltpu.VMEM((2, F_TILE, D_MODEL), F32),
                pltpu.VMEM((D_MODEL, 2 * F_TILE), BF16),
                pltpu.VMEM((2, N_YT, D_EXPERT, Y_TILE), BF16),
                pltpu.SemaphoreType.DMA((2, ITEM_BLKS)),
                pltpu.SemaphoreType.DMA((1,)),
                pltpu.SemaphoreType.DMA((3, 2)),
            ],
        ),
        out_shape=jax.ShapeDtypeStruct((n_blocks * ROW_BLK, HALF), jnp.uint32),
        compiler_params=pltpu.CompilerParams(dimension_semantics=("arbitrary",),
                                             vmem_limit_bytes=VMEM_LIMIT),
        name="moe",
    )(item_e, item_r0, item_nb, meta, xs, w_gate, w_up, w_down)


def _combine_kernel(cur_ref, nxt_ref, h_ref, w_ref, g_ref, y_ref, o_ref, ya, yb, sem, *, tm):
    i = pl.program_id(0)
    n = pl.num_programs(0)
    slot = i % 2

    def fetch(idx_ref, buf):
        def issue(r, c):
            pltpu.make_async_copy(y_ref.at[pl.ds(idx_ref[TOP_K * r], 1)], ya.at[buf, pl.ds(r, 1)],
                                  sem.at[buf]).start()
            pltpu.make_async_copy(y_ref.at[pl.ds(idx_ref[TOP_K * r + 1], 1)], yb.at[buf, pl.ds(r, 1)],
                                  sem.at[buf]).start()
            return c
        lax.fori_loop(0, tm, issue, 0, unroll=8)

    @pl.when(i == 0)
    def _():
        fetch(cur_ref, 0)

    @pl.when(i + 1 < n)
    def _():
        fetch(nxt_ref, 1 - slot)

    pltpu.make_async_copy(ya.at[slot], ya.at[slot], sem.at[slot]).wait()
    pltpu.make_async_copy(yb.at[slot], yb.at[slot], sem.at[slot]).wait()

    wa = w_ref[:, 0:1]
    wb = w_ref[:, 1:2]
    half_t = Y_TILE // 2
    ssq = jnp.zeros((tm, 1), F32)
    for n in range(N_YT):
        words = slice(n * half_t, (n + 1) * half_t)
        a_lo, a_hi = _unpack_bf16_pair(ya[slot, :, words])
        b_lo, b_hi = _unpack_bf16_pair(yb[slot, :, words])
        lo_cols = slice(n * Y_TILE, n * Y_TILE + half_t)
        hi_cols = slice(n * Y_TILE + half_t, (n + 1) * Y_TILE)
        lo = h_ref[:, lo_cols] + wa * a_lo + wb * b_lo
        hi = h_ref[:, hi_cols] + wa * a_hi + wb * b_hi
        ssq = ssq + jnp.sum(lo * lo, axis=-1, keepdims=True) + jnp.sum(hi * hi, axis=-1, keepdims=True)
        o_ref[:, lo_cols] = lo
        o_ref[:, hi_cols] = hi
    inv = lax.rsqrt(ssq * (1.0 / D_MODEL) + RMS_EPS)
    o_ref[...] = o_ref[...] * inv * g_ref[...]


def _combine(dest_flat, h2, wts, g, y, tm):
    rows = h2.shape[0]
    steps = rows // tm
    idx_spec = lambda fn: pl.BlockSpec((TOP_K * tm,), fn, memory_space=pltpu.SMEM)
    return pl.pallas_call(
        functools.partial(_combine_kernel, tm=tm),
        grid=(steps,),
        in_specs=[
            idx_spec(lambda i: (i,)),
            idx_spec(lambda i: (jnp.minimum(i + 1, steps - 1),)),
            pl.BlockSpec((tm, D_MODEL), lambda i: (i, 0)),
            pl.BlockSpec((tm, LANES), lambda i: (i, 0)),
            pl.BlockSpec((1, D_MODEL), lambda i: (0, 0)),
            pl.BlockSpec(memory_space=pl.ANY),
        ],
        out_specs=pl.BlockSpec((tm, D_MODEL), lambda i: (i, 0)),
        scratch_shapes=[
            pltpu.VMEM((2, tm, HALF), jnp.uint32),
            pltpu.VMEM((2, tm, HALF), jnp.uint32),
            pltpu.SemaphoreType.DMA((2,)),
        ],
        out_shape=jax.ShapeDtypeStruct((rows, D_MODEL), F32),
        compiler_params=pltpu.CompilerParams(dimension_semantics=("arbitrary",),
                                             vmem_limit_bytes=VMEM_LIMIT),
        name="combine",
    )(dest_flat, dest_flat, h2, wts, g, y)


ROUTE_TM = 512


def _route_kernel(lg_ref, info_ref, wts_ref, tab_ref, counts, starts, carry):
    phase = pl.program_id(0)
    i = pl.program_id(1)
    tm = ROUTE_TM
    lane = lax.broadcasted_iota(jnp.int32, (tm, LANES), 1)
    lg = lg_ref[...]

    def first_lane_of_max(v):
        vmax = jnp.max(v, axis=-1, keepdims=True)
        return vmax, jnp.min(jnp.where(v == vmax, lane, LANES), axis=-1, keepdims=True)

    gl = jnp.where(lane < N_GROUPS, lg, NEG_INF)
    gmax, g_sel = first_lane_of_max(gl)
    gate = 1.0 / jnp.sum(jnp.exp(gl - gmax), axis=-1, keepdims=True)
    lo = N_GROUPS + EXPERTS_PER_GROUP * g_sel
    el = jnp.where((lane >= lo) & (lane < lo + EXPERTS_PER_GROUP), lg, NEG_INF)
    m1, l1 = first_lane_of_max(el)
    m2, l2 = first_lane_of_max(jnp.where(lane == l1, NEG_INF, el))
    ratio = jnp.exp(m2 - m1)
    w1 = gate / (1.0 + ratio)
    w2 = w1 * ratio
    e1 = l1 - N_GROUPS
    e2 = l2 - N_GROUPS
    hit1 = lane == e1
    hit2 = lane == e2
    onehot = (hit1 | hit2).astype(F32)
    tile_counts = jnp.sum(onehot, axis=0, keepdims=True)

    @pl.when((phase == 0) & (i == 0))
    def _():
        counts[...] = jnp.zeros_like(counts)

    @pl.when(phase == 0)
    def _():
        counts[...] += tile_counts

    @pl.when((phase == 1) & (i == 0))
    def _():
        blocks = jnp.floor((counts[...] + (ROW_BLK - 1)) * (1.0 / ROW_BLK))
        r = lax.broadcasted_iota(jnp.int32, (LANES, LANES), 0)
        c = lax.broadcasted_iota(jnp.int32, (LANES, LANES), 1)
        before = (r < c).astype(F32)
        first = jnp.dot(jnp.broadcast_to(blocks, (8, LANES)), before,
                        preferred_element_type=F32, precision=lax.Precision.HIGHEST)[0:1]
        starts[...] = first
        carry[...] = jnp.zeros_like(carry)
        used = jnp.sum(blocks, axis=-1, keepdims=True)
        row = lax.broadcasted_iota(jnp.int32, (8, LANES), 0)
        tab = jnp.where(row == 0, blocks, jnp.where(row == 1, first, jnp.where(row == 2, used, 0.0)))
        tab_ref[...] = tab.astype(jnp.int32)

    @pl.when(phase == 1)
    def _():
        r = lax.broadcasted_iota(jnp.int32, (tm, tm), 0)
        c = lax.broadcasted_iota(jnp.int32, (tm, tm), 1)
        earlier = (c < r).astype(BF16)
        before = jnp.dot(earlier, onehot.astype(BF16), preferred_element_type=F32)
        slot = starts[...] * ROW_BLK + carry[...] + before
        d1 = jnp.sum(jnp.where(hit1, slot, 0.0), axis=-1, keepdims=True)
        d2 = jnp.sum(jnp.where(hit2, slot, 0.0), axis=-1, keepdims=True)
        info_ref[...] = jnp.where(lane == 0, d1, jnp.where(lane == 1, d2, 0.0)).astype(jnp.int32)
        wts_ref[...] = jnp.where(lane == 0, w1, jnp.where(lane == 1, w2, 0.0))
        carry[...] += tile_counts


def _route(logits):
    rows = logits.shape[0]
    tile = lambda p, i: (i * p, 0)
    return pl.pallas_call(
        _route_kernel,
        grid=(2, rows // ROUTE_TM),
        in_specs=[pl.BlockSpec((ROUTE_TM, LANES), lambda p, i: (i, 0))],
        out_specs=(pl.BlockSpec((ROUTE_TM, LANES), tile), pl.BlockSpec((ROUTE_TM, LANES), tile),
                   pl.BlockSpec((8, LANES), lambda p, i: (0, 0))),
        out_shape=(jax.ShapeDtypeStruct((rows, LANES), jnp.int32),
                   jax.ShapeDtypeStruct((rows, LANES), F32),
                   jax.ShapeDtypeStruct((8, LANES), jnp.int32)),
        scratch_shapes=[pltpu.VMEM((1, LANES), F32)] * 3,
        compiler_params=pltpu.CompilerParams(dimension_semantics=("arbitrary", "arbitrary"),
                                             vmem_limit_bytes=VMEM_LIMIT),
        name="route",
    )(logits)


def _work_items(blocks_e, blk_start, n_items):
    items_e = (blocks_e + ITEM_BLKS - 1) // ITEM_BLKS
    item_end = jnp.cumsum(items_e)
    item_start = item_end - items_e
    total = item_end[-1]
    idx = jnp.arange(n_items, dtype=jnp.int32)
    live = idx < total
    ref = jnp.minimum(idx, total - 1)
    e = jnp.searchsorted(item_end, ref, side='right').astype(jnp.int32)
    j = ref - item_start[e]
    r0 = blk_start[e] + ITEM_BLKS * j
    nb = jnp.where(live, jnp.clip(blocks_e[e] - ITEM_BLKS * j, 0, ITEM_BLKS), 0)
    return e, r0.astype(jnp.int32), nb.astype(jnp.int32), total.astype(jnp.int32)


def kernel(x, meta_tokens, attn_norm_g, w_in, swa_sinks, diff_lambda, diff_subln_g, w_out,
           ffn_norm_g, w_router_group, w_router_expert, w_gate, w_up, w_down, final_norm_g):
    batch, seq, _ = x.shape
    n_tok = batch * seq
    nblk = seq // Q_BLOCK
    x2 = x.reshape(n_tok, D_MODEL)

    w_in_bf = w_in[0].astype(BF16)
    g_attn = attn_norm_g[0].reshape(1, D_MODEL)
    cos_x, sin_x = _rope_tables(N_META + jnp.arange(seq))
    cos_m, sin_m = _rope_tables(jnp.maximum(jnp.arange(Q_BLOCK) - PAD_FRONT, 0))
    qs, ks, vs, qd, kd, vd = _inproj(x2, g_attn, w_in_bf, cos_x, sin_x, 512, seq)
    meta_blk = jnp.concatenate(
        [jnp.zeros((PAD_FRONT, D_MODEL), x.dtype), meta_tokens.astype(x.dtype)], axis=0)
    _, ks_m, vs_m, _, kd_m, vd_m = _inproj(meta_blk, g_attn, w_in_bf, cos_m, sin_m, Q_BLOCK, Q_BLOCK)

    o_s = _swa(swa_sinks[0].astype(F32), qs, ks, vs, ks_m, vs_m, batch, nblk)
    o_d = _diff(diff_lambda[0].astype(F32), diff_subln_g[0].astype(F32).reshape(1, LANES),
                qd, kd, vd, kd_m, vd_m, batch, seq)

    w_router = jnp.concatenate(
        [w_router_group[0], w_router_expert[0],
         jnp.zeros((D_MODEL, LANES - N_GROUPS - N_EXPERTS), F32)], axis=1)
    h2, f_packed, logits = _outproj(o_s, o_d, x2, w_out[0].astype(BF16),
                                    ffn_norm_g[0].reshape(1, D_MODEL), w_router, 512)

    info, wts, tab = _route(logits)
    dest_flat = info[:, :TOP_K].reshape(-1)
    blocks_e, blk_start, n_used = tab[0, :N_EXPERTS], tab[1, :N_EXPERTS], tab[2, 0:1]
    n_blocks = n_tok * TOP_K // ROW_BLK + N_EXPERTS
    n_items = n_blocks // ITEM_BLKS + N_EXPERTS
    blk = jnp.arange(n_blocks, dtype=jnp.int32)
    is_last = jnp.any((blk[:, None] + 1 == (blk_start + blocks_e)[None, :])
                      & (blocks_e > 0)[None, :], axis=1)
    need_zero = is_last | (blk >= n_used[0])
    zero_blk = jnp.nonzero(need_zero, size=2 * N_EXPERTS, fill_value=0)[0].astype(jnp.int32)
    n_zero = jnp.sum(need_zero).astype(jnp.int32).reshape(1)
    xs = _dispatch(zero_blk, n_zero, dest_flat, f_packed, n_blocks)
    item_e, item_r0, item_nb, n_live = _work_items(blocks_e, blk_start, n_items)
    y = _moe(item_e, item_r0, item_nb, jnp.concatenate([n_used, n_live.reshape(1)]), xs,
             w_gate[0], w_up[0], w_down[0], n_blocks)

    out = _combine(dest_flat, h2, wts, final_norm_g.reshape(1, D_MODEL), y, 256)
    return out.reshape(batch, seq, D_MODEL)
```

```python
import functools
import math

import jax
import jax.numpy as jnp
from jax import lax
from jax.experimental import pallas as pl
from jax.experimental.pallas import tpu as pltpu

D_MODEL = 2048
N_META = 16
CHUNK = 64
Q_BLOCK = 128
PAD_FRONT = Q_BLOCK - N_META
HEAD_DIM = 64
ROPE_THETA = 10000.0
SWA_Q_HEADS = 16
SWA_KV_HEADS = 2
DIFF_HEADS = 8
SWA_Q_W = SWA_Q_HEADS * HEAD_DIM
SWA_KV_W = SWA_KV_HEADS * HEAD_DIM
DIFF_W = DIFF_HEADS * 2 * HEAD_DIM
IN_WIDTH = SWA_Q_W + 2 * SWA_KV_W + 3 * DIFF_W
N_GROUPS = 8
EXPERTS_PER_GROUP = 8
N_EXPERTS = N_GROUPS * EXPERTS_PER_GROUP
TOP_K = 2
D_EXPERT = 1408
RMS_EPS = 1e-6
NEG_INF = -1e30
LAM_INIT = 0.8 - 0.6 * math.exp(-0.3 * 0)

LANES = 128
ROW_BLK = 128
VMEM_LIMIT = 56 * 1024 * 1024

BF16 = jnp.bfloat16
F32 = jnp.float32


def _nt_dot(a, b):
    return lax.dot_general(a, b, (((1,), (1,)), ((), ())), preferred_element_type=F32)


LOG2_E = math.log2(math.e)
Q_SCALE = HEAD_DIM ** -0.5 * LOG2_E
_SEGS = (
    ("qs", 0, 512, True, Q_SCALE), ("qs", 512, 512, True, Q_SCALE),
    ("ks", 1024, 128, True, 1.0), ("vs", 1152, 128, False, 1.0),
    ("qd", 1280, 512, True, Q_SCALE), ("qd", 1792, 512, True, Q_SCALE),
    ("kd", 2304, 512, True, 1.0), ("kd", 2816, 512, True, 1.0),
    ("vd", 3328, 512, False, 1.0), ("vd", 3840, 512, False, 1.0),
)
_SEG_BASE = {"qs": 0, "ks": 1024, "vs": 1152, "qd": 1280, "kd": 2304, "vd": 3328}


def _inproj_kernel(x_ref, g_ref, w_ref, cos_ref, sin_ref,
                   qs_ref, ks_ref, vs_ref, qd_ref, kd_ref, vd_ref):
    x = x_ref[...]
    ms = jnp.mean(x * x, axis=-1, keepdims=True)
    a = (x * lax.rsqrt(ms + RMS_EPS) * g_ref[...]).astype(BF16)
    cos = cos_ref[...]
    sin = sin_ref[...]
    lane = lax.broadcasted_iota(jnp.int32, cos.shape, 1)
    first_half = (lane % HEAD_DIM) < (HEAD_DIM // 2)
    low_head = lane < HEAD_DIM
    outs = {"qs": qs_ref, "ks": ks_ref, "vs": vs_ref, "qd": qd_ref, "kd": kd_ref, "vd": vd_ref}
    for name, c0, width, rotary, scale in _SEGS:
        y = jnp.dot(a, w_ref[:, c0:c0 + width], preferred_element_type=F32)
        o0 = c0 - _SEG_BASE[name]
        for t in range(width // LANES):
            yt = y[:, t * LANES:(t + 1) * LANES]
            if rotary:
                partner = jnp.where(first_half, pltpu.roll(yt, LANES - HEAD_DIM // 2, 1),
                                    pltpu.roll(yt, HEAD_DIM // 2, 1))
                yt = yt * cos + partner * sin
            if scale != 1.0:
                yt = yt * scale
            if name in ("ks", "vs"):
                other = pltpu.roll(yt, HEAD_DIM, 1)
                outs[name][:, 0:LANES] = jnp.where(low_head, yt, other).astype(BF16)
                outs[name][:, LANES:2 * LANES] = jnp.where(low_head, other, yt).astype(BF16)
            else:
                outs[name][:, o0 + t * LANES:o0 + (t + 1) * LANES] = yt.astype(BF16)


def _inproj(x2, g, w_bf, cos_t, sin_t, tm, rows_per_seq):
    rows = x2.shape[0]
    nseq = rows_per_seq // tm
    row_spec = lambda width: pl.BlockSpec((tm, width), lambda i: (i, 0))
    out_shapes = (
        jax.ShapeDtypeStruct((rows, SWA_Q_W), BF16), jax.ShapeDtypeStruct((rows, 2 * SWA_KV_W), BF16),
        jax.ShapeDtypeStruct((rows, 2 * SWA_KV_W), BF16), jax.ShapeDtypeStruct((rows, DIFF_W), BF16),
        jax.ShapeDtypeStruct((rows, DIFF_W), BF16), jax.ShapeDtypeStruct((rows, DIFF_W), BF16),
    )
    return pl.pallas_call(
        _inproj_kernel,
        grid=(rows // tm,),
        in_specs=[
            row_spec(D_MODEL),
            pl.BlockSpec((1, D_MODEL), lambda i: (0, 0)),
            pl.BlockSpec((D_MODEL, IN_WIDTH), lambda i: (0, 0), pipeline_mode=pl.Buffered(1)),
            pl.BlockSpec((tm, LANES), lambda i: (i % nseq, 0)),
            pl.BlockSpec((tm, LANES), lambda i: (i % nseq, 0)),
        ],
        out_specs=(row_spec(SWA_Q_W), row_spec(2 * SWA_KV_W), row_spec(2 * SWA_KV_W),
                   row_spec(DIFF_W), row_spec(DIFF_W), row_spec(DIFF_W)),
        out_shape=out_shapes,
        compiler_params=pltpu.CompilerParams(dimension_semantics=("arbitrary",),
                                             vmem_limit_bytes=VMEM_LIMIT),
        name="inproj",
    )(x2, g, w_bf, cos_t, sin_t)


def _rope_tables(positions):
    inv = 1.0 / (ROPE_THETA ** (jnp.arange(0, HEAD_DIM, 2, dtype=F32) / HEAD_DIM))
    ang = positions.astype(F32)[:, None] * inv[None, :]
    cos, sin = jnp.cos(ang), jnp.sin(ang)
    cos_t = jnp.tile(cos, (1, 2 * LANES // HEAD_DIM))
    sin_t = jnp.tile(jnp.concatenate([-sin, sin], axis=1), (1, LANES // HEAD_DIM))
    return cos_t, sin_t


SWA_SLAB = 256
SWA_BAND = 3 * CHUNK + N_META


def _swa_kernel(sink_ref, q_ref, kp_ref, kc_ref, km_ref, vp_ref, vc_ref, vm_ref, o_ref):
    i = pl.program_id(1)
    pairs = SWA_Q_HEADS // SWA_KV_HEADS // 2
    m_rows = pairs * 2 * CHUNK
    col = lax.broadcasted_iota(jnp.int32, (m_rows, SWA_SLAB), 1)
    row = lax.broadcasted_iota(jnp.int32, (m_rows, 1), 0)
    lane = lax.broadcasted_iota(jnp.int32, (CHUNK, LANES), 1)
    low_head = lane < HEAD_DIM
    zero = jnp.zeros((CHUNK, LANES), BF16)
    pad = jnp.zeros((SWA_SLAB - SWA_BAND, LANES), BF16)
    meta_rows = slice(PAD_FRONT, Q_BLOCK)
    for g in range(SWA_KV_HEADS):
        gl = slice(g * LANES, (g + 1) * LANES)
        sink = jnp.zeros((m_rows, 1), F32)
        for h in range(2 * pairs):
            sink = jnp.where(row // CHUNK == h, sink_ref[2 * g * pairs + h], sink)
        sink = sink * LOG2_E
        for half in range(2):
            q_rows = slice(half * CHUNK, (half + 1) * CHUNK)
            if half == 0:
                pieces = lambda p, c, m: [p[:, gl], c[0:CHUNK, gl], m[meta_rows, gl], pad]
                n_prev = 2 * CHUNK
            else:
                pieces = lambda p, c, m: [p[CHUNK:, gl], c[:, gl], m[meta_rows, gl], pad]
                n_prev = CHUNK
            k_slab = jnp.concatenate(pieces(kp_ref, kc_ref, km_ref), axis=0)
            v_slab = jnp.concatenate(pieces(vp_ref, vc_ref, vm_ref), axis=0)
            valid = (col < SWA_BAND) & ((i > 0) | (col >= n_prev))
            q2 = []
            for c in range(g * pairs, (g + 1) * pairs):
                qp = q_ref[q_rows, c * LANES:(c + 1) * LANES]
                q2 += [jnp.where(low_head, qp, zero), jnp.where(low_head, zero, qp)]
            s = jnp.where(valid, _nt_dot(jnp.concatenate(q2, axis=0), k_slab), NEG_INF)
            m = jnp.maximum(jnp.max(s, axis=-1, keepdims=True), sink)
            p = jnp.exp2(s - m)
            denom = jnp.sum(p, axis=-1, keepdims=True) + jnp.exp2(sink - m)
            o2 = jnp.dot(p.astype(BF16), v_slab, preferred_element_type=F32) / denom
            for j, c in enumerate(range(g * pairs, (g + 1) * pairs)):
                lo = o2[(2 * j) * CHUNK:(2 * j + 1) * CHUNK]
                hi = o2[(2 * j + 1) * CHUNK:(2 * j + 2) * CHUNK]
                o_ref[q_rows, c * LANES:(c + 1) * LANES] = jnp.where(low_head, lo, hi).astype(BF16)


def _swa(sinks, qs, ks, vs, ks_meta, vs_meta, batch, nblk):
    rows = qs.shape[0]
    cur = lambda b, i: (b * nblk + i, 0)
    prev = lambda b, i: (b * nblk + jnp.maximum(i - 1, 0), 0)
    kv_w = 2 * SWA_KV_W
    return pl.pallas_call(
        _swa_kernel,
        grid=(batch, nblk),
        in_specs=[
            pl.BlockSpec(memory_space=pltpu.SMEM),
            pl.BlockSpec((Q_BLOCK, SWA_Q_W), cur),
            pl.BlockSpec((Q_BLOCK, kv_w), prev), pl.BlockSpec((Q_BLOCK, kv_w), cur),
            pl.BlockSpec((Q_BLOCK, kv_w), lambda b, i: (0, 0)),
            pl.BlockSpec((Q_BLOCK, kv_w), prev), pl.BlockSpec((Q_BLOCK, kv_w), cur),
            pl.BlockSpec((Q_BLOCK, kv_w), lambda b, i: (0, 0)),
        ],
        out_specs=pl.BlockSpec((Q_BLOCK, SWA_Q_W), cur),
        out_shape=jax.ShapeDtypeStruct((rows, SWA_Q_W), BF16),
        compiler_params=pltpu.CompilerParams(dimension_semantics=("arbitrary", "arbitrary"),
                                             vmem_limit_bytes=VMEM_LIMIT),
        name="swa",
    )(sinks, qs, ks, ks, ks_meta, vs, vs, vs_meta)


DIFF_TQ = 256
DIFF_TK = 512
DIFF_HEADS_PER_STEP = 2


def _diff_kernel(lam_ref, g_ref, q_ref, k_ref, v_ref, km_ref, vm_ref, o_ref, *, seq):
    dl = lam_ref[...]
    lam = (jnp.exp(jnp.sum(dl[0:1] * dl[1:2], axis=-1, keepdims=True))
           - jnp.exp(jnp.sum(dl[2:3] * dl[3:4], axis=-1, keepdims=True)) + LAM_INIT)
    lane = lax.broadcasted_iota(jnp.int32, (DIFF_TQ, LANES), 1)
    low_map = lane < HEAD_DIM
    zero = jnp.zeros((DIFF_TQ, LANES), BF16)

    def last_mask(n_keys):
        shape = (2 * DIFF_TQ, Q_BLOCK + n_keys)
        col = lax.broadcasted_iota(jnp.int32, shape, 1)
        q_chunk = (lax.broadcasted_iota(jnp.int32, shape, 0) % DIFF_TQ) // CHUNK
        first_chunk = (DIFF_TQ - n_keys) // CHUNK
        k_chunk = (col - Q_BLOCK) // CHUNK + first_chunk
        return ((col >= PAD_FRONT) & (col < Q_BLOCK)) | ((col >= Q_BLOCK) & (k_chunk <= q_chunk))

    def step(q2, k_t, v_t, mask, state):
        s = _nt_dot(q2, k_t)
        if mask is not None:
            s = jnp.where(mask, s, NEG_INF)
        smax = jnp.max(s, axis=-1, keepdims=True)
        if state is None:
            p = jnp.exp2(s - smax)
            return smax, jnp.sum(p, axis=-1, keepdims=True), jnp.dot(
                p.astype(BF16), v_t, preferred_element_type=F32)
        m_old, l_old, acc = state
        m_new = jnp.maximum(m_old, smax)
        alpha = jnp.exp2(m_old - m_new)
        p = jnp.exp2(s - m_new)
        return (m_new, alpha * l_old + jnp.sum(p, axis=-1, keepdims=True),
                alpha * acc + jnp.dot(p.astype(BF16), v_t, preferred_element_type=F32))

    heads = [slice(h * LANES, (h + 1) * LANES) for h in range(DIFF_HEADS_PER_STEP)]

    tiles_per_k = DIFF_TK // DIFF_TQ
    masks = [last_mask((t + 1) * DIFF_TQ) for t in range(tiles_per_k)]

    def q_tile(n_full, t):
        if n_full is None:
            aligned = lambda v: v
            q0 = t * DIFF_TQ
        else:
            aligned = lambda v: pl.multiple_of(v, DIFF_TQ)
            q0 = aligned((n_full * tiles_per_k + t) * DIFF_TQ)
        q2s = []
        for hs in heads:
            q = q_ref[pl.ds(q0, DIFF_TQ), hs]
            q2s.append(jnp.concatenate(
                [jnp.where(low_map, q, zero), jnp.where(low_map, zero, q)], axis=0))

        def full_tiles(kj, states):
            rows = pl.ds(pl.multiple_of(kj * DIFF_TK, DIFF_TK), DIFF_TK)
            out = ()
            for h, hs in enumerate(heads):
                st = None if states is None else states[3 * h:3 * h + 3]
                out = out + step(q2s[h], k_ref[rows, hs], v_ref[rows, hs], None, st)
            return out

        if n_full is None:
            states = None
        else:
            states = lax.fori_loop(1, n_full, full_tiles, full_tiles(0, None))

        n_keys = (t + 1) * DIFF_TQ
        rows = pl.ds(aligned(q0 + DIFF_TQ - n_keys), n_keys)
        for h, hs in enumerate(heads):
            st = None if states is None else states[3 * h:3 * h + 3]
            k_t = jnp.concatenate([km_ref[:, hs], k_ref[rows, hs]], axis=0)
            v_t = jnp.concatenate([vm_ref[:, hs], v_ref[rows, hs]], axis=0)
            _, l, acc = step(q2s[h], k_t, v_t, masks[t], st)
            o = acc / l
            o = o[:DIFF_TQ] - lam * o[DIFF_TQ:]
            o = o * lax.rsqrt(jnp.mean(o * o, axis=-1, keepdims=True) + RMS_EPS)
            o_ref[pl.ds(q0, DIFF_TQ), hs] = (o * g_ref[...] * (1.0 - LAM_INIT)).astype(BF16)

    def q_group(n_full, carry):
        for t in range(tiles_per_k):
            q_tile(n_full, t)
        return carry

    q_group(None, 0)
    lax.fori_loop(1, seq // DIFF_TK, q_group, 0)


def _diff(diff_lambda, subln_g, qd, kd, vd, kd_meta, vd_meta, batch, seq):
    rows = qd.shape[0]
    head = lambda b, h: (b, h)
    meta = lambda b, h: (0, h)
    width = DIFF_HEADS_PER_STEP * LANES
    return pl.pallas_call(
        functools.partial(_diff_kernel, seq=seq),
        grid=(batch, DIFF_HEADS // DIFF_HEADS_PER_STEP),
        in_specs=[
            pl.BlockSpec((4, HEAD_DIM), lambda b, h: (0, 0)),
            pl.BlockSpec((1, LANES), lambda b, h: (0, 0)),
            pl.BlockSpec((seq, width), head), pl.BlockSpec((seq, width), head),
            pl.BlockSpec((seq, width), head),
            pl.BlockSpec((Q_BLOCK, width), meta), pl.BlockSpec((Q_BLOCK, width), meta),
        ],
        out_specs=pl.BlockSpec((seq, width), head),
        out_shape=jax.ShapeDtypeStruct((rows, DIFF_W), BF16),
        compiler_params=pltpu.CompilerParams(dimension_semantics=("arbitrary", "arbitrary"),
                                             vmem_limit_bytes=VMEM_LIMIT),
        name="diffattn",
    )(diff_lambda, subln_g, qd, kd, vd, kd_meta, vd_meta)


HALF = D_MODEL // 2


def _pack_bf16_pair(a, b):
    hi = pltpu.bitcast(a.astype(BF16).astype(F32), jnp.uint32)
    lo = pltpu.bitcast(b.astype(BF16).astype(F32), jnp.uint32)
    return hi | (lo >> 16)


def _unpack_bf16_pair(w):
    a = pltpu.bitcast(w & jnp.uint32(0xFFFF0000), F32)
    b = pltpu.bitcast(w << 16, F32)
    return a, b


OUT_CHUNK = 512


def _outproj_kernel(os_ref, od_ref, x_ref, w_ref, g_ref, wr_ref, h_ref, fp_ref, lg_ref):
    tm = x_ref.shape[0]
    ssq = jnp.zeros((tm, 1), F32)
    for c in range(D_MODEL // OUT_CHUNK):
        cols = slice(c * OUT_CHUNK, (c + 1) * OUT_CHUNK)
        y = jnp.dot(os_ref[...], w_ref[0:SWA_Q_W, cols], preferred_element_type=F32)
        y = y + jnp.dot(od_ref[...], w_ref[SWA_Q_W:, cols], preferred_element_type=F32)
        h = x_ref[:, cols] + y
        h_ref[:, cols] = h
        ssq = ssq + jnp.sum(h * h, axis=-1, keepdims=True)
    inv = lax.rsqrt(ssq * (1.0 / D_MODEL) + RMS_EPS)
    lg = jnp.zeros((tm, LANES), F32)
    for c in range(HALF // OUT_CHUNK):
        ca = slice(c * OUT_CHUNK, (c + 1) * OUT_CHUNK)
        cb = slice(HALF + c * OUT_CHUNK, HALF + (c + 1) * OUT_CHUNK)
        fa = h_ref[:, ca] * inv * g_ref[:, ca]
        fb = h_ref[:, cb] * inv * g_ref[:, cb]
        fp_ref[:, ca] = _pack_bf16_pair(fa, fb)
        for f, cc in ((fa, ca), (fb, cb)):
            f_hi = f.astype(BF16)
            f_lo = (f - f_hi.astype(F32)).astype(BF16)
            w = wr_ref[cc, :]
            w_hi = w.astype(BF16)
            w_lo = (w - w_hi.astype(F32)).astype(BF16)
            p = jnp.dot(f_hi, jnp.concatenate([w_hi, w_lo], axis=1), preferred_element_type=F32)
            lg = lg + p[:, 0:LANES] + p[:, LANES:] + jnp.dot(f_lo, w_hi, preferred_element_type=F32)
    lg_ref[...] = lg


def _outproj(o_s, o_d, x2, w_bf, g, w_router, tm):
    rows = x2.shape[0]
    row_spec = lambda width: pl.BlockSpec((tm, width), lambda i: (i, 0))
    return pl.pallas_call(
        _outproj_kernel,
        grid=(rows // tm,),
        in_specs=[
            row_spec(SWA_Q_W), row_spec(DIFF_W), row_spec(D_MODEL),
            pl.BlockSpec((D_MODEL, D_MODEL), lambda i: (0, 0), pipeline_mode=pl.Buffered(1)),
            pl.BlockSpec((1, D_MODEL), lambda i: (0, 0)),
            pl.BlockSpec((D_MODEL, LANES), lambda i: (0, 0)),
        ],
        out_specs=(row_spec(D_MODEL), row_spec(HALF), row_spec(LANES)),
        out_shape=(jax.ShapeDtypeStruct((rows, D_MODEL), F32),
                   jax.ShapeDtypeStruct((rows, HALF), jnp.uint32),
                   jax.ShapeDtypeStruct((rows, LANES), F32)),
        compiler_params=pltpu.CompilerParams(dimension_semantics=("arbitrary",),
                                             vmem_limit_bytes=VMEM_LIMIT),
        name="outproj",
    )(o_s, o_d, x2, w_bf, g, w_router)


DISPATCH_TM = 1024


def _dispatch_kernel(zero_blk_ref, n_zero_ref, dest_ref, f_ref, xs_ref, zeros, sem, zsem):
    i = pl.program_id(0)

    @pl.when(i == 0)
    def _():
        zeros[...] = jnp.zeros_like(zeros)

        def zero_copy(j):
            rows = pl.ds(zero_blk_ref[j] * ROW_BLK, ROW_BLK)
            return pltpu.make_async_copy(zeros, xs_ref.at[rows], zsem.at[0])

        def start(j, c):
            zero_copy(j).start()
            return c

        def wait(j, c):
            zero_copy(j).wait()
            return c
        lax.fori_loop(0, n_zero_ref[0], start, 0)
        lax.fori_loop(0, n_zero_ref[0], wait, 0)

    def issue(r, c):
        for kk in range(TOP_K):
            pltpu.make_async_copy(f_ref.at[pl.ds(r, 1)],
                                  xs_ref.at[pl.ds(dest_ref[TOP_K * r + kk], 1)], sem.at[0]).start()
        return c
    lax.fori_loop(0, DISPATCH_TM, issue, 0, unroll=8)
    n_rows = TOP_K * DISPATCH_TM
    pltpu.make_async_copy(xs_ref.at[pl.ds(0, n_rows)], xs_ref.at[pl.ds(0, n_rows)], sem.at[0]).wait()


def _dispatch(zero_blk, n_zero, dest_flat, f_packed, n_blocks):
    rows = f_packed.shape[0]
    return pl.pallas_call(
        _dispatch_kernel,
        grid_spec=pltpu.PrefetchScalarGridSpec(
            num_scalar_prefetch=2,
            grid=(rows // DISPATCH_TM,),
            in_specs=[
                pl.BlockSpec((TOP_K * DISPATCH_TM,), lambda i, z, n: (i,), memory_space=pltpu.SMEM),
                pl.BlockSpec((DISPATCH_TM, HALF), lambda i, z, n: (i, 0)),
            ],
            out_specs=pl.BlockSpec(memory_space=pl.ANY),
            scratch_shapes=[pltpu.VMEM((ROW_BLK, HALF), jnp.uint32),
                            pltpu.SemaphoreType.DMA((1,)),
                            pltpu.SemaphoreType.DMA((1,))],
        ),
        out_shape=jax.ShapeDtypeStruct((n_blocks * ROW_BLK, HALF), jnp.uint32),
        compiler_params=pltpu.CompilerParams(dimension_semantics=("arbitrary",),
                                             vmem_limit_bytes=VMEM_LIMIT),
        name="dispatch",
    )(zero_blk, n_zero, dest_flat, f_packed)


ITEM_BLKS = 6
W_BUFS = 3
ITEM_ROWS = ITEM_BLKS * ROW_BLK
F_TILE = 256
N_FT = -(-D_EXPERT // F_TILE)
Y_TILE = 512
N_YT = D_MODEL // Y_TILE


def _f_tile_start(k):
    return pl.multiple_of(jnp.minimum(k * F_TILE, D_EXPERT - F_TILE), LANES)


def _moe_kernel(e_ref, r0_ref, nb_ref, meta_ref, xs_ref, wg_hbm, wu_hbm, wd_hbm, y_ref,
                xraw, xb, hid, ystage, wgf, wuf, wdf, wgu, wdr, xsem, ysem, wsem):
    n_used = meta_ref[0]
    n_live = meta_ref[1]

    def for_blocks(count, fn):
        def body(j, c):
            fn(j)
            return c
        lax.fori_loop(0, count, body, 0)

    def x_copy(item, buf, j):
        src = xs_ref.at[pl.ds((r0_ref[item] + j) * ROW_BLK, ROW_BLK)]
        return pltpu.make_async_copy(src, xraw.at[buf, pl.ds(j * ROW_BLK, ROW_BLK)],
                                     xsem.at[buf, j])

    def w_copies(item, k, buf):
        e = e_ref[item]
        cols = pl.ds(_f_tile_start(k), F_TILE)
        return (pltpu.make_async_copy(wg_hbm.at[e, :, cols], wgf.at[buf], wsem.at[0, buf]),
                pltpu.make_async_copy(wu_hbm.at[e, :, cols], wuf.at[buf], wsem.at[1, buf]),
                pltpu.make_async_copy(wd_hbm.at[e, cols, :], wdf.at[buf], wsem.at[2, buf]))

    def zero_copy(j):
        dst = y_ref.at[pl.ds((n_used + j) * ROW_BLK, ROW_BLK)]
        return pltpu.make_async_copy(ystage.at[pl.ds(0, ROW_BLK)], dst, ysem.at[0])
    n_unused = y_ref.shape[0] // ROW_BLK - n_used
    ystage[0:ROW_BLK, :] = jnp.zeros((ROW_BLK, HALF), jnp.uint32)
    for_blocks(n_unused, lambda j: zero_copy(j).start())
    for_blocks(n_unused, lambda j: zero_copy(j).wait())

    for_blocks(nb_ref[0], lambda j: x_copy(0, 0, j).start())

    @pl.when(nb_ref[0] > 0)
    def _():
        for s in range(W_BUFS - 1):
            for c in w_copies(0, s, s):
                c.start()

    def item_body(i, carry):
        slot = i % 2
        nb = nb_ref[i]
        prev = jnp.maximum(i - 1, 0)
        nb_prev = jnp.where(i > 0, nb_ref[prev], 0)

        def y_copy(j):
            dst = y_ref.at[pl.ds((r0_ref[prev] + j) * ROW_BLK, ROW_BLK)]
            return pltpu.make_async_copy(ystage.at[pl.ds(j * ROW_BLK, ROW_BLK)], dst, ysem.at[0])

        for_blocks(nb_ref[i + 1], lambda j: x_copy(i + 1, 1 - slot, j).start())

        def arrive(j):
            x_copy(i, slot, j).wait()
            rows = pl.ds(pl.multiple_of(j * ROW_BLK, ROW_BLK), ROW_BLK)
            lo, hi = _unpack_bf16_pair(xraw[slot, rows, :])
            xb[rows, 0:HALF] = lo.astype(BF16)
            xb[rows, HALF:] = hi.astype(BF16)
        for_blocks(nb, arrive)

        def tile_body(k, carry):
            step = i * N_FT + k
            wbuf = step % W_BUFS
            last = k == N_FT - 1
            ahead = k + (W_BUFS - 1)
            nxt_i = jnp.where(ahead >= N_FT, i + 1, i)
            nxt_k = jnp.where(ahead >= N_FT, ahead - N_FT, ahead)

            @pl.when(nb_ref[nxt_i] > 0)
            def _():
                for c in w_copies(nxt_i, nxt_k, (step + (W_BUFS - 1)) % W_BUFS):
                    c.start()

            def cast_tiles():
                wgu[:, 0:F_TILE] = wgf[wbuf].astype(BF16)
                wgu[:, F_TILE:] = wuf[wbuf].astype(BF16)
                f_rows = pl.ds(_f_tile_start(k), F_TILE)
                for n in range(N_YT):
                    wdr[slot, n, f_rows, :] = wdf[wbuf, :, n * Y_TILE:(n + 1) * Y_TILE].astype(BF16)

            def gate_up(row0, nrows):
                rows = pl.ds(pl.multiple_of(row0, ROW_BLK), nrows)
                gu = jnp.dot(xb[rows, :], wgu[...], preferred_element_type=F32)
                gate, up = gu[:, 0:F_TILE], gu[:, F_TILE:]
                h = (gate * (1.0 / (1.0 + jnp.exp(-gate))) * up).astype(BF16)
                hid[slot, rows, pl.ds(_f_tile_start(k), F_TILE)] = h

            k_y = jnp.minimum(k, N_YT - 1)

            def down(row0, nrows):
                rows = pl.ds(pl.multiple_of(row0, ROW_BLK), nrows)
                yt = jnp.dot(hid[1 - slot, rows, :], wdr[1 - slot, k_y], preferred_element_type=F32)
                packed = _pack_bf16_pair(yt[:, 0:Y_TILE // 2], yt[:, Y_TILE // 2:])
                words = pl.ds(pl.multiple_of(k_y * (Y_TILE // 2), Y_TILE // 2), Y_TILE // 2)
                ystage[rows, words] = packed

            def for_row_tiles(n_blk, fn):
                big = 4 * ROW_BLK
                for_blocks(n_blk // 4, lambda t: fn(t * big, big))
                for bit in (2, 1):
                    @pl.when((n_blk & bit) != 0)
                    def _():
                        fn((n_blk // (2 * bit)) * (2 * bit) * ROW_BLK, bit * ROW_BLK)

            @pl.when(nb > 0)
            def _():
                for c in w_copies(i, k, wbuf):
                    c.wait()
                cast_tiles()

            for_row_tiles(nb, gate_up)
            for_row_tiles(jnp.where(k < N_YT, nb_prev, 0), down)

            @pl.when(k == N_YT)
            def _():
                for_blocks(nb_prev, lambda j: y_copy(j).start())

            @pl.when(last)
            def _():
                for_blocks(nb_prev, lambda j: y_copy(j).wait())
            return carry

        lax.fori_loop(0, N_FT, tile_body, 0)
        return carry

    lax.fori_loop(0, n_live + 1, item_body, 0)


def _moe(item_e, item_r0, item_nb, meta, xs, w_gate, w_up, w_down, n_blocks):
    return pl.pallas_call(
        _moe_kernel,
        grid_spec=pltpu.PrefetchScalarGridSpec(
            num_scalar_prefetch=4,
            grid=(1,),
            in_specs=[pl.BlockSpec(memory_space=pl.ANY)] * 4,
            out_specs=pl.BlockSpec(memory_space=pl.ANY),
            scratch_shapes=[
                pltpu.VMEM((2, ITEM_ROWS, HALF), jnp.uint32),
                pltpu.VMEM((ITEM_ROWS, D_MODEL), BF16),
                pltpu.VMEM((2, ITEM_ROWS, D_EXPERT), BF16),
                pltpu.VMEM((ITEM_ROWS, HALF), jnp.uint32),
                pltpu.VMEM((W_BUFS, D_MODEL, F_TILE), F32),
                pltpu.VMEM((W_BUFS, D_MODEL, F_TILE), F32),
                pltpu.VMEM((W_BUFS, F_TILE, D_MODEL), F32),
                pltpu.VMEM((D_MODEL, 2 * F_TILE), BF16),
                pltpu.VMEM((2, N_YT, D_EXPERT, Y_TILE), BF16),
                pltpu.SemaphoreType.DMA((2, ITEM_BLKS)),
                pltpu.SemaphoreType.DMA((1,)),
                pltpu.SemaphoreType.DMA((3, W_BUFS)),
            ],
        ),
        out_shape=jax.ShapeDtypeStruct((n_blocks * ROW_BLK, HALF), jnp.uint32),
        compiler_params=pltpu.CompilerParams(dimension_semantics=("arbitrary",),
                                             vmem_limit_bytes=VMEM_LIMIT),
        name="moe",
    )(item_e, item_r0, item_nb, meta, xs, w_gate, w_up, w_down)


def _combine_kernel(cur_ref, nxt_ref, h_ref, w_ref, g_ref, y_ref, o_ref, ya, yb, sem, *, tm):
    i = pl.program_id(0)
    n = pl.num_programs(0)
    slot = i % 2

    def fetch(idx_ref, buf):
        def issue(r, c):
            pltpu.make_async_copy(y_ref.at[pl.ds(idx_ref[TOP_K * r], 1)], ya.at[buf, pl.ds(r, 1)],
                                  sem.at[buf]).start()
            pltpu.make_async_copy(y_ref.at[pl.ds(idx_ref[TOP_K * r + 1], 1)], yb.at[buf, pl.ds(r, 1)],
                                  sem.at[buf]).start()
            return c
        lax.fori_loop(0, tm, issue, 0, unroll=8)

    @pl.when(i == 0)
    def _():
        fetch(cur_ref, 0)

    @pl.when(i + 1 < n)
    def _():
        fetch(nxt_ref, 1 - slot)

    pltpu.make_async_copy(ya.at[slot], ya.at[slot], sem.at[slot]).wait()
    pltpu.make_async_copy(yb.at[slot], yb.at[slot], sem.at[slot]).wait()

    wa = w_ref[:, 0:1]
    wb = w_ref[:, 1:2]
    half_t = Y_TILE // 2
    ssq = jnp.zeros((tm, 1), F32)
    for n in range(N_YT):
        words = slice(n * half_t, (n + 1) * half_t)
        a_lo, a_hi = _unpack_bf16_pair(ya[slot, :, words])
        b_lo, b_hi = _unpack_bf16_pair(yb[slot, :, words])
        lo_cols = slice(n * Y_TILE, n * Y_TILE + half_t)
        hi_cols = slice(n * Y_TILE + half_t, (n + 1) * Y_TILE)
        lo = h_ref[:, lo_cols] + wa * a_lo + wb * b_lo
        hi = h_ref[:, hi_cols] + wa * a_hi + wb * b_hi
        ssq = ssq + jnp.sum(lo * lo, axis=-1, keepdims=True) + jnp.sum(hi * hi, axis=-1, keepdims=True)
        o_ref[:, lo_cols] = lo
        o_ref[:, hi_cols] = hi
    inv = lax.rsqrt(ssq * (1.0 / D_MODEL) + RMS_EPS)
    o_ref[...] = o_ref[...] * inv * g_ref[...]


def _combine(dest_flat, h2, wts, g, y, tm):
    rows = h2.shape[0]
    steps = rows // tm
    idx_spec = lambda fn: pl.BlockSpec((TOP_K * tm,), fn, memory_space=pltpu.SMEM)
    return pl.pallas_call(
        functools.partial(_combine_kernel, tm=tm),
        grid=(steps,),
        in_specs=[
            idx_spec(lambda i: (i,)),
            idx_spec(lambda i: (jnp.minimum(i + 1, steps - 1),)),
            pl.BlockSpec((tm, D_MODEL), lambda i: (i, 0)),
            pl.BlockSpec((tm, LANES), lambda i: (i, 0)),
            pl.BlockSpec((1, D_MODEL), lambda i: (0, 0)),
            pl.BlockSpec(memory_space=pl.ANY),
        ],
        out_specs=pl.BlockSpec((tm, D_MODEL), lambda i: (i, 0)),
        scratch_shapes=[
            pltpu.VMEM((2, tm, HALF), jnp.uint32),
            pltpu.VMEM((2, tm, HALF), jnp.uint32),
            pltpu.SemaphoreType.DMA((2,)),
        ],
        out_shape=jax.ShapeDtypeStruct((rows, D_MODEL), F32),
        compiler_params=pltpu.CompilerParams(dimension_semantics=("arbitrary",),
                                             vmem_limit_bytes=VMEM_LIMIT),
        name="combine",
    )(dest_flat, dest_flat, h2, wts, g, y)


ROUTE_TM = 512


def _route_kernel(lg_ref, info_ref, wts_ref, tab_ref, counts, starts, carry):
    phase = pl.program_id(0)
    i = pl.program_id(1)
    tm = ROUTE_TM
    lane = lax.broadcasted_iota(jnp.int32, (tm, LANES), 1)
    lg = lg_ref[...]

    def first_lane_of_max(v):
        vmax = jnp.max(v, axis=-1, keepdims=True)
        return vmax, jnp.min(jnp.where(v == vmax, lane, LANES), axis=-1, keepdims=True)

    gl = jnp.where(lane < N_GROUPS, lg, NEG_INF)
    gmax, g_sel = first_lane_of_max(gl)
    gate = 1.0 / jnp.sum(jnp.exp(gl - gmax), axis=-1, keepdims=True)
    lo = N_GROUPS + EXPERTS_PER_GROUP * g_sel
    el = jnp.where((lane >= lo) & (lane < lo + EXPERTS_PER_GROUP), lg, NEG_INF)
    m1, l1 = first_lane_of_max(el)
    m2, l2 = first_lane_of_max(jnp.where(lane == l1, NEG_INF, el))
    ratio = jnp.exp(m2 - m1)
    w1 = gate / (1.0 + ratio)
    w2 = w1 * ratio
    e1 = l1 - N_GROUPS
    e2 = l2 - N_GROUPS
    hit1 = lane == e1
    hit2 = lane == e2
    onehot = (hit1 | hit2).astype(F32)
    tile_counts = jnp.sum(onehot, axis=0, keepdims=True)

    @pl.when((phase == 0) & (i == 0))
    def _():
        counts[...] = jnp.zeros_like(counts)

    @pl.when(phase == 0)
    def _():
        counts[...] += tile_counts

    @pl.when((phase == 1) & (i == 0))
    def _():
        blocks = jnp.floor((counts[...] + (ROW_BLK - 1)) * (1.0 / ROW_BLK))
        r = lax.broadcasted_iota(jnp.int32, (LANES, LANES), 0)
        c = lax.broadcasted_iota(jnp.int32, (LANES, LANES), 1)
        before = (r < c).astype(F32)
        first = jnp.dot(jnp.broadcast_to(blocks, (8, LANES)), before,
                        preferred_element_type=F32, precision=lax.Precision.HIGHEST)[0:1]
        starts[...] = first
        carry[...] = jnp.zeros_like(carry)
        used = jnp.sum(blocks, axis=-1, keepdims=True)
        row = lax.broadcasted_iota(jnp.int32, (8, LANES), 0)
        tab = jnp.where(row == 0, blocks, jnp.where(row == 1, first, jnp.where(row == 2, used, 0.0)))
        tab_ref[...] = tab.astype(jnp.int32)

    @pl.when(phase == 1)
    def _():
        r = lax.broadcasted_iota(jnp.int32, (tm, tm), 0)
        c = lax.broadcasted_iota(jnp.int32, (tm, tm), 1)
        earlier = (c < r).astype(BF16)
        before = jnp.dot(earlier, onehot.astype(BF16), preferred_element_type=F32)
        slot = starts[...] * ROW_BLK + carry[...] + before
        d1 = jnp.sum(jnp.where(hit1, slot, 0.0), axis=-1, keepdims=True)
        d2 = jnp.sum(jnp.where(hit2, slot, 0.0), axis=-1, keepdims=True)
        info_ref[...] = jnp.where(lane == 0, d1, jnp.where(lane == 1, d2, 0.0)).astype(jnp.int32)
        wts_ref[...] = jnp.where(lane == 0, w1, jnp.where(lane == 1, w2, 0.0))
        carry[...] += tile_counts


def _route(logits):
    rows = logits.shape[0]
    tile = lambda p, i: (i * p, 0)
    return pl.pallas_call(
        _route_kernel,
        grid=(2, rows // ROUTE_TM),
        in_specs=[pl.BlockSpec((ROUTE_TM, LANES), lambda p, i: (i, 0))],
        out_specs=(pl.BlockSpec((ROUTE_TM, LANES), tile), pl.BlockSpec((ROUTE_TM, LANES), tile),
                   pl.BlockSpec((8, LANES), lambda p, i: (0, 0))),
        out_shape=(jax.ShapeDtypeStruct((rows, LANES), jnp.int32),
                   jax.ShapeDtypeStruct((rows, LANES), F32),
                   jax.ShapeDtypeStruct((8, LANES), jnp.int32)),
        scratch_shapes=[pltpu.VMEM((1, LANES), F32)] * 3,
        compiler_params=pltpu.CompilerParams(dimension_semantics=("arbitrary", "arbitrary"),
                                             vmem_limit_bytes=VMEM_LIMIT),
        name="route",
    )(logits)


def _work_items(blocks_e, blk_start, n_items):
    items_e = (blocks_e + ITEM_BLKS - 1) // ITEM_BLKS
    item_end = jnp.cumsum(items_e)
    item_start = item_end - items_e
    total = item_end[-1]
    idx = jnp.arange(n_items, dtype=jnp.int32)
    live = idx < total
    ref = jnp.minimum(idx, total - 1)
    e = jnp.searchsorted(item_end, ref, side='right').astype(jnp.int32)
    j = ref - item_start[e]
    r0 = blk_start[e] + ITEM_BLKS * j
    nb = jnp.where(live, jnp.clip(blocks_e[e] - ITEM_BLKS * j, 0, ITEM_BLKS), 0)
    return e, r0.astype(jnp.int32), nb.astype(jnp.int32), total.astype(jnp.int32)


def kernel(x, meta_tokens, attn_norm_g, w_in, swa_sinks, diff_lambda, diff_subln_g, w_out,
           ffn_norm_g, w_router_group, w_router_expert, w_gate, w_up, w_down, final_norm_g):
    batch, seq, _ = x.shape
    n_tok = batch * seq
    nblk = seq // Q_BLOCK
    x2 = x.reshape(n_tok, D_MODEL)

    w_in_bf = w_in[0].astype(BF16)
    g_attn = attn_norm_g[0].reshape(1, D_MODEL)
    cos_x, sin_x = _rope_tables(N_META + jnp.arange(seq))
    cos_m, sin_m = _rope_tables(jnp.maximum(jnp.arange(Q_BLOCK) - PAD_FRONT, 0))
    qs, ks, vs, qd, kd, vd = _inproj(x2, g_attn, w_in_bf, cos_x, sin_x, 512, seq)
    meta_blk = jnp.concatenate(
        [jnp.zeros((PAD_FRONT, D_MODEL), x.dtype), meta_tokens.astype(x.dtype)], axis=0)
    _, ks_m, vs_m, _, kd_m, vd_m = _inproj(meta_blk, g_attn, w_in_bf, cos_m, sin_m, Q_BLOCK, Q_BLOCK)

    o_s = _swa(swa_sinks[0].astype(F32), qs, ks, vs, ks_m, vs_m, batch, nblk)
    o_d = _diff(diff_lambda[0].astype(F32), diff_subln_g[0].astype(F32).reshape(1, LANES),
                qd, kd, vd, kd_m, vd_m, batch, seq)

    w_router = jnp.concatenate(
        [w_router_group[0], w_router_expert[0],
         jnp.zeros((D_MODEL, LANES - N_GROUPS - N_EXPERTS), F32)], axis=1)
    h2, f_packed, logits = _outproj(o_s, o_d, x2, w_out[0].astype(BF16),
                                    ffn_norm_g[0].reshape(1, D_MODEL), w_router, 512)

    info, wts, tab = _route(logits)
    dest_flat = info[:, :TOP_K].reshape(-1)
    blocks_e, blk_start, n_used = tab[0, :N_EXPERTS], tab[1, :N_EXPERTS], tab[2, 0:1]
    n_blocks = n_tok * TOP_K // ROW_BLK + N_EXPERTS
    n_items = n_blocks // ITEM_BLKS + N_EXPERTS
    blk = jnp.arange(n_blocks, dtype=jnp.int32)
    is_last = jnp.any((blk[:, None] + 1 == (blk_start + blocks_e)[None, :])
                      & (blocks_e > 0)[None, :], axis=1)
    need_zero = is_last | (blk >= n_used[0])
    zero_blk = jnp.nonzero(need_zero, size=2 * N_EXPERTS, fill_value=0)[0].astype(jnp.int32)
    n_zero = jnp.sum(need_zero).astype(jnp.int32).reshape(1)
    xs = _dispatch(zero_blk, n_zero, dest_flat, f_packed, n_blocks)
    item_e, item_r0, item_nb, n_live = _work_items(blocks_e, blk_start, n_items)
    y = _moe(item_e, item_r0, item_nb, jnp.concatenate([n_used, n_live.reshape(1)]), xs,
             w_gate[0], w_up[0], w_down[0], n_blocks)

    out = _combine(dest_flat, h2, wts, final_norm_g.reshape(1, D_MODEL), y, 256)
    return out.reshape(batch, seq, D_MODEL)
```

```python
import functools
import math

import jax
import jax.numpy as jnp
from jax import lax
from jax.experimental import pallas as pl
from jax.experimental.pallas import tpu as pltpu

D_MODEL = 2048
N_META = 16
CHUNK = 64
Q_BLOCK = 128
PAD_FRONT = Q_BLOCK - N_META
HEAD_DIM = 64
ROPE_THETA = 10000.0
SWA_Q_HEADS = 16
SWA_KV_HEADS = 2
DIFF_HEADS = 8
SWA_Q_W = SWA_Q_HEADS * HEAD_DIM
SWA_KV_W = SWA_KV_HEADS * HEAD_DIM
DIFF_W = DIFF_HEADS * 2 * HEAD_DIM
IN_WIDTH = SWA_Q_W + 2 * SWA_KV_W + 3 * DIFF_W
N_GROUPS = 8
EXPERTS_PER_GROUP = 8
N_EXPERTS = N_GROUPS * EXPERTS_PER_GROUP
TOP_K = 2
D_EXPERT = 1408
RMS_EPS = 1e-6
NEG_INF = -1e30
LAM_INIT = 0.8 - 0.6 * math.exp(-0.3 * 0)

LANES = 128
ROW_BLK = 128
VMEM_LIMIT = 56 * 1024 * 1024

BF16 = jnp.bfloat16
F32 = jnp.float32


def _nt_dot(a, b):
    return lax.dot_general(a, b, (((1,), (1,)), ((), ())), preferred_element_type=F32)


LOG2_E = math.log2(math.e)
Q_SCALE = HEAD_DIM ** -0.5 * LOG2_E
_SEGS = (
    ("qs", 0, 512, True, Q_SCALE), ("qs", 512, 512, True, Q_SCALE),
    ("ks", 1024, 128, True, 1.0), ("vs", 1152, 128, False, 1.0),
    ("qd", 1280, 512, True, Q_SCALE), ("qd", 1792, 512, True, Q_SCALE),
    ("kd", 2304, 512, True, 1.0), ("kd", 2816, 512, True, 1.0),
    ("vd", 3328, 512, False, 1.0), ("vd", 3840, 512, False, 1.0),
)
_SEG_BASE = {"qs": 0, "ks": 1024, "vs": 1152, "qd": 1280, "kd": 2304, "vd": 3328}


def _inproj_kernel(x_ref, g_ref, w_ref, cos_ref, sin_ref,
                   qs_ref, ks_ref, vs_ref, qd_ref, kd_ref, vd_ref):
    x = x_ref[...]
    ms = jnp.mean(x * x, axis=-1, keepdims=True)
    a = (x * lax.rsqrt(ms + RMS_EPS) * g_ref[...]).astype(BF16)
    cos = cos_ref[...]
    sin = sin_ref[...]
    lane = lax.broadcasted_iota(jnp.int32, cos.shape, 1)
    first_half = (lane % HEAD_DIM) < (HEAD_DIM // 2)
    low_head = lane < HEAD_DIM
    outs = {"qs": qs_ref, "ks": ks_ref, "vs": vs_ref, "qd": qd_ref, "kd": kd_ref, "vd": vd_ref}
    for name, c0, width, rotary, scale in _SEGS:
        y = jnp.dot(a, w_ref[:, c0:c0 + width], preferred_element_type=F32)
        o0 = c0 - _SEG_BASE[name]
        for t in range(width // LANES):
            yt = y[:, t * LANES:(t + 1) * LANES]
            if rotary:
                partner = jnp.where(first_half, pltpu.roll(yt, LANES - HEAD_DIM // 2, 1),
                                    pltpu.roll(yt, HEAD_DIM // 2, 1))
                yt = yt * cos + partner * sin
            if scale != 1.0:
                yt = yt * scale
            if name in ("ks", "vs"):
                other = pltpu.roll(yt, HEAD_DIM, 1)
                outs[name][:, 0:LANES] = jnp.where(low_head, yt, other).astype(BF16)
                outs[name][:, LANES:2 * LANES] = jnp.where(low_head, other, yt).astype(BF16)
            else:
                outs[name][:, o0 + t * LANES:o0 + (t + 1) * LANES] = yt.astype(BF16)


def _inproj(x2, g, w_bf, cos_t, sin_t, tm, rows_per_seq):
    rows = x2.shape[0]
    nseq = rows_per_seq // tm
    row_spec = lambda width: pl.BlockSpec((tm, width), lambda i: (i, 0))
    out_shapes = (
        jax.ShapeDtypeStruct((rows, SWA_Q_W), BF16), jax.ShapeDtypeStruct((rows, 2 * SWA_KV_W), BF16),
        jax.ShapeDtypeStruct((rows, 2 * SWA_KV_W), BF16), jax.ShapeDtypeStruct((rows, DIFF_W), BF16),
        jax.ShapeDtypeStruct((rows, DIFF_W), BF16), jax.ShapeDtypeStruct((rows, DIFF_W), BF16),
    )
    return pl.pallas_call(
        _inproj_kernel,
        grid=(rows // tm,),
        in_specs=[
            row_spec(D_MODEL),
            pl.BlockSpec((1, D_MODEL), lambda i: (0, 0)),
            pl.BlockSpec((D_MODEL, IN_WIDTH), lambda i: (0, 0), pipeline_mode=pl.Buffered(1)),
            pl.BlockSpec((tm, LANES), lambda i: (i % nseq, 0)),
            pl.BlockSpec((tm, LANES), lambda i: (i % nseq, 0)),
        ],
        out_specs=(row_spec(SWA_Q_W), row_spec(2 * SWA_KV_W), row_spec(2 * SWA_KV_W),
                   row_spec(DIFF_W), row_spec(DIFF_W), row_spec(DIFF_W)),
        out_shape=out_shapes,
        compiler_params=pltpu.CompilerParams(dimension_semantics=("arbitrary",),
                                             vmem_limit_bytes=VMEM_LIMIT),
        name="inproj",
    )(x2, g, w_bf, cos_t, sin_t)


def _rope_tables(positions):
    inv = 1.0 / (ROPE_THETA ** (jnp.arange(0, HEAD_DIM, 2, dtype=F32) / HEAD_DIM))
    ang = positions.astype(F32)[:, None] * inv[None, :]
    cos, sin = jnp.cos(ang), jnp.sin(ang)
    cos_t = jnp.tile(cos, (1, 2 * LANES // HEAD_DIM))
    sin_t = jnp.tile(jnp.concatenate([-sin, sin], axis=1), (1, LANES // HEAD_DIM))
    return cos_t, sin_t


SWA_SLAB = 256
SWA_BAND = 3 * CHUNK + N_META


def _swa_kernel(sink_ref, q_ref, kp_ref, kc_ref, km_ref, vp_ref, vc_ref, vm_ref, o_ref):
    i = pl.program_id(1)
    pairs = SWA_Q_HEADS // SWA_KV_HEADS // 2
    m_rows = pairs * 2 * CHUNK
    col = lax.broadcasted_iota(jnp.int32, (m_rows, SWA_SLAB), 1)
    row = lax.broadcasted_iota(jnp.int32, (m_rows, 1), 0)
    lane = lax.broadcasted_iota(jnp.int32, (CHUNK, LANES), 1)
    low_head = lane < HEAD_DIM
    zero = jnp.zeros((CHUNK, LANES), BF16)
    pad = jnp.zeros((SWA_SLAB - SWA_BAND, LANES), BF16)
    meta_rows = slice(PAD_FRONT, Q_BLOCK)
    for g in range(SWA_KV_HEADS):
        gl = slice(g * LANES, (g + 1) * LANES)
        sink = jnp.zeros((m_rows, 1), F32)
        for h in range(2 * pairs):
            sink = jnp.where(row // CHUNK == h, sink_ref[2 * g * pairs + h], sink)
        sink = sink * LOG2_E
        for half in range(2):
            q_rows = slice(half * CHUNK, (half + 1) * CHUNK)
            if half == 0:
                pieces = lambda p, c, m: [p[:, gl], c[0:CHUNK, gl], m[meta_rows, gl], pad]
                n_prev = 2 * CHUNK
            else:
                pieces = lambda p, c, m: [p[CHUNK:, gl], c[:, gl], m[meta_rows, gl], pad]
                n_prev = CHUNK
            k_slab = jnp.concatenate(pieces(kp_ref, kc_ref, km_ref), axis=0)
            v_slab = jnp.concatenate(pieces(vp_ref, vc_ref, vm_ref), axis=0)
            valid = (col < SWA_BAND) & ((i > 0) | (col >= n_prev))
            q2 = []
            for c in range(g * pairs, (g + 1) * pairs):
                qp = q_ref[q_rows, c * LANES:(c + 1) * LANES]
                q2 += [jnp.where(low_head, qp, zero), jnp.where(low_head, zero, qp)]
            s = jnp.where(valid, _nt_dot(jnp.concatenate(q2, axis=0), k_slab), NEG_INF)
            m = jnp.maximum(jnp.max(s, axis=-1, keepdims=True), sink)
            p = jnp.exp2(s - m)
            denom = jnp.sum(p, axis=-1, keepdims=True) + jnp.exp2(sink - m)
            o2 = jnp.dot(p.astype(BF16), v_slab, preferred_element_type=F32) / denom
            for j, c in enumerate(range(g * pairs, (g + 1) * pairs)):
                lo = o2[(2 * j) * CHUNK:(2 * j + 1) * CHUNK]
                hi = o2[(2 * j + 1) * CHUNK:(2 * j + 2) * CHUNK]
                o_ref[q_rows, c * LANES:(c + 1) * LANES] = jnp.where(low_head, lo, hi).astype(BF16)


def _swa(sinks, qs, ks, vs, ks_meta, vs_meta, batch, nblk):
    rows = qs.shape[0]
    cur = lambda b, i: (b * nblk + i, 0)
    prev = lambda b, i: (b * nblk + jnp.maximum(i - 1, 0), 0)
    kv_w = 2 * SWA_KV_W
    return pl.pallas_call(
        _swa_kernel,
        grid=(batch, nblk),
        in_specs=[
            pl.BlockSpec(memory_space=pltpu.SMEM),
            pl.BlockSpec((Q_BLOCK, SWA_Q_W), cur),
            pl.BlockSpec((Q_BLOCK, kv_w), prev), pl.BlockSpec((Q_BLOCK, kv_w), cur),
            pl.BlockSpec((Q_BLOCK, kv_w), lambda b, i: (0, 0)),
            pl.BlockSpec((Q_BLOCK, kv_w), prev), pl.BlockSpec((Q_BLOCK, kv_w), cur),
            pl.BlockSpec((Q_BLOCK, kv_w), lambda b, i: (0, 0)),
        ],
        out_specs=pl.BlockSpec((Q_BLOCK, SWA_Q_W), cur),
        out_shape=jax.ShapeDtypeStruct((rows, SWA_Q_W), BF16),
        compiler_params=pltpu.CompilerParams(dimension_semantics=("arbitrary", "arbitrary"),
                                             vmem_limit_bytes=VMEM_LIMIT),
        name="swa",
    )(sinks, qs, ks, ks, ks_meta, vs, vs, vs_meta)


DIFF_TQ = 256
DIFF_TK = 512
DIFF_HEADS_PER_STEP = 2


def _diff_kernel(lam_ref, g_ref, q_ref, k_ref, v_ref, km_ref, vm_ref, o_ref, *, seq):
    dl = lam_ref[...]
    lam = (jnp.exp(jnp.sum(dl[0:1] * dl[1:2], axis=-1, keepdims=True))
           - jnp.exp(jnp.sum(dl[2:3] * dl[3:4], axis=-1, keepdims=True)) + LAM_INIT)
    lane = lax.broadcasted_iota(jnp.int32, (DIFF_TQ, LANES), 1)
    low_map = lane < HEAD_DIM
    zero = jnp.zeros((DIFF_TQ, LANES), BF16)

    def last_mask(n_keys):
        shape = (2 * DIFF_TQ, Q_BLOCK + n_keys)
        col = lax.broadcasted_iota(jnp.int32, shape, 1)
        q_chunk = (lax.broadcasted_iota(jnp.int32, shape, 0) % DIFF_TQ) // CHUNK
        first_chunk = (DIFF_TQ - n_keys) // CHUNK
        k_chunk = (col - Q_BLOCK) // CHUNK + first_chunk
        return ((col >= PAD_FRONT) & (col < Q_BLOCK)) | ((col >= Q_BLOCK) & (k_chunk <= q_chunk))

    def step(q2, k_t, v_t, mask, state):
        s = _nt_dot(q2, k_t)
        if mask is not None:
            s = jnp.where(mask, s, NEG_INF)
        smax = jnp.max(s, axis=-1, keepdims=True)
        if state is None:
            p = jnp.exp2(s - smax)
            return smax, jnp.sum(p, axis=-1, keepdims=True), jnp.dot(
                p.astype(BF16), v_t, preferred_element_type=F32)
        m_old, l_old, acc = state
        m_new = jnp.maximum(m_old, smax)
        alpha = jnp.exp2(m_old - m_new)
        p = jnp.exp2(s - m_new)
        return (m_new, alpha * l_old + jnp.sum(p, axis=-1, keepdims=True),
                alpha * acc + jnp.dot(p.astype(BF16), v_t, preferred_element_type=F32))

    heads = [slice(h * LANES, (h + 1) * LANES) for h in range(DIFF_HEADS_PER_STEP)]

    tiles_per_k = DIFF_TK // DIFF_TQ
    masks = [last_mask((t + 1) * DIFF_TQ) for t in range(tiles_per_k)]

    def q_tile(n_full, t):
        if n_full is None:
            aligned = lambda v: v
            q0 = t * DIFF_TQ
        else:
            aligned = lambda v: pl.multiple_of(v, DIFF_TQ)
            q0 = aligned((n_full * tiles_per_k + t) * DIFF_TQ)
        q2s = []
        for hs in heads:
            q = q_ref[pl.ds(q0, DIFF_TQ), hs]
            q2s.append(jnp.concatenate(
                [jnp.where(low_map, q, zero), jnp.where(low_map, zero, q)], axis=0))

        def full_tiles(kj, states):
            rows = pl.ds(pl.multiple_of(kj * DIFF_TK, DIFF_TK), DIFF_TK)
            out = ()
            for h, hs in enumerate(heads):
                st = None if states is None else states[3 * h:3 * h + 3]
                out = out + step(q2s[h], k_ref[rows, hs], v_ref[rows, hs], None, st)
            return out

        if n_full is None:
            states = None
        else:
            states = lax.fori_loop(1, n_full, full_tiles, full_tiles(0, None))

        n_keys = (t + 1) * DIFF_TQ
        rows = pl.ds(aligned(q0 + DIFF_TQ - n_keys), n_keys)
        for h, hs in enumerate(heads):
            st = None if states is None else states[3 * h:3 * h + 3]
            k_t = jnp.concatenate([km_ref[:, hs], k_ref[rows, hs]], axis=0)
            v_t = jnp.concatenate([vm_ref[:, hs], v_ref[rows, hs]], axis=0)
            _, l, acc = step(q2s[h], k_t, v_t, masks[t], st)
            o = acc / l
            o = o[:DIFF_TQ] - lam * o[DIFF_TQ:]
            o = o * lax.rsqrt(jnp.mean(o * o, axis=-1, keepdims=True) + RMS_EPS)
            o_ref[pl.ds(q0, DIFF_TQ), hs] = (o * g_ref[...] * (1.0 - LAM_INIT)).astype(BF16)

    def q_group(n_full, carry):
        for t in range(tiles_per_k):
            q_tile(n_full, t)
        return carry

    q_group(None, 0)
    lax.fori_loop(1, seq // DIFF_TK, q_group, 0)


def _diff(diff_lambda, subln_g, qd, kd, vd, kd_meta, vd_meta, batch, seq):
    rows = qd.shape[0]
    head = lambda b, h: (b, h)
    meta = lambda b, h: (0, h)
    width = DIFF_HEADS_PER_STEP * LANES
    return pl.pallas_call(
        functools.partial(_diff_kernel, seq=seq),
        grid=(batch, DIFF_HEADS // DIFF_HEADS_PER_STEP),
        in_specs=[
            pl.BlockSpec((4, HEAD_DIM), lambda b, h: (0, 0)),
            pl.BlockSpec((1, LANES), lambda b, h: (0, 0)),
            pl.BlockSpec((seq, width), head), pl.BlockSpec((seq, width), head),
            pl.BlockSpec((seq, width), head),
            pl.BlockSpec((Q_BLOCK, width), meta), pl.BlockSpec((Q_BLOCK, width), meta),
        ],
        out_specs=pl.BlockSpec((seq, width), head),
        out_shape=jax.ShapeDtypeStruct((rows, DIFF_W), BF16),
        compiler_params=pltpu.CompilerParams(dimension_semantics=("arbitrary", "arbitrary"),
                                             vmem_limit_bytes=VMEM_LIMIT),
        name="diffattn",
    )(diff_lambda, subln_g, qd, kd, vd, kd_meta, vd_meta)


HALF = D_MODEL // 2


def _pack_bf16_pair(a, b):
    hi = pltpu.bitcast(a.astype(BF16).astype(F32), jnp.uint32)
    lo = pltpu.bitcast(b.astype(BF16).astype(F32), jnp.uint32)
    return hi | (lo >> 16)


def _unpack_bf16_pair(w):
    a = pltpu.bitcast(w & jnp.uint32(0xFFFF0000), F32)
    b = pltpu.bitcast(w << 16, F32)
    return a, b


OUT_CHUNK = 512


def _outproj_kernel(os_ref, od_ref, x_ref, w_ref, g_ref, wr_ref, h_ref, fp_ref, lg_ref):
    tm = x_ref.shape[0]
    ssq = jnp.zeros((tm, 1), F32)
    for c in range(D_MODEL // OUT_CHUNK):
        cols = slice(c * OUT_CHUNK, (c + 1) * OUT_CHUNK)
        y = jnp.dot(os_ref[...], w_ref[0:SWA_Q_W, cols], preferred_element_type=F32)
        y = y + jnp.dot(od_ref[...], w_ref[SWA_Q_W:, cols], preferred_element_type=F32)
        h = x_ref[:, cols] + y
        h_ref[:, cols] = h
        ssq = ssq + jnp.sum(h * h, axis=-1, keepdims=True)
    inv = lax.rsqrt(ssq * (1.0 / D_MODEL) + RMS_EPS)
    lg = jnp.zeros((tm, LANES), F32)
    for c in range(HALF // OUT_CHUNK):
        ca = slice(c * OUT_CHUNK, (c + 1) * OUT_CHUNK)
        cb = slice(HALF + c * OUT_CHUNK, HALF + (c + 1) * OUT_CHUNK)
        fa = h_ref[:, ca] * inv * g_ref[:, ca]
        fb = h_ref[:, cb] * inv * g_ref[:, cb]
        fp_ref[:, ca] = _pack_bf16_pair(fa, fb)
        for f, cc in ((fa, ca), (fb, cb)):
            f_hi = f.astype(BF16)
            f_lo = (f - f_hi.astype(F32)).astype(BF16)
            w = wr_ref[cc, :]
            w_hi = w.astype(BF16)
            w_lo = (w - w_hi.astype(F32)).astype(BF16)
            p = jnp.dot(f_hi, jnp.concatenate([w_hi, w_lo], axis=1), preferred_element_type=F32)
            lg = lg + p[:, 0:LANES] + p[:, LANES:] + jnp.dot(f_lo, w_hi, preferred_element_type=F32)
    lg_ref[...] = lg


def _outproj(o_s, o_d, x2, w_bf, g, w_router, tm):
    rows = x2.shape[0]
    row_spec = lambda width: pl.BlockSpec((tm, width), lambda i: (i, 0))
    return pl.pallas_call(
        _outproj_kernel,
        grid=(rows // tm,),
        in_specs=[
            row_spec(SWA_Q_W), row_spec(DIFF_W), row_spec(D_MODEL),
            pl.BlockSpec((D_MODEL, D_MODEL), lambda i: (0, 0), pipeline_mode=pl.Buffered(1)),
            pl.BlockSpec((1, D_MODEL), lambda i: (0, 0)),
            pl.BlockSpec((D_MODEL, LANES), lambda i: (0, 0)),
        ],
        out_specs=(row_spec(D_MODEL), row_spec(HALF), row_spec(LANES)),
        out_shape=(jax.ShapeDtypeStruct((rows, D_MODEL), F32),
                   jax.ShapeDtypeStruct((rows, HALF), jnp.uint32),
                   jax.ShapeDtypeStruct((rows, LANES), F32)),
        compiler_params=pltpu.CompilerParams(dimension_semantics=("arbitrary",),
                                             vmem_limit_bytes=VMEM_LIMIT),
        name="outproj",
    )(o_s, o_d, x2, w_bf, g, w_router)


DISPATCH_TM = 1024


def _dispatch_kernel(zero_blk_ref, n_zero_ref, dest_ref, f_ref, xs_ref, zeros, sem, zsem):
    i = pl.program_id(0)

    @pl.when(i == 0)
    def _():
        zeros[...] = jnp.zeros_like(zeros)

        def zero_copy(j):
            rows = pl.ds(zero_blk_ref[j] * ROW_BLK, ROW_BLK)
            return pltpu.make_async_copy(zeros, xs_ref.at[rows], zsem.at[0])

        def start(j, c):
            zero_copy(j).start()
            return c

        def wait(j, c):
            zero_copy(j).wait()
            return c
        lax.fori_loop(0, n_zero_ref[0], start, 0)
        lax.fori_loop(0, n_zero_ref[0], wait, 0)

    def issue(r, c):
        for kk in range(TOP_K):
            pltpu.make_async_copy(f_ref.at[pl.ds(r, 1)],
                                  xs_ref.at[pl.ds(dest_ref[TOP_K * r + kk], 1)], sem.at[0]).start()
        return c
    lax.fori_loop(0, DISPATCH_TM, issue, 0, unroll=8)
    n_rows = TOP_K * DISPATCH_TM
    pltpu.make_async_copy(xs_ref.at[pl.ds(0, n_rows)], xs_ref.at[pl.ds(0, n_rows)], sem.at[0]).wait()


def _dispatch(zero_blk, n_zero, dest_flat, f_packed, n_blocks):
    rows = f_packed.shape[0]
    return pl.pallas_call(
        _dispatch_kernel,
        grid_spec=pltpu.PrefetchScalarGridSpec(
            num_scalar_prefetch=2,
            grid=(rows // DISPATCH_TM,),
            in_specs=[
                pl.BlockSpec((TOP_K * DISPATCH_TM,), lambda i, z, n: (i,), memory_space=pltpu.SMEM),
                pl.BlockSpec((DISPATCH_TM, HALF), lambda i, z, n: (i, 0)),
            ],
            out_specs=pl.BlockSpec(memory_space=pl.ANY),
            scratch_shapes=[pltpu.VMEM((ROW_BLK, HALF), jnp.uint32),
                            pltpu.SemaphoreType.DMA((1,)),
                            pltpu.SemaphoreType.DMA((1,))],
        ),
        out_shape=jax.ShapeDtypeStruct((n_blocks * ROW_BLK, HALF), jnp.uint32),
        compiler_params=pltpu.CompilerParams(dimension_semantics=("arbitrary",),
                                             vmem_limit_bytes=VMEM_LIMIT),
        name="dispatch",
    )(zero_blk, n_zero, dest_flat, f_packed)


ITEM_BLKS = 6
W_BUFS = 3
ITEM_ROWS = ITEM_BLKS * ROW_BLK
F_TILE = 256
N_FT = -(-D_EXPERT // F_TILE)
Y_TILE = 512
N_YT = D_MODEL // Y_TILE


def _f_tile_start(k):
    return pl.multiple_of(jnp.minimum(k * F_TILE, D_EXPERT - F_TILE), LANES)


def _moe_kernel(e_ref, r0_ref, nb_ref, meta_ref, xs_ref, wg_hbm, wu_hbm, wd_hbm, y_ref,
                xraw, xb, hid, ystage, wgf, wuf, wdf, wgu, wdr, xsem, ysem, wsem):
    n_used = meta_ref[0]
    n_live = meta_ref[1]

    def for_blocks(count, fn):
        def body(j, c):
            fn(j)
            return c
        lax.fori_loop(0, count, body, 0)

    def x_copy(item, buf, j):
        src = xs_ref.at[pl.ds((r0_ref[item] + j) * ROW_BLK, ROW_BLK)]
        return pltpu.make_async_copy(src, xraw.at[buf, pl.ds(j * ROW_BLK, ROW_BLK)],
                                     xsem.at[buf, j])

    def w_copies(item, k, buf):
        e = e_ref[item]
        cols = pl.ds(_f_tile_start(k), F_TILE)
        return (pltpu.make_async_copy(wg_hbm.at[e, :, cols], wgf.at[buf], wsem.at[0, buf]),
                pltpu.make_async_copy(wu_hbm.at[e, :, cols], wuf.at[buf], wsem.at[1, buf]),
                pltpu.make_async_copy(wd_hbm.at[e, cols, :], wdf.at[buf], wsem.at[2, buf]))

    def zero_copy(j):
        dst = y_ref.at[pl.ds((n_used + j) * ROW_BLK, ROW_BLK)]
        return pltpu.make_async_copy(ystage.at[pl.ds(0, ROW_BLK)], dst, ysem.at[0])
    n_unused = y_ref.shape[0] // ROW_BLK - n_used
    ystage[0:ROW_BLK, :] = jnp.zeros((ROW_BLK, HALF), jnp.uint32)
    for_blocks(n_unused, lambda j: zero_copy(j).start())
    for_blocks(n_unused, lambda j: zero_copy(j).wait())

    for_blocks(nb_ref[0], lambda j: x_copy(0, 0, j).start())

    @pl.when(nb_ref[0] > 0)
    def _():
        for s in range(W_BUFS - 1):
            for c in w_copies(0, s, s):
                c.start()

    def item_body(i, carry):
        slot = i % 2
        nb = nb_ref[i]
        prev = jnp.maximum(i - 1, 0)
        nb_prev = jnp.where(i > 0, nb_ref[prev], 0)

        def y_copy(j):
            dst = y_ref.at[pl.ds((r0_ref[prev] + j) * ROW_BLK, ROW_BLK)]
            return pltpu.make_async_copy(ystage.at[pl.ds(j * ROW_BLK, ROW_BLK)], dst, ysem.at[0])

        for_blocks(nb_ref[i + 1], lambda j: x_copy(i + 1, 1 - slot, j).start())

        def arrive(j):
            x_copy(i, slot, j).wait()
            rows = pl.ds(pl.multiple_of(j * ROW_BLK, ROW_BLK), ROW_BLK)
            lo, hi = _unpack_bf16_pair(xraw[slot, rows, :])
            xb[rows, 0:HALF] = lo.astype(BF16)
            xb[rows, HALF:] = hi.astype(BF16)
        for_blocks(nb, arrive)

        def tile_body(k, carry):
            step = i * N_FT + k
            wbuf = step % W_BUFS
            last = k == N_FT - 1
            ahead = k + (W_BUFS - 1)
            nxt_i = jnp.where(ahead >= N_FT, i + 1, i)
            nxt_k = jnp.where(ahead >= N_FT, ahead - N_FT, ahead)

            @pl.when(nb_ref[nxt_i] > 0)
            def _():
                for c in w_copies(nxt_i, nxt_k, (step + (W_BUFS - 1)) % W_BUFS):
                    c.start()

            def cast_tiles():
                wgu[:, 0:F_TILE] = wgf[wbuf].astype(BF16)
                wgu[:, F_TILE:] = wuf[wbuf].astype(BF16)
                f_rows = pl.ds(_f_tile_start(k), F_TILE)
                for n in range(N_YT):
                    wdr[slot, n, f_rows, :] = wdf[wbuf, :, n * Y_TILE:(n + 1) * Y_TILE].astype(BF16)

            def gate_up(row0, nrows):
                rows = pl.ds(pl.multiple_of(row0, ROW_BLK), nrows)
                gu = jnp.dot(xb[rows, :], wgu[...], preferred_element_type=F32)
                gate, up = gu[:, 0:F_TILE], gu[:, F_TILE:]
                h = (gate * (1.0 / (1.0 + jnp.exp(-gate))) * up).astype(BF16)
                hid[slot, rows, pl.ds(_f_tile_start(k), F_TILE)] = h

            k_y = jnp.minimum(k, N_YT - 1)

            def down(row0, nrows):
                rows = pl.ds(pl.multiple_of(row0, ROW_BLK), nrows)
                yt = jnp.dot(hid[1 - slot, rows, :], wdr[1 - slot, k_y], preferred_element_type=F32)
                packed = _pack_bf16_pair(yt[:, 0:Y_TILE // 2], yt[:, Y_TILE // 2:])
                words = pl.ds(pl.multiple_of(k_y * (Y_TILE // 2), Y_TILE // 2), Y_TILE // 2)
                ystage[rows, words] = packed

            def for_row_tiles(n_blk, fn):
                big = 4 * ROW_BLK
                for_blocks(n_blk // 4, lambda t: fn(t * big, big))
                for bit in (2, 1):
                    @pl.when((n_blk & bit) != 0)
                    def _():
                        fn((n_blk // (2 * bit)) * (2 * bit) * ROW_BLK, bit * ROW_BLK)

            @pl.when(nb > 0)
            def _():
                for c in w_copies(i, k, wbuf):
                    c.wait()
                cast_tiles()

            for_row_tiles(nb, gate_up)
            for_row_tiles(jnp.where(k < N_YT, nb_prev, 0), down)

            @pl.when(k == N_YT)
            def _():
                for_blocks(nb_prev, lambda j: y_copy(j).start())

            @pl.when(last)
            def _():
                for_blocks(nb_prev, lambda j: y_copy(j).wait())
            return carry

        lax.fori_loop(0, N_FT, tile_body, 0)
        return carry

    lax.fori_loop(0, n_live + 1, item_body, 0)


def _moe(item_e, item_r0, item_nb, meta, xs, w_gate, w_up, w_down, n_blocks):
    return pl.pallas_call(
        _moe_kernel,
        grid_spec=pltpu.PrefetchScalarGridSpec(
            num_scalar_prefetch=4,
            grid=(1,),
            in_specs=[pl.BlockSpec(memory_space=pl.ANY)] * 4,
            out_specs=pl.BlockSpec(memory_space=pl.ANY),
            scratch_shapes=[
                pltpu.VMEM((2, ITEM_ROWS, HALF), jnp.uint32),
                pltpu.VMEM((ITEM_ROWS, D_MODEL), BF16),
                pltpu.VMEM((2, ITEM_ROWS, D_EXPERT), BF16),
                pltpu.VMEM((ITEM_ROWS, HALF), jnp.uint32),
                pltpu.VMEM((W_BUFS, D_MODEL, F_TILE), F32),
                pltpu.VMEM((W_BUFS, D_MODEL, F_TILE), F32),
                pltpu.VMEM((W_BUFS, F_TILE, D_MODEL), F32),
                pltpu.VMEM((D_MODEL, 2 * F_TILE), BF16),
                pltpu.VMEM((2, N_YT, D_EXPERT, Y_TILE), BF16),
                pltpu.SemaphoreType.DMA((2, ITEM_BLKS)),
                pltpu.SemaphoreType.DMA((1,)),
                pltpu.SemaphoreType.DMA((3, W_BUFS)),
            ],
        ),
        out_shape=jax.ShapeDtypeStruct((n_blocks * ROW_BLK, HALF), jnp.uint32),
        compiler_params=pltpu.CompilerParams(dimension_semantics=("arbitrary",),
                                             vmem_limit_bytes=VMEM_LIMIT),
        name="moe",
    )(item_e, item_r0, item_nb, meta, xs, w_gate, w_up, w_down)


def _combine_kernel(cur_ref, nxt_ref, h_ref, w_ref, g_ref, y_ref, o_ref, ya, yb, sem, *, tm):
    i = pl.program_id(0)
    n_steps = pl.num_programs(0)
    slot = i % 2

    def issue(idx_ref, buf, r):
        pltpu.make_async_copy(y_ref.at[pl.ds(idx_ref[TOP_K * r], 1)], ya.at[buf, pl.ds(r, 1)],
                              sem.at[buf]).start()
        pltpu.make_async_copy(y_ref.at[pl.ds(idx_ref[TOP_K * r + 1], 1)], yb.at[buf, pl.ds(r, 1)],
                              sem.at[buf]).start()

    def wait(buf):
        pltpu.make_async_copy(ya.at[buf], ya.at[buf], sem.at[buf]).wait()
        pltpu.make_async_copy(yb.at[buf], yb.at[buf], sem.at[buf]).wait()

    @pl.when(i == 0)
    def _():
        def body(r, c):
            issue(cur_ref, 0, r)
            return c
        lax.fori_loop(0, tm, body, 0, unroll=8)

    wait(slot)
    for r in range(tm):
        issue(nxt_ref, 1 - slot, r)

    wa = w_ref[:, 0:1]
    wb = w_ref[:, 1:2]
    half_t = Y_TILE // 2
    ssq = jnp.zeros((tm, 1), F32)
    for n in range(N_YT):
        words = slice(n * half_t, (n + 1) * half_t)
        a_lo, a_hi = _unpack_bf16_pair(ya[slot, :, words])
        b_lo, b_hi = _unpack_bf16_pair(yb[slot, :, words])
        lo_cols = slice(n * Y_TILE, n * Y_TILE + half_t)
        hi_cols = slice(n * Y_TILE + half_t, (n + 1) * Y_TILE)
        lo = h_ref[:, lo_cols] + wa * a_lo + wb * b_lo
        hi = h_ref[:, hi_cols] + wa * a_hi + wb * b_hi
        ssq = ssq + jnp.sum(lo * lo, axis=-1, keepdims=True) + jnp.sum(hi * hi, axis=-1, keepdims=True)
        o_ref[:, lo_cols] = lo
        o_ref[:, hi_cols] = hi
    inv = lax.rsqrt(ssq * (1.0 / D_MODEL) + RMS_EPS)
    o_ref[...] = o_ref[...] * inv * g_ref[...]

    @pl.when(i == n_steps - 1)
    def _():
        wait(1 - slot)


def _combine(dest_flat, h2, wts, g, y, tm):
    rows = h2.shape[0]
    steps = rows // tm
    idx_spec = lambda fn: pl.BlockSpec((TOP_K * tm,), fn, memory_space=pltpu.SMEM)
    return pl.pallas_call(
        functools.partial(_combine_kernel, tm=tm),
        grid=(steps,),
        in_specs=[
            idx_spec(lambda i: (i,)),
            idx_spec(lambda i: (jnp.minimum(i + 1, steps - 1),)),
            pl.BlockSpec((tm, D_MODEL), lambda i: (i, 0)),
            pl.BlockSpec((tm, LANES), lambda i: (i, 0)),
            pl.BlockSpec((1, D_MODEL), lambda i: (0, 0)),
            pl.BlockSpec(memory_space=pl.ANY),
        ],
        out_specs=pl.BlockSpec((tm, D_MODEL), lambda i: (i, 0)),
        scratch_shapes=[
            pltpu.VMEM((2, tm, HALF), jnp.uint32),
            pltpu.VMEM((2, tm, HALF), jnp.uint32),
            pltpu.SemaphoreType.DMA((2,)),
        ],
        out_shape=jax.ShapeDtypeStruct((rows, D_MODEL), F32),
        compiler_params=pltpu.CompilerParams(dimension_semantics=("arbitrary",),
                                             vmem_limit_bytes=VMEM_LIMIT),
        name="combine",
    )(dest_flat, dest_flat, h2, wts, g, y)


ROUTE_TM = 512


def _route_kernel(lg_ref, info_ref, wts_ref, tab_ref, counts, starts, carry):
    phase = pl.program_id(0)
    i = pl.program_id(1)
    tm = ROUTE_TM
    lane = lax.broadcasted_iota(jnp.int32, (tm, LANES), 1)
    lg = lg_ref[...]

    def first_lane_of_max(v):
        vmax = jnp.max(v, axis=-1, keepdims=True)
        return vmax, jnp.min(jnp.where(v == vmax, lane, LANES), axis=-1, keepdims=True)

    gl = jnp.where(lane < N_GROUPS, lg, NEG_INF)
    gmax, g_sel = first_lane_of_max(gl)
    gate = 1.0 / jnp.sum(jnp.exp(gl - gmax), axis=-1, keepdims=True)
    lo = N_GROUPS + EXPERTS_PER_GROUP * g_sel
    el = jnp.where((lane >= lo) & (lane < lo + EXPERTS_PER_GROUP), lg, NEG_INF)
    m1, l1 = first_lane_of_max(el)
    m2, l2 = first_lane_of_max(jnp.where(lane == l1, NEG_INF, el))
    ratio = jnp.exp(m2 - m1)
    w1 = gate / (1.0 + ratio)
    w2 = w1 * ratio
    e1 = l1 - N_GROUPS
    e2 = l2 - N_GROUPS
    hit1 = lane == e1
    hit2 = lane == e2
    onehot = (hit1 | hit2).astype(F32)
    tile_counts = jnp.sum(onehot, axis=0, keepdims=True)

    @pl.when((phase == 0) & (i == 0))
    def _():
        counts[...] = jnp.zeros_like(counts)

    @pl.when(phase == 0)
    def _():
        counts[...] += tile_counts

    @pl.when((phase == 1) & (i == 0))
    def _():
        blocks = jnp.floor((counts[...] + (ROW_BLK - 1)) * (1.0 / ROW_BLK))
        r = lax.broadcasted_iota(jnp.int32, (LANES, LANES), 0)
        c = lax.broadcasted_iota(jnp.int32, (LANES, LANES), 1)
        before = (r < c).astype(F32)
        first = jnp.dot(jnp.broadcast_to(blocks, (8, LANES)), before,
                        preferred_element_type=F32, precision=lax.Precision.HIGHEST)[0:1]
        starts[...] = first
        carry[...] = jnp.zeros_like(carry)
        used = jnp.sum(blocks, axis=-1, keepdims=True)
        row = lax.broadcasted_iota(jnp.int32, (8, LANES), 0)
        tab = jnp.where(row == 0, blocks, jnp.where(row == 1, first, jnp.where(row == 2, used, 0.0)))
        tab_ref[...] = tab.astype(jnp.int32)

    @pl.when(phase == 1)
    def _():
        r = lax.broadcasted_iota(jnp.int32, (tm, tm), 0)
        c = lax.broadcasted_iota(jnp.int32, (tm, tm), 1)
        earlier = (c < r).astype(BF16)
        before = jnp.dot(earlier, onehot.astype(BF16), preferred_element_type=F32)
        slot = starts[...] * ROW_BLK + carry[...] + before
        d1 = jnp.sum(jnp.where(hit1, slot, 0.0), axis=-1, keepdims=True)
        d2 = jnp.sum(jnp.where(hit2, slot, 0.0), axis=-1, keepdims=True)
        info_ref[...] = jnp.where(lane == 0, d1, jnp.where(lane == 1, d2, 0.0)).astype(jnp.int32)
        wts_ref[...] = jnp.where(lane == 0, w1, jnp.where(lane == 1, w2, 0.0))
        carry[...] += tile_counts


def _route(logits):
    rows = logits.shape[0]
    tile = lambda p, i: (i * p, 0)
    return pl.pallas_call(
        _route_kernel,
        grid=(2, rows // ROUTE_TM),
        in_specs=[pl.BlockSpec((ROUTE_TM, LANES), lambda p, i: (i, 0))],
        out_specs=(pl.BlockSpec((ROUTE_TM, LANES), tile), pl.BlockSpec((ROUTE_TM, LANES), tile),
                   pl.BlockSpec((8, LANES), lambda p, i: (0, 0))),
        out_shape=(jax.ShapeDtypeStruct((rows, LANES), jnp.int32),
                   jax.ShapeDtypeStruct((rows, LANES), F32),
                   jax.ShapeDtypeStruct((8, LANES), jnp.int32)),
        scratch_shapes=[pltpu.VMEM((1, LANES), F32)] * 3,
        compiler_params=pltpu.CompilerParams(dimension_semantics=("arbitrary", "arbitrary"),
                                             vmem_limit_bytes=VMEM_LIMIT),
        name="route",
    )(logits)


def _work_items(blocks_e, blk_start, n_items):
    items_e = (blocks_e + ITEM_BLKS - 1) // ITEM_BLKS
    item_end = jnp.cumsum(items_e)
    item_start = item_end - items_e
    total = item_end[-1]
    idx = jnp.arange(n_items, dtype=jnp.int32)
    live = idx < total
    ref = jnp.minimum(idx, total - 1)
    e = jnp.searchsorted(item_end, ref, side='right').astype(jnp.int32)
    j = ref - item_start[e]
    r0 = blk_start[e] + ITEM_BLKS * j
    nb = jnp.where(live, jnp.clip(blocks_e[e] - ITEM_BLKS * j, 0, ITEM_BLKS), 0)
    return e, r0.astype(jnp.int32), nb.astype(jnp.int32), total.astype(jnp.int32)


def kernel(x, meta_tokens, attn_norm_g, w_in, swa_sinks, diff_lambda, diff_subln_g, w_out,
           ffn_norm_g, w_router_group, w_router_expert, w_gate, w_up, w_down, final_norm_g):
    batch, seq, _ = x.shape
    n_tok = batch * seq
    nblk = seq // Q_BLOCK
    x2 = x.reshape(n_tok, D_MODEL)

    w_in_bf = w_in[0].astype(BF16)
    g_attn = attn_norm_g[0].reshape(1, D_MODEL)
    cos_x, sin_x = _rope_tables(N_META + jnp.arange(seq))
    cos_m, sin_m = _rope_tables(jnp.maximum(jnp.arange(Q_BLOCK) - PAD_FRONT, 0))
    qs, ks, vs, qd, kd, vd = _inproj(x2, g_attn, w_in_bf, cos_x, sin_x, 512, seq)
    meta_blk = jnp.concatenate(
        [jnp.zeros((PAD_FRONT, D_MODEL), x.dtype), meta_tokens.astype(x.dtype)], axis=0)
    _, ks_m, vs_m, _, kd_m, vd_m = _inproj(meta_blk, g_attn, w_in_bf, cos_m, sin_m, Q_BLOCK, Q_BLOCK)

    o_s = _swa(swa_sinks[0].astype(F32), qs, ks, vs, ks_m, vs_m, batch, nblk)
    o_d = _diff(diff_lambda[0].astype(F32), diff_subln_g[0].astype(F32).reshape(1, LANES),
                qd, kd, vd, kd_m, vd_m, batch, seq)

    w_router = jnp.concatenate(
        [w_router_group[0], w_router_expert[0],
         jnp.zeros((D_MODEL, LANES - N_GROUPS - N_EXPERTS), F32)], axis=1)
    h2, f_packed, logits = _outproj(o_s, o_d, x2, w_out[0].astype(BF16),
                                    ffn_norm_g[0].reshape(1, D_MODEL), w_router, 512)

    info, wts, tab = _route(logits)
    dest_flat = info[:, :TOP_K].reshape(-1)
    blocks_e, blk_start, n_used = tab[0, :N_EXPERTS], tab[1, :N_EXPERTS], tab[2, 0:1]
    n_blocks = n_tok * TOP_K // ROW_BLK + N_EXPERTS
    n_items = n_blocks // ITEM_BLKS + N_EXPERTS
    blk = jnp.arange(n_blocks, dtype=jnp.int32)
    is_last = jnp.any((blk[:, None] + 1 == (blk_start + blocks_e)[None, :])
                      & (blocks_e > 0)[None, :], axis=1)
    need_zero = is_last | (blk >= n_used[0])
    zero_blk = jnp.nonzero(need_zero, size=2 * N_EXPERTS, fill_value=0)[0].astype(jnp.int32)
    n_zero = jnp.sum(need_zero).astype(jnp.int32).reshape(1)
    xs = _dispatch(zero_blk, n_zero, dest_flat, f_packed, n_blocks)
    item_e, item_r0, item_nb, n_live = _work_items(blocks_e, blk_start, n_items)
    y = _moe(item_e, item_r0, item_nb, jnp.concatenate([n_used, n_live.reshape(1)]), xs,
             w_gate[0], w_up[0], w_down[0], n_blocks)

    out = _combine(dest_flat, h2, wts, final_norm_g.reshape(1, D_MODEL), y, 256)
    return out.reshape(batch, seq, D_MODEL)
```

```python
import functools
import math

import jax
import jax.numpy as jnp
from jax import lax
from jax.experimental import pallas as pl
from jax.experimental.pallas import tpu as pltpu

D_MODEL = 2048
N_META = 16
CHUNK = 64
Q_BLOCK = 128
PAD_FRONT = Q_BLOCK - N_META
HEAD_DIM = 64
ROPE_THETA = 10000.0
SWA_Q_HEADS = 16
SWA_KV_HEADS = 2
DIFF_HEADS = 8
SWA_Q_W = SWA_Q_HEADS * HEAD_DIM
SWA_KV_W = SWA_KV_HEADS * HEAD_DIM
DIFF_W = DIFF_HEADS * 2 * HEAD_DIM
IN_WIDTH = SWA_Q_W + 2 * SWA_KV_W + 3 * DIFF_W
N_GROUPS = 8
EXPERTS_PER_GROUP = 8
N_EXPERTS = N_GROUPS * EXPERTS_PER_GROUP
TOP_K = 2
D_EXPERT = 1408
RMS_EPS = 1e-6
NEG_INF = -1e30
LAM_INIT = 0.8 - 0.6 * math.exp(-0.3 * 0)

LANES = 128
ROW_BLK = 128
VMEM_LIMIT = 56 * 1024 * 1024

BF16 = jnp.bfloat16
F32 = jnp.float32


def _nt_dot(a, b):
    return lax.dot_general(a, b, (((1,), (1,)), ((), ())), preferred_element_type=F32)


LOG2_E = math.log2(math.e)
Q_SCALE = HEAD_DIM ** -0.5 * LOG2_E
_SEGS = (
    ("qs", 0, 512, True, Q_SCALE), ("qs", 512, 512, True, Q_SCALE),
    ("ks", 1024, 128, True, 1.0), ("vs", 1152, 128, False, 1.0),
    ("qd", 1280, 512, True, Q_SCALE), ("qd", 1792, 512, True, Q_SCALE),
    ("kd", 2304, 512, True, 1.0), ("kd", 2816, 512, True, 1.0),
    ("vd", 3328, 512, False, 1.0), ("vd", 3840, 512, False, 1.0),
)
_SEG_BASE = {"qs": 0, "ks": 1024, "vs": 1152, "qd": 1280, "kd": 2304, "vd": 3328}


def _inproj_kernel(x_ref, g_ref, w_ref, cos_ref, sin_ref,
                   qs_ref, ks_ref, vs_ref, qd_ref, kd_ref, vd_ref):
    x = x_ref[...]
    ms = jnp.mean(x * x, axis=-1, keepdims=True)
    a = (x * lax.rsqrt(ms + RMS_EPS) * g_ref[...]).astype(BF16)
    cos = cos_ref[...]
    sin = sin_ref[...]
    lane = lax.broadcasted_iota(jnp.int32, cos.shape, 1)
    first_half = (lane % HEAD_DIM) < (HEAD_DIM // 2)
    low_head = lane < HEAD_DIM
    outs = {"qs": qs_ref, "ks": ks_ref, "vs": vs_ref, "qd": qd_ref, "kd": kd_ref, "vd": vd_ref}
    for name, c0, width, rotary, scale in _SEGS:
        y = jnp.dot(a, w_ref[:, c0:c0 + width], preferred_element_type=F32)
        o0 = c0 - _SEG_BASE[name]
        for t in range(width // LANES):
            yt = y[:, t * LANES:(t + 1) * LANES]
            if rotary:
                partner = jnp.where(first_half, pltpu.roll(yt, LANES - HEAD_DIM // 2, 1),
                                    pltpu.roll(yt, HEAD_DIM // 2, 1))
                yt = yt * cos + partner * sin
            if scale != 1.0:
                yt = yt * scale
            if name in ("ks", "vs"):
                other = pltpu.roll(yt, HEAD_DIM, 1)
                outs[name][:, 0:LANES] = jnp.where(low_head, yt, other).astype(BF16)
                outs[name][:, LANES:2 * LANES] = jnp.where(low_head, other, yt).astype(BF16)
            else:
                outs[name][:, o0 + t * LANES:o0 + (t + 1) * LANES] = yt.astype(BF16)


def _inproj(x2, g, w_bf, cos_t, sin_t, tm, rows_per_seq):
    rows = x2.shape[0]
    nseq = rows_per_seq // tm
    row_spec = lambda width: pl.BlockSpec((tm, width), lambda i: (i, 0))
    out_shapes = (
        jax.ShapeDtypeStruct((rows, SWA_Q_W), BF16), jax.ShapeDtypeStruct((rows, 2 * SWA_KV_W), BF16),
        jax.ShapeDtypeStruct((rows, 2 * SWA_KV_W), BF16), jax.ShapeDtypeStruct((rows, DIFF_W), BF16),
        jax.ShapeDtypeStruct((rows, DIFF_W), BF16), jax.ShapeDtypeStruct((rows, DIFF_W), BF16),
    )
    return pl.pallas_call(
        _inproj_kernel,
        grid=(rows // tm,),
        in_specs=[
            row_spec(D_MODEL),
            pl.BlockSpec((1, D_MODEL), lambda i: (0, 0)),
            pl.BlockSpec((D_MODEL, IN_WIDTH), lambda i: (0, 0), pipeline_mode=pl.Buffered(1)),
            pl.BlockSpec((tm, LANES), lambda i: (i % nseq, 0)),
            pl.BlockSpec((tm, LANES), lambda i: (i % nseq, 0)),
        ],
        out_specs=(row_spec(SWA_Q_W), row_spec(2 * SWA_KV_W), row_spec(2 * SWA_KV_W),
                   row_spec(DIFF_W), row_spec(DIFF_W), row_spec(DIFF_W)),
        out_shape=out_shapes,
        compiler_params=pltpu.CompilerParams(dimension_semantics=("arbitrary",),
                                             vmem_limit_bytes=VMEM_LIMIT),
        name="inproj",
    )(x2, g, w_bf, cos_t, sin_t)


def _rope_tables(positions):
    inv = 1.0 / (ROPE_THETA ** (jnp.arange(0, HEAD_DIM, 2, dtype=F32) / HEAD_DIM))
    ang = positions.astype(F32)[:, None] * inv[None, :]
    cos, sin = jnp.cos(ang), jnp.sin(ang)
    cos_t = jnp.tile(cos, (1, 2 * LANES // HEAD_DIM))
    sin_t = jnp.tile(jnp.concatenate([-sin, sin], axis=1), (1, LANES // HEAD_DIM))
    return cos_t, sin_t


SWA_SLAB = 256
SWA_BAND = 3 * CHUNK + N_META


def _swa_kernel(sink_ref, q_ref, kp_ref, kc_ref, km_ref, vp_ref, vc_ref, vm_ref, o_ref):
    i = pl.program_id(1)
    pairs = SWA_Q_HEADS // SWA_KV_HEADS // 2
    m_rows = pairs * 2 * CHUNK
    col = lax.broadcasted_iota(jnp.int32, (m_rows, SWA_SLAB), 1)
    row = lax.broadcasted_iota(jnp.int32, (m_rows, 1), 0)
    lane = lax.broadcasted_iota(jnp.int32, (CHUNK, LANES), 1)
    low_head = lane < HEAD_DIM
    zero = jnp.zeros((CHUNK, LANES), BF16)
    pad = jnp.zeros((SWA_SLAB - SWA_BAND, LANES), BF16)
    meta_rows = slice(PAD_FRONT, Q_BLOCK)
    for g in range(SWA_KV_HEADS):
        gl = slice(g * LANES, (g + 1) * LANES)
        sink = jnp.zeros((m_rows, 1), F32)
        for h in range(2 * pairs):
            sink = jnp.where(row // CHUNK == h, sink_ref[2 * g * pairs + h], sink)
        sink = sink * LOG2_E
        for half in range(2):
            q_rows = slice(half * CHUNK, (half + 1) * CHUNK)
            if half == 0:
                pieces = lambda p, c, m: [p[:, gl], c[0:CHUNK, gl], m[meta_rows, gl], pad]
                n_prev = 2 * CHUNK
            else:
                pieces = lambda p, c, m: [p[CHUNK:, gl], c[:, gl], m[meta_rows, gl], pad]
                n_prev = CHUNK
            k_slab = jnp.concatenate(pieces(kp_ref, kc_ref, km_ref), axis=0)
            v_slab = jnp.concatenate(pieces(vp_ref, vc_ref, vm_ref), axis=0)
            valid = (col < SWA_BAND) & ((i > 0) | (col >= n_prev))
            q2 = []
            for c in range(g * pairs, (g + 1) * pairs):
                qp = q_ref[q_rows, c * LANES:(c + 1) * LANES]
                q2 += [jnp.where(low_head, qp, zero), jnp.where(low_head, zero, qp)]
            s = jnp.where(valid, _nt_dot(jnp.concatenate(q2, axis=0), k_slab), NEG_INF)
            m = jnp.maximum(jnp.max(s, axis=-1, keepdims=True), sink)
            p = jnp.exp2(s - m)
            denom = jnp.sum(p, axis=-1, keepdims=True) + jnp.exp2(sink - m)
            o2 = jnp.dot(p.astype(BF16), v_slab, preferred_element_type=F32) / denom
            for j, c in enumerate(range(g * pairs, (g + 1) * pairs)):
                lo = o2[(2 * j) * CHUNK:(2 * j + 1) * CHUNK]
                hi = o2[(2 * j + 1) * CHUNK:(2 * j + 2) * CHUNK]
                o_ref[q_rows, c * LANES:(c + 1) * LANES] = jnp.where(low_head, lo, hi).astype(BF16)


def _swa(sinks, qs, ks, vs, ks_meta, vs_meta, batch, nblk):
    rows = qs.shape[0]
    cur = lambda b, i: (b * nblk + i, 0)
    prev = lambda b, i: (b * nblk + jnp.maximum(i - 1, 0), 0)
    kv_w = 2 * SWA_KV_W
    return pl.pallas_call(
        _swa_kernel,
        grid=(batch, nblk),
        in_specs=[
            pl.BlockSpec(memory_space=pltpu.SMEM),
            pl.BlockSpec((Q_BLOCK, SWA_Q_W), cur),
            pl.BlockSpec((Q_BLOCK, kv_w), prev), pl.BlockSpec((Q_BLOCK, kv_w), cur),
            pl.BlockSpec((Q_BLOCK, kv_w), lambda b, i: (0, 0)),
            pl.BlockSpec((Q_BLOCK, kv_w), prev), pl.BlockSpec((Q_BLOCK, kv_w), cur),
            pl.BlockSpec((Q_BLOCK, kv_w), lambda b, i: (0, 0)),
        ],
        out_specs=pl.BlockSpec((Q_BLOCK, SWA_Q_W), cur),
        out_shape=jax.ShapeDtypeStruct((rows, SWA_Q_W), BF16),
        compiler_params=pltpu.CompilerParams(dimension_semantics=("arbitrary", "arbitrary"),
                                             vmem_limit_bytes=VMEM_LIMIT),
        name="swa",
    )(sinks, qs, ks, ks, ks_meta, vs, vs, vs_meta)


DIFF_TQ = 256
DIFF_TK = 512
DIFF_HEADS_PER_STEP = 2


def _diff_kernel(lam_ref, g_ref, q_ref, k_ref, v_ref, km_ref, vm_ref, o_ref, *, seq):
    dl = lam_ref[...]
    lam = (jnp.exp(jnp.sum(dl[0:1] * dl[1:2], axis=-1, keepdims=True))
           - jnp.exp(jnp.sum(dl[2:3] * dl[3:4], axis=-1, keepdims=True)) + LAM_INIT)
    lane = lax.broadcasted_iota(jnp.int32, (DIFF_TQ, LANES), 1)
    low_map = lane < HEAD_DIM
    zero = jnp.zeros((DIFF_TQ, LANES), BF16)

    def last_mask(n_keys):
        shape = (2 * DIFF_TQ, Q_BLOCK + n_keys)
        col = lax.broadcasted_iota(jnp.int32, shape, 1)
        q_chunk = (lax.broadcasted_iota(jnp.int32, shape, 0) % DIFF_TQ) // CHUNK
        first_chunk = (DIFF_TQ - n_keys) // CHUNK
        k_chunk = (col - Q_BLOCK) // CHUNK + first_chunk
        return ((col >= PAD_FRONT) & (col < Q_BLOCK)) | ((col >= Q_BLOCK) & (k_chunk <= q_chunk))

    def step(q2, k_t, v_t, mask, state):
        s = _nt_dot(q2, k_t)
        if mask is not None:
            s = jnp.where(mask, s, NEG_INF)
        smax = jnp.max(s, axis=-1, keepdims=True)
        if state is None:
            p = jnp.exp2(s - smax)
            return smax, jnp.sum(p, axis=-1, keepdims=True), jnp.dot(
                p.astype(BF16), v_t, preferred_element_type=F32)
        m_old, l_old, acc = state
        m_new = jnp.maximum(m_old, smax)
        alpha = jnp.exp2(m_old - m_new)
        p = jnp.exp2(s - m_new)
        return (m_new, alpha * l_old + jnp.sum(p, axis=-1, keepdims=True),
                alpha * acc + jnp.dot(p.astype(BF16), v_t, preferred_element_type=F32))

    heads = [slice(h * LANES, (h + 1) * LANES) for h in range(DIFF_HEADS_PER_STEP)]

    tiles_per_k = DIFF_TK // DIFF_TQ
    masks = [last_mask((t + 1) * DIFF_TQ) for t in range(tiles_per_k)]

    def q_tile(n_full, t):
        if n_full is None:
            aligned = lambda v: v
            q0 = t * DIFF_TQ
        else:
            aligned = lambda v: pl.multiple_of(v, DIFF_TQ)
            q0 = aligned((n_full * tiles_per_k + t) * DIFF_TQ)
        q2s = []
        for hs in heads:
            q = q_ref[pl.ds(q0, DIFF_TQ), hs]
            q2s.append(jnp.concatenate(
                [jnp.where(low_map, q, zero), jnp.where(low_map, zero, q)], axis=0))

        def full_tiles(kj, states):
            rows = pl.ds(pl.multiple_of(kj * DIFF_TK, DIFF_TK), DIFF_TK)
            out = ()
            for h, hs in enumerate(heads):
                st = None if states is None else states[3 * h:3 * h + 3]
                out = out + step(q2s[h], k_ref[rows, hs], v_ref[rows, hs], None, st)
            return out

        if n_full is None:
            states = None
        else:
            states = lax.fori_loop(1, n_full, full_tiles, full_tiles(0, None))

        n_keys = (t + 1) * DIFF_TQ
        rows = pl.ds(aligned(q0 + DIFF_TQ - n_keys), n_keys)
        for h, hs in enumerate(heads):
            st = None if states is None else states[3 * h:3 * h + 3]
            k_t = jnp.concatenate([km_ref[:, hs], k_ref[rows, hs]], axis=0)
            v_t = jnp.concatenate([vm_ref[:, hs], v_ref[rows, hs]], axis=0)
            _, l, acc = step(q2s[h], k_t, v_t, masks[t], st)
            o = acc / l
            o = o[:DIFF_TQ] - lam * o[DIFF_TQ:]
            o = o * lax.rsqrt(jnp.mean(o * o, axis=-1, keepdims=True) + RMS_EPS)
            o_ref[pl.ds(q0, DIFF_TQ), hs] = (o * g_ref[...] * (1.0 - LAM_INIT)).astype(BF16)

    def q_group(n_full, carry):
        for t in range(tiles_per_k):
            q_tile(n_full, t)
        return carry

    q_group(None, 0)
    lax.fori_loop(1, seq // DIFF_TK, q_group, 0)


def _diff(diff_lambda, subln_g, qd, kd, vd, kd_meta, vd_meta, batch, seq):
    rows = qd.shape[0]
    head = lambda b, h: (b, h)
    meta = lambda b, h: (0, h)
    width = DIFF_HEADS_PER_STEP * LANES
    return pl.pallas_call(
        functools.partial(_diff_kernel, seq=seq),
        grid=(batch, DIFF_HEADS // DIFF_HEADS_PER_STEP),
        in_specs=[
            pl.BlockSpec((4, HEAD_DIM), lambda b, h: (0, 0)),
            pl.BlockSpec((1, LANES), lambda b, h: (0, 0)),
            pl.BlockSpec((seq, width), head), pl.BlockSpec((seq, width), head),
            pl.BlockSpec((seq, width), head),
            pl.BlockSpec((Q_BLOCK, width), meta), pl.BlockSpec((Q_BLOCK, width), meta),
        ],
        out_specs=pl.BlockSpec((seq, width), head),
        out_shape=jax.ShapeDtypeStruct((rows, DIFF_W), BF16),
        compiler_params=pltpu.CompilerParams(dimension_semantics=("arbitrary", "arbitrary"),
                                             vmem_limit_bytes=VMEM_LIMIT),
        name="diffattn",
    )(diff_lambda, subln_g, qd, kd, vd, kd_meta, vd_meta)


HALF = D_MODEL // 2


def _pack_bf16_pair(a, b):
    hi = pltpu.bitcast(a.astype(BF16).astype(F32), jnp.uint32)
    lo = pltpu.bitcast(b.astype(BF16).astype(F32), jnp.uint32)
    return hi | (lo >> 16)


def _unpack_bf16_pair(w):
    a = pltpu.bitcast(w & jnp.uint32(0xFFFF0000), F32)
    b = pltpu.bitcast(w << 16, F32)
    return a, b


OUT_CHUNK = 512


def _outproj_kernel(os_ref, od_ref, x_ref, w_ref, g_ref, wr_ref, h_ref, fp_ref, lg_ref):
    tm = x_ref.shape[0]
    ssq = jnp.zeros((tm, 1), F32)
    for c in range(D_MODEL // OUT_CHUNK):
        cols = slice(c * OUT_CHUNK, (c + 1) * OUT_CHUNK)
        y = jnp.dot(os_ref[...], w_ref[0:SWA_Q_W, cols], preferred_element_type=F32)
        y = y + jnp.dot(od_ref[...], w_ref[SWA_Q_W:, cols], preferred_element_type=F32)
        h = x_ref[:, cols] + y
        h_ref[:, cols] = h
        ssq = ssq + jnp.sum(h * h, axis=-1, keepdims=True)
    inv = lax.rsqrt(ssq * (1.0 / D_MODEL) + RMS_EPS)
    lg = jnp.zeros((tm, LANES), F32)
    for c in range(HALF // OUT_CHUNK):
        ca = slice(c * OUT_CHUNK, (c + 1) * OUT_CHUNK)
        cb = slice(HALF + c * OUT_CHUNK, HALF + (c + 1) * OUT_CHUNK)
        fa = h_ref[:, ca] * inv * g_ref[:, ca]
        fb = h_ref[:, cb] * inv * g_ref[:, cb]
        fp_ref[:, ca] = _pack_bf16_pair(fa, fb)
        for f, cc in ((fa, ca), (fb, cb)):
            f_hi = f.astype(BF16)
            f_lo = (f - f_hi.astype(F32)).astype(BF16)
            w = wr_ref[cc, :]
            w_hi = w.astype(BF16)
            w_lo = (w - w_hi.astype(F32)).astype(BF16)
            p = jnp.dot(f_hi, jnp.concatenate([w_hi, w_lo], axis=1), preferred_element_type=F32)
            lg = lg + p[:, 0:LANES] + p[:, LANES:] + jnp.dot(f_lo, w_hi, preferred_element_type=F32)
    lg_ref[...] = lg


def _outproj(o_s, o_d, x2, w_bf, g, w_router, tm):
    rows = x2.shape[0]
    row_spec = lambda width: pl.BlockSpec((tm, width), lambda i: (i, 0))
    return pl.pallas_call(
        _outproj_kernel,
        grid=(rows // tm,),
        in_specs=[
            row_spec(SWA_Q_W), row_spec(DIFF_W), row_spec(D_MODEL),
            pl.BlockSpec((D_MODEL, D_MODEL), lambda i: (0, 0), pipeline_mode=pl.Buffered(1)),
            pl.BlockSpec((1, D_MODEL), lambda i: (0, 0)),
            pl.BlockSpec((D_MODEL, LANES), lambda i: (0, 0)),
        ],
        out_specs=(row_spec(D_MODEL), row_spec(HALF), row_spec(LANES)),
        out_shape=(jax.ShapeDtypeStruct((rows, D_MODEL), F32),
                   jax.ShapeDtypeStruct((rows, HALF), jnp.uint32),
                   jax.ShapeDtypeStruct((rows, LANES), F32)),
        compiler_params=pltpu.CompilerParams(dimension_semantics=("arbitrary",),
                                             vmem_limit_bytes=VMEM_LIMIT),
        name="outproj",
    )(o_s, o_d, x2, w_bf, g, w_router)


DISPATCH_TM = 1024


def _dispatch_kernel(zero_blk_ref, n_zero_ref, dest_ref, f_ref, xs_ref, zeros, sem, zsem):
    i = pl.program_id(0)

    @pl.when(i == 0)
    def _():
        zeros[...] = jnp.zeros_like(zeros)

        def zero_copy(j):
            rows = pl.ds(zero_blk_ref[j] * ROW_BLK, ROW_BLK)
            return pltpu.make_async_copy(zeros, xs_ref.at[rows], zsem.at[0])

        def start(j, c):
            zero_copy(j).start()
            return c

        def wait(j, c):
            zero_copy(j).wait()
            return c
        lax.fori_loop(0, n_zero_ref[0], start, 0)
        lax.fori_loop(0, n_zero_ref[0], wait, 0)

    def issue(r, c):
        for kk in range(TOP_K):
            pltpu.make_async_copy(f_ref.at[pl.ds(r, 1)], xs_ref.at[pl.ds(dest_ref[TOP_K * r + kk], 1)],
                                  sem.at[0]).start(priority=kk % 2)
        return c
    lax.fori_loop(0, DISPATCH_TM, issue, 0, unroll=8)
    n_rows = TOP_K * DISPATCH_TM
    pltpu.make_async_copy(xs_ref.at[pl.ds(0, n_rows)], xs_ref.at[pl.ds(0, n_rows)], sem.at[0]).wait()


def _dispatch(zero_blk, n_zero, dest_flat, f_packed, n_blocks):
    rows = f_packed.shape[0]
    return pl.pallas_call(
        _dispatch_kernel,
        grid_spec=pltpu.PrefetchScalarGridSpec(
            num_scalar_prefetch=2,
            grid=(rows // DISPATCH_TM,),
            in_specs=[
                pl.BlockSpec((TOP_K * DISPATCH_TM,), lambda i, z, n: (i,), memory_space=pltpu.SMEM),
                pl.BlockSpec((DISPATCH_TM, HALF), lambda i, z, n: (i, 0)),
            ],
            out_specs=pl.BlockSpec(memory_space=pl.ANY),
            scratch_shapes=[pltpu.VMEM((ROW_BLK, HALF), jnp.uint32),
                            pltpu.SemaphoreType.DMA((1,)),
                            pltpu.SemaphoreType.DMA((1,))],
        ),
        out_shape=jax.ShapeDtypeStruct((n_blocks * ROW_BLK, HALF), jnp.uint32),
        compiler_params=pltpu.CompilerParams(dimension_semantics=("arbitrary",),
                                             vmem_limit_bytes=VMEM_LIMIT),
        name="dispatch",
    )(zero_blk, n_zero, dest_flat, f_packed)


ITEM_BLKS = 6
W_BUFS = 3
ITEM_ROWS = ITEM_BLKS * ROW_BLK
F_TILE = 256
N_FT = -(-D_EXPERT // F_TILE)
Y_TILE = 512
N_YT = D_MODEL // Y_TILE


def _f_tile_start(k):
    return pl.multiple_of(jnp.minimum(k * F_TILE, D_EXPERT - F_TILE), LANES)


def _moe_kernel(e_ref, r0_ref, nb_ref, meta_ref, xs_ref, wg_hbm, wu_hbm, wd_hbm, y_ref,
                xraw, xb, hid, ystage, wgf, wuf, wdf, wgu, wdr, xsem, ysem, wsem):
    n_used = meta_ref[0]
    n_live = meta_ref[1]

    def for_blocks(count, fn):
        def body(j, c):
            fn(j)
            return c
        lax.fori_loop(0, count, body, 0)

    def x_copy(item, buf, j):
        src = xs_ref.at[pl.ds((r0_ref[item] + j) * ROW_BLK, ROW_BLK)]
        return pltpu.make_async_copy(src, xraw.at[buf, pl.ds(j * ROW_BLK, ROW_BLK)],
                                     xsem.at[buf, j])

    def w_copies(item, k, buf):
        e = e_ref[item]
        cols = pl.ds(_f_tile_start(k), F_TILE)
        return (pltpu.make_async_copy(wg_hbm.at[e, :, cols], wgf.at[buf], wsem.at[0, buf]),
                pltpu.make_async_copy(wu_hbm.at[e, :, cols], wuf.at[buf], wsem.at[1, buf]),
                pltpu.make_async_copy(wd_hbm.at[e, cols, :], wdf.at[buf], wsem.at[2, buf]))

    def zero_copy(j):
        dst = y_ref.at[pl.ds((n_used + j) * ROW_BLK, ROW_BLK)]
        return pltpu.make_async_copy(ystage.at[pl.ds(0, ROW_BLK)], dst, ysem.at[0])
    n_unused = y_ref.shape[0] // ROW_BLK - n_used
    ystage[0:ROW_BLK, :] = jnp.zeros((ROW_BLK, HALF), jnp.uint32)
    for_blocks(n_unused, lambda j: zero_copy(j).start())
    for_blocks(n_unused, lambda j: zero_copy(j).wait())

    for_blocks(nb_ref[0], lambda j: x_copy(0, 0, j).start())

    @pl.when(nb_ref[0] > 0)
    def _():
        for s in range(W_BUFS - 1):
            for c in w_copies(0, s, s):
                c.start()

    def item_body(i, carry):
        slot = i % 2
        nb = nb_ref[i]
        prev = jnp.maximum(i - 1, 0)
        nb_prev = jnp.where(i > 0, nb_ref[prev], 0)

        def y_copy(j):
            dst = y_ref.at[pl.ds((r0_ref[prev] + j) * ROW_BLK, ROW_BLK)]
            return pltpu.make_async_copy(ystage.at[pl.ds(j * ROW_BLK, ROW_BLK)], dst, ysem.at[0])

        for_blocks(nb_ref[i + 1], lambda j: x_copy(i + 1, 1 - slot, j).start())

        def arrive(j):
            x_copy(i, slot, j).wait()
            rows = pl.ds(pl.multiple_of(j * ROW_BLK, ROW_BLK), ROW_BLK)
            lo, hi = _unpack_bf16_pair(xraw[slot, rows, :])
            xb[rows, 0:HALF] = lo.astype(BF16)
            xb[rows, HALF:] = hi.astype(BF16)
        for_blocks(nb, arrive)

        def tile_body(k, carry):
            step = i * N_FT + k
            wbuf = step % W_BUFS
            last = k == N_FT - 1
            ahead = k + (W_BUFS - 1)
            nxt_i = jnp.where(ahead >= N_FT, i + 1, i)
            nxt_k = jnp.where(ahead >= N_FT, ahead - N_FT, ahead)

            @pl.when(nb_ref[nxt_i] > 0)
            def _():
                for c in w_copies(nxt_i, nxt_k, (step + (W_BUFS - 1)) % W_BUFS):
                    c.start()

            def cast_tiles():
                wgu[:, 0:F_TILE] = wgf[wbuf].astype(BF16)
                wgu[:, F_TILE:] = wuf[wbuf].astype(BF16)
                f_rows = pl.ds(_f_tile_start(k), F_TILE)
                for n in range(N_YT):
                    wdr[slot, n, f_rows, :] = wdf[wbuf, :, n * Y_TILE:(n + 1) * Y_TILE].astype(BF16)

            def gate_up(row0, nrows):
                rows = pl.ds(pl.multiple_of(row0, ROW_BLK), nrows)
                gu = jnp.dot(xb[rows, :], wgu[...], preferred_element_type=F32)
                gate, up = gu[:, 0:F_TILE], gu[:, F_TILE:]
                h = (gate * (1.0 / (1.0 + jnp.exp(-gate))) * up).astype(BF16)
                hid[slot, rows, pl.ds(_f_tile_start(k), F_TILE)] = h

            k_y = jnp.minimum(k, N_YT - 1)

            def down(row0, nrows):
                rows = pl.ds(pl.multiple_of(row0, ROW_BLK), nrows)
                yt = jnp.dot(hid[1 - slot, rows, :], wdr[1 - slot, k_y], preferred_element_type=F32)
                packed = _pack_bf16_pair(yt[:, 0:Y_TILE // 2], yt[:, Y_TILE // 2:])
                words = pl.ds(pl.multiple_of(k_y * (Y_TILE // 2), Y_TILE // 2), Y_TILE // 2)
                ystage[rows, words] = packed

            def for_row_tiles(n_blk, fn):
                big = 4 * ROW_BLK
                for_blocks(n_blk // 4, lambda t: fn(t * big, big))
                for bit in (2, 1):
                    @pl.when((n_blk & bit) != 0)
                    def _():
                        fn((n_blk // (2 * bit)) * (2 * bit) * ROW_BLK, bit * ROW_BLK)

            @pl.when(nb > 0)
            def _():
                for c in w_copies(i, k, wbuf):
                    c.wait()
                cast_tiles()

            for_row_tiles(nb, gate_up)
            for_row_tiles(jnp.where(k < N_YT, nb_prev, 0), down)

            @pl.when(k == N_YT)
            def _():
                for_blocks(nb_prev, lambda j: y_copy(j).start())

            @pl.when(last)
            def _():
                for_blocks(nb_prev, lambda j: y_copy(j).wait())
            return carry

        lax.fori_loop(0, N_FT, tile_body, 0)
        return carry

    lax.fori_loop(0, n_live + 1, item_body, 0)


def _moe(item_e, item_r0, item_nb, meta, xs, w_gate, w_up, w_down, n_blocks):
    return pl.pallas_call(
        _moe_kernel,
        grid_spec=pltpu.PrefetchScalarGridSpec(
            num_scalar_prefetch=4,
            grid=(1,),
            in_specs=[pl.BlockSpec(memory_space=pl.ANY)] * 4,
            out_specs=pl.BlockSpec(memory_space=pl.ANY),
            scratch_shapes=[
                pltpu.VMEM((2, ITEM_ROWS, HALF), jnp.uint32),
                pltpu.VMEM((ITEM_ROWS, D_MODEL), BF16),
                pltpu.VMEM((2, ITEM_ROWS, D_EXPERT), BF16),
                pltpu.VMEM((ITEM_ROWS, HALF), jnp.uint32),
                pltpu.VMEM((W_BUFS, D_MODEL, F_TILE), F32),
                pltpu.VMEM((W_BUFS, D_MODEL, F_TILE), F32),
                pltpu.VMEM((W_BUFS, F_TILE, D_MODEL), F32),
                pltpu.VMEM((D_MODEL, 2 * F_TILE), BF16),
                pltpu.VMEM((2, N_YT, D_EXPERT, Y_TILE), BF16),
                pltpu.SemaphoreType.DMA((2, ITEM_BLKS)),
                pltpu.SemaphoreType.DMA((1,)),
                pltpu.SemaphoreType.DMA((3, W_BUFS)),
            ],
        ),
        out_shape=jax.ShapeDtypeStruct((n_blocks * ROW_BLK, HALF), jnp.uint32),
        compiler_params=pltpu.CompilerParams(dimension_semantics=("arbitrary",),
                                             vmem_limit_bytes=VMEM_LIMIT),
        name="moe",
    )(item_e, item_r0, item_nb, meta, xs, w_gate, w_up, w_down)


def _combine_kernel(cur_ref, nxt_ref, h_ref, w_ref, g_ref, y_ref, o_ref, ya, yb, sem, *, tm):
    i = pl.program_id(0)
    n_steps = pl.num_programs(0)
    slot = i % 2

    def issue(idx_ref, buf, r):
        pltpu.make_async_copy(y_ref.at[pl.ds(idx_ref[TOP_K * r], 1)], ya.at[buf, pl.ds(r, 1)],
                              sem.at[buf]).start(priority=0)
        pltpu.make_async_copy(y_ref.at[pl.ds(idx_ref[TOP_K * r + 1], 1)], yb.at[buf, pl.ds(r, 1)],
                              sem.at[buf]).start(priority=1)

    def wait(buf):
        pltpu.make_async_copy(ya.at[buf], ya.at[buf], sem.at[buf]).wait()
        pltpu.make_async_copy(yb.at[buf], yb.at[buf], sem.at[buf]).wait()

    @pl.when(i == 0)
    def _():
        def body(r, c):
            issue(cur_ref, 0, r)
            return c
        lax.fori_loop(0, tm, body, 0, unroll=8)

    wait(slot)
    for r in range(tm):
        issue(nxt_ref, 1 - slot, r)

    wa = w_ref[:, 0:1]
    wb = w_ref[:, 1:2]
    half_t = Y_TILE // 2
    ssq = jnp.zeros((tm, 1), F32)
    for n in range(N_YT):
        words = slice(n * half_t, (n + 1) * half_t)
        a_lo, a_hi = _unpack_bf16_pair(ya[slot, :, words])
        b_lo, b_hi = _unpack_bf16_pair(yb[slot, :, words])
        lo_cols = slice(n * Y_TILE, n * Y_TILE + half_t)
        hi_cols = slice(n * Y_TILE + half_t, (n + 1) * Y_TILE)
        lo = h_ref[:, lo_cols] + wa * a_lo + wb * b_lo
        hi = h_ref[:, hi_cols] + wa * a_hi + wb * b_hi
        ssq = ssq + jnp.sum(lo * lo, axis=-1, keepdims=True) + jnp.sum(hi * hi, axis=-1, keepdims=True)
        o_ref[:, lo_cols] = lo
        o_ref[:, hi_cols] = hi
    inv = lax.rsqrt(ssq * (1.0 / D_MODEL) + RMS_EPS)
    o_ref[...] = o_ref[...] * inv * g_ref[...]

    @pl.when(i == n_steps - 1)
    def _():
        wait(1 - slot)


def _combine(dest_flat, h2, wts, g, y, tm):
    rows = h2.shape[0]
    steps = rows // tm
    idx_spec = lambda fn: pl.BlockSpec((TOP_K * tm,), fn, memory_space=pltpu.SMEM)
    return pl.pallas_call(
        functools.partial(_combine_kernel, tm=tm),
        grid=(steps,),
        in_specs=[
            idx_spec(lambda i: (i,)),
            idx_spec(lambda i: (jnp.minimum(i + 1, steps - 1),)),
            pl.BlockSpec((tm, D_MODEL), lambda i: (i, 0)),
            pl.BlockSpec((tm, LANES), lambda i: (i, 0)),
            pl.BlockSpec((1, D_MODEL), lambda i: (0, 0)),
            pl.BlockSpec(memory_space=pl.ANY),
        ],
        out_specs=pl.BlockSpec((tm, D_MODEL), lambda i: (i, 0)),
        scratch_shapes=[
            pltpu.VMEM((2, tm, HALF), jnp.uint32),
            pltpu.VMEM((2, tm, HALF), jnp.uint32),
            pltpu.SemaphoreType.DMA((2,)),
        ],
        out_shape=jax.ShapeDtypeStruct((rows, D_MODEL), F32),
        compiler_params=pltpu.CompilerParams(dimension_semantics=("arbitrary",),
                                             vmem_limit_bytes=VMEM_LIMIT),
        name="combine",
    )(dest_flat, dest_flat, h2, wts, g, y)


ROUTE_TM = 512


def _route_kernel(lg_ref, info_ref, wts_ref, tab_ref, counts, starts, carry):
    phase = pl.program_id(0)
    i = pl.program_id(1)
    tm = ROUTE_TM
    lane = lax.broadcasted_iota(jnp.int32, (tm, LANES), 1)
    lg = lg_ref[...]

    def first_lane_of_max(v):
        vmax = jnp.max(v, axis=-1, keepdims=True)
        return vmax, jnp.min(jnp.where(v == vmax, lane, LANES), axis=-1, keepdims=True)

    gl = jnp.where(lane < N_GROUPS, lg, NEG_INF)
    gmax, g_sel = first_lane_of_max(gl)
    gate = 1.0 / jnp.sum(jnp.exp(gl - gmax), axis=-1, keepdims=True)
    lo = N_GROUPS + EXPERTS_PER_GROUP * g_sel
    el = jnp.where((lane >= lo) & (lane < lo + EXPERTS_PER_GROUP), lg, NEG_INF)
    m1, l1 = first_lane_of_max(el)
    m2, l2 = first_lane_of_max(jnp.where(lane == l1, NEG_INF, el))
    ratio = jnp.exp(m2 - m1)
    w1 = gate / (1.0 + ratio)
    w2 = w1 * ratio
    e1 = l1 - N_GROUPS
    e2 = l2 - N_GROUPS
    hit1 = lane == e1
    hit2 = lane == e2
    onehot = (hit1 | hit2).astype(F32)
    tile_counts = jnp.sum(onehot, axis=0, keepdims=True)

    @pl.when((phase == 0) & (i == 0))
    def _():
        counts[...] = jnp.zeros_like(counts)

    @pl.when(phase == 0)
    def _():
        counts[...] += tile_counts

    @pl.when((phase == 1) & (i == 0))
    def _():
        blocks = jnp.floor((counts[...] + (ROW_BLK - 1)) * (1.0 / ROW_BLK))
        r = lax.broadcasted_iota(jnp.int32, (LANES, LANES), 0)
        c = lax.broadcasted_iota(jnp.int32, (LANES, LANES), 1)
        before = (r < c).astype(F32)
        first = jnp.dot(jnp.broadcast_to(blocks, (8, LANES)), before,
                        preferred_element_type=F32, precision=lax.Precision.HIGHEST)[0:1]
        starts[...] = first
        carry[...] = jnp.zeros_like(carry)
        used = jnp.sum(blocks, axis=-1, keepdims=True)
        row = lax.broadcasted_iota(jnp.int32, (8, LANES), 0)
        tab = jnp.where(row == 0, blocks, jnp.where(row == 1, first, jnp.where(row == 2, used, 0.0)))
        tab_ref[...] = tab.astype(jnp.int32)

    @pl.when(phase == 1)
    def _():
        r = lax.broadcasted_iota(jnp.int32, (tm, tm), 0)
        c = lax.broadcasted_iota(jnp.int32, (tm, tm), 1)
        earlier = (c < r).astype(BF16)
        before = jnp.dot(earlier, onehot.astype(BF16), preferred_element_type=F32)
        slot = starts[...] * ROW_BLK + carry[...] + before
        d1 = jnp.sum(jnp.where(hit1, slot, 0.0), axis=-1, keepdims=True)
        d2 = jnp.sum(jnp.where(hit2, slot, 0.0), axis=-1, keepdims=True)
        info_ref[...] = jnp.where(lane == 0, d1, jnp.where(lane == 1, d2, 0.0)).astype(jnp.int32)
        wts_ref[...] = jnp.where(lane == 0, w1, jnp.where(lane == 1, w2, 0.0))
        carry[...] += tile_counts


def _route(logits):
    rows = logits.shape[0]
    tile = lambda p, i: (i * p, 0)
    return pl.pallas_call(
        _route_kernel,
        grid=(2, rows // ROUTE_TM),
        in_specs=[pl.BlockSpec((ROUTE_TM, LANES), lambda p, i: (i, 0))],
        out_specs=(pl.BlockSpec((ROUTE_TM, LANES), tile), pl.BlockSpec((ROUTE_TM, LANES), tile),
                   pl.BlockSpec((8, LANES), lambda p, i: (0, 0))),
        out_shape=(jax.ShapeDtypeStruct((rows, LANES), jnp.int32),
                   jax.ShapeDtypeStruct((rows, LANES), F32),
                   jax.ShapeDtypeStruct((8, LANES), jnp.int32)),
        scratch_shapes=[pltpu.VMEM((1, LANES), F32)] * 3,
        compiler_params=pltpu.CompilerParams(dimension_semantics=("arbitrary", "arbitrary"),
                                             vmem_limit_bytes=VMEM_LIMIT),
        name="route",
    )(logits)


def _work_items(blocks_e, blk_start, n_items):
    items_e = (blocks_e + ITEM_BLKS - 1) // ITEM_BLKS
    item_end = jnp.cumsum(items_e)
    item_start = item_end - items_e
    total = item_end[-1]
    idx = jnp.arange(n_items, dtype=jnp.int32)
    live = idx < total
    ref = jnp.minimum(idx, total - 1)
    e = jnp.searchsorted(item_end, ref, side='right').astype(jnp.int32)
    j = ref - item_start[e]
    r0 = blk_start[e] + ITEM_BLKS * j
    nb = jnp.where(live, jnp.clip(blocks_e[e] - ITEM_BLKS * j, 0, ITEM_BLKS), 0)
    return e, r0.astype(jnp.int32), nb.astype(jnp.int32), total.astype(jnp.int32)


def kernel(x, meta_tokens, attn_norm_g, w_in, swa_sinks, diff_lambda, diff_subln_g, w_out,
           ffn_norm_g, w_router_group, w_router_expert, w_gate, w_up, w_down, final_norm_g):
    batch, seq, _ = x.shape
    n_tok = batch * seq
    nblk = seq // Q_BLOCK
    x2 = x.reshape(n_tok, D_MODEL)

    w_in_bf = w_in[0].astype(BF16)
    g_attn = attn_norm_g[0].reshape(1, D_MODEL)
    cos_x, sin_x = _rope_tables(N_META + jnp.arange(seq))
    cos_m, sin_m = _rope_tables(jnp.maximum(jnp.arange(Q_BLOCK) - PAD_FRONT, 0))
    qs, ks, vs, qd, kd, vd = _inproj(x2, g_attn, w_in_bf, cos_x, sin_x, 512, seq)
    meta_blk = jnp.concatenate(
        [jnp.zeros((PAD_FRONT, D_MODEL), x.dtype), meta_tokens.astype(x.dtype)], axis=0)
    _, ks_m, vs_m, _, kd_m, vd_m = _inproj(meta_blk, g_attn, w_in_bf, cos_m, sin_m, Q_BLOCK, Q_BLOCK)

    o_s = _swa(swa_sinks[0].astype(F32), qs, ks, vs, ks_m, vs_m, batch, nblk)
    o_d = _diff(diff_lambda[0].astype(F32), diff_subln_g[0].astype(F32).reshape(1, LANES),
                qd, kd, vd, kd_m, vd_m, batch, seq)

    w_router = jnp.concatenate(
        [w_router_group[0], w_router_expert[0],
         jnp.zeros((D_MODEL, LANES - N_GROUPS - N_EXPERTS), F32)], axis=1)
    h2, f_packed, logits = _outproj(o_s, o_d, x2, w_out[0].astype(BF16),
                                    ffn_norm_g[0].reshape(1, D_MODEL), w_router, 512)

    info, wts, tab = _route(logits)
    dest_flat = info[:, :TOP_K].reshape(-1)
    blocks_e, blk_start, n_used = tab[0, :N_EXPERTS], tab[1, :N_EXPERTS], tab[2, 0:1]
    n_blocks = n_tok * TOP_K // ROW_BLK + N_EXPERTS
    n_items = n_blocks // ITEM_BLKS + N_EXPERTS
    blk = jnp.arange(n_blocks, dtype=jnp.int32)
    is_last = jnp.any((blk[:, None] + 1 == (blk_start + blocks_e)[None, :])
                      & (blocks_e > 0)[None, :], axis=1)
    need_zero = is_last | (blk >= n_used[0])
    zero_blk = jnp.nonzero(need_zero, size=2 * N_EXPERTS, fill_value=0)[0].astype(jnp.int32)
    n_zero = jnp.sum(need_zero).astype(jnp.int32).reshape(1)
    xs = _dispatch(zero_blk, n_zero, dest_flat, f_packed, n_blocks)
    item_e, item_r0, item_nb, n_live = _work_items(blocks_e, blk_start, n_items)
    y = _moe(item_e, item_r0, item_nb, jnp.concatenate([n_used, n_live.reshape(1)]), xs,
             w_gate[0], w_up[0], w_down[0], n_blocks)

    out = _combine(dest_flat, h2, wts, final_norm_g.reshape(1, D_MODEL), y, 256)
    return out.reshape(batch, seq, D_MODEL)
```

```python
import functools
import math

import jax
import jax.numpy as jnp
from jax import lax
from jax.experimental import pallas as pl
from jax.experimental.pallas import tpu as pltpu

D_MODEL = 2048
N_META = 16
CHUNK = 64
Q_BLOCK = 128
PAD_FRONT = Q_BLOCK - N_META
HEAD_DIM = 64
ROPE_THETA = 10000.0
SWA_Q_HEADS = 16
SWA_KV_HEADS = 2
DIFF_HEADS = 8
SWA_Q_W = SWA_Q_HEADS * HEAD_DIM
SWA_KV_W = SWA_KV_HEADS * HEAD_DIM
DIFF_W = DIFF_HEADS * 2 * HEAD_DIM
IN_WIDTH = SWA_Q_W + 2 * SWA_KV_W + 3 * DIFF_W
N_GROUPS = 8
EXPERTS_PER_GROUP = 8
N_EXPERTS = N_GROUPS * EXPERTS_PER_GROUP
TOP_K = 2
D_EXPERT = 1408
RMS_EPS = 1e-6
NEG_INF = -1e30
LAM_INIT = 0.8 - 0.6 * math.exp(-0.3 * 0)

LANES = 128
ROW_BLK = 128
VMEM_LIMIT = 56 * 1024 * 1024

BF16 = jnp.bfloat16
F32 = jnp.float32


def _nt_dot(a, b):
    return lax.dot_general(a, b, (((1,), (1,)), ((), ())), preferred_element_type=F32)


LOG2_E = math.log2(math.e)
Q_SCALE = HEAD_DIM ** -0.5 * LOG2_E
_SEGS = (
    ("qs", 0, 512, True, Q_SCALE), ("qs", 512, 512, True, Q_SCALE),
    ("ks", 1024, 128, True, 1.0), ("vs", 1152, 128, False, 1.0),
    ("qd", 1280, 512, True, Q_SCALE), ("qd", 1792, 512, True, Q_SCALE),
    ("kd", 2304, 512, True, 1.0), ("kd", 2816, 512, True, 1.0),
    ("vd", 3328, 512, False, 1.0), ("vd", 3840, 512, False, 1.0),
)
_SEG_BASE = {"qs": 0, "ks": 1024, "vs": 1152, "qd": 1280, "kd": 2304, "vd": 3328}


def _inproj_kernel(x_ref, g_ref, w_ref, cos_ref, sin_ref,
                   qs_ref, ks_ref, vs_ref, qd_ref, kd_ref, vd_ref):
    x = x_ref[...]
    ms = jnp.mean(x * x, axis=-1, keepdims=True)
    a = (x * lax.rsqrt(ms + RMS_EPS) * g_ref[...]).astype(BF16)
    cos = cos_ref[...]
    sin = sin_ref[...]
    lane = lax.broadcasted_iota(jnp.int32, cos.shape, 1)
    first_half = (lane % HEAD_DIM) < (HEAD_DIM // 2)
    low_head = lane < HEAD_DIM
    outs = {"qs": qs_ref, "ks": ks_ref, "vs": vs_ref, "qd": qd_ref, "kd": kd_ref, "vd": vd_ref}
    for name, c0, width, rotary, scale in _SEGS:
        y = jnp.dot(a, w_ref[:, c0:c0 + width], preferred_element_type=F32)
        o0 = c0 - _SEG_BASE[name]
        for t in range(width // LANES):
            yt = y[:, t * LANES:(t + 1) * LANES]
            if rotary:
                partner = jnp.where(first_half, pltpu.roll(yt, LANES - HEAD_DIM // 2, 1),
                                    pltpu.roll(yt, HEAD_DIM // 2, 1))
                yt = yt * cos + partner * sin
            if scale != 1.0:
                yt = yt * scale
            if name in ("ks", "vs"):
                other = pltpu.roll(yt, HEAD_DIM, 1)
                outs[name][:, 0:LANES] = jnp.where(low_head, yt, other).astype(BF16)
                outs[name][:, LANES:2 * LANES] = jnp.where(low_head, other, yt).astype(BF16)
            else:
                outs[name][:, o0 + t * LANES:o0 + (t + 1) * LANES] = yt.astype(BF16)


def _inproj(x2, g, w_bf, cos_t, sin_t, tm, rows_per_seq):
    rows = x2.shape[0]
    nseq = rows_per_seq // tm
    row_spec = lambda width: pl.BlockSpec((tm, width), lambda i: (i, 0))
    out_shapes = (
        jax.ShapeDtypeStruct((rows, SWA_Q_W), BF16), jax.ShapeDtypeStruct((rows, 2 * SWA_KV_W), BF16),
        jax.ShapeDtypeStruct((rows, 2 * SWA_KV_W), BF16), jax.ShapeDtypeStruct((rows, DIFF_W), BF16),
        jax.ShapeDtypeStruct((rows, DIFF_W), BF16), jax.ShapeDtypeStruct((rows, DIFF_W), BF16),
    )
    return pl.pallas_call(
        _inproj_kernel,
        grid=(rows // tm,),
        in_specs=[
            row_spec(D_MODEL),
            pl.BlockSpec((1, D_MODEL), lambda i: (0, 0)),
            pl.BlockSpec((D_MODEL, IN_WIDTH), lambda i: (0, 0), pipeline_mode=pl.Buffered(1)),
            pl.BlockSpec((tm, LANES), lambda i: (i % nseq, 0)),
            pl.BlockSpec((tm, LANES), lambda i: (i % nseq, 0)),
        ],
        out_specs=(row_spec(SWA_Q_W), row_spec(2 * SWA_KV_W), row_spec(2 * SWA_KV_W),
                   row_spec(DIFF_W), row_spec(DIFF_W), row_spec(DIFF_W)),
        out_shape=out_shapes,
        compiler_params=pltpu.CompilerParams(dimension_semantics=("arbitrary",),
                                             vmem_limit_bytes=VMEM_LIMIT),
        name="inproj",
    )(x2, g, w_bf, cos_t, sin_t)


def _rope_tables(positions):
    inv = 1.0 / (ROPE_THETA ** (jnp.arange(0, HEAD_DIM, 2, dtype=F32) / HEAD_DIM))
    ang = positions.astype(F32)[:, None] * inv[None, :]
    cos, sin = jnp.cos(ang), jnp.sin(ang)
    cos_t = jnp.tile(cos, (1, 2 * LANES // HEAD_DIM))
    sin_t = jnp.tile(jnp.concatenate([-sin, sin], axis=1), (1, LANES // HEAD_DIM))
    return cos_t, sin_t


SWA_SLAB = 256
SWA_BAND = 3 * CHUNK + N_META


def _swa_kernel(sink_ref, q_ref, kp_ref, kc_ref, km_ref, vp_ref, vc_ref, vm_ref, o_ref):
    i = pl.program_id(1)
    pairs = SWA_Q_HEADS // SWA_KV_HEADS // 2
    m_rows = pairs * 2 * CHUNK
    col = lax.broadcasted_iota(jnp.int32, (m_rows, SWA_SLAB), 1)
    row = lax.broadcasted_iota(jnp.int32, (m_rows, 1), 0)
    lane = lax.broadcasted_iota(jnp.int32, (CHUNK, LANES), 1)
    low_head = lane < HEAD_DIM
    zero = jnp.zeros((CHUNK, LANES), BF16)
    pad = jnp.zeros((SWA_SLAB - SWA_BAND, LANES), BF16)
    meta_rows = slice(PAD_FRONT, Q_BLOCK)
    for g in range(SWA_KV_HEADS):
        gl = slice(g * LANES, (g + 1) * LANES)
        sink = jnp.zeros((m_rows, 1), F32)
        for h in range(2 * pairs):
            sink = jnp.where(row // CHUNK == h, sink_ref[2 * g * pairs + h], sink)
        sink = sink * LOG2_E
        for half in range(2):
            q_rows = slice(half * CHUNK, (half + 1) * CHUNK)
            if half == 0:
                pieces = lambda p, c, m: [p[:, gl], c[0:CHUNK, gl], m[meta_rows, gl], pad]
                n_prev = 2 * CHUNK
            else:
                pieces = lambda p, c, m: [p[CHUNK:, gl], c[:, gl], m[meta_rows, gl], pad]
                n_prev = CHUNK
            k_slab = jnp.concatenate(pieces(kp_ref, kc_ref, km_ref), axis=0)
            v_slab = jnp.concatenate(pieces(vp_ref, vc_ref, vm_ref), axis=0)
            valid = (col < SWA_BAND) & ((i > 0) | (col >= n_prev))
            q2 = []
            for c in range(g * pairs, (g + 1) * pairs):
                qp = q_ref[q_rows, c * LANES:(c + 1) * LANES]
                q2 += [jnp.where(low_head, qp, zero), jnp.where(low_head, zero, qp)]
            s = jnp.where(valid, _nt_dot(jnp.concatenate(q2, axis=0), k_slab), NEG_INF)
            m = jnp.maximum(jnp.max(s, axis=-1, keepdims=True), sink)
            p = jnp.exp2(s - m)
            denom = jnp.sum(p, axis=-1, keepdims=True) + jnp.exp2(sink - m)
            o2 = jnp.dot(p.astype(BF16), v_slab, preferred_element_type=F32) / denom
            for j, c in enumerate(range(g * pairs, (g + 1) * pairs)):
                lo = o2[(2 * j) * CHUNK:(2 * j + 1) * CHUNK]
                hi = o2[(2 * j + 1) * CHUNK:(2 * j + 2) * CHUNK]
                o_ref[q_rows, c * LANES:(c + 1) * LANES] = jnp.where(low_head, lo, hi).astype(BF16)


def _swa(sinks, qs, ks, vs, ks_meta, vs_meta, batch, nblk):
    rows = qs.shape[0]
    cur = lambda b, i: (b * nblk + i, 0)
    prev = lambda b, i: (b * nblk + jnp.maximum(i - 1, 0), 0)
    kv_w = 2 * SWA_KV_W
    return pl.pallas_call(
        _swa_kernel,
        grid=(batch, nblk),
        in_specs=[
            pl.BlockSpec(memory_space=pltpu.SMEM),
            pl.BlockSpec((Q_BLOCK, SWA_Q_W), cur),
            pl.BlockSpec((Q_BLOCK, kv_w), prev), pl.BlockSpec((Q_BLOCK, kv_w), cur),
            pl.BlockSpec((Q_BLOCK, kv_w), lambda b, i: (0, 0)),
            pl.BlockSpec((Q_BLOCK, kv_w), prev), pl.BlockSpec((Q_BLOCK, kv_w), cur),
            pl.BlockSpec((Q_BLOCK, kv_w), lambda b, i: (0, 0)),
        ],
        out_specs=pl.BlockSpec((Q_BLOCK, SWA_Q_W), cur),
        out_shape=jax.ShapeDtypeStruct((rows, SWA_Q_W), BF16),
        compiler_params=pltpu.CompilerParams(dimension_semantics=("arbitrary", "arbitrary"),
                                             vmem_limit_bytes=VMEM_LIMIT),
        name="swa",
    )(sinks, qs, ks, ks, ks_meta, vs, vs, vs_meta)


DIFF_TQ = 256
DIFF_TK = 512
DIFF_HEADS_PER_STEP = 2


def _diff_kernel(lam_ref, g_ref, q_ref, k_ref, v_ref, km_ref, vm_ref, o_ref, *, seq):
    dl = lam_ref[...]
    lam = (jnp.exp(jnp.sum(dl[0:1] * dl[1:2], axis=-1, keepdims=True))
           - jnp.exp(jnp.sum(dl[2:3] * dl[3:4], axis=-1, keepdims=True)) + LAM_INIT)
    lane = lax.broadcasted_iota(jnp.int32, (DIFF_TQ, LANES), 1)
    low_map = lane < HEAD_DIM
    zero = jnp.zeros((DIFF_TQ, LANES), BF16)

    def last_mask(n_keys):
        shape = (2 * DIFF_TQ, Q_BLOCK + n_keys)
        col = lax.broadcasted_iota(jnp.int32, shape, 1)
        q_chunk = (lax.broadcasted_iota(jnp.int32, shape, 0) % DIFF_TQ) // CHUNK
        first_chunk = (DIFF_TQ - n_keys) // CHUNK
        k_chunk = (col - Q_BLOCK) // CHUNK + first_chunk
        return ((col >= PAD_FRONT) & (col < Q_BLOCK)) | ((col >= Q_BLOCK) & (k_chunk <= q_chunk))

    def step(q2, k_t, v_t, mask, state):
        s = _nt_dot(q2, k_t)
        if mask is not None:
            s = jnp.where(mask, s, NEG_INF)
        smax = jnp.max(s, axis=-1, keepdims=True)
        if state is None:
            p = jnp.exp2(s - smax)
            return smax, jnp.sum(p, axis=-1, keepdims=True), jnp.dot(
                p.astype(BF16), v_t, preferred_element_type=F32)
        m_old, l_old, acc = state
        m_new = jnp.maximum(m_old, smax)
        alpha = jnp.exp2(m_old - m_new)
        p = jnp.exp2(s - m_new)
        return (m_new, alpha * l_old + jnp.sum(p, axis=-1, keepdims=True),
                alpha * acc + jnp.dot(p.astype(BF16), v_t, preferred_element_type=F32))

    heads = [slice(h * LANES, (h + 1) * LANES) for h in range(DIFF_HEADS_PER_STEP)]

    tiles_per_k = DIFF_TK // DIFF_TQ
    masks = [last_mask((t + 1) * DIFF_TQ) for t in range(tiles_per_k)]

    def q_tile(n_full, t):
        if n_full is None:
            aligned = lambda v: v
            q0 = t * DIFF_TQ
        else:
            aligned = lambda v: pl.multiple_of(v, DIFF_TQ)
            q0 = aligned((n_full * tiles_per_k + t) * DIFF_TQ)
        q2s = []
        for hs in heads:
            q = q_ref[pl.ds(q0, DIFF_TQ), hs]
            q2s.append(jnp.concatenate(
                [jnp.where(low_map, q, zero), jnp.where(low_map, zero, q)], axis=0))

        def full_tiles(kj, states):
            rows = pl.ds(pl.multiple_of(kj * DIFF_TK, DIFF_TK), DIFF_TK)
            out = ()
            for h, hs in enumerate(heads):
                st = None if states is None else states[3 * h:3 * h + 3]
                out = out + step(q2s[h], k_ref[rows, hs], v_ref[rows, hs], None, st)
            return out

        if n_full is None:
            states = None
        else:
            states = lax.fori_loop(1, n_full, full_tiles, full_tiles(0, None))

        n_keys = (t + 1) * DIFF_TQ
        rows = pl.ds(aligned(q0 + DIFF_TQ - n_keys), n_keys)
        for h, hs in enumerate(heads):
            st = None if states is None else states[3 * h:3 * h + 3]
            k_t = jnp.concatenate([km_ref[:, hs], k_ref[rows, hs]], axis=0)
            v_t = jnp.concatenate([vm_ref[:, hs], v_ref[rows, hs]], axis=0)
            _, l, acc = step(q2s[h], k_t, v_t, masks[t], st)
            o = acc / l
            o = o[:DIFF_TQ] - lam * o[DIFF_TQ:]
            o = o * lax.rsqrt(jnp.mean(o * o, axis=-1, keepdims=True) + RMS_EPS)
            o_ref[pl.ds(q0, DIFF_TQ), hs] = (o * g_ref[...] * (1.0 - LAM_INIT)).astype(BF16)

    def q_group(n_full, carry):
        for t in range(tiles_per_k):
            q_tile(n_full, t)
        return carry

    q_group(None, 0)
    lax.fori_loop(1, seq // DIFF_TK, q_group, 0)


def _diff(diff_lambda, subln_g, qd, kd, vd, kd_meta, vd_meta, batch, seq):
    rows = qd.shape[0]
    head = lambda b, h: (b, h)
    meta = lambda b, h: (0, h)
    width = DIFF_HEADS_PER_STEP * LANES
    return pl.pallas_call(
        functools.partial(_diff_kernel, seq=seq),
        grid=(batch, DIFF_HEADS // DIFF_HEADS_PER_STEP),
        in_specs=[
            pl.BlockSpec((4, HEAD_DIM), lambda b, h: (0, 0)),
            pl.BlockSpec((1, LANES), lambda b, h: (0, 0)),
            pl.BlockSpec((seq, width), head), pl.BlockSpec((seq, width), head),
            pl.BlockSpec((seq, width), head),
            pl.BlockSpec((Q_BLOCK, width), meta), pl.BlockSpec((Q_BLOCK, width), meta),
        ],
        out_specs=pl.BlockSpec((seq, width), head),
        out_shape=jax.ShapeDtypeStruct((rows, DIFF_W), BF16),
        compiler_params=pltpu.CompilerParams(dimension_semantics=("arbitrary", "arbitrary"),
                                             vmem_limit_bytes=VMEM_LIMIT),
        name="diffattn",
    )(diff_lambda, subln_g, qd, kd, vd, kd_meta, vd_meta)


HALF = D_MODEL // 2


def _pack_bf16_pair(a, b):
    hi = pltpu.bitcast(a.astype(BF16).astype(F32), jnp.uint32)
    lo = pltpu.bitcast(b.astype(BF16).astype(F32), jnp.uint32)
    return hi | (lo >> 16)


def _unpack_bf16_pair(w):
    a = pltpu.bitcast(w & jnp.uint32(0xFFFF0000), F32)
    b = pltpu.bitcast(w << 16, F32)
    return a, b


OUT_CHUNK = 512


def _outproj_kernel(os_ref, od_ref, x_ref, w_ref, g_ref, wr_ref, h_ref, fp_ref, lg_ref):
    tm = x_ref.shape[0]
    ssq = jnp.zeros((tm, 1), F32)
    for c in range(D_MODEL // OUT_CHUNK):
        cols = slice(c * OUT_CHUNK, (c + 1) * OUT_CHUNK)
        y = jnp.dot(os_ref[...], w_ref[0:SWA_Q_W, cols], preferred_element_type=F32)
        y = y + jnp.dot(od_ref[...], w_ref[SWA_Q_W:, cols], preferred_element_type=F32)
        h = x_ref[:, cols] + y
        h_ref[:, cols] = h
        ssq = ssq + jnp.sum(h * h, axis=-1, keepdims=True)
    inv = lax.rsqrt(ssq * (1.0 / D_MODEL) + RMS_EPS)
    lg = jnp.zeros((tm, LANES), F32)
    for c in range(HALF // OUT_CHUNK):
        ca = slice(c * OUT_CHUNK, (c + 1) * OUT_CHUNK)
        cb = slice(HALF + c * OUT_CHUNK, HALF + (c + 1) * OUT_CHUNK)
        fa = h_ref[:, ca] * inv * g_ref[:, ca]
        fb = h_ref[:, cb] * inv * g_ref[:, cb]
        fp_ref[:, ca] = _pack_bf16_pair(fa, fb)
        for f, cc in ((fa, ca), (fb, cb)):
            f_hi = f.astype(BF16)
            f_lo = (f - f_hi.astype(F32)).astype(BF16)
            w = wr_ref[cc, :]
            w_hi = w.astype(BF16)
            w_lo = (w - w_hi.astype(F32)).astype(BF16)
            p = jnp.dot(f_hi, jnp.concatenate([w_hi, w_lo], axis=1), preferred_element_type=F32)
            lg = lg + p[:, 0:LANES] + p[:, LANES:] + jnp.dot(f_lo, w_hi, preferred_element_type=F32)
    lg_ref[...] = lg


def _outproj(o_s, o_d, x2, w_bf, g, w_router, tm):
    rows = x2.shape[0]
    row_spec = lambda width: pl.BlockSpec((tm, width), lambda i: (i, 0))
    return pl.pallas_call(
        _outproj_kernel,
        grid=(rows // tm,),
        in_specs=[
            row_spec(SWA_Q_W), row_spec(DIFF_W), row_spec(D_MODEL),
            pl.BlockSpec((D_MODEL, D_MODEL), lambda i: (0, 0), pipeline_mode=pl.Buffered(1)),
            pl.BlockSpec((1, D_MODEL), lambda i: (0, 0)),
            pl.BlockSpec((D_MODEL, LANES), lambda i: (0, 0)),
        ],
        out_specs=(row_spec(D_MODEL), row_spec(HALF), row_spec(LANES)),
        out_shape=(jax.ShapeDtypeStruct((rows, D_MODEL), F32),
                   jax.ShapeDtypeStruct((rows, HALF), jnp.uint32),
                   jax.ShapeDtypeStruct((rows, LANES), F32)),
        compiler_params=pltpu.CompilerParams(dimension_semantics=("arbitrary",),
                                             vmem_limit_bytes=VMEM_LIMIT),
        name="outproj",
    )(o_s, o_d, x2, w_bf, g, w_router)


DISPATCH_TM = 1024


def _dispatch_kernel(zero_blk_ref, n_zero_ref, dest_ref, f_ref, xs_ref, zeros, sem, zsem):
    i = pl.program_id(0)

    @pl.when(i == 0)
    def _():
        zeros[...] = jnp.zeros_like(zeros)

        def zero_copy(j):
            rows = pl.ds(zero_blk_ref[j] * ROW_BLK, ROW_BLK)
            return pltpu.make_async_copy(zeros, xs_ref.at[rows], zsem.at[0])

        def start(j, c):
            zero_copy(j).start()
            return c

        def wait(j, c):
            zero_copy(j).wait()
            return c
        lax.fori_loop(0, n_zero_ref[0], start, 0)
        lax.fori_loop(0, n_zero_ref[0], wait, 0)

    def issue(r, c):
        for kk in range(TOP_K):
            pltpu.make_async_copy(f_ref.at[pl.ds(r, 1)], xs_ref.at[pl.ds(dest_ref[TOP_K * r + kk], 1)],
                                  sem.at[0]).start(priority=kk % 2)
        return c
    lax.fori_loop(0, DISPATCH_TM, issue, 0, unroll=8)
    n_rows = TOP_K * DISPATCH_TM
    pltpu.make_async_copy(xs_ref.at[pl.ds(0, n_rows)], xs_ref.at[pl.ds(0, n_rows)], sem.at[0]).wait()


def _dispatch(zero_blk, n_zero, dest_flat, f_packed, n_blocks):
    rows = f_packed.shape[0]
    return pl.pallas_call(
        _dispatch_kernel,
        grid_spec=pltpu.PrefetchScalarGridSpec(
            num_scalar_prefetch=2,
            grid=(rows // DISPATCH_TM,),
            in_specs=[
                pl.BlockSpec((TOP_K * DISPATCH_TM,), lambda i, z, n: (i,), memory_space=pltpu.SMEM),
                pl.BlockSpec((DISPATCH_TM, HALF), lambda i, z, n: (i, 0)),
            ],
            out_specs=pl.BlockSpec(memory_space=pl.ANY),
            scratch_shapes=[pltpu.VMEM((ROW_BLK, HALF), jnp.uint32),
                            pltpu.SemaphoreType.DMA((1,)),
                            pltpu.SemaphoreType.DMA((1,))],
        ),
        out_shape=jax.ShapeDtypeStruct((n_blocks * ROW_BLK, HALF), jnp.uint32),
        compiler_params=pltpu.CompilerParams(dimension_semantics=("arbitrary",),
                                             vmem_limit_bytes=VMEM_LIMIT),
        name="dispatch",
    )(zero_blk, n_zero, dest_flat, f_packed)


ITEM_BLKS = 6
W_BUFS = 3
ITEM_ROWS = ITEM_BLKS * ROW_BLK
F_TILE = 256
N_FT = -(-D_EXPERT // F_TILE)
Y_TILE = 512
N_YT = D_MODEL // Y_TILE


def _f_tile_start(k):
    return pl.multiple_of(jnp.minimum(k * F_TILE, D_EXPERT - F_TILE), LANES)


def _moe_kernel(e_ref, r0_ref, nb_ref, meta_ref, xs_ref, wg_hbm, wu_hbm, wd_hbm, y_ref,
                xraw, xb, hid, ystage, wgf, wuf, wdf, wgu, wdr, xsem, ysem, wsem):
    n_used = meta_ref[0]
    n_live = meta_ref[1]

    def for_blocks(count, fn):
        def body(j, c):
            fn(j)
            return c
        lax.fori_loop(0, count, body, 0)

    def x_copy(item, buf, j):
        src = xs_ref.at[pl.ds((r0_ref[item] + j) * ROW_BLK, ROW_BLK)]
        return pltpu.make_async_copy(src, xraw.at[buf, pl.ds(j * ROW_BLK, ROW_BLK)],
                                     xsem.at[buf, j])

    def w_copies(item, k, buf):
        e = e_ref[item]
        cols = pl.ds(_f_tile_start(k), F_TILE)
        return (pltpu.make_async_copy(wg_hbm.at[e, :, cols], wgf.at[buf], wsem.at[0, buf]),
                pltpu.make_async_copy(wu_hbm.at[e, :, cols], wuf.at[buf], wsem.at[1, buf]),
                pltpu.make_async_copy(wd_hbm.at[e, cols, :], wdf.at[buf], wsem.at[2, buf]))

    def zero_copy(j):
        dst = y_ref.at[pl.ds((n_used + j) * ROW_BLK, ROW_BLK)]
        return pltpu.make_async_copy(ystage.at[pl.ds(0, ROW_BLK)], dst, ysem.at[0])
    n_unused = y_ref.shape[0] // ROW_BLK - n_used
    ystage[0:ROW_BLK, :] = jnp.zeros((ROW_BLK, HALF), jnp.uint32)
    for_blocks(n_unused, lambda j: zero_copy(j).start())
    for_blocks(n_unused, lambda j: zero_copy(j).wait())

    for_blocks(nb_ref[0], lambda j: x_copy(0, 0, j).start())

    @pl.when(nb_ref[0] > 0)
    def _():
        for s in range(W_BUFS - 1):
            for c in w_copies(0, s, s):
                c.start()

    def item_body(i, carry):
        slot = i % 2
        nb = nb_ref[i]
        prev = jnp.maximum(i - 1, 0)
        nb_prev = jnp.where(i > 0, nb_ref[prev], 0)

        def y_copy(j):
            dst = y_ref.at[pl.ds((r0_ref[prev] + j) * ROW_BLK, ROW_BLK)]
            return pltpu.make_async_copy(ystage.at[pl.ds(j * ROW_BLK, ROW_BLK)], dst, ysem.at[0])

        for_blocks(nb_ref[i + 1], lambda j: x_copy(i + 1, 1 - slot, j).start())

        def arrive(j):
            x_copy(i, slot, j).wait()
            rows = pl.ds(pl.multiple_of(j * ROW_BLK, ROW_BLK), ROW_BLK)
            lo, hi = _unpack_bf16_pair(xraw[slot, rows, :])
            xb[rows, 0:HALF] = lo.astype(BF16)
            xb[rows, HALF:] = hi.astype(BF16)
        for_blocks(nb, arrive)

        def tile_body(k, carry):
            step = i * N_FT + k
            wbuf = step % W_BUFS
            last = k == N_FT - 1
            ahead = k + (W_BUFS - 1)
            nxt_i = jnp.where(ahead >= N_FT, i + 1, i)
            nxt_k = jnp.where(ahead >= N_FT, ahead - N_FT, ahead)

            @pl.when(nb_ref[nxt_i] > 0)
            def _():
                for c in w_copies(nxt_i, nxt_k, (step + (W_BUFS - 1)) % W_BUFS):
                    c.start()

            def cast_tiles():
                wgu[:, 0:F_TILE] = wgf[wbuf].astype(BF16)
                wgu[:, F_TILE:] = wuf[wbuf].astype(BF16)
                f_rows = pl.ds(_f_tile_start(k), F_TILE)
                for n in range(N_YT):
                    wdr[slot, n, f_rows, :] = wdf[wbuf, :, n * Y_TILE:(n + 1) * Y_TILE].astype(BF16)

            def gate_up(row0, nrows):
                rows = pl.ds(pl.multiple_of(row0, ROW_BLK), nrows)
                gu = jnp.dot(xb[rows, :], wgu[...], preferred_element_type=F32)
                gate, up = gu[:, 0:F_TILE], gu[:, F_TILE:]
                h = (gate * (1.0 / (1.0 + jnp.exp(-gate))) * up).astype(BF16)
                hid[slot, rows, pl.ds(_f_tile_start(k), F_TILE)] = h

            k_y = jnp.minimum(k, N_YT - 1)

            def down(row0, nrows):
                rows = pl.ds(pl.multiple_of(row0, ROW_BLK), nrows)
                yt = jnp.dot(hid[1 - slot, rows, :], wdr[1 - slot, k_y], preferred_element_type=F32)
                packed = _pack_bf16_pair(yt[:, 0:Y_TILE // 2], yt[:, Y_TILE // 2:])
                words = pl.ds(pl.multiple_of(k_y * (Y_TILE // 2), Y_TILE // 2), Y_TILE // 2)
                ystage[rows, words] = packed

            def for_row_tiles(n_blk, fn):
                for whole in range(1, ITEM_BLKS + 1):
                    @pl.when(n_blk == whole)
                    def _():
                        fn(0, whole * ROW_BLK)

            @pl.when(nb > 0)
            def _():
                for c in w_copies(i, k, wbuf):
                    c.wait()
                cast_tiles()

            for_row_tiles(nb, gate_up)
            for_row_tiles(jnp.where(k < N_YT, nb_prev, 0), down)

            @pl.when(k == N_YT)
            def _():
                for_blocks(nb_prev, lambda j: y_copy(j).start())

            @pl.when(last)
            def _():
                for_blocks(nb_prev, lambda j: y_copy(j).wait())
            return carry

        lax.fori_loop(0, N_FT, tile_body, 0)
        return carry

    lax.fori_loop(0, n_live + 1, item_body, 0)


def _moe(item_e, item_r0, item_nb, meta, xs, w_gate, w_up, w_down, n_blocks):
    return pl.pallas_call(
        _moe_kernel,
        grid_spec=pltpu.PrefetchScalarGridSpec(
            num_scalar_prefetch=4,
            grid=(1,),
            in_specs=[pl.BlockSpec(memory_space=pl.ANY)] * 4,
            out_specs=pl.BlockSpec(memory_space=pl.ANY),
            scratch_shapes=[
                pltpu.VMEM((2, ITEM_ROWS, HALF), jnp.uint32),
                pltpu.VMEM((ITEM_ROWS, D_MODEL), BF16),
                pltpu.VMEM((2, ITEM_ROWS, D_EXPERT), BF16),
                pltpu.VMEM((ITEM_ROWS, HALF), jnp.uint32),
                pltpu.VMEM((W_BUFS, D_MODEL, F_TILE), F32),
                pltpu.VMEM((W_BUFS, D_MODEL, F_TILE), F32),
                pltpu.VMEM((W_BUFS, F_TILE, D_MODEL), F32),
                pltpu.VMEM((D_MODEL, 2 * F_TILE), BF16),
                pltpu.VMEM((2, N_YT, D_EXPERT, Y_TILE), BF16),
                pltpu.SemaphoreType.DMA((2, ITEM_BLKS)),
                pltpu.SemaphoreType.DMA((1,)),
                pltpu.SemaphoreType.DMA((3, W_BUFS)),
            ],
        ),
        out_shape=jax.ShapeDtypeStruct((n_blocks * ROW_BLK, HALF), jnp.uint32),
        compiler_params=pltpu.CompilerParams(dimension_semantics=("arbitrary",),
                                             vmem_limit_bytes=VMEM_LIMIT),
        name="moe",
    )(item_e, item_r0, item_nb, meta, xs, w_gate, w_up, w_down)


def _combine_kernel(cur_ref, nxt_ref, h_ref, w_ref, g_ref, y_ref, o_ref, ya, yb, sem, *, tm):
    i = pl.program_id(0)
    n_steps = pl.num_programs(0)
    slot = i % 2

    def issue(idx_ref, buf, r):
        pltpu.make_async_copy(y_ref.at[pl.ds(idx_ref[TOP_K * r], 1)], ya.at[buf, pl.ds(r, 1)],
                              sem.at[buf]).start(priority=0)
        pltpu.make_async_copy(y_ref.at[pl.ds(idx_ref[TOP_K * r + 1], 1)], yb.at[buf, pl.ds(r, 1)],
                              sem.at[buf]).start(priority=1)

    def wait(buf):
        pltpu.make_async_copy(ya.at[buf], ya.at[buf], sem.at[buf]).wait()
        pltpu.make_async_copy(yb.at[buf], yb.at[buf], sem.at[buf]).wait()

    @pl.when(i == 0)
    def _():
        def body(r, c):
            issue(cur_ref, 0, r)
            return c
        lax.fori_loop(0, tm, body, 0, unroll=8)

    wait(slot)
    for r in range(tm):
        issue(nxt_ref, 1 - slot, r)

    wa = w_ref[:, 0:1]
    wb = w_ref[:, 1:2]
    half_t = Y_TILE // 2
    ssq = jnp.zeros((tm, 1), F32)
    for n in range(N_YT):
        words = slice(n * half_t, (n + 1) * half_t)
        a_lo, a_hi = _unpack_bf16_pair(ya[slot, :, words])
        b_lo, b_hi = _unpack_bf16_pair(yb[slot, :, words])
        lo_cols = slice(n * Y_TILE, n * Y_TILE + half_t)
        hi_cols = slice(n * Y_TILE + half_t, (n + 1) * Y_TILE)
        lo = h_ref[:, lo_cols] + wa * a_lo + wb * b_lo
        hi = h_ref[:, hi_cols] + wa * a_hi + wb * b_hi
        ssq = ssq + jnp.sum(lo * lo, axis=-1, keepdims=True) + jnp.sum(hi * hi, axis=-1, keepdims=True)
        o_ref[:, lo_cols] = lo
        o_ref[:, hi_cols] = hi
    inv = lax.rsqrt(ssq * (1.0 / D_MODEL) + RMS_EPS)
    o_ref[...] = o_ref[...] * inv * g_ref[...]

    @pl.when(i == n_steps - 1)
    def _():
        wait(1 - slot)


def _combine(dest_flat, h2, wts, g, y, tm):
    rows = h2.shape[0]
    steps = rows // tm
    idx_spec = lambda fn: pl.BlockSpec((TOP_K * tm,), fn, memory_space=pltpu.SMEM)
    return pl.pallas_call(
        functools.partial(_combine_kernel, tm=tm),
        grid=(steps,),
        in_specs=[
            idx_spec(lambda i: (i,)),
            idx_spec(lambda i: (jnp.minimum(i + 1, steps - 1),)),
            pl.BlockSpec((tm, D_MODEL), lambda i: (i, 0)),
            pl.BlockSpec((tm, LANES), lambda i: (i, 0)),
            pl.BlockSpec((1, D_MODEL), lambda i: (0, 0)),
            pl.BlockSpec(memory_space=pl.ANY),
        ],
        out_specs=pl.BlockSpec((tm, D_MODEL), lambda i: (i, 0)),
        scratch_shapes=[
            pltpu.VMEM((2, tm, HALF), jnp.uint32),
            pltpu.VMEM((2, tm, HALF), jnp.uint32),
            pltpu.SemaphoreType.DMA((2,)),
        ],
        out_shape=jax.ShapeDtypeStruct((rows, D_MODEL), F32),
        compiler_params=pltpu.CompilerParams(dimension_semantics=("arbitrary",),
                                             vmem_limit_bytes=VMEM_LIMIT),
        name="combine",
    )(dest_flat, dest_flat, h2, wts, g, y)


ROUTE_TM = 512


def _route_kernel(lg_ref, info_ref, wts_ref, tab_ref, counts, starts, carry):
    phase = pl.program_id(0)
    i = pl.program_id(1)
    tm = ROUTE_TM
    lane = lax.broadcasted_iota(jnp.int32, (tm, LANES), 1)
    lg = lg_ref[...]

    def first_lane_of_max(v):
        vmax = jnp.max(v, axis=-1, keepdims=True)
        return vmax, jnp.min(jnp.where(v == vmax, lane, LANES), axis=-1, keepdims=True)

    gl = jnp.where(lane < N_GROUPS, lg, NEG_INF)
    gmax, g_sel = first_lane_of_max(gl)
    gate = 1.0 / jnp.sum(jnp.exp(gl - gmax), axis=-1, keepdims=True)
    lo = N_GROUPS + EXPERTS_PER_GROUP * g_sel
    el = jnp.where((lane >= lo) & (lane < lo + EXPERTS_PER_GROUP), lg, NEG_INF)
    m1, l1 = first_lane_of_max(el)
    m2, l2 = first_lane_of_max(jnp.where(lane == l1, NEG_INF, el))
    ratio = jnp.exp(m2 - m1)
    w1 = gate / (1.0 + ratio)
    w2 = w1 * ratio
    e1 = l1 - N_GROUPS
    e2 = l2 - N_GROUPS
    hit1 = lane == e1
    hit2 = lane == e2
    onehot = (hit1 | hit2).astype(F32)
    tile_counts = jnp.sum(onehot, axis=0, keepdims=True)

    @pl.when((phase == 0) & (i == 0))
    def _():
        counts[...] = jnp.zeros_like(counts)

    @pl.when(phase == 0)
    def _():
        counts[...] += tile_counts

    @pl.when((phase == 1) & (i == 0))
    def _():
        blocks = jnp.floor((counts[...] + (ROW_BLK - 1)) * (1.0 / ROW_BLK))
        r = lax.broadcasted_iota(jnp.int32, (LANES, LANES), 0)
        c = lax.broadcasted_iota(jnp.int32, (LANES, LANES), 1)
        before = (r < c).astype(F32)
        first = jnp.dot(jnp.broadcast_to(blocks, (8, LANES)), before,
                        preferred_element_type=F32, precision=lax.Precision.HIGHEST)[0:1]
        starts[...] = first
        carry[...] = jnp.zeros_like(carry)
        used = jnp.sum(blocks, axis=-1, keepdims=True)
        row = lax.broadcasted_iota(jnp.int32, (8, LANES), 0)
        tab = jnp.where(row == 0, blocks, jnp.where(row == 1, first, jnp.where(row == 2, used, 0.0)))
        tab_ref[...] = tab.astype(jnp.int32)

    @pl.when(phase == 1)
    def _():
        r = lax.broadcasted_iota(jnp.int32, (tm, tm), 0)
        c = lax.broadcasted_iota(jnp.int32, (tm, tm), 1)
        earlier = (c < r).astype(BF16)
        before = jnp.dot(earlier, onehot.astype(BF16), preferred_element_type=F32)
        slot = starts[...] * ROW_BLK + carry[...] + before
        d1 = jnp.sum(jnp.where(hit1, slot, 0.0), axis=-1, keepdims=True)
        d2 = jnp.sum(jnp.where(hit2, slot, 0.0), axis=-1, keepdims=True)
        info_ref[...] = jnp.where(lane == 0, d1, jnp.where(lane == 1, d2, 0.0)).astype(jnp.int32)
        wts_ref[...] = jnp.where(lane == 0, w1, jnp.where(lane == 1, w2, 0.0))
        carry[...] += tile_counts


def _route(logits):
    rows = logits.shape[0]
    tile = lambda p, i: (i * p, 0)
    return pl.pallas_call(
        _route_kernel,
        grid=(2, rows // ROUTE_TM),
        in_specs=[pl.BlockSpec((ROUTE_TM, LANES), lambda p, i: (i, 0))],
        out_specs=(pl.BlockSpec((ROUTE_TM, LANES), tile), pl.BlockSpec((ROUTE_TM, LANES), tile),
                   pl.BlockSpec((8, LANES), lambda p, i: (0, 0))),
        out_shape=(jax.ShapeDtypeStruct((rows, LANES), jnp.int32),
                   jax.ShapeDtypeStruct((rows, LANES), F32),
                   jax.ShapeDtypeStruct((8, LANES), jnp.int32)),
        scratch_shapes=[pltpu.VMEM((1, LANES), F32)] * 3,
        compiler_params=pltpu.CompilerParams(dimension_semantics=("arbitrary", "arbitrary"),
                                             vmem_limit_bytes=VMEM_LIMIT),
        name="route",
    )(logits)


def _work_items(blocks_e, blk_start, n_items):
    items_e = (blocks_e + ITEM_BLKS - 1) // ITEM_BLKS
    item_end = jnp.cumsum(items_e)
    item_start = item_end - items_e
    total = item_end[-1]
    idx = jnp.arange(n_items, dtype=jnp.int32)
    live = idx < total
    ref = jnp.minimum(idx, total - 1)
    e = jnp.searchsorted(item_end, ref, side='right').astype(jnp.int32)
    j = ref - item_start[e]
    r0 = blk_start[e] + ITEM_BLKS * j
    nb = jnp.where(live, jnp.clip(blocks_e[e] - ITEM_BLKS * j, 0, ITEM_BLKS), 0)
    return e, r0.astype(jnp.int32), nb.astype(jnp.int32), total.astype(jnp.int32)


def kernel(x, meta_tokens, attn_norm_g, w_in, swa_sinks, diff_lambda, diff_subln_g, w_out,
           ffn_norm_g, w_router_group, w_router_expert, w_gate, w_up, w_down, final_norm_g):
    batch, seq, _ = x.shape
    n_tok = batch * seq
    nblk = seq // Q_BLOCK
    x2 = x.reshape(n_tok, D_MODEL)

    w_in_bf = w_in[0].astype(BF16)
    g_attn = attn_norm_g[0].reshape(1, D_MODEL)
    cos_x, sin_x = _rope_tables(N_META + jnp.arange(seq))
    cos_m, sin_m = _rope_tables(jnp.maximum(jnp.arange(Q_BLOCK) - PAD_FRONT, 0))
    qs, ks, vs, qd, kd, vd = _inproj(x2, g_attn, w_in_bf, cos_x, sin_x, 512, seq)
    meta_blk = jnp.concatenate(
        [jnp.zeros((PAD_FRONT, D_MODEL), x.dtype), meta_tokens.astype(x.dtype)], axis=0)
    _, ks_m, vs_m, _, kd_m, vd_m = _inproj(meta_blk, g_attn, w_in_bf, cos_m, sin_m, Q_BLOCK, Q_BLOCK)

    o_s = _swa(swa_sinks[0].astype(F32), qs, ks, vs, ks_m, vs_m, batch, nblk)
    o_d = _diff(diff_lambda[0].astype(F32), diff_subln_g[0].astype(F32).reshape(1, LANES),
                qd, kd, vd, kd_m, vd_m, batch, seq)

    w_router = jnp.concatenate(
        [w_router_group[0], w_router_expert[0],
         jnp.zeros((D_MODEL, LANES - N_GROUPS - N_EXPERTS), F32)], axis=1)
    h2, f_packed, logits = _outproj(o_s, o_d, x2, w_out[0].astype(BF16),
                                    ffn_norm_g[0].reshape(1, D_MODEL), w_router, 512)

    info, wts, tab = _route(logits)
    dest_flat = info[:, :TOP_K].reshape(-1)
    blocks_e, blk_start, n_used = tab[0, :N_EXPERTS], tab[1, :N_EXPERTS], tab[2, 0:1]
    n_blocks = n_tok * TOP_K // ROW_BLK + N_EXPERTS
    n_items = n_blocks // ITEM_BLKS + N_EXPERTS
    blk = jnp.arange(n_blocks, dtype=jnp.int32)
    is_last = jnp.any((blk[:, None] + 1 == (blk_start + blocks_e)[None, :])
                      & (blocks_e > 0)[None, :], axis=1)
    need_zero = is_last | (blk >= n_used[0])
    zero_blk = jnp.nonzero(need_zero, size=2 * N_EXPERTS, fill_value=0)[0].astype(jnp.int32)
    n_zero = jnp.sum(need_zero).astype(jnp.int32).reshape(1)
    xs = _dispatch(zero_blk, n_zero, dest_flat, f_packed, n_blocks)
    item_e, item_r0, item_nb, n_live = _work_items(blocks_e, blk_start, n_items)
    y = _moe(item_e, item_r0, item_nb, jnp.concatenate([n_used, n_live.reshape(1)]), xs,
             w_gate[0], w_up[0], w_down[0], n_blocks)

    out = _combine(dest_flat, h2, wts, final_norm_g.reshape(1, D_MODEL), y, 256)
    return out.reshape(batch, seq, D_MODEL)
```

```python
import functools
import math

import jax
import jax.numpy as jnp
from jax import lax
from jax.experimental import pallas as pl
from jax.experimental.pallas import tpu as pltpu

D_MODEL = 2048
N_META = 16
CHUNK = 64
Q_BLOCK = 128
PAD_FRONT = Q_BLOCK - N_META
HEAD_DIM = 64
ROPE_THETA = 10000.0
SWA_Q_HEADS = 16
SWA_KV_HEADS = 2
DIFF_HEADS = 8
SWA_Q_W = SWA_Q_HEADS * HEAD_DIM
SWA_KV_W = SWA_KV_HEADS * HEAD_DIM
DIFF_W = DIFF_HEADS * 2 * HEAD_DIM
IN_WIDTH = SWA_Q_W + 2 * SWA_KV_W + 3 * DIFF_W
N_GROUPS = 8
EXPERTS_PER_GROUP = 8
N_EXPERTS = N_GROUPS * EXPERTS_PER_GROUP
TOP_K = 2
D_EXPERT = 1408
RMS_EPS = 1e-6
NEG_INF = -1e30
LAM_INIT = 0.8 - 0.6 * math.exp(-0.3 * 0)

LANES = 128
ROW_BLK = 128
VMEM_LIMIT = 56 * 1024 * 1024

BF16 = jnp.bfloat16
F32 = jnp.float32


def _nt_dot(a, b):
    return lax.dot_general(a, b, (((1,), (1,)), ((), ())), preferred_element_type=F32)


LOG2_E = math.log2(math.e)
Q_SCALE = HEAD_DIM ** -0.5 * LOG2_E
_SEGS = (
    ("qs", 0, 512, True, Q_SCALE), ("qs", 512, 512, True, Q_SCALE),
    ("ks", 1024, 128, True, 1.0), ("vs", 1152, 128, False, 1.0),
    ("qd", 1280, 512, True, Q_SCALE), ("qd", 1792, 512, True, Q_SCALE),
    ("kd", 2304, 512, True, 1.0), ("kd", 2816, 512, True, 1.0),
    ("vd", 3328, 512, False, 1.0), ("vd", 3840, 512, False, 1.0),
)
_SEG_BASE = {"qs": 0, "ks": 1024, "vs": 1152, "qd": 1280, "kd": 2304, "vd": 3328}


def _inproj_kernel(x_ref, g_ref, w_ref, cos_ref, sin_ref,
                   qs_ref, ks_ref, vs_ref, qd_ref, kd_ref, vd_ref):
    x = x_ref[...]
    ms = jnp.mean(x * x, axis=-1, keepdims=True)
    a = (x * lax.rsqrt(ms + RMS_EPS) * g_ref[...]).astype(BF16)
    cos = cos_ref[...]
    sin = sin_ref[...]
    lane = lax.broadcasted_iota(jnp.int32, cos.shape, 1)
    first_half = (lane % HEAD_DIM) < (HEAD_DIM // 2)
    low_head = lane < HEAD_DIM
    outs = {"qs": qs_ref, "ks": ks_ref, "vs": vs_ref, "qd": qd_ref, "kd": kd_ref, "vd": vd_ref}
    for name, c0, width, rotary, scale in _SEGS:
        y = jnp.dot(a, w_ref[:, c0:c0 + width], preferred_element_type=F32)
        o0 = c0 - _SEG_BASE[name]
        for t in range(width // LANES):
            yt = y[:, t * LANES:(t + 1) * LANES]
            if rotary:
                partner = jnp.where(first_half, pltpu.roll(yt, LANES - HEAD_DIM // 2, 1),
                                    pltpu.roll(yt, HEAD_DIM // 2, 1))
                yt = yt * cos + partner * sin
            if scale != 1.0:
                yt = yt * scale
            if name in ("ks", "vs"):
                other = pltpu.roll(yt, HEAD_DIM, 1)
                outs[name][:, 0:LANES] = jnp.where(low_head, yt, other).astype(BF16)
                outs[name][:, LANES:2 * LANES] = jnp.where(low_head, other, yt).astype(BF16)
            else:
                outs[name][:, o0 + t * LANES:o0 + (t + 1) * LANES] = yt.astype(BF16)


def _inproj(x2, g, w_bf, cos_t, sin_t, tm, rows_per_seq):
    rows = x2.shape[0]
    nseq = rows_per_seq // tm
    row_spec = lambda width: pl.BlockSpec((tm, width), lambda i: (i, 0))
    out_shapes = (
        jax.ShapeDtypeStruct((rows, SWA_Q_W), BF16), jax.ShapeDtypeStruct((rows, 2 * SWA_KV_W), BF16),
        jax.ShapeDtypeStruct((rows, 2 * SWA_KV_W), BF16), jax.ShapeDtypeStruct((rows, DIFF_W), BF16),
        jax.ShapeDtypeStruct((rows, DIFF_W), BF16), jax.ShapeDtypeStruct((rows, DIFF_W), BF16),
    )
    return pl.pallas_call(
        _inproj_kernel,
        grid=(rows // tm,),
        in_specs=[
            row_spec(D_MODEL),
            pl.BlockSpec((1, D_MODEL), lambda i: (0, 0)),
            pl.BlockSpec((D_MODEL, IN_WIDTH), lambda i: (0, 0), pipeline_mode=pl.Buffered(1)),
            pl.BlockSpec((tm, LANES), lambda i: (i % nseq, 0)),
            pl.BlockSpec((tm, LANES), lambda i: (i % nseq, 0)),
        ],
        out_specs=(row_spec(SWA_Q_W), row_spec(2 * SWA_KV_W), row_spec(2 * SWA_KV_W),
                   row_spec(DIFF_W), row_spec(DIFF_W), row_spec(DIFF_W)),
        out_shape=out_shapes,
        compiler_params=pltpu.CompilerParams(dimension_semantics=("arbitrary",),
                                             vmem_limit_bytes=VMEM_LIMIT),
        name="inproj",
    )(x2, g, w_bf, cos_t, sin_t)


def _rope_tables(positions):
    inv = 1.0 / (ROPE_THETA ** (jnp.arange(0, HEAD_DIM, 2, dtype=F32) / HEAD_DIM))
    ang = positions.astype(F32)[:, None] * inv[None, :]
    cos, sin = jnp.cos(ang), jnp.sin(ang)
    cos_t = jnp.tile(cos, (1, 2 * LANES // HEAD_DIM))
    sin_t = jnp.tile(jnp.concatenate([-sin, sin], axis=1), (1, LANES // HEAD_DIM))
    return cos_t, sin_t


SWA_BLKS_PER_STEP = 8
SWA_SLAB = 256
SWA_BAND = 3 * CHUNK + N_META


def _swa_kernel(sink_ref, q_ref, kp_ref, kc_ref, km_ref, vp_ref, vc_ref, vm_ref, o_ref):
    i = pl.program_id(1)
    pairs = SWA_Q_HEADS // SWA_KV_HEADS // 2
    m_rows = pairs * 2 * CHUNK
    col = lax.broadcasted_iota(jnp.int32, (m_rows, SWA_SLAB), 1)
    row = lax.broadcasted_iota(jnp.int32, (m_rows, 1), 0)
    lane = lax.broadcasted_iota(jnp.int32, (CHUNK, LANES), 1)
    low_head = lane < HEAD_DIM
    zero = jnp.zeros((CHUNK, LANES), BF16)
    pad = jnp.zeros((SWA_SLAB - SWA_BAND, LANES), BF16)
    meta_rows = slice(PAD_FRONT, Q_BLOCK)
    for g in range(SWA_KV_HEADS):
        gl = slice(g * LANES, (g + 1) * LANES)
        sink = jnp.zeros((m_rows, 1), F32)
        for h in range(2 * pairs):
            sink = jnp.where(row // CHUNK == h, sink_ref[2 * g * pairs + h], sink)
        sink = sink * LOG2_E
        k_seq = jnp.concatenate([kp_ref[:, gl], kc_ref[:, gl]], axis=0)
        v_seq = jnp.concatenate([vp_ref[:, gl], vc_ref[:, gl]], axis=0)
        for half in range(2 * SWA_BLKS_PER_STEP):
            q_rows = slice(half * CHUNK, (half + 1) * CHUNK)
            band = slice(half * CHUNK, (half + 3) * CHUNK)
            k_slab = jnp.concatenate([k_seq[band], km_ref[meta_rows, gl], pad], axis=0)
            v_slab = jnp.concatenate([v_seq[band], vm_ref[meta_rows, gl], pad], axis=0)
            n_missing = max(0, (2 - half) * CHUNK)
            valid = (col < SWA_BAND) & ((i > 0) | (col >= n_missing))
            q2 = []
            for c in range(g * pairs, (g + 1) * pairs):
                qp = q_ref[q_rows, c * LANES:(c + 1) * LANES]
                q2 += [jnp.where(low_head, qp, zero), jnp.where(low_head, zero, qp)]
            s = jnp.where(valid, _nt_dot(jnp.concatenate(q2, axis=0), k_slab), NEG_INF)
            m = jnp.maximum(jnp.max(s, axis=-1, keepdims=True), sink)
            p = jnp.exp2(s - m)
            denom = jnp.sum(p, axis=-1, keepdims=True) + jnp.exp2(sink - m)
            o2 = jnp.dot(p.astype(BF16), v_slab, preferred_element_type=F32) / denom
            for j, c in enumerate(range(g * pairs, (g + 1) * pairs)):
                lo = o2[(2 * j) * CHUNK:(2 * j + 1) * CHUNK]
                hi = o2[(2 * j + 1) * CHUNK:(2 * j + 2) * CHUNK]
                o_ref[q_rows, c * LANES:(c + 1) * LANES] = jnp.where(low_head, lo, hi).astype(BF16)


def _swa(sinks, qs, ks, vs, ks_meta, vs_meta, batch, nblk):
    rows = qs.shape[0]
    steps = nblk // SWA_BLKS_PER_STEP
    step_rows = SWA_BLKS_PER_STEP * Q_BLOCK
    cur = lambda b, i: (b * steps + i, 0)
    prev = lambda b, i: (b * nblk + jnp.maximum(SWA_BLKS_PER_STEP * i - 1, 0), 0)
    kv_w = 2 * SWA_KV_W
    return pl.pallas_call(
        _swa_kernel,
        grid=(batch, steps),
        in_specs=[
            pl.BlockSpec(memory_space=pltpu.SMEM),
            pl.BlockSpec((step_rows, SWA_Q_W), cur),
            pl.BlockSpec((Q_BLOCK, kv_w), prev), pl.BlockSpec((step_rows, kv_w), cur),
            pl.BlockSpec((Q_BLOCK, kv_w), lambda b, i: (0, 0)),
            pl.BlockSpec((Q_BLOCK, kv_w), prev), pl.BlockSpec((step_rows, kv_w), cur),
            pl.BlockSpec((Q_BLOCK, kv_w), lambda b, i: (0, 0)),
        ],
        out_specs=pl.BlockSpec((step_rows, SWA_Q_W), cur),
        out_shape=jax.ShapeDtypeStruct((rows, SWA_Q_W), BF16),
        compiler_params=pltpu.CompilerParams(dimension_semantics=("arbitrary", "arbitrary"),
                                             vmem_limit_bytes=VMEM_LIMIT),
        name="swa",
    )(sinks, qs, ks, ks, ks_meta, vs, vs, vs_meta)


DIFF_TQ = 256
DIFF_TK = 512
DIFF_HEADS_PER_STEP = 2


def _diff_kernel(lam_ref, g_ref, q_ref, k_ref, v_ref, km_ref, vm_ref, o_ref, *, seq):
    dl = lam_ref[...]
    lam = (jnp.exp(jnp.sum(dl[0:1] * dl[1:2], axis=-1, keepdims=True))
           - jnp.exp(jnp.sum(dl[2:3] * dl[3:4], axis=-1, keepdims=True)) + LAM_INIT)
    lane = lax.broadcasted_iota(jnp.int32, (DIFF_TQ, LANES), 1)
    low_map = lane < HEAD_DIM
    zero = jnp.zeros((DIFF_TQ, LANES), BF16)

    def last_mask(n_keys):
        shape = (2 * DIFF_TQ, Q_BLOCK + n_keys)
        col = lax.broadcasted_iota(jnp.int32, shape, 1)
        q_chunk = (lax.broadcasted_iota(jnp.int32, shape, 0) % DIFF_TQ) // CHUNK
        first_chunk = (DIFF_TQ - n_keys) // CHUNK
        k_chunk = (col - Q_BLOCK) // CHUNK + first_chunk
        return ((col >= PAD_FRONT) & (col < Q_BLOCK)) | ((col >= Q_BLOCK) & (k_chunk <= q_chunk))

    def step(q2, k_t, v_t, mask, state):
        s = _nt_dot(q2, k_t)
        if mask is not None:
            s = jnp.where(mask, s, NEG_INF)
        smax = jnp.max(s, axis=-1, keepdims=True)
        if state is None:
            p = jnp.exp2(s - smax)
            return smax, jnp.sum(p, axis=-1, keepdims=True), jnp.dot(
                p.astype(BF16), v_t, preferred_element_type=F32)
        m_old, l_old, acc = state
        m_new = jnp.maximum(m_old, smax)
        alpha = jnp.exp2(m_old - m_new)
        p = jnp.exp2(s - m_new)
        return (m_new, alpha * l_old + jnp.sum(p, axis=-1, keepdims=True),
                alpha * acc + jnp.dot(p.astype(BF16), v_t, preferred_element_type=F32))

    heads = [slice(h * LANES, (h + 1) * LANES) for h in range(DIFF_HEADS_PER_STEP)]

    tiles_per_k = DIFF_TK // DIFF_TQ
    masks = [last_mask((t + 1) * DIFF_TQ) for t in range(tiles_per_k)]

    def q_tile(n_full, t):
        if n_full is None:
            aligned = lambda v: v
            q0 = t * DIFF_TQ
        else:
            aligned = lambda v: pl.multiple_of(v, DIFF_TQ)
            q0 = aligned((n_full * tiles_per_k + t) * DIFF_TQ)
        q2s = []
        for hs in heads:
            q = q_ref[pl.ds(q0, DIFF_TQ), hs]
            q2s.append(jnp.concatenate(
                [jnp.where(low_map, q, zero), jnp.where(low_map, zero, q)], axis=0))

        def full_tiles(kj, states):
            rows = pl.ds(pl.multiple_of(kj * DIFF_TK, DIFF_TK), DIFF_TK)
            out = ()
            for h, hs in enumerate(heads):
                st = None if states is None else states[3 * h:3 * h + 3]
                out = out + step(q2s[h], k_ref[rows, hs], v_ref[rows, hs], None, st)
            return out

        if n_full is None:
            states = None
        else:
            states = lax.fori_loop(1, n_full, full_tiles, full_tiles(0, None))

        n_keys = (t + 1) * DIFF_TQ
        rows = pl.ds(aligned(q0 + DIFF_TQ - n_keys), n_keys)
        for h, hs in enumerate(heads):
            st = None if states is None else states[3 * h:3 * h + 3]
            k_t = jnp.concatenate([km_ref[:, hs], k_ref[rows, hs]], axis=0)
            v_t = jnp.concatenate([vm_ref[:, hs], v_ref[rows, hs]], axis=0)
            _, l, acc = step(q2s[h], k_t, v_t, masks[t], st)
            o = acc / l
            o = o[:DIFF_TQ] - lam * o[DIFF_TQ:]
            o = o * lax.rsqrt(jnp.mean(o * o, axis=-1, keepdims=True) + RMS_EPS)
            o_ref[pl.ds(q0, DIFF_TQ), hs] = (o * g_ref[...] * (1.0 - LAM_INIT)).astype(BF16)

    def q_group(n_full, carry):
        for t in range(tiles_per_k):
            q_tile(n_full, t)
        return carry

    q_group(None, 0)
    lax.fori_loop(1, seq // DIFF_TK, q_group, 0)


def _diff(diff_lambda, subln_g, qd, kd, vd, kd_meta, vd_meta, batch, seq):
    rows = qd.shape[0]
    head = lambda b, h: (b, h)
    meta = lambda b, h: (0, h)
    width = DIFF_HEADS_PER_STEP * LANES
    return pl.pallas_call(
        functools.partial(_diff_kernel, seq=seq),
        grid=(batch, DIFF_HEADS // DIFF_HEADS_PER_STEP),
        in_specs=[
            pl.BlockSpec((4, HEAD_DIM), lambda b, h: (0, 0)),
            pl.BlockSpec((1, LANES), lambda b, h: (0, 0)),
            pl.BlockSpec((seq, width), head), pl.BlockSpec((seq, width), head),
            pl.BlockSpec((seq, width), head),
            pl.BlockSpec((Q_BLOCK, width), meta), pl.BlockSpec((Q_BLOCK, width), meta),
        ],
        out_specs=pl.BlockSpec((seq, width), head),
        out_shape=jax.ShapeDtypeStruct((rows, DIFF_W), BF16),
        compiler_params=pltpu.CompilerParams(dimension_semantics=("arbitrary", "arbitrary"),
                                             vmem_limit_bytes=VMEM_LIMIT),
        name="diffattn",
    )(diff_lambda, subln_g, qd, kd, vd, kd_meta, vd_meta)


HALF = D_MODEL // 2


def _pack_bf16_pair(a, b):
    hi = pltpu.bitcast(a.astype(BF16).astype(F32), jnp.uint32)
    lo = pltpu.bitcast(b.astype(BF16).astype(F32), jnp.uint32)
    return hi | (lo >> 16)


def _unpack_bf16_pair(w):
    a = pltpu.bitcast(w & jnp.uint32(0xFFFF0000), F32)
    b = pltpu.bitcast(w << 16, F32)
    return a, b


OUT_CHUNK = 512


def _outproj_kernel(os_ref, od_ref, x_ref, w_ref, g_ref, wr_ref, h_ref, fp_ref, lg_ref):
    tm = x_ref.shape[0]
    ssq = jnp.zeros((tm, 1), F32)
    for c in range(D_MODEL // OUT_CHUNK):
        cols = slice(c * OUT_CHUNK, (c + 1) * OUT_CHUNK)
        y = jnp.dot(os_ref[...], w_ref[0:SWA_Q_W, cols], preferred_element_type=F32)
        y = y + jnp.dot(od_ref[...], w_ref[SWA_Q_W:, cols], preferred_element_type=F32)
        h = x_ref[:, cols] + y
        h_ref[:, cols] = h
        ssq = ssq + jnp.sum(h * h, axis=-1, keepdims=True)
    inv = lax.rsqrt(ssq * (1.0 / D_MODEL) + RMS_EPS)
    lg = jnp.zeros((tm, LANES), F32)
    for c in range(HALF // OUT_CHUNK):
        ca = slice(c * OUT_CHUNK, (c + 1) * OUT_CHUNK)
        cb = slice(HALF + c * OUT_CHUNK, HALF + (c + 1) * OUT_CHUNK)
        fa = h_ref[:, ca] * inv * g_ref[:, ca]
        fb = h_ref[:, cb] * inv * g_ref[:, cb]
        fp_ref[:, ca] = _pack_bf16_pair(fa, fb)
        for f, cc in ((fa, ca), (fb, cb)):
            f_hi = f.astype(BF16)
            f_lo = (f - f_hi.astype(F32)).astype(BF16)
            w = wr_ref[cc, :]
            w_hi = w.astype(BF16)
            w_lo = (w - w_hi.astype(F32)).astype(BF16)
            p = jnp.dot(f_hi, jnp.concatenate([w_hi, w_lo], axis=1), preferred_element_type=F32)
            lg = lg + p[:, 0:LANES] + p[:, LANES:] + jnp.dot(f_lo, w_hi, preferred_element_type=F32)
    lg_ref[...] = lg


def _outproj(o_s, o_d, x2, w_bf, g, w_router, tm):
    rows = x2.shape[0]
    row_spec = lambda width: pl.BlockSpec((tm, width), lambda i: (i, 0))
    return pl.pallas_call(
        _outproj_kernel,
        grid=(rows // tm,),
        in_specs=[
            row_spec(SWA_Q_W), row_spec(DIFF_W), row_spec(D_MODEL),
            pl.BlockSpec((D_MODEL, D_MODEL), lambda i: (0, 0), pipeline_mode=pl.Buffered(1)),
            pl.BlockSpec((1, D_MODEL), lambda i: (0, 0)),
            pl.BlockSpec((D_MODEL, LANES), lambda i: (0, 0)),
        ],
        out_specs=(row_spec(D_MODEL), row_spec(HALF), row_spec(LANES)),
        out_shape=(jax.ShapeDtypeStruct((rows, D_MODEL), F32),
                   jax.ShapeDtypeStruct((rows, HALF), jnp.uint32),
                   jax.ShapeDtypeStruct((rows, LANES), F32)),
        compiler_params=pltpu.CompilerParams(dimension_semantics=("arbitrary",),
                                             vmem_limit_bytes=VMEM_LIMIT),
        name="outproj",
    )(o_s, o_d, x2, w_bf, g, w_router)


DISPATCH_TM = 1024


def _dispatch_kernel(zero_blk_ref, n_zero_ref, dest_ref, f_ref, xs_ref, zeros, sem, zsem):
    i = pl.program_id(0)

    @pl.when(i == 0)
    def _():
        zeros[...] = jnp.zeros_like(zeros)

        def zero_copy(j):
            rows = pl.ds(zero_blk_ref[j] * ROW_BLK, ROW_BLK)
            return pltpu.make_async_copy(zeros, xs_ref.at[rows], zsem.at[0])

        def start(j, c):
            zero_copy(j).start()
            return c

        def wait(j, c):
            zero_copy(j).wait()
            return c
        lax.fori_loop(0, n_zero_ref[0], start, 0)
        lax.fori_loop(0, n_zero_ref[0], wait, 0)

    def issue(r, c):
        for kk in range(TOP_K):
            pltpu.make_async_copy(f_ref.at[pl.ds(r, 1)], xs_ref.at[pl.ds(dest_ref[TOP_K * r + kk], 1)],
                                  sem.at[0]).start(priority=kk % 2)
        return c
    lax.fori_loop(0, DISPATCH_TM, issue, 0, unroll=8)
    n_rows = TOP_K * DISPATCH_TM
    pltpu.make_async_copy(xs_ref.at[pl.ds(0, n_rows)], xs_ref.at[pl.ds(0, n_rows)], sem.at[0]).wait()


def _dispatch(zero_blk, n_zero, dest_flat, f_packed, n_blocks):
    rows = f_packed.shape[0]
    return pl.pallas_call(
        _dispatch_kernel,
        grid_spec=pltpu.PrefetchScalarGridSpec(
            num_scalar_prefetch=2,
            grid=(rows // DISPATCH_TM,),
            in_specs=[
                pl.BlockSpec((TOP_K * DISPATCH_TM,), lambda i, z, n: (i,), memory_space=pltpu.SMEM),
                pl.BlockSpec((DISPATCH_TM, HALF), lambda i, z, n: (i, 0)),
            ],
            out_specs=pl.BlockSpec(memory_space=pl.ANY),
            scratch_shapes=[pltpu.VMEM((ROW_BLK, HALF), jnp.uint32),
                            pltpu.SemaphoreType.DMA((1,)),
                            pltpu.SemaphoreType.DMA((1,))],
        ),
        out_shape=jax.ShapeDtypeStruct((n_blocks * ROW_BLK, HALF), jnp.uint32),
        compiler_params=pltpu.CompilerParams(dimension_semantics=("arbitrary",),
                                             vmem_limit_bytes=VMEM_LIMIT),
        name="dispatch",
    )(zero_blk, n_zero, dest_flat, f_packed)


ITEM_BLKS = 6
W_BUFS = 3
ITEM_ROWS = ITEM_BLKS * ROW_BLK
F_TILE = 256
N_FT = -(-D_EXPERT // F_TILE)
Y_TILE = 512
N_YT = D_MODEL // Y_TILE


def _f_tile_start(k):
    return pl.multiple_of(jnp.minimum(k * F_TILE, D_EXPERT - F_TILE), LANES)


def _moe_kernel(e_ref, r0_ref, nb_ref, meta_ref, xs_ref, wg_hbm, wu_hbm, wd_hbm, y_ref,
                xraw, xb, hid, ystage, wgf, wuf, wdf, wgu, wdr, xsem, ysem, wsem):
    n_used = meta_ref[0]
    n_live = meta_ref[1]

    def for_blocks(count, fn):
        def body(j, c):
            fn(j)
            return c
        lax.fori_loop(0, count, body, 0)

    def x_copy(item, buf, j):
        src = xs_ref.at[pl.ds((r0_ref[item] + j) * ROW_BLK, ROW_BLK)]
        return pltpu.make_async_copy(src, xraw.at[buf, pl.ds(j * ROW_BLK, ROW_BLK)],
                                     xsem.at[buf, j])

    def w_copies(item, k, buf):
        e = e_ref[item]
        cols = pl.ds(_f_tile_start(k), F_TILE)
        return (pltpu.make_async_copy(wg_hbm.at[e, :, cols], wgf.at[buf], wsem.at[0, buf]),
                pltpu.make_async_copy(wu_hbm.at[e, :, cols], wuf.at[buf], wsem.at[1, buf]),
                pltpu.make_async_copy(wd_hbm.at[e, cols, :], wdf.at[buf], wsem.at[2, buf]))

    def zero_copy(j):
        dst = y_ref.at[pl.ds((n_used + j) * ROW_BLK, ROW_BLK)]
        return pltpu.make_async_copy(ystage.at[pl.ds(0, ROW_BLK)], dst, ysem.at[0])
    n_unused = y_ref.shape[0] // ROW_BLK - n_used
    ystage[0:ROW_BLK, :] = jnp.zeros((ROW_BLK, HALF), jnp.uint32)
    for_blocks(n_unused, lambda j: zero_copy(j).start())
    for_blocks(n_unused, lambda j: zero_copy(j).wait())

    for_blocks(nb_ref[0], lambda j: x_copy(0, 0, j).start())

    @pl.when(nb_ref[0] > 0)
    def _():
        for s in range(W_BUFS - 1):
            for c in w_copies(0, s, s):
                c.start()

    def item_body(i, carry):
        slot = i % 2
        nb = nb_ref[i]
        prev = jnp.maximum(i - 1, 0)
        nb_prev = jnp.where(i > 0, nb_ref[prev], 0)

        def y_copy(j):
            dst = y_ref.at[pl.ds((r0_ref[prev] + j) * ROW_BLK, ROW_BLK)]
            return pltpu.make_async_copy(ystage.at[pl.ds(j * ROW_BLK, ROW_BLK)], dst, ysem.at[0])

        for_blocks(nb_ref[i + 1], lambda j: x_copy(i + 1, 1 - slot, j).start())

        def arrive(j):
            x_copy(i, slot, j).wait()
            rows = pl.ds(pl.multiple_of(j * ROW_BLK, ROW_BLK), ROW_BLK)
            lo, hi = _unpack_bf16_pair(xraw[slot, rows, :])
            xb[rows, 0:HALF] = lo.astype(BF16)
            xb[rows, HALF:] = hi.astype(BF16)
        for_blocks(nb, arrive)

        def tile_body(k, carry):
            step = i * N_FT + k
            wbuf = step % W_BUFS
            last = k == N_FT - 1
            ahead = k + (W_BUFS - 1)
            nxt_i = jnp.where(ahead >= N_FT, i + 1, i)
            nxt_k = jnp.where(ahead >= N_FT, ahead - N_FT, ahead)

            @pl.when(nb_ref[nxt_i] > 0)
            def _():
                for c in w_copies(nxt_i, nxt_k, (step + (W_BUFS - 1)) % W_BUFS):
                    c.start()

            def cast_tiles():
                wgu[:, 0:F_TILE] = wgf[wbuf].astype(BF16)
                wgu[:, F_TILE:] = wuf[wbuf].astype(BF16)
                f_rows = pl.ds(_f_tile_start(k), F_TILE)
                for n in range(N_YT):
                    wdr[slot, n, f_rows, :] = wdf[wbuf, :, n * Y_TILE:(n + 1) * Y_TILE].astype(BF16)

            def gate_up(row0, nrows):
                rows = pl.ds(pl.multiple_of(row0, ROW_BLK), nrows)
                gu = jnp.dot(xb[rows, :], wgu[...], preferred_element_type=F32)
                gate, up = gu[:, 0:F_TILE], gu[:, F_TILE:]
                h = (gate * (1.0 / (1.0 + jnp.exp(-gate))) * up).astype(BF16)
                hid[slot, rows, pl.ds(_f_tile_start(k), F_TILE)] = h

            k_y = jnp.minimum(k, N_YT - 1)

            def down(row0, nrows):
                rows = pl.ds(pl.multiple_of(row0, ROW_BLK), nrows)
                yt = jnp.dot(hid[1 - slot, rows, :], wdr[1 - slot, k_y], preferred_element_type=F32)
                packed = _pack_bf16_pair(yt[:, 0:Y_TILE // 2], yt[:, Y_TILE // 2:])
                words = pl.ds(pl.multiple_of(k_y * (Y_TILE // 2), Y_TILE // 2), Y_TILE // 2)
                ystage[rows, words] = packed

            def for_row_tiles(n_blk, fn):
                for whole in range(1, ITEM_BLKS + 1):
                    @pl.when(n_blk == whole)
                    def _():
                        fn(0, whole * ROW_BLK)

            @pl.when(nb > 0)
            def _():
                for c in w_copies(i, k, wbuf):
                    c.wait()
                cast_tiles()

            for_row_tiles(nb, gate_up)
            for_row_tiles(jnp.where(k < N_YT, nb_prev, 0), down)

            @pl.when(k == N_YT)
            def _():
                for_blocks(nb_prev, lambda j: y_copy(j).start())

            @pl.when(last)
            def _():
                for_blocks(nb_prev, lambda j: y_copy(j).wait())
            return carry

        lax.fori_loop(0, N_FT, tile_body, 0)
        return carry

    lax.fori_loop(0, n_live + 1, item_body, 0)


def _moe(item_e, item_r0, item_nb, meta, xs, w_gate, w_up, w_down, n_blocks):
    return pl.pallas_call(
        _moe_kernel,
        grid_spec=pltpu.PrefetchScalarGridSpec(
            num_scalar_prefetch=4,
            grid=(1,),
            in_specs=[pl.BlockSpec(memory_space=pl.ANY)] * 4,
            out_specs=pl.BlockSpec(memory_space=pl.ANY),
            scratch_shapes=[
                pltpu.VMEM((2, ITEM_ROWS, HALF), jnp.uint32),
                pltpu.VMEM((ITEM_ROWS, D_MODEL), BF16),
                pltpu.VMEM((2, ITEM_ROWS, D_EXPERT), BF16),
                pltpu.VMEM((ITEM_ROWS, HALF), jnp.uint32),
                pltpu.VMEM((W_BUFS, D_MODEL, F_TILE), F32),
                pltpu.VMEM((W_BUFS, D_MODEL, F_TILE), F32),
                pltpu.VMEM((W_BUFS, F_TILE, D_MODEL), F32),
                pltpu.VMEM((D_MODEL, 2 * F_TILE), BF16),
                pltpu.VMEM((2, N_YT, D_EXPERT, Y_TILE), BF16),
                pltpu.SemaphoreType.DMA((2, ITEM_BLKS)),
                pltpu.SemaphoreType.DMA((1,)),
                pltpu.SemaphoreType.DMA((3, W_BUFS)),
            ],
        ),
        out_shape=jax.ShapeDtypeStruct((n_blocks * ROW_BLK, HALF), jnp.uint32),
        compiler_params=pltpu.CompilerParams(dimension_semantics=("arbitrary",),
                                             vmem_limit_bytes=VMEM_LIMIT),
        name="moe",
    )(item_e, item_r0, item_nb, meta, xs, w_gate, w_up, w_down)


def _combine_kernel(cur_ref, nxt_ref, h_ref, w_ref, g_ref, y_ref, o_ref, ya, yb, sem, *, tm):
    i = pl.program_id(0)
    n_steps = pl.num_programs(0)
    slot = i % 2

    def issue(idx_ref, buf, r):
        pltpu.make_async_copy(y_ref.at[pl.ds(idx_ref[TOP_K * r], 1)], ya.at[buf, pl.ds(r, 1)],
                              sem.at[buf]).start(priority=0)
        pltpu.make_async_copy(y_ref.at[pl.ds(idx_ref[TOP_K * r + 1], 1)], yb.at[buf, pl.ds(r, 1)],
                              sem.at[buf]).start(priority=1)

    def wait(buf):
        pltpu.make_async_copy(ya.at[buf], ya.at[buf], sem.at[buf]).wait()
        pltpu.make_async_copy(yb.at[buf], yb.at[buf], sem.at[buf]).wait()

    @pl.when(i == 0)
    def _():
        def body(r, c):
            issue(cur_ref, 0, r)
            return c
        lax.fori_loop(0, tm, body, 0, unroll=8)

    wait(slot)
    for r in range(tm):
        issue(nxt_ref, 1 - slot, r)

    wa = w_ref[:, 0:1]
    wb = w_ref[:, 1:2]
    half_t = Y_TILE // 2
    ssq = jnp.zeros((tm, 1), F32)
    for n in range(N_YT):
        words = slice(n * half_t, (n + 1) * half_t)
        a_lo, a_hi = _unpack_bf16_pair(ya[slot, :, words])
        b_lo, b_hi = _unpack_bf16_pair(yb[slot, :, words])
        lo_cols = slice(n * Y_TILE, n * Y_TILE + half_t)
        hi_cols = slice(n * Y_TILE + half_t, (n + 1) * Y_TILE)
        lo = h_ref[:, lo_cols] + wa * a_lo + wb * b_lo
        hi = h_ref[:, hi_cols] + wa * a_hi + wb * b_hi
        ssq = ssq + jnp.sum(lo * lo, axis=-1, keepdims=True) + jnp.sum(hi * hi, axis=-1, keepdims=True)
        o_ref[:, lo_cols] = lo
        o_ref[:, hi_cols] = hi
    inv = lax.rsqrt(ssq * (1.0 / D_MODEL) + RMS_EPS)
    o_ref[...] = o_ref[...] * inv * g_ref[...]

    @pl.when(i == n_steps - 1)
    def _():
        wait(1 - slot)


def _combine(dest_flat, h2, wts, g, y, tm):
    rows = h2.shape[0]
    steps = rows // tm
    idx_spec = lambda fn: pl.BlockSpec((TOP_K * tm,), fn, memory_space=pltpu.SMEM)
    return pl.pallas_call(
        functools.partial(_combine_kernel, tm=tm),
        grid=(steps,),
        in_specs=[
            idx_spec(lambda i: (i,)),
            idx_spec(lambda i: (jnp.minimum(i + 1, steps - 1),)),
            pl.BlockSpec((tm, D_MODEL), lambda i: (i, 0)),
            pl.BlockSpec((tm, LANES), lambda i: (i, 0)),
            pl.BlockSpec((1, D_MODEL), lambda i: (0, 0)),
            pl.BlockSpec(memory_space=pl.ANY),
        ],
        out_specs=pl.BlockSpec((tm, D_MODEL), lambda i: (i, 0)),
        scratch_shapes=[
            pltpu.VMEM((2, tm, HALF), jnp.uint32),
            pltpu.VMEM((2, tm, HALF), jnp.uint32),
            pltpu.SemaphoreType.DMA((2,)),
        ],
        out_shape=jax.ShapeDtypeStruct((rows, D_MODEL), F32),
        compiler_params=pltpu.CompilerParams(dimension_semantics=("arbitrary",),
                                             vmem_limit_bytes=VMEM_LIMIT),
        name="combine",
    )(dest_flat, dest_flat, h2, wts, g, y)


ROUTE_TM = 512


def _route_kernel(lg_ref, info_ref, wts_ref, tab_ref, counts, starts, carry):
    phase = pl.program_id(0)
    i = pl.program_id(1)
    tm = ROUTE_TM
    lane = lax.broadcasted_iota(jnp.int32, (tm, LANES), 1)
    lg = lg_ref[...]

    def first_lane_of_max(v):
        vmax = jnp.max(v, axis=-1, keepdims=True)
        return vmax, jnp.min(jnp.where(v == vmax, lane, LANES), axis=-1, keepdims=True)

    gl = jnp.where(lane < N_GROUPS, lg, NEG_INF)
    gmax, g_sel = first_lane_of_max(gl)
    gate = 1.0 / jnp.sum(jnp.exp(gl - gmax), axis=-1, keepdims=True)
    lo = N_GROUPS + EXPERTS_PER_GROUP * g_sel
    el = jnp.where((lane >= lo) & (lane < lo + EXPERTS_PER_GROUP), lg, NEG_INF)
    m1, l1 = first_lane_of_max(el)
    m2, l2 = first_lane_of_max(jnp.where(lane == l1, NEG_INF, el))
    ratio = jnp.exp(m2 - m1)
    w1 = gate / (1.0 + ratio)
    w2 = w1 * ratio
    e1 = l1 - N_GROUPS
    e2 = l2 - N_GROUPS
    hit1 = lane == e1
    hit2 = lane == e2
    onehot = (hit1 | hit2).astype(F32)
    tile_counts = jnp.sum(onehot, axis=0, keepdims=True)

    @pl.when((phase == 0) & (i == 0))
    def _():
        counts[...] = jnp.zeros_like(counts)

    @pl.when(phase == 0)
    def _():
        counts[...] += tile_counts

    @pl.when((phase == 1) & (i == 0))
    def _():
        blocks = jnp.floor((counts[...] + (ROW_BLK - 1)) * (1.0 / ROW_BLK))
        r = lax.broadcasted_iota(jnp.int32, (LANES, LANES), 0)
        c = lax.broadcasted_iota(jnp.int32, (LANES, LANES), 1)
        before = (r < c).astype(F32)
        first = jnp.dot(jnp.broadcast_to(blocks, (8, LANES)), before,
                        preferred_element_type=F32, precision=lax.Precision.HIGHEST)[0:1]
        starts[...] = first
        carry[...] = jnp.zeros_like(carry)
        used = jnp.sum(blocks, axis=-1, keepdims=True)
        row = lax.broadcasted_iota(jnp.int32, (8, LANES), 0)
        tab = jnp.where(row == 0, blocks, jnp.where(row == 1, first, jnp.where(row == 2, used, 0.0)))
        tab_ref[...] = tab.astype(jnp.int32)

    @pl.when(phase == 1)
    def _():
        r = lax.broadcasted_iota(jnp.int32, (tm, tm), 0)
        c = lax.broadcasted_iota(jnp.int32, (tm, tm), 1)
        earlier = (c < r).astype(BF16)
        before = jnp.dot(earlier, onehot.astype(BF16), preferred_element_type=F32)
        slot = starts[...] * ROW_BLK + carry[...] + before
        d1 = jnp.sum(jnp.where(hit1, slot, 0.0), axis=-1, keepdims=True)
        d2 = jnp.sum(jnp.where(hit2, slot, 0.0), axis=-1, keepdims=True)
        info_ref[...] = jnp.where(lane == 0, d1, jnp.where(lane == 1, d2, 0.0)).astype(jnp.int32)
        wts_ref[...] = jnp.where(lane == 0, w1, jnp.where(lane == 1, w2, 0.0))
        carry[...] += tile_counts


def _route(logits):
    rows = logits.shape[0]
    tile = lambda p, i: (i * p, 0)
    return pl.pallas_call(
        _route_kernel,
        grid=(2, rows // ROUTE_TM),
        in_specs=[pl.BlockSpec((ROUTE_TM, LANES), lambda p, i: (i, 0))],
        out_specs=(pl.BlockSpec((ROUTE_TM, LANES), tile), pl.BlockSpec((ROUTE_TM, LANES), tile),
                   pl.BlockSpec((8, LANES), lambda p, i: (0, 0))),
        out_shape=(jax.ShapeDtypeStruct((rows, LANES), jnp.int32),
                   jax.ShapeDtypeStruct((rows, LANES), F32),
                   jax.ShapeDtypeStruct((8, LANES), jnp.int32)),
        scratch_shapes=[pltpu.VMEM((1, LANES), F32)] * 3,
        compiler_params=pltpu.CompilerParams(dimension_semantics=("arbitrary", "arbitrary"),
                                             vmem_limit_bytes=VMEM_LIMIT),
        name="route",
    )(logits)


def _work_items(blocks_e, blk_start, n_items):
    items_e = (blocks_e + ITEM_BLKS - 1) // ITEM_BLKS
    item_end = jnp.cumsum(items_e)
    item_start = item_end - items_e
    total = item_end[-1]
    idx = jnp.arange(n_items, dtype=jnp.int32)
    live = idx < total
    ref = jnp.minimum(idx, total - 1)
    e = jnp.searchsorted(item_end, ref, side='right').astype(jnp.int32)
    j = ref - item_start[e]
    r0 = blk_start[e] + ITEM_BLKS * j
    nb = jnp.where(live, jnp.clip(blocks_e[e] - ITEM_BLKS * j, 0, ITEM_BLKS), 0)
    return e, r0.astype(jnp.int32), nb.astype(jnp.int32), total.astype(jnp.int32)


def kernel(x, meta_tokens, attn_norm_g, w_in, swa_sinks, diff_lambda, diff_subln_g, w_out,
           ffn_norm_g, w_router_group, w_router_expert, w_gate, w_up, w_down, final_norm_g):
    batch, seq, _ = x.shape
    n_tok = batch * seq
    nblk = seq // Q_BLOCK
    x2 = x.reshape(n_tok, D_MODEL)

    w_in_bf = w_in[0].astype(BF16)
    g_attn = attn_norm_g[0].reshape(1, D_MODEL)
    cos_x, sin_x = _rope_tables(N_META + jnp.arange(seq))
    cos_m, sin_m = _rope_tables(jnp.maximum(jnp.arange(Q_BLOCK) - PAD_FRONT, 0))
    qs, ks, vs, qd, kd, vd = _inproj(x2, g_attn, w_in_bf, cos_x, sin_x, 512, seq)
    meta_blk = jnp.concatenate(
        [jnp.zeros((PAD_FRONT, D_MODEL), x.dtype), meta_tokens.astype(x.dtype)], axis=0)
    _, ks_m, vs_m, _, kd_m, vd_m = _inproj(meta_blk, g_attn, w_in_bf, cos_m, sin_m, Q_BLOCK, Q_BLOCK)

    o_s = _swa(swa_sinks[0].astype(F32), qs, ks, vs, ks_m, vs_m, batch, nblk)
    o_d = _diff(diff_lambda[0].astype(F32), diff_subln_g[0].astype(F32).reshape(1, LANES),
                qd, kd, vd, kd_m, vd_m, batch, seq)

    w_router = jnp.concatenate(
        [w_router_group[0], w_router_expert[0],
         jnp.zeros((D_MODEL, LANES - N_GROUPS - N_EXPERTS), F32)], axis=1)
    h2, f_packed, logits = _outproj(o_s, o_d, x2, w_out[0].astype(BF16),
                                    ffn_norm_g[0].reshape(1, D_MODEL), w_router, 512)

    info, wts, tab = _route(logits)
    dest_flat = info[:, :TOP_K].reshape(-1)
    blocks_e, blk_start, n_used = tab[0, :N_EXPERTS], tab[1, :N_EXPERTS], tab[2, 0:1]
    n_blocks = n_tok * TOP_K // ROW_BLK + N_EXPERTS
    n_items = n_blocks // ITEM_BLKS + N_EXPERTS
    blk = jnp.arange(n_blocks, dtype=jnp.int32)
    is_last = jnp.any((blk[:, None] + 1 == (blk_start + blocks_e)[None, :])
                      & (blocks_e > 0)[None, :], axis=1)
    need_zero = is_last | (blk >= n_used[0])
    zero_blk = jnp.nonzero(need_zero, size=2 * N_EXPERTS, fill_value=0)[0].astype(jnp.int32)
    n_zero = jnp.sum(need_zero).astype(jnp.int32).reshape(1)
    xs = _dispatch(zero_blk, n_zero, dest_flat, f_packed, n_blocks)
    item_e, item_r0, item_nb, n_live = _work_items(blocks_e, blk_start, n_items)
    y = _moe(item_e, item_r0, item_nb, jnp.concatenate([n_used, n_live.reshape(1)]), xs,
             w_gate[0], w_up[0], w_down[0], n_blocks)

    out = _combine(dest_flat, h2, wts, final_norm_g.reshape(1, D_MODEL), y, 256)
    return out.reshape(batch, seq, D_MODEL)
```

```python
import functools
import math

import jax
import jax.numpy as jnp
from jax import lax
from jax.experimental import pallas as pl
from jax.experimental.pallas import tpu as pltpu

D_MODEL = 2048
N_META = 16
CHUNK = 64
Q_BLOCK = 128
PAD_FRONT = Q_BLOCK - N_META
HEAD_DIM = 64
ROPE_THETA = 10000.0
SWA_Q_HEADS = 16
SWA_KV_HEADS = 2
DIFF_HEADS = 8
SWA_Q_W = SWA_Q_HEADS * HEAD_DIM
SWA_KV_W = SWA_KV_HEADS * HEAD_DIM
DIFF_W = DIFF_HEADS * 2 * HEAD_DIM
IN_WIDTH = SWA_Q_W + 2 * SWA_KV_W + 3 * DIFF_W
N_GROUPS = 8
EXPERTS_PER_GROUP = 8
N_EXPERTS = N_GROUPS * EXPERTS_PER_GROUP
TOP_K = 2
D_EXPERT = 1408
RMS_EPS = 1e-6
NEG_INF = -1e30
LAM_INIT = 0.8 - 0.6 * math.exp(-0.3 * 0)

LANES = 128
ROW_BLK = 128
VMEM_LIMIT = 56 * 1024 * 1024

BF16 = jnp.bfloat16
F32 = jnp.float32


def _nt_dot(a, b):
    return lax.dot_general(a, b, (((1,), (1,)), ((), ())), preferred_element_type=F32)


LOG2_E = math.log2(math.e)
Q_SCALE = HEAD_DIM ** -0.5 * LOG2_E
_SEGS = (
    ("qs", 0, 512, True, Q_SCALE), ("qs", 512, 512, True, Q_SCALE),
    ("ks", 1024, 128, True, 1.0), ("vs", 1152, 128, False, 1.0),
    ("qd", 1280, 512, True, Q_SCALE), ("qd", 1792, 512, True, Q_SCALE),
    ("kd", 2304, 512, True, 1.0), ("kd", 2816, 512, True, 1.0),
    ("vd", 3328, 512, False, 1.0), ("vd", 3840, 512, False, 1.0),
)
_SEG_BASE = {"qs": 0, "ks": 1024, "vs": 1152, "qd": 1280, "kd": 2304, "vd": 3328}


def _inproj_kernel(x_ref, g_ref, w_ref, cos_ref, sin_ref,
                   qs_ref, ks_ref, vs_ref, qd_ref, kd_ref, vd_ref):
    x = x_ref[...]
    ms = jnp.mean(x * x, axis=-1, keepdims=True)
    a = (x * lax.rsqrt(ms + RMS_EPS) * g_ref[...]).astype(BF16)
    cos = cos_ref[...]
    sin = sin_ref[...]
    lane = lax.broadcasted_iota(jnp.int32, cos.shape, 1)
    first_half = (lane % HEAD_DIM) < (HEAD_DIM // 2)
    low_head = lane < HEAD_DIM
    outs = {"qs": qs_ref, "ks": ks_ref, "vs": vs_ref, "qd": qd_ref, "kd": kd_ref, "vd": vd_ref}
    for name, c0, width, rotary, scale in _SEGS:
        y = jnp.dot(a, w_ref[:, c0:c0 + width], preferred_element_type=F32)
        o0 = c0 - _SEG_BASE[name]
        for t in range(width // LANES):
            yt = y[:, t * LANES:(t + 1) * LANES]
            if rotary:
                partner = jnp.where(first_half, pltpu.roll(yt, LANES - HEAD_DIM // 2, 1),
                                    pltpu.roll(yt, HEAD_DIM // 2, 1))
                yt = yt * cos + partner * sin
            if scale != 1.0:
                yt = yt * scale
            if name in ("ks", "vs"):
                other = pltpu.roll(yt, HEAD_DIM, 1)
                outs[name][:, 0:LANES] = jnp.where(low_head, yt, other).astype(BF16)
                outs[name][:, LANES:2 * LANES] = jnp.where(low_head, other, yt).astype(BF16)
            else:
                outs[name][:, o0 + t * LANES:o0 + (t + 1) * LANES] = yt.astype(BF16)


def _inproj(x2, g, w_bf, cos_t, sin_t, tm, rows_per_seq):
    rows = x2.shape[0]
    nseq = rows_per_seq // tm
    row_spec = lambda width: pl.BlockSpec((tm, width), lambda i: (i, 0))
    out_shapes = (
        jax.ShapeDtypeStruct((rows, SWA_Q_W), BF16), jax.ShapeDtypeStruct((rows, 2 * SWA_KV_W), BF16),
        jax.ShapeDtypeStruct((rows, 2 * SWA_KV_W), BF16), jax.ShapeDtypeStruct((rows, DIFF_W), BF16),
        jax.ShapeDtypeStruct((rows, DIFF_W), BF16), jax.ShapeDtypeStruct((rows, DIFF_W), BF16),
    )
    return pl.pallas_call(
        _inproj_kernel,
        grid=(rows // tm,),
        in_specs=[
            row_spec(D_MODEL),
            pl.BlockSpec((1, D_MODEL), lambda i: (0, 0)),
            pl.BlockSpec((D_MODEL, IN_WIDTH), lambda i: (0, 0), pipeline_mode=pl.Buffered(1)),
            pl.BlockSpec((tm, LANES), lambda i: (i % nseq, 0)),
            pl.BlockSpec((tm, LANES), lambda i: (i % nseq, 0)),
        ],
        out_specs=(row_spec(SWA_Q_W), row_spec(2 * SWA_KV_W), row_spec(2 * SWA_KV_W),
                   row_spec(DIFF_W), row_spec(DIFF_W), row_spec(DIFF_W)),
        out_shape=out_shapes,
        compiler_params=pltpu.CompilerParams(dimension_semantics=("arbitrary",),
                                             vmem_limit_bytes=VMEM_LIMIT),
        name="inproj",
    )(x2, g, w_bf, cos_t, sin_t)


def _rope_tables(positions):
    inv = 1.0 / (ROPE_THETA ** (jnp.arange(0, HEAD_DIM, 2, dtype=F32) / HEAD_DIM))
    ang = positions.astype(F32)[:, None] * inv[None, :]
    cos, sin = jnp.cos(ang), jnp.sin(ang)
    cos_t = jnp.tile(cos, (1, 2 * LANES // HEAD_DIM))
    sin_t = jnp.tile(jnp.concatenate([-sin, sin], axis=1), (1, LANES // HEAD_DIM))
    return cos_t, sin_t


SWA_BLKS_PER_STEP = 8
SWA_SLAB = 256
SWA_BAND = 3 * CHUNK + N_META


def _swa_kernel(sink_ref, q_ref, kp_ref, kc_ref, km_ref, vp_ref, vc_ref, vm_ref, o_ref):
    i = pl.program_id(1)
    pairs = SWA_Q_HEADS // SWA_KV_HEADS // 2
    m_rows = pairs * 2 * CHUNK
    col = lax.broadcasted_iota(jnp.int32, (m_rows, SWA_SLAB), 1)
    row = lax.broadcasted_iota(jnp.int32, (m_rows, 1), 0)
    lane = lax.broadcasted_iota(jnp.int32, (CHUNK, LANES), 1)
    low_head = lane < HEAD_DIM
    zero = jnp.zeros((CHUNK, LANES), BF16)
    pad = jnp.zeros((SWA_SLAB - SWA_BAND, LANES), BF16)
    meta_rows = slice(PAD_FRONT, Q_BLOCK)
    for g in range(SWA_KV_HEADS):
        gl = slice(g * LANES, (g + 1) * LANES)
        sink = jnp.zeros((m_rows, 1), F32)
        for h in range(2 * pairs):
            sink = jnp.where(row // CHUNK == h, sink_ref[2 * g * pairs + h], sink)
        sink = sink * LOG2_E
        k_seq = jnp.concatenate([kp_ref[:, gl], kc_ref[:, gl]], axis=0)
        v_seq = jnp.concatenate([vp_ref[:, gl], vc_ref[:, gl]], axis=0)
        for half in range(2 * SWA_BLKS_PER_STEP):
            q_rows = slice(half * CHUNK, (half + 1) * CHUNK)
            band = slice(half * CHUNK, (half + 3) * CHUNK)
            k_slab = jnp.concatenate([k_seq[band], km_ref[meta_rows, gl], pad], axis=0)
            v_slab = jnp.concatenate([v_seq[band], vm_ref[meta_rows, gl], pad], axis=0)
            n_missing = max(0, (2 - half) * CHUNK)
            valid = (col < SWA_BAND) & ((i > 0) | (col >= n_missing))
            q2 = []
            for c in range(g * pairs, (g + 1) * pairs):
                qp = q_ref[q_rows, c * LANES:(c + 1) * LANES]
                q2 += [jnp.where(low_head, qp, zero), jnp.where(low_head, zero, qp)]
            s = jnp.where(valid, _nt_dot(jnp.concatenate(q2, axis=0), k_slab), NEG_INF)
            m = jnp.maximum(jnp.max(s, axis=-1, keepdims=True), sink)
            p = jnp.exp2(s - m)
            denom = jnp.sum(p, axis=-1, keepdims=True) + jnp.exp2(sink - m)
            o2 = jnp.dot(p.astype(BF16), v_slab, preferred_element_type=F32) / denom
            for j, c in enumerate(range(g * pairs, (g + 1) * pairs)):
                lo = o2[(2 * j) * CHUNK:(2 * j + 1) * CHUNK]
                hi = o2[(2 * j + 1) * CHUNK:(2 * j + 2) * CHUNK]
                o_ref[q_rows, c * LANES:(c + 1) * LANES] = jnp.where(low_head, lo, hi).astype(BF16)


def _swa(sinks, qs, ks, vs, ks_meta, vs_meta, batch, nblk):
    rows = qs.shape[0]
    steps = nblk // SWA_BLKS_PER_STEP
    step_rows = SWA_BLKS_PER_STEP * Q_BLOCK
    cur = lambda b, i: (b * steps + i, 0)
    prev = lambda b, i: (b * nblk + jnp.maximum(SWA_BLKS_PER_STEP * i - 1, 0), 0)
    kv_w = 2 * SWA_KV_W
    return pl.pallas_call(
        _swa_kernel,
        grid=(batch, steps),
        in_specs=[
            pl.BlockSpec(memory_space=pltpu.SMEM),
            pl.BlockSpec((step_rows, SWA_Q_W), cur),
            pl.BlockSpec((Q_BLOCK, kv_w), prev), pl.BlockSpec((step_rows, kv_w), cur),
            pl.BlockSpec((Q_BLOCK, kv_w), lambda b, i: (0, 0)),
            pl.BlockSpec((Q_BLOCK, kv_w), prev), pl.BlockSpec((step_rows, kv_w), cur),
            pl.BlockSpec((Q_BLOCK, kv_w), lambda b, i: (0, 0)),
        ],
        out_specs=pl.BlockSpec((step_rows, SWA_Q_W), cur),
        out_shape=jax.ShapeDtypeStruct((rows, SWA_Q_W), BF16),
        compiler_params=pltpu.CompilerParams(dimension_semantics=("arbitrary", "arbitrary"),
                                             vmem_limit_bytes=VMEM_LIMIT),
        name="swa",
    )(sinks, qs, ks, ks, ks_meta, vs, vs, vs_meta)


DIFF_TQ = 256
DIFF_TK = 512
DIFF_HEADS_PER_STEP = 2


def _diff_kernel(lam_ref, g_ref, q_ref, k_ref, v_ref, km_ref, vm_ref, o_ref, *, seq):
    dl = lam_ref[...]
    lam = (jnp.exp(jnp.sum(dl[0:1] * dl[1:2], axis=-1, keepdims=True))
           - jnp.exp(jnp.sum(dl[2:3] * dl[3:4], axis=-1, keepdims=True)) + LAM_INIT)
    lane = lax.broadcasted_iota(jnp.int32, (DIFF_TQ, LANES), 1)
    low_map = lane < HEAD_DIM
    zero = jnp.zeros((DIFF_TQ, LANES), BF16)

    def last_mask(n_keys):
        shape = (2 * DIFF_TQ, Q_BLOCK + n_keys)
        col = lax.broadcasted_iota(jnp.int32, shape, 1)
        q_chunk = (lax.broadcasted_iota(jnp.int32, shape, 0) % DIFF_TQ) // CHUNK
        first_chunk = (DIFF_TQ - n_keys) // CHUNK
        k_chunk = (col - Q_BLOCK) // CHUNK + first_chunk
        return ((col >= PAD_FRONT) & (col < Q_BLOCK)) | ((col >= Q_BLOCK) & (k_chunk <= q_chunk))

    def step(q2, k_t, v_t, mask, state):
        s = _nt_dot(q2, k_t)
        if mask is not None:
            s = jnp.where(mask, s, NEG_INF)
        smax = jnp.max(s, axis=-1, keepdims=True)
        if state is None:
            p = jnp.exp2(s - smax)
            return smax, jnp.sum(p, axis=-1, keepdims=True), jnp.dot(
                p.astype(BF16), v_t, preferred_element_type=F32)
        m_old, l_old, acc = state
        m_new = jnp.maximum(m_old, smax)
        alpha = jnp.exp2(m_old - m_new)
        p = jnp.exp2(s - m_new)
        return (m_new, alpha * l_old + jnp.sum(p, axis=-1, keepdims=True),
                alpha * acc + jnp.dot(p.astype(BF16), v_t, preferred_element_type=F32))

    heads = [slice(h * LANES, (h + 1) * LANES) for h in range(DIFF_HEADS_PER_STEP)]

    tiles_per_k = DIFF_TK // DIFF_TQ
    masks = [last_mask((t + 1) * DIFF_TQ) for t in range(tiles_per_k)]

    def q_tile(n_full, t):
        if n_full is None:
            aligned = lambda v: v
            q0 = t * DIFF_TQ
        else:
            aligned = lambda v: pl.multiple_of(v, DIFF_TQ)
            q0 = aligned((n_full * tiles_per_k + t) * DIFF_TQ)
        q2s = []
        for hs in heads:
            q = q_ref[pl.ds(q0, DIFF_TQ), hs]
            q2s.append(jnp.concatenate(
                [jnp.where(low_map, q, zero), jnp.where(low_map, zero, q)], axis=0))

        def full_tiles(kj, states):
            rows = pl.ds(pl.multiple_of(kj * DIFF_TK, DIFF_TK), DIFF_TK)
            out = ()
            for h, hs in enumerate(heads):
                st = None if states is None else states[3 * h:3 * h + 3]
                out = out + step(q2s[h], k_ref[rows, hs], v_ref[rows, hs], None, st)
            return out

        if n_full is None:
            states = None
        else:
            states = lax.fori_loop(1, n_full, full_tiles, full_tiles(0, None))

        n_keys = (t + 1) * DIFF_TQ
        rows = pl.ds(aligned(q0 + DIFF_TQ - n_keys), n_keys)
        for h, hs in enumerate(heads):
            st = None if states is None else states[3 * h:3 * h + 3]
            k_t = jnp.concatenate([km_ref[:, hs], k_ref[rows, hs]], axis=0)
            v_t = jnp.concatenate([vm_ref[:, hs], v_ref[rows, hs]], axis=0)
            _, l, acc = step(q2s[h], k_t, v_t, masks[t], st)
            o = acc / l
            o = o[:DIFF_TQ] - lam * o[DIFF_TQ:]
            o = o * lax.rsqrt(jnp.mean(o * o, axis=-1, keepdims=True) + RMS_EPS)
            o_ref[pl.ds(q0, DIFF_TQ), hs] = (o * g_ref[...] * (1.0 - LAM_INIT)).astype(BF16)

    def q_group(n_full, carry):
        for t in range(tiles_per_k):
            q_tile(n_full, t)
        return carry

    q_group(None, 0)
    lax.fori_loop(1, seq // DIFF_TK, q_group, 0)


def _diff(diff_lambda, subln_g, qd, kd, vd, kd_meta, vd_meta, batch, seq):
    rows = qd.shape[0]
    head = lambda b, h: (b, h)
    meta = lambda b, h: (0, h)
    width = DIFF_HEADS_PER_STEP * LANES
    return pl.pallas_call(
        functools.partial(_diff_kernel, seq=seq),
        grid=(batch, DIFF_HEADS // DIFF_HEADS_PER_STEP),
        in_specs=[
            pl.BlockSpec((4, HEAD_DIM), lambda b, h: (0, 0)),
            pl.BlockSpec((1, LANES), lambda b, h: (0, 0)),
            pl.BlockSpec((seq, width), head), pl.BlockSpec((seq, width), head),
            pl.BlockSpec((seq, width), head),
            pl.BlockSpec((Q_BLOCK, width), meta), pl.BlockSpec((Q_BLOCK, width), meta),
        ],
        out_specs=pl.BlockSpec((seq, width), head),
        out_shape=jax.ShapeDtypeStruct((rows, DIFF_W), BF16),
        compiler_params=pltpu.CompilerParams(dimension_semantics=("arbitrary", "arbitrary"),
                                             vmem_limit_bytes=VMEM_LIMIT),
        name="diffattn",
    )(diff_lambda, subln_g, qd, kd, vd, kd_meta, vd_meta)


HALF = D_MODEL // 2


def _pack_bf16_pair(a, b):
    hi = pltpu.bitcast(a.astype(BF16).astype(F32), jnp.uint32)
    lo = pltpu.bitcast(b.astype(BF16).astype(F32), jnp.uint32)
    return hi | (lo >> 16)


def _unpack_bf16_pair(w):
    a = pltpu.bitcast(w & jnp.uint32(0xFFFF0000), F32)
    b = pltpu.bitcast(w << 16, F32)
    return a, b


OUT_CHUNK = 512


def _outproj_kernel(os_ref, od_ref, x_ref, w_ref, g_ref, wr_ref, h_ref, fp_ref, lg_ref):
    tm = x_ref.shape[0]
    ssq = jnp.zeros((tm, 1), F32)
    for c in range(D_MODEL // OUT_CHUNK):
        cols = slice(c * OUT_CHUNK, (c + 1) * OUT_CHUNK)
        y = jnp.dot(os_ref[...], w_ref[0:SWA_Q_W, cols], preferred_element_type=F32)
        y = y + jnp.dot(od_ref[...], w_ref[SWA_Q_W:, cols], preferred_element_type=F32)
        h = x_ref[:, cols] + y
        h_ref[:, cols] = h
        ssq = ssq + jnp.sum(h * h, axis=-1, keepdims=True)
    inv = lax.rsqrt(ssq * (1.0 / D_MODEL) + RMS_EPS)
    lg = jnp.zeros((tm, LANES), F32)
    for c in range(HALF // OUT_CHUNK):
        ca = slice(c * OUT_CHUNK, (c + 1) * OUT_CHUNK)
        cb = slice(HALF + c * OUT_CHUNK, HALF + (c + 1) * OUT_CHUNK)
        fa = h_ref[:, ca] * inv * g_ref[:, ca]
        fb = h_ref[:, cb] * inv * g_ref[:, cb]
        fp_ref[:, ca] = _pack_bf16_pair(fa, fb)
        for f, cc in ((fa, ca), (fb, cb)):
            f_hi = f.astype(BF16)
            f_lo = (f - f_hi.astype(F32)).astype(BF16)
            w = wr_ref[cc, :]
            w_hi = w.astype(BF16)
            w_lo = (w - w_hi.astype(F32)).astype(BF16)
            p = jnp.dot(f_hi, jnp.concatenate([w_hi, w_lo], axis=1), preferred_element_type=F32)
            lg = lg + p[:, 0:LANES] + p[:, LANES:] + jnp.dot(f_lo, w_hi, preferred_element_type=F32)
    lg_ref[...] = lg


def _outproj(o_s, o_d, x2, w_bf, g, w_router, tm):
    rows = x2.shape[0]
    row_spec = lambda width: pl.BlockSpec((tm, width), lambda i: (i, 0))
    return pl.pallas_call(
        _outproj_kernel,
        grid=(rows // tm,),
        in_specs=[
            row_spec(SWA_Q_W), row_spec(DIFF_W), row_spec(D_MODEL),
            pl.BlockSpec((D_MODEL, D_MODEL), lambda i: (0, 0), pipeline_mode=pl.Buffered(1)),
            pl.BlockSpec((1, D_MODEL), lambda i: (0, 0)),
            pl.BlockSpec((D_MODEL, LANES), lambda i: (0, 0)),
        ],
        out_specs=(row_spec(D_MODEL), row_spec(HALF), row_spec(LANES)),
        out_shape=(jax.ShapeDtypeStruct((rows, D_MODEL), F32),
                   jax.ShapeDtypeStruct((rows, HALF), jnp.uint32),
                   jax.ShapeDtypeStruct((rows, LANES), F32)),
        compiler_params=pltpu.CompilerParams(dimension_semantics=("arbitrary",),
                                             vmem_limit_bytes=VMEM_LIMIT),
        name="outproj",
    )(o_s, o_d, x2, w_bf, g, w_router)


DISPATCH_TM = 1024


def _dispatch_kernel(zero_blk_ref, n_zero_ref, dest_ref, f_ref, xs_ref, zeros, sem, zsem):
    i = pl.program_id(0)

    @pl.when(i == 0)
    def _():
        zeros[...] = jnp.zeros_like(zeros)

        def zero_copy(j):
            rows = pl.ds(zero_blk_ref[j] * ROW_BLK, ROW_BLK)
            return pltpu.make_async_copy(zeros, xs_ref.at[rows], zsem.at[0])

        def start(j, c):
            zero_copy(j).start()
            return c

        def wait(j, c):
            zero_copy(j).wait()
            return c
        lax.fori_loop(0, n_zero_ref[0], start, 0)
        lax.fori_loop(0, n_zero_ref[0], wait, 0)

    def issue(r, c):
        for kk in range(TOP_K):
            pltpu.make_async_copy(f_ref.at[pl.ds(r, 1)], xs_ref.at[pl.ds(dest_ref[TOP_K * r + kk], 1)],
                                  sem.at[0]).start(priority=kk % 2)
        return c
    lax.fori_loop(0, DISPATCH_TM, issue, 0, unroll=8)
    n_rows = TOP_K * DISPATCH_TM
    pltpu.make_async_copy(xs_ref.at[pl.ds(0, n_rows)], xs_ref.at[pl.ds(0, n_rows)], sem.at[0]).wait()


def _dispatch(zero_blk, n_zero, dest_flat, f_packed, n_blocks):
    rows = f_packed.shape[0]
    return pl.pallas_call(
        _dispatch_kernel,
        grid_spec=pltpu.PrefetchScalarGridSpec(
            num_scalar_prefetch=2,
            grid=(rows // DISPATCH_TM,),
            in_specs=[
                pl.BlockSpec((TOP_K * DISPATCH_TM,), lambda i, z, n: (i,), memory_space=pltpu.SMEM),
                pl.BlockSpec((DISPATCH_TM, HALF), lambda i, z, n: (i, 0)),
            ],
            out_specs=pl.BlockSpec(memory_space=pl.ANY),
            scratch_shapes=[pltpu.VMEM((ROW_BLK, HALF), jnp.uint32),
                            pltpu.SemaphoreType.DMA((1,)),
                            pltpu.SemaphoreType.DMA((1,))],
        ),
        out_shape=jax.ShapeDtypeStruct((n_blocks * ROW_BLK, HALF), jnp.uint32),
        compiler_params=pltpu.CompilerParams(dimension_semantics=("arbitrary",),
                                             vmem_limit_bytes=VMEM_LIMIT),
        name="dispatch",
    )(zero_blk, n_zero, dest_flat, f_packed)


ITEM_BLKS = 6
W_BUFS = 3
ITEM_ROWS = ITEM_BLKS * ROW_BLK
F_TILE = 256
N_FT = -(-D_EXPERT // F_TILE)
Y_TILE = 512
N_YT = D_MODEL // Y_TILE


def _f_tile_start(k):
    return pl.multiple_of(jnp.minimum(k * F_TILE, D_EXPERT - F_TILE), LANES)


def _moe_kernel(e_ref, r0_ref, nb_ref, meta_ref, xs_ref, wg_hbm, wu_hbm, wd_hbm, y_ref,
                xraw, xb, hid, ystage, wgf, wuf, wdf, wgu, wdr, xsem, ysem, wsem):
    n_used = meta_ref[0]
    n_live = meta_ref[1]

    def for_blocks(count, fn):
        def body(j, c):
            fn(j)
            return c
        lax.fori_loop(0, count, body, 0)

    def x_copy(item, buf, j):
        src = xs_ref.at[pl.ds((r0_ref[item] + j) * ROW_BLK, ROW_BLK)]
        return pltpu.make_async_copy(src, xraw.at[buf, pl.ds(j * ROW_BLK, ROW_BLK)],
                                     xsem.at[buf, j])

    def w_copies(item, k, buf):
        e = e_ref[item]
        cols = pl.ds(_f_tile_start(k), F_TILE)
        return (pltpu.make_async_copy(wg_hbm.at[e, :, cols], wgf.at[buf], wsem.at[0, buf]),
                pltpu.make_async_copy(wu_hbm.at[e, :, cols], wuf.at[buf], wsem.at[1, buf]),
                pltpu.make_async_copy(wd_hbm.at[e, cols, :], wdf.at[buf], wsem.at[2, buf]))

    def zero_copy(j):
        dst = y_ref.at[pl.ds((n_used + j) * ROW_BLK, ROW_BLK)]
        return pltpu.make_async_copy(ystage.at[pl.ds(0, ROW_BLK)], dst, ysem.at[0])
    n_unused = y_ref.shape[0] // ROW_BLK - n_used
    ystage[0:ROW_BLK, :] = jnp.zeros((ROW_BLK, HALF), jnp.uint32)
    for_blocks(n_unused, lambda j: zero_copy(j).start())
    for_blocks(n_unused, lambda j: zero_copy(j).wait())

    for_blocks(nb_ref[0], lambda j: x_copy(0, 0, j).start())

    @pl.when(nb_ref[0] > 0)
    def _():
        for s in range(W_BUFS - 1):
            for c in w_copies(0, s, s):
                c.start()

    def item_body(i, carry):
        slot = i % 2
        nb = nb_ref[i]
        prev = jnp.maximum(i - 1, 0)
        nb_prev = jnp.where(i > 0, nb_ref[prev], 0)

        def y_copy(j):
            dst = y_ref.at[pl.ds((r0_ref[prev] + j) * ROW_BLK, ROW_BLK)]
            return pltpu.make_async_copy(ystage.at[pl.ds(j * ROW_BLK, ROW_BLK)], dst, ysem.at[0])

        for_blocks(nb_ref[i + 1], lambda j: x_copy(i + 1, 1 - slot, j).start())

        def arrive(j):
            x_copy(i, slot, j).wait()
            rows = pl.ds(pl.multiple_of(j * ROW_BLK, ROW_BLK), ROW_BLK)
            lo, hi = _unpack_bf16_pair(xraw[slot, rows, :])
            xb[rows, 0:HALF] = lo.astype(BF16)
            xb[rows, HALF:] = hi.astype(BF16)
        for_blocks(nb, arrive)

        def tile_body(k, carry):
            step = i * N_FT + k
            wbuf = step % W_BUFS
            last = k == N_FT - 1
            ahead = k + (W_BUFS - 1)
            nxt_i = jnp.where(ahead >= N_FT, i + 1, i)
            nxt_k = jnp.where(ahead >= N_FT, ahead - N_FT, ahead)

            @pl.when(nb_ref[nxt_i] > 0)
            def _():
                for c in w_copies(nxt_i, nxt_k, (step + (W_BUFS - 1)) % W_BUFS):
                    c.start()

            def cast_tiles():
                wgu[:, 0:F_TILE] = wgf[wbuf].astype(BF16)
                wgu[:, F_TILE:] = wuf[wbuf].astype(BF16)
                f_rows = pl.ds(_f_tile_start(k), F_TILE)
                for n in range(N_YT):
                    wdr[slot, n, f_rows, :] = wdf[wbuf, :, n * Y_TILE:(n + 1) * Y_TILE].astype(BF16)

            def gate_up(row0, nrows):
                rows = pl.ds(pl.multiple_of(row0, ROW_BLK), nrows)
                gu = jnp.dot(xb[rows, :], wgu[...], preferred_element_type=F32)
                gate, up = gu[:, 0:F_TILE], gu[:, F_TILE:]
                h = (gate * (1.0 / (1.0 + jnp.exp(-gate))) * up).astype(BF16)
                hid[slot, rows, pl.ds(_f_tile_start(k), F_TILE)] = h

            k_y = jnp.minimum(k, N_YT - 1)

            def down(row0, nrows):
                rows = pl.ds(pl.multiple_of(row0, ROW_BLK), nrows)
                yt = jnp.dot(hid[1 - slot, rows, :], wdr[1 - slot, k_y], preferred_element_type=F32)
                packed = _pack_bf16_pair(yt[:, 0:Y_TILE // 2], yt[:, Y_TILE // 2:])
                words = pl.ds(pl.multiple_of(k_y * (Y_TILE // 2), Y_TILE // 2), Y_TILE // 2)
                ystage[rows, words] = packed

            def for_row_tiles(n_blk, fn):
                for whole in range(1, ITEM_BLKS + 1):
                    @pl.when(n_blk == whole)
                    def _():
                        fn(0, whole * ROW_BLK)

            @pl.when(nb > 0)
            def _():
                for c in w_copies(i, k, wbuf):
                    c.wait()
                cast_tiles()

            for_row_tiles(nb, gate_up)
            for_row_tiles(jnp.where(k < N_YT, nb_prev, 0), down)

            @pl.when(k == N_YT)
            def _():
                for_blocks(nb_prev, lambda j: y_copy(j).start())

            @pl.when(last)
            def _():
                for_blocks(nb_prev, lambda j: y_copy(j).wait())
            return carry

        lax.fori_loop(0, N_FT, tile_body, 0)
        return carry

    lax.fori_loop(0, n_live + 1, item_body, 0)


def _moe(item_e, item_r0, item_nb, meta, xs, w_gate, w_up, w_down, n_blocks):
    return pl.pallas_call(
        _moe_kernel,
        grid_spec=pltpu.PrefetchScalarGridSpec(
            num_scalar_prefetch=4,
            grid=(1,),
            in_specs=[pl.BlockSpec(memory_space=pl.ANY)] * 4,
            out_specs=pl.BlockSpec(memory_space=pl.ANY),
            scratch_shapes=[
                pltpu.VMEM((2, ITEM_ROWS, HALF), jnp.uint32),
                pltpu.VMEM((ITEM_ROWS, D_MODEL), BF16),
                pltpu.VMEM((2, ITEM_ROWS, D_EXPERT), BF16),
                pltpu.VMEM((ITEM_ROWS, HALF), jnp.uint32),
                pltpu.VMEM((W_BUFS, D_MODEL, F_TILE), F32),
                pltpu.VMEM((W_BUFS, D_MODEL, F_TILE), F32),
                pltpu.VMEM((W_BUFS, F_TILE, D_MODEL), F32),
                pltpu.VMEM((D_MODEL, 2 * F_TILE), BF16),
                pltpu.VMEM((2, N_YT, D_EXPERT, Y_TILE), BF16),
                pltpu.SemaphoreType.DMA((2, ITEM_BLKS)),
                pltpu.SemaphoreType.DMA((1,)),
                pltpu.SemaphoreType.DMA((3, W_BUFS)),
            ],
        ),
        out_shape=jax.ShapeDtypeStruct((n_blocks * ROW_BLK, HALF), jnp.uint32),
        compiler_params=pltpu.CompilerParams(dimension_semantics=("arbitrary",),
                                             vmem_limit_bytes=VMEM_LIMIT),
        name="moe",
    )(item_e, item_r0, item_nb, meta, xs, w_gate, w_up, w_down)


def _combine_kernel(cur_ref, nxt_ref, h_ref, w_ref, g_ref, y_ref, o_ref, ya, yb, sem, *, tm):
    i = pl.program_id(0)
    n_steps = pl.num_programs(0)
    slot = i % 2

    def issue(idx_ref, buf, r):
        pltpu.make_async_copy(y_ref.at[pl.ds(idx_ref[TOP_K * r], 1)], ya.at[buf, pl.ds(r, 1)],
                              sem.at[buf]).start(priority=0)
        pltpu.make_async_copy(y_ref.at[pl.ds(idx_ref[TOP_K * r + 1], 1)], yb.at[buf, pl.ds(r, 1)],
                              sem.at[buf]).start(priority=1)

    def wait(buf):
        pltpu.make_async_copy(ya.at[buf], ya.at[buf], sem.at[buf]).wait()
        pltpu.make_async_copy(yb.at[buf], yb.at[buf], sem.at[buf]).wait()

    @pl.when(i == 0)
    def _():
        def body(r, c):
            issue(cur_ref, 0, r)
            return c
        lax.fori_loop(0, tm, body, 0, unroll=8)

    wait(slot)
    for r in range(tm):
        issue(nxt_ref, 1 - slot, r)

    wa = w_ref[:, 0:1]
    wb = w_ref[:, 1:2]
    half_t = Y_TILE // 2
    ssq = jnp.zeros((tm, 1), F32)
    for n in range(N_YT):
        words = slice(n * half_t, (n + 1) * half_t)
        a_lo, a_hi = _unpack_bf16_pair(ya[slot, :, words])
        b_lo, b_hi = _unpack_bf16_pair(yb[slot, :, words])
        lo_cols = slice(n * Y_TILE, n * Y_TILE + half_t)
        hi_cols = slice(n * Y_TILE + half_t, (n + 1) * Y_TILE)
        lo = h_ref[:, lo_cols] + wa * a_lo + wb * b_lo
        hi = h_ref[:, hi_cols] + wa * a_hi + wb * b_hi
        ssq = ssq + jnp.sum(lo * lo, axis=-1, keepdims=True) + jnp.sum(hi * hi, axis=-1, keepdims=True)
        o_ref[:, lo_cols] = lo
        o_ref[:, hi_cols] = hi
    inv = lax.rsqrt(ssq * (1.0 / D_MODEL) + RMS_EPS)
    o_ref[...] = o_ref[...] * inv * g_ref[...]

    @pl.when(i == n_steps - 1)
    def _():
        wait(1 - slot)


def _combine(dest_flat, h2, wts, g, y, tm):
    rows = h2.shape[0]
    steps = rows // tm
    idx_spec = lambda fn: pl.BlockSpec((TOP_K * tm,), fn, memory_space=pltpu.SMEM)
    return pl.pallas_call(
        functools.partial(_combine_kernel, tm=tm),
        grid=(steps,),
        in_specs=[
            idx_spec(lambda i: (i,)),
            idx_spec(lambda i: (jnp.minimum(i + 1, steps - 1),)),
            pl.BlockSpec((tm, D_MODEL), lambda i: (i, 0)),
            pl.BlockSpec((tm, LANES), lambda i: (i, 0)),
            pl.BlockSpec((1, D_MODEL), lambda i: (0, 0)),
            pl.BlockSpec(memory_space=pl.ANY),
        ],
        out_specs=pl.BlockSpec((tm, D_MODEL), lambda i: (i, 0)),
        scratch_shapes=[
            pltpu.VMEM((2, tm, HALF), jnp.uint32),
            pltpu.VMEM((2, tm, HALF), jnp.uint32),
            pltpu.SemaphoreType.DMA((2,)),
        ],
        out_shape=jax.ShapeDtypeStruct((rows, D_MODEL), F32),
        compiler_params=pltpu.CompilerParams(dimension_semantics=("arbitrary",),
                                             vmem_limit_bytes=VMEM_LIMIT),
        name="combine",
    )(dest_flat, dest_flat, h2, wts, g, y)


ROUTE_TM = 1024


def _route_kernel(lg_ref, info_ref, wts_ref, tab_ref, counts, starts, carry):
    phase = pl.program_id(0)
    i = pl.program_id(1)
    tm = ROUTE_TM
    lane = lax.broadcasted_iota(jnp.int32, (tm, LANES), 1)
    lg = lg_ref[...]

    def first_lane_of_max(v):
        vmax = jnp.max(v, axis=-1, keepdims=True)
        return vmax, jnp.min(jnp.where(v == vmax, lane, LANES), axis=-1, keepdims=True)

    gl = jnp.where(lane < N_GROUPS, lg, NEG_INF)
    gmax, g_sel = first_lane_of_max(gl)
    gate = 1.0 / jnp.sum(jnp.exp(gl - gmax), axis=-1, keepdims=True)
    lo = N_GROUPS + EXPERTS_PER_GROUP * g_sel
    el = jnp.where((lane >= lo) & (lane < lo + EXPERTS_PER_GROUP), lg, NEG_INF)
    m1, l1 = first_lane_of_max(el)
    m2, l2 = first_lane_of_max(jnp.where(lane == l1, NEG_INF, el))
    ratio = jnp.exp(m2 - m1)
    w1 = gate / (1.0 + ratio)
    w2 = w1 * ratio
    e1 = l1 - N_GROUPS
    e2 = l2 - N_GROUPS
    hit1 = lane == e1
    hit2 = lane == e2
    onehot = (hit1 | hit2).astype(F32)
    tile_counts = jnp.sum(onehot, axis=0, keepdims=True)

    @pl.when((phase == 0) & (i == 0))
    def _():
        counts[...] = jnp.zeros_like(counts)

    @pl.when(phase == 0)
    def _():
        counts[...] += tile_counts

    @pl.when((phase == 1) & (i == 0))
    def _():
        blocks = jnp.floor((counts[...] + (ROW_BLK - 1)) * (1.0 / ROW_BLK))
        r = lax.broadcasted_iota(jnp.int32, (LANES, LANES), 0)
        c = lax.broadcasted_iota(jnp.int32, (LANES, LANES), 1)
        before = (r < c).astype(F32)
        first = jnp.dot(jnp.broadcast_to(blocks, (8, LANES)), before,
                        preferred_element_type=F32, precision=lax.Precision.HIGHEST)[0:1]
        starts[...] = first
        carry[...] = jnp.zeros_like(carry)
        used = jnp.sum(blocks, axis=-1, keepdims=True)
        row = lax.broadcasted_iota(jnp.int32, (8, LANES), 0)
        tab = jnp.where(row == 0, blocks, jnp.where(row == 1, first, jnp.where(row == 2, used, 0.0)))
        tab_ref[...] = tab.astype(jnp.int32)

    @pl.when(phase == 1)
    def _():
        r = lax.broadcasted_iota(jnp.int32, (tm, tm), 0)
        c = lax.broadcasted_iota(jnp.int32, (tm, tm), 1)
        earlier = (c < r).astype(BF16)
        before = jnp.dot(earlier, onehot.astype(BF16), preferred_element_type=F32)
        slot = starts[...] * ROW_BLK + carry[...] + before
        d1 = jnp.sum(jnp.where(hit1, slot, 0.0), axis=-1, keepdims=True)
        d2 = jnp.sum(jnp.where(hit2, slot, 0.0), axis=-1, keepdims=True)
        info_ref[...] = jnp.where(lane == 0, d1, jnp.where(lane == 1, d2, 0.0)).astype(jnp.int32)
        wts_ref[...] = jnp.where(lane == 0, w1, jnp.where(lane == 1, w2, 0.0))
        carry[...] += tile_counts


def _route(logits):
    rows = logits.shape[0]
    tile = lambda p, i: (i * p, 0)
    return pl.pallas_call(
        _route_kernel,
        grid=(2, rows // ROUTE_TM),
        in_specs=[pl.BlockSpec((ROUTE_TM, LANES), lambda p, i: (i, 0))],
        out_specs=(pl.BlockSpec((ROUTE_TM, LANES), tile), pl.BlockSpec((ROUTE_TM, LANES), tile),
                   pl.BlockSpec((8, LANES), lambda p, i: (0, 0))),
        out_shape=(jax.ShapeDtypeStruct((rows, LANES), jnp.int32),
                   jax.ShapeDtypeStruct((rows, LANES), F32),
                   jax.ShapeDtypeStruct((8, LANES), jnp.int32)),
        scratch_shapes=[pltpu.VMEM((1, LANES), F32)] * 3,
        compiler_params=pltpu.CompilerParams(dimension_semantics=("arbitrary", "arbitrary"),
                                             vmem_limit_bytes=VMEM_LIMIT),
        name="route",
    )(logits)


def _work_items(blocks_e, blk_start, n_items):
    items_e = (blocks_e + ITEM_BLKS - 1) // ITEM_BLKS
    item_end = jnp.cumsum(items_e)
    item_start = item_end - items_e
    total = item_end[-1]
    idx = jnp.arange(n_items, dtype=jnp.int32)
    live = idx < total
    ref = jnp.minimum(idx, total - 1)
    e = jnp.searchsorted(item_end, ref, side='right').astype(jnp.int32)
    j = ref - item_start[e]
    r0 = blk_start[e] + ITEM_BLKS * j
    nb = jnp.where(live, jnp.clip(blocks_e[e] - ITEM_BLKS * j, 0, ITEM_BLKS), 0)
    return e, r0.astype(jnp.int32), nb.astype(jnp.int32), total.astype(jnp.int32)


def kernel(x, meta_tokens, attn_norm_g, w_in, swa_sinks, diff_lambda, diff_subln_g, w_out,
           ffn_norm_g, w_router_group, w_router_expert, w_gate, w_up, w_down, final_norm_g):
    batch, seq, _ = x.shape
    n_tok = batch * seq
    nblk = seq // Q_BLOCK
    x2 = x.reshape(n_tok, D_MODEL)

    w_in_bf = w_in[0].astype(BF16)
    g_attn = attn_norm_g[0].reshape(1, D_MODEL)
    cos_x, sin_x = _rope_tables(N_META + jnp.arange(seq))
    cos_m, sin_m = _rope_tables(jnp.maximum(jnp.arange(Q_BLOCK) - PAD_FRONT, 0))
    qs, ks, vs, qd, kd, vd = _inproj(x2, g_attn, w_in_bf, cos_x, sin_x, 512, seq)
    meta_blk = jnp.concatenate(
        [jnp.zeros((PAD_FRONT, D_MODEL), x.dtype), meta_tokens.astype(x.dtype)], axis=0)
    _, ks_m, vs_m, _, kd_m, vd_m = _inproj(meta_blk, g_attn, w_in_bf, cos_m, sin_m, Q_BLOCK, Q_BLOCK)

    o_s = _swa(swa_sinks[0].astype(F32), qs, ks, vs, ks_m, vs_m, batch, nblk)
    o_d = _diff(diff_lambda[0].astype(F32), diff_subln_g[0].astype(F32).reshape(1, LANES),
                qd, kd, vd, kd_m, vd_m, batch, seq)

    w_router = jnp.concatenate(
        [w_router_group[0], w_router_expert[0],
         jnp.zeros((D_MODEL, LANES - N_GROUPS - N_EXPERTS), F32)], axis=1)
    h2, f_packed, logits = _outproj(o_s, o_d, x2, w_out[0].astype(BF16),
                                    ffn_norm_g[0].reshape(1, D_MODEL), w_router, 512)

    info, wts, tab = _route(logits)
    dest_flat = info[:, :TOP_K].reshape(-1)
    blocks_e, blk_start, n_used = tab[0, :N_EXPERTS], tab[1, :N_EXPERTS], tab[2, 0:1]
    n_blocks = n_tok * TOP_K // ROW_BLK + N_EXPERTS
    n_items = n_blocks // ITEM_BLKS + N_EXPERTS
    blk = jnp.arange(n_blocks, dtype=jnp.int32)
    is_last = jnp.any((blk[:, None] + 1 == (blk_start + blocks_e)[None, :])
                      & (blocks_e > 0)[None, :], axis=1)
    need_zero = is_last | (blk >= n_used[0])
    zero_blk = jnp.nonzero(need_zero, size=2 * N_EXPERTS, fill_value=0)[0].astype(jnp.int32)
    n_zero = jnp.sum(need_zero).astype(jnp.int32).reshape(1)
    xs = _dispatch(zero_blk, n_zero, dest_flat, f_packed, n_blocks)
    item_e, item_r0, item_nb, n_live = _work_items(blocks_e, blk_start, n_items)
    y = _moe(item_e, item_r0, item_nb, jnp.concatenate([n_used, n_live.reshape(1)]), xs,
             w_gate[0], w_up[0], w_down[0], n_blocks)

    out = _combine(dest_flat, h2, wts, final_norm_g.reshape(1, D_MODEL), y, 512)
    return out.reshape(batch, seq, D_MODEL)
```

```python
import functools
import math

import jax
import jax.numpy as jnp
from jax import lax
from jax.experimental import pallas as pl
from jax.experimental.pallas import tpu as pltpu

D_MODEL = 2048
N_META = 16
CHUNK = 64
Q_BLOCK = 128
PAD_FRONT = Q_BLOCK - N_META
HEAD_DIM = 64
ROPE_THETA = 10000.0
SWA_Q_HEADS = 16
SWA_KV_HEADS = 2
DIFF_HEADS = 8
SWA_Q_W = SWA_Q_HEADS * HEAD_DIM
SWA_KV_W = SWA_KV_HEADS * HEAD_DIM
DIFF_W = DIFF_HEADS * 2 * HEAD_DIM
IN_WIDTH = SWA_Q_W + 2 * SWA_KV_W + 3 * DIFF_W
N_GROUPS = 8
EXPERTS_PER_GROUP = 8
N_EXPERTS = N_GROUPS * EXPERTS_PER_GROUP
TOP_K = 2
D_EXPERT = 1408
RMS_EPS = 1e-6
NEG_INF = -1e30
LAM_INIT = 0.8 - 0.6 * math.exp(-0.3 * 0)

LANES = 128
ROW_BLK = 128
VMEM_LIMIT = 56 * 1024 * 1024

BF16 = jnp.bfloat16
F32 = jnp.float32


def _nt_dot(a, b):
    return lax.dot_general(a, b, (((1,), (1,)), ((), ())), preferred_element_type=F32)


LOG2_E = math.log2(math.e)
Q_SCALE = HEAD_DIM ** -0.5 * LOG2_E
_SEGS = (
    ("qs", 0, 512, True, Q_SCALE), ("qs", 512, 512, True, Q_SCALE),
    ("ks", 1024, 128, True, 1.0), ("vs", 1152, 128, False, 1.0),
    ("qd", 1280, 512, True, Q_SCALE), ("qd", 1792, 512, True, Q_SCALE),
    ("kd", 2304, 512, True, 1.0), ("kd", 2816, 512, True, 1.0),
    ("vd", 3328, 512, False, 1.0), ("vd", 3840, 512, False, 1.0),
)
_SEG_BASE = {"qs": 0, "ks": 1024, "vs": 1152, "qd": 1280, "kd": 2304, "vd": 3328}


def _inproj_kernel(x_ref, g_ref, w_ref, cos_ref, sin_ref,
                   qs_ref, ks_ref, vs_ref, qd_ref, kd_ref, vd_ref):
    x = x_ref[...]
    ms = jnp.mean(x * x, axis=-1, keepdims=True)
    a = (x * lax.rsqrt(ms + RMS_EPS) * g_ref[...]).astype(BF16)
    cos = cos_ref[...]
    sin = sin_ref[...]
    lane = lax.broadcasted_iota(jnp.int32, cos.shape, 1)
    first_half = (lane % HEAD_DIM) < (HEAD_DIM // 2)
    low_head = lane < HEAD_DIM
    outs = {"qs": qs_ref, "ks": ks_ref, "vs": vs_ref, "qd": qd_ref, "kd": kd_ref, "vd": vd_ref}
    for name, c0, width, rotary, scale in _SEGS:
        y = jnp.dot(a, w_ref[:, c0:c0 + width], preferred_element_type=F32)
        o0 = c0 - _SEG_BASE[name]
        for t in range(width // LANES):
            yt = y[:, t * LANES:(t + 1) * LANES]
            if rotary:
                partner = jnp.where(first_half, pltpu.roll(yt, LANES - HEAD_DIM // 2, 1),
                                    pltpu.roll(yt, HEAD_DIM // 2, 1))
                yt = yt * cos + partner * sin
            if scale != 1.0:
                yt = yt * scale
            if name in ("ks", "vs"):
                other = pltpu.roll(yt, HEAD_DIM, 1)
                outs[name][:, 0:LANES] = jnp.where(low_head, yt, other).astype(BF16)
                outs[name][:, LANES:2 * LANES] = jnp.where(low_head, other, yt).astype(BF16)
            else:
                outs[name][:, o0 + t * LANES:o0 + (t + 1) * LANES] = yt.astype(BF16)


def _inproj(x2, g, w_bf, cos_t, sin_t, tm, rows_per_seq):
    rows = x2.shape[0]
    nseq = rows_per_seq // tm
    row_spec = lambda width: pl.BlockSpec((tm, width), lambda i: (i, 0))
    out_shapes = (
        jax.ShapeDtypeStruct((rows, SWA_Q_W), BF16), jax.ShapeDtypeStruct((rows, 2 * SWA_KV_W), BF16),
        jax.ShapeDtypeStruct((rows, 2 * SWA_KV_W), BF16), jax.ShapeDtypeStruct((rows, DIFF_W), BF16),
        jax.ShapeDtypeStruct((rows, DIFF_W), BF16), jax.ShapeDtypeStruct((rows, DIFF_W), BF16),
    )
    return pl.pallas_call(
        _inproj_kernel,
        grid=(rows // tm,),
        in_specs=[
            row_spec(D_MODEL),
            pl.BlockSpec((1, D_MODEL), lambda i: (0, 0)),
            pl.BlockSpec((D_MODEL, IN_WIDTH), lambda i: (0, 0), pipeline_mode=pl.Buffered(1)),
            pl.BlockSpec((tm, LANES), lambda i: (i % nseq, 0)),
            pl.BlockSpec((tm, LANES), lambda i: (i % nseq, 0)),
        ],
        out_specs=(row_spec(SWA_Q_W), row_spec(2 * SWA_KV_W), row_spec(2 * SWA_KV_W),
                   row_spec(DIFF_W), row_spec(DIFF_W), row_spec(DIFF_W)),
        out_shape=out_shapes,
        compiler_params=pltpu.CompilerParams(dimension_semantics=("arbitrary",),
                                             vmem_limit_bytes=VMEM_LIMIT),
        name="inproj",
    )(x2, g, w_bf, cos_t, sin_t)


def _rope_tables(positions):
    inv = 1.0 / (ROPE_THETA ** (jnp.arange(0, HEAD_DIM, 2, dtype=F32) / HEAD_DIM))
    ang = positions.astype(F32)[:, None] * inv[None, :]
    cos, sin = jnp.cos(ang), jnp.sin(ang)
    cos_t = jnp.tile(cos, (1, 2 * LANES // HEAD_DIM))
    sin_t = jnp.tile(jnp.concatenate([-sin, sin], axis=1), (1, LANES // HEAD_DIM))
    return cos_t, sin_t


SWA_BLKS_PER_STEP = 8
SWA_SLAB = 256
SWA_BAND = 3 * CHUNK + N_META


def _swa_kernel(sink_ref, q_ref, kp_ref, kc_ref, km_ref, vp_ref, vc_ref, vm_ref, o_ref):
    i = pl.program_id(1)
    pairs = SWA_Q_HEADS // SWA_KV_HEADS // 2
    m_rows = pairs * 2 * CHUNK
    col = lax.broadcasted_iota(jnp.int32, (m_rows, SWA_SLAB), 1)
    row = lax.broadcasted_iota(jnp.int32, (m_rows, 1), 0)
    lane = lax.broadcasted_iota(jnp.int32, (CHUNK, LANES), 1)
    low_head = lane < HEAD_DIM
    zero = jnp.zeros((CHUNK, LANES), BF16)
    pad = jnp.zeros((SWA_SLAB - SWA_BAND, LANES), BF16)
    meta_rows = slice(PAD_FRONT, Q_BLOCK)
    for g in range(SWA_KV_HEADS):
        gl = slice(g * LANES, (g + 1) * LANES)
        sink = jnp.zeros((m_rows, 1), F32)
        for h in range(2 * pairs):
            sink = jnp.where(row // CHUNK == h, sink_ref[2 * g * pairs + h], sink)
        sink = sink * LOG2_E
        k_seq = jnp.concatenate([kp_ref[:, gl], kc_ref[:, gl]], axis=0)
        v_seq = jnp.concatenate([vp_ref[:, gl], vc_ref[:, gl]], axis=0)
        for half in range(2 * SWA_BLKS_PER_STEP):
            q_rows = slice(half * CHUNK, (half + 1) * CHUNK)
            band = slice(half * CHUNK, (half + 3) * CHUNK)
            k_slab = jnp.concatenate([k_seq[band], km_ref[meta_rows, gl], pad], axis=0)
            v_slab = jnp.concatenate([v_seq[band], vm_ref[meta_rows, gl], pad], axis=0)
            n_missing = max(0, (2 - half) * CHUNK)
            valid = (col < SWA_BAND) & ((i > 0) | (col >= n_missing))
            q2 = []
            for c in range(g * pairs, (g + 1) * pairs):
                qp = q_ref[q_rows, c * LANES:(c + 1) * LANES]
                q2 += [jnp.where(low_head, qp, zero), jnp.where(low_head, zero, qp)]
            s = jnp.where(valid, _nt_dot(jnp.concatenate(q2, axis=0), k_slab), NEG_INF)
            m = jnp.maximum(jnp.max(s, axis=-1, keepdims=True), sink)
            p = jnp.exp2(s - m)
            denom = jnp.sum(p, axis=-1, keepdims=True) + jnp.exp2(sink - m)
            o2 = jnp.dot(p.astype(BF16), v_slab, preferred_element_type=F32) / denom
            for j, c in enumerate(range(g * pairs, (g + 1) * pairs)):
                lo = o2[(2 * j) * CHUNK:(2 * j + 1) * CHUNK]
                hi = o2[(2 * j + 1) * CHUNK:(2 * j + 2) * CHUNK]
                o_ref[q_rows, c * LANES:(c + 1) * LANES] = jnp.where(low_head, lo, hi).astype(BF16)


def _swa(sinks, qs, ks, vs, ks_meta, vs_meta, batch, nblk):
    rows = qs.shape[0]
    steps = nblk // SWA_BLKS_PER_STEP
    step_rows = SWA_BLKS_PER_STEP * Q_BLOCK
    cur = lambda b, i: (b * steps + i, 0)
    prev = lambda b, i: (b * nblk + jnp.maximum(SWA_BLKS_PER_STEP * i - 1, 0), 0)
    kv_w = 2 * SWA_KV_W
    return pl.pallas_call(
        _swa_kernel,
        grid=(batch, steps),
        in_specs=[
            pl.BlockSpec(memory_space=pltpu.SMEM),
            pl.BlockSpec((step_rows, SWA_Q_W), cur),
            pl.BlockSpec((Q_BLOCK, kv_w), prev), pl.BlockSpec((step_rows, kv_w), cur),
            pl.BlockSpec((Q_BLOCK, kv_w), lambda b, i: (0, 0)),
            pl.BlockSpec((Q_BLOCK, kv_w), prev), pl.BlockSpec((step_rows, kv_w), cur),
            pl.BlockSpec((Q_BLOCK, kv_w), lambda b, i: (0, 0)),
        ],
        out_specs=pl.BlockSpec((step_rows, SWA_Q_W), cur),
        out_shape=jax.ShapeDtypeStruct((rows, SWA_Q_W), BF16),
        compiler_params=pltpu.CompilerParams(dimension_semantics=("arbitrary", "arbitrary"),
                                             vmem_limit_bytes=VMEM_LIMIT),
        name="swa",
    )(sinks, qs, ks, ks, ks_meta, vs, vs, vs_meta)


DIFF_TQ = 256
DIFF_TK = 512
DIFF_HEADS_PER_STEP = 2


def _diff_kernel(lam_ref, g_ref, q_ref, k_ref, v_ref, km_ref, vm_ref, o_ref, *, seq):
    dl = lam_ref[...]
    lam = (jnp.exp(jnp.sum(dl[0:1] * dl[1:2], axis=-1, keepdims=True))
           - jnp.exp(jnp.sum(dl[2:3] * dl[3:4], axis=-1, keepdims=True)) + LAM_INIT)
    lane = lax.broadcasted_iota(jnp.int32, (DIFF_TQ, LANES), 1)
    low_map = lane < HEAD_DIM
    zero = jnp.zeros((DIFF_TQ, LANES), BF16)

    def last_mask(n_keys):
        shape = (2 * DIFF_TQ, Q_BLOCK + n_keys)
        col = lax.broadcasted_iota(jnp.int32, shape, 1)
        q_chunk = (lax.broadcasted_iota(jnp.int32, shape, 0) % DIFF_TQ) // CHUNK
        first_chunk = (DIFF_TQ - n_keys) // CHUNK
        k_chunk = (col - Q_BLOCK) // CHUNK + first_chunk
        return ((col >= PAD_FRONT) & (col < Q_BLOCK)) | ((col >= Q_BLOCK) & (k_chunk <= q_chunk))

    def step(q2, k_t, v_t, mask, state):
        s = _nt_dot(q2, k_t)
        if mask is not None:
            s = jnp.where(mask, s, NEG_INF)
        smax = jnp.max(s, axis=-1, keepdims=True)
        if state is None:
            p = jnp.exp2(s - smax)
            return smax, jnp.sum(p, axis=-1, keepdims=True), jnp.dot(
                p.astype(BF16), v_t, preferred_element_type=F32)
        m_old, l_old, acc = state
        m_new = jnp.maximum(m_old, smax)
        alpha = jnp.exp2(m_old - m_new)
        p = jnp.exp2(s - m_new)
        return (m_new, alpha * l_old + jnp.sum(p, axis=-1, keepdims=True),
                alpha * acc + jnp.dot(p.astype(BF16), v_t, preferred_element_type=F32))

    heads = [slice(h * LANES, (h + 1) * LANES) for h in range(DIFF_HEADS_PER_STEP)]

    tiles_per_k = DIFF_TK // DIFF_TQ
    masks = [last_mask((t + 1) * DIFF_TQ) for t in range(tiles_per_k)]

    def q_tile(n_full, t):
        if n_full is None:
            aligned = lambda v: v
            q0 = t * DIFF_TQ
        else:
            aligned = lambda v: pl.multiple_of(v, DIFF_TQ)
            q0 = aligned((n_full * tiles_per_k + t) * DIFF_TQ)
        q2s = []
        for hs in heads:
            q = q_ref[pl.ds(q0, DIFF_TQ), hs]
            q2s.append(jnp.concatenate(
                [jnp.where(low_map, q, zero), jnp.where(low_map, zero, q)], axis=0))

        def full_tiles(kj, states):
            rows = pl.ds(pl.multiple_of(kj * DIFF_TK, DIFF_TK), DIFF_TK)
            out = ()
            for h, hs in enumerate(heads):
                st = None if states is None else states[3 * h:3 * h + 3]
                out = out + step(q2s[h], k_ref[rows, hs], v_ref[rows, hs], None, st)
            return out

        if n_full is None:
            states = None
        else:
            states = lax.fori_loop(1, n_full, full_tiles, full_tiles(0, None))

        n_keys = (t + 1) * DIFF_TQ
        rows = pl.ds(aligned(q0 + DIFF_TQ - n_keys), n_keys)
        for h, hs in enumerate(heads):
            st = None if states is None else states[3 * h:3 * h + 3]
            k_t = jnp.concatenate([km_ref[:, hs], k_ref[rows, hs]], axis=0)
            v_t = jnp.concatenate([vm_ref[:, hs], v_ref[rows, hs]], axis=0)
            _, l, acc = step(q2s[h], k_t, v_t, masks[t], st)
            o = acc / l
            o = o[:DIFF_TQ] - lam * o[DIFF_TQ:]
            o = o * lax.rsqrt(jnp.mean(o * o, axis=-1, keepdims=True) + RMS_EPS)
            o_ref[pl.ds(q0, DIFF_TQ), hs] = (o * g_ref[...] * (1.0 - LAM_INIT)).astype(BF16)

    def q_group(n_full, carry):
        for t in range(tiles_per_k):
            q_tile(n_full, t)
        return carry

    q_group(None, 0)
    lax.fori_loop(1, seq // DIFF_TK, q_group, 0)


def _diff(diff_lambda, subln_g, qd, kd, vd, kd_meta, vd_meta, batch, seq):
    rows = qd.shape[0]
    head = lambda b, h: (b, h)
    meta = lambda b, h: (0, h)
    width = DIFF_HEADS_PER_STEP * LANES
    return pl.pallas_call(
        functools.partial(_diff_kernel, seq=seq),
        grid=(batch, DIFF_HEADS // DIFF_HEADS_PER_STEP),
        in_specs=[
            pl.BlockSpec((4, HEAD_DIM), lambda b, h: (0, 0)),
            pl.BlockSpec((1, LANES), lambda b, h: (0, 0)),
            pl.BlockSpec((seq, width), head), pl.BlockSpec((seq, width), head),
            pl.BlockSpec((seq, width), head),
            pl.BlockSpec((Q_BLOCK, width), meta), pl.BlockSpec((Q_BLOCK, width), meta),
        ],
        out_specs=pl.BlockSpec((seq, width), head),
        out_shape=jax.ShapeDtypeStruct((rows, DIFF_W), BF16),
        compiler_params=pltpu.CompilerParams(dimension_semantics=("arbitrary", "arbitrary"),
                                             vmem_limit_bytes=VMEM_LIMIT),
        name="diffattn",
    )(diff_lambda, subln_g, qd, kd, vd, kd_meta, vd_meta)


HALF = D_MODEL // 2


def _pack_bf16_pair(a, b):
    hi = pltpu.bitcast(a.astype(BF16).astype(F32), jnp.uint32)
    lo = pltpu.bitcast(b.astype(BF16).astype(F32), jnp.uint32)
    return hi | (lo >> 16)


def _unpack_bf16_pair(w):
    a = pltpu.bitcast(w & jnp.uint32(0xFFFF0000), F32)
    b = pltpu.bitcast(w << 16, F32)
    return a, b


OUT_CHUNK = 512


def _outproj_kernel(os_ref, od_ref, x_ref, w_ref, g_ref, wr_ref, h_ref, fp_ref, lg_ref):
    tm = x_ref.shape[0]
    ssq = jnp.zeros((tm, 1), F32)
    for c in range(D_MODEL // OUT_CHUNK):
        cols = slice(c * OUT_CHUNK, (c + 1) * OUT_CHUNK)
        y = jnp.dot(os_ref[...], w_ref[0:SWA_Q_W, cols], preferred_element_type=F32)
        y = y + jnp.dot(od_ref[...], w_ref[SWA_Q_W:, cols], preferred_element_type=F32)
        h = x_ref[:, cols] + y
        h_ref[:, cols] = h
        ssq = ssq + jnp.sum(h * h, axis=-1, keepdims=True)
    inv = lax.rsqrt(ssq * (1.0 / D_MODEL) + RMS_EPS)
    lg = jnp.zeros((tm, LANES), F32)
    for c in range(HALF // OUT_CHUNK):
        ca = slice(c * OUT_CHUNK, (c + 1) * OUT_CHUNK)
        cb = slice(HALF + c * OUT_CHUNK, HALF + (c + 1) * OUT_CHUNK)
        fa = h_ref[:, ca] * inv * g_ref[:, ca]
        fb = h_ref[:, cb] * inv * g_ref[:, cb]
        fp_ref[:, ca] = _pack_bf16_pair(fa, fb)
        for f, cc in ((fa, ca), (fb, cb)):
            f_hi = f.astype(BF16)
            f_lo = (f - f_hi.astype(F32)).astype(BF16)
            w = wr_ref[cc, :]
            w_hi = w.astype(BF16)
            w_lo = (w - w_hi.astype(F32)).astype(BF16)
            p = jnp.dot(f_hi, jnp.concatenate([w_hi, w_lo], axis=1), preferred_element_type=F32)
            lg = lg + p[:, 0:LANES] + p[:, LANES:] + jnp.dot(f_lo, w_hi, preferred_element_type=F32)
    lg_ref[...] = lg


def _outproj(o_s, o_d, x2, w_bf, g, w_router, tm):
    rows = x2.shape[0]
    row_spec = lambda width: pl.BlockSpec((tm, width), lambda i: (i, 0))
    return pl.pallas_call(
        _outproj_kernel,
        grid=(rows // tm,),
        in_specs=[
            row_spec(SWA_Q_W), row_spec(DIFF_W), row_spec(D_MODEL),
            pl.BlockSpec((D_MODEL, D_MODEL), lambda i: (0, 0), pipeline_mode=pl.Buffered(1)),
            pl.BlockSpec((1, D_MODEL), lambda i: (0, 0)),
            pl.BlockSpec((D_MODEL, LANES), lambda i: (0, 0)),
        ],
        out_specs=(row_spec(D_MODEL), row_spec(HALF), row_spec(LANES)),
        out_shape=(jax.ShapeDtypeStruct((rows, D_MODEL), F32),
                   jax.ShapeDtypeStruct((rows, HALF), jnp.uint32),
                   jax.ShapeDtypeStruct((rows, LANES), F32)),
        compiler_params=pltpu.CompilerParams(dimension_semantics=("arbitrary",),
                                             vmem_limit_bytes=VMEM_LIMIT),
        name="outproj",
    )(o_s, o_d, x2, w_bf, g, w_router)


DISPATCH_TM = 1024


def _dispatch_kernel(zero_blk_ref, n_zero_ref, dest_ref, f_ref, xs_ref, zeros, sem, zsem):
    i = pl.program_id(0)

    @pl.when(i == 0)
    def _():
        zeros[...] = jnp.zeros_like(zeros)

        def zero_copy(j):
            rows = pl.ds(zero_blk_ref[j] * ROW_BLK, ROW_BLK)
            return pltpu.make_async_copy(zeros, xs_ref.at[rows], zsem.at[0])

        def start(j, c):
            zero_copy(j).start()
            return c

        def wait(j, c):
            zero_copy(j).wait()
            return c
        lax.fori_loop(0, n_zero_ref[0], start, 0)
        lax.fori_loop(0, n_zero_ref[0], wait, 0)

    def issue(r, c):
        for kk in range(TOP_K):
            pltpu.make_async_copy(f_ref.at[pl.ds(r, 1)], xs_ref.at[pl.ds(dest_ref[TOP_K * r + kk], 1)],
                                  sem.at[0]).start(priority=kk % 2)
        return c
    lax.fori_loop(0, DISPATCH_TM, issue, 0, unroll=8)
    n_rows = TOP_K * DISPATCH_TM
    pltpu.make_async_copy(xs_ref.at[pl.ds(0, n_rows)], xs_ref.at[pl.ds(0, n_rows)], sem.at[0]).wait()


def _dispatch(zero_blk, n_zero, dest_flat, f_packed, n_blocks):
    rows = f_packed.shape[0]
    return pl.pallas_call(
        _dispatch_kernel,
        grid_spec=pltpu.PrefetchScalarGridSpec(
            num_scalar_prefetch=2,
            grid=(rows // DISPATCH_TM,),
            in_specs=[
                pl.BlockSpec((TOP_K * DISPATCH_TM,), lambda i, z, n: (i,), memory_space=pltpu.SMEM),
                pl.BlockSpec((DISPATCH_TM, HALF), lambda i, z, n: (i, 0)),
            ],
            out_specs=pl.BlockSpec(memory_space=pl.ANY),
            scratch_shapes=[pltpu.VMEM((ROW_BLK, HALF), jnp.uint32),
                            pltpu.SemaphoreType.DMA((1,)),
                            pltpu.SemaphoreType.DMA((1,))],
        ),
        out_shape=jax.ShapeDtypeStruct((n_blocks * ROW_BLK, HALF), jnp.uint32),
        compiler_params=pltpu.CompilerParams(dimension_semantics=("arbitrary",),
                                             vmem_limit_bytes=VMEM_LIMIT),
        name="dispatch",
    )(zero_blk, n_zero, dest_flat, f_packed)


ITEM_BLKS = 6
W_BUFS = 3
ITEM_ROWS = ITEM_BLKS * ROW_BLK
F_TILE = 256
N_FT = -(-D_EXPERT // F_TILE)
Y_TILE = 512
N_YT = D_MODEL // Y_TILE


def _f_tile_start(k):
    return pl.multiple_of(jnp.minimum(k * F_TILE, D_EXPERT - F_TILE), LANES)


def _moe_kernel(e_ref, r0_ref, nb_ref, meta_ref, xs_ref, wg_hbm, wu_hbm, wd_hbm, y_ref,
                xraw, xb, hid, ystage, wgf, wuf, wdf, wgu, wdr, xsem, ysem, wsem):
    n_used = meta_ref[0]
    n_live = meta_ref[1]

    def for_blocks(count, fn):
        def body(j, c):
            fn(j)
            return c
        lax.fori_loop(0, count, body, 0)

    def x_copy(item, buf, j):
        src = xs_ref.at[pl.ds((r0_ref[item] + j) * ROW_BLK, ROW_BLK)]
        return pltpu.make_async_copy(src, xraw.at[buf, pl.ds(j * ROW_BLK, ROW_BLK)],
                                     xsem.at[buf, j])

    def w_copies(item, k, buf):
        e = e_ref[item]
        cols = pl.ds(_f_tile_start(k), F_TILE)
        return (pltpu.make_async_copy(wg_hbm.at[e, :, cols], wgf.at[buf], wsem.at[0, buf]),
                pltpu.make_async_copy(wu_hbm.at[e, :, cols], wuf.at[buf], wsem.at[1, buf]),
                pltpu.make_async_copy(wd_hbm.at[e, cols, :], wdf.at[buf], wsem.at[2, buf]))

    def zero_copy(j):
        dst = y_ref.at[pl.ds((n_used + j) * ROW_BLK, ROW_BLK)]
        return pltpu.make_async_copy(ystage.at[pl.ds(0, ROW_BLK)], dst, ysem.at[0])
    n_unused = y_ref.shape[0] // ROW_BLK - n_used
    ystage[0:ROW_BLK, :] = jnp.zeros((ROW_BLK, HALF), jnp.uint32)
    for_blocks(n_unused, lambda j: zero_copy(j).start())
    for_blocks(n_unused, lambda j: zero_copy(j).wait())

    for_blocks(nb_ref[0], lambda j: x_copy(0, 0, j).start())

    @pl.when(nb_ref[0] > 0)
    def _():
        for s in range(W_BUFS - 1):
            for c in w_copies(0, s, s):
                c.start()

    def item_body(i, carry):
        slot = i % 2
        nb = nb_ref[i]
        prev = jnp.maximum(i - 1, 0)
        nb_prev = jnp.where(i > 0, nb_ref[prev], 0)

        def y_copy(j):
            dst = y_ref.at[pl.ds((r0_ref[prev] + j) * ROW_BLK, ROW_BLK)]
            return pltpu.make_async_copy(ystage.at[pl.ds(j * ROW_BLK, ROW_BLK)], dst, ysem.at[0])

        for_blocks(nb_ref[i + 1], lambda j: x_copy(i + 1, 1 - slot, j).start())

        def arrive(j):
            x_copy(i, slot, j).wait()
            rows = pl.ds(pl.multiple_of(j * ROW_BLK, ROW_BLK), ROW_BLK)
            lo, hi = _unpack_bf16_pair(xraw[slot, rows, :])
            xb[rows, 0:HALF] = lo.astype(BF16)
            xb[rows, HALF:] = hi.astype(BF16)
        for_blocks(nb, arrive)

        def tile_body(k, carry):
            step = i * N_FT + k
            wbuf = step % W_BUFS
            last = k == N_FT - 1
            ahead = k + (W_BUFS - 1)
            nxt_i = jnp.where(ahead >= N_FT, i + 1, i)
            nxt_k = jnp.where(ahead >= N_FT, ahead - N_FT, ahead)

            @pl.when(nb_ref[nxt_i] > 0)
            def _():
                for c in w_copies(nxt_i, nxt_k, (step + (W_BUFS - 1)) % W_BUFS):
                    c.start()

            def cast_tiles():
                wgu[:, 0:F_TILE] = wgf[wbuf].astype(BF16)
                wgu[:, F_TILE:] = wuf[wbuf].astype(BF16)
                f_rows = pl.ds(_f_tile_start(k), F_TILE)
                for n in range(N_YT):
                    wdr[slot, n, f_rows, :] = wdf[wbuf, :, n * Y_TILE:(n + 1) * Y_TILE].astype(BF16)

            def gate_up(row0, nrows):
                rows = pl.ds(pl.multiple_of(row0, ROW_BLK), nrows)
                gu = jnp.dot(xb[rows, :], wgu[...], preferred_element_type=F32)
                gate, up = gu[:, 0:F_TILE], gu[:, F_TILE:]
                h = (gate * (1.0 / (1.0 + jnp.exp(-gate))) * up).astype(BF16)
                hid[slot, rows, pl.ds(_f_tile_start(k), F_TILE)] = h

            k_y = jnp.minimum(k, N_YT - 1)

            def down(row0, nrows):
                rows = pl.ds(pl.multiple_of(row0, ROW_BLK), nrows)
                yt = jnp.dot(hid[1 - slot, rows, :], wdr[1 - slot, k_y], preferred_element_type=F32)
                packed = _pack_bf16_pair(yt[:, 0:Y_TILE // 2], yt[:, Y_TILE // 2:])
                words = pl.ds(pl.multiple_of(k_y * (Y_TILE // 2), Y_TILE // 2), Y_TILE // 2)
                ystage[rows, words] = packed

            def for_row_tiles(n_blk, fn):
                for whole in range(1, ITEM_BLKS + 1):
                    @pl.when(n_blk == whole)
                    def _():
                        fn(0, whole * ROW_BLK)

            @pl.when(nb > 0)
            def _():
                for c in w_copies(i, k, wbuf):
                    c.wait()
                cast_tiles()

            for_row_tiles(nb, gate_up)
            for_row_tiles(jnp.where(k < N_YT, nb_prev, 0), down)

            @pl.when(k == N_YT)
            def _():
                for_blocks(nb_prev, lambda j: y_copy(j).start())

            @pl.when(last)
            def _():
                for_blocks(nb_prev, lambda j: y_copy(j).wait())
            return carry

        lax.fori_loop(0, N_FT, tile_body, 0)
        return carry

    lax.fori_loop(0, n_live + 1, item_body, 0)


def _moe(item_e, item_r0, item_nb, meta, xs, w_gate, w_up, w_down, n_blocks):
    return pl.pallas_call(
        _moe_kernel,
        grid_spec=pltpu.PrefetchScalarGridSpec(
            num_scalar_prefetch=4,
            grid=(1,),
            in_specs=[pl.BlockSpec(memory_space=pl.ANY)] * 4,
            out_specs=pl.BlockSpec(memory_space=pl.ANY),
            scratch_shapes=[
                pltpu.VMEM((2, ITEM_ROWS, HALF), jnp.uint32),
                pltpu.VMEM((ITEM_ROWS, D_MODEL), BF16),
                pltpu.VMEM((2, ITEM_ROWS, D_EXPERT), BF16),
                pltpu.VMEM((ITEM_ROWS, HALF), jnp.uint32),
                pltpu.VMEM((W_BUFS, D_MODEL, F_TILE), F32),
                pltpu.VMEM((W_BUFS, D_MODEL, F_TILE), F32),
                pltpu.VMEM((W_BUFS, F_TILE, D_MODEL), F32),
                pltpu.VMEM((D_MODEL, 2 * F_TILE), BF16),
                pltpu.VMEM((2, N_YT, D_EXPERT, Y_TILE), BF16),
                pltpu.SemaphoreType.DMA((2, ITEM_BLKS)),
                pltpu.SemaphoreType.DMA((1,)),
                pltpu.SemaphoreType.DMA((3, W_BUFS)),
            ],
        ),
        out_shape=jax.ShapeDtypeStruct((n_blocks * ROW_BLK, HALF), jnp.uint32),
        compiler_params=pltpu.CompilerParams(dimension_semantics=("arbitrary",),
                                             vmem_limit_bytes=VMEM_LIMIT),
        name="moe",
    )(item_e, item_r0, item_nb, meta, xs, w_gate, w_up, w_down)


def _combine_kernel(cur_ref, nxt_ref, h_ref, w_ref, g_ref, y_ref, o_ref, ya, yb, sem, *, tm):
    i = pl.program_id(0)
    n_steps = pl.num_programs(0)
    slot = i % 2

    def issue(idx_ref, buf, r):
        pltpu.make_async_copy(y_ref.at[pl.ds(idx_ref[TOP_K * r], 1)], ya.at[buf, pl.ds(r, 1)],
                              sem.at[buf]).start(priority=0)
        pltpu.make_async_copy(y_ref.at[pl.ds(idx_ref[TOP_K * r + 1], 1)], yb.at[buf, pl.ds(r, 1)],
                              sem.at[buf]).start(priority=1)

    def wait(buf):
        pltpu.make_async_copy(ya.at[buf], ya.at[buf], sem.at[buf]).wait()
        pltpu.make_async_copy(yb.at[buf], yb.at[buf], sem.at[buf]).wait()

    @pl.when(i == 0)
    def _():
        def body(r, c):
            issue(cur_ref, 0, r)
            return c
        lax.fori_loop(0, tm, body, 0, unroll=8)

    wait(slot)
    for r in range(tm):
        issue(nxt_ref, 1 - slot, r)

    wa = w_ref[:, 0:1]
    wb = w_ref[:, 1:2]
    half_t = Y_TILE // 2
    ssq = jnp.zeros((tm, 1), F32)
    for n in range(N_YT):
        words = slice(n * half_t, (n + 1) * half_t)
        a_lo, a_hi = _unpack_bf16_pair(ya[slot, :, words])
        b_lo, b_hi = _unpack_bf16_pair(yb[slot, :, words])
        lo_cols = slice(n * Y_TILE, n * Y_TILE + half_t)
        hi_cols = slice(n * Y_TILE + half_t, (n + 1) * Y_TILE)
        lo = h_ref[:, lo_cols] + wa * a_lo + wb * b_lo
        hi = h_ref[:, hi_cols] + wa * a_hi + wb * b_hi
        ssq = ssq + jnp.sum(lo * lo, axis=-1, keepdims=True) + jnp.sum(hi * hi, axis=-1, keepdims=True)
        o_ref[:, lo_cols] = lo
        o_ref[:, hi_cols] = hi
    inv = lax.rsqrt(ssq * (1.0 / D_MODEL) + RMS_EPS)
    o_ref[...] = o_ref[...] * inv * g_ref[...]

    @pl.when(i == n_steps - 1)
    def _():
        wait(1 - slot)


def _combine(dest_flat, h2, wts, g, y, tm):
    rows = h2.shape[0]
    steps = rows // tm
    idx_spec = lambda fn: pl.BlockSpec((TOP_K * tm,), fn, memory_space=pltpu.SMEM)
    return pl.pallas_call(
        functools.partial(_combine_kernel, tm=tm),
        grid=(steps,),
        in_specs=[
            idx_spec(lambda i: (i,)),
            idx_spec(lambda i: (jnp.minimum(i + 1, steps - 1),)),
            pl.BlockSpec((tm, D_MODEL), lambda i: (i, 0)),
            pl.BlockSpec((tm, LANES), lambda i: (i, 0)),
            pl.BlockSpec((1, D_MODEL), lambda i: (0, 0)),
            pl.BlockSpec(memory_space=pl.ANY),
        ],
        out_specs=pl.BlockSpec((tm, D_MODEL), lambda i: (i, 0)),
        scratch_shapes=[
            pltpu.VMEM((2, tm, HALF), jnp.uint32),
            pltpu.VMEM((2, tm, HALF), jnp.uint32),
            pltpu.SemaphoreType.DMA((2,)),
        ],
        out_shape=jax.ShapeDtypeStruct((rows, D_MODEL), F32),
        compiler_params=pltpu.CompilerParams(dimension_semantics=("arbitrary",),
                                             vmem_limit_bytes=VMEM_LIMIT),
        name="combine",
    )(dest_flat, dest_flat, h2, wts, g, y)


ROUTE_TM = 1024


def _route_kernel(lg_ref, info_ref, wts_ref, tab_ref, counts, starts, carry):
    phase = pl.program_id(0)
    i = pl.program_id(1)
    tm = ROUTE_TM
    lane = lax.broadcasted_iota(jnp.int32, (tm, LANES), 1)
    lg = lg_ref[...]

    def first_lane_of_max(v):
        vmax = jnp.max(v, axis=-1, keepdims=True)
        return vmax, jnp.min(jnp.where(v == vmax, lane, LANES), axis=-1, keepdims=True)

    gl = jnp.where(lane < N_GROUPS, lg, NEG_INF)
    gmax, g_sel = first_lane_of_max(gl)
    gate = 1.0 / jnp.sum(jnp.exp(gl - gmax), axis=-1, keepdims=True)
    lo = N_GROUPS + EXPERTS_PER_GROUP * g_sel
    el = jnp.where((lane >= lo) & (lane < lo + EXPERTS_PER_GROUP), lg, NEG_INF)
    m1, l1 = first_lane_of_max(el)
    m2, l2 = first_lane_of_max(jnp.where(lane == l1, NEG_INF, el))
    ratio = jnp.exp(m2 - m1)
    w1 = gate / (1.0 + ratio)
    w2 = w1 * ratio
    e1 = l1 - N_GROUPS
    e2 = l2 - N_GROUPS
    hit1 = lane == e1
    hit2 = lane == e2
    onehot = (hit1 | hit2).astype(F32)
    tile_counts = jnp.sum(onehot, axis=0, keepdims=True)

    @pl.when((phase == 0) & (i == 0))
    def _():
        counts[...] = jnp.zeros_like(counts)

    @pl.when(phase == 0)
    def _():
        counts[...] += tile_counts

    @pl.when((phase == 1) & (i == 0))
    def _():
        blocks = jnp.floor((counts[...] + (ROW_BLK - 1)) * (1.0 / ROW_BLK))
        r = lax.broadcasted_iota(jnp.int32, (LANES, LANES), 0)
        c = lax.broadcasted_iota(jnp.int32, (LANES, LANES), 1)
        before = (r < c).astype(F32)
        first = jnp.dot(jnp.broadcast_to(blocks, (8, LANES)), before,
                        preferred_element_type=F32, precision=lax.Precision.HIGHEST)[0:1]
        starts[...] = first
        carry[...] = jnp.zeros_like(carry)
        used = jnp.sum(blocks, axis=-1, keepdims=True)
        row = lax.broadcasted_iota(jnp.int32, (8, LANES), 0)
        tab = jnp.where(row == 0, blocks, jnp.where(row == 1, first, jnp.where(row == 2, used, 0.0)))
        tab_ref[...] = tab.astype(jnp.int32)

    @pl.when(phase == 1)
    def _():
        r = lax.broadcasted_iota(jnp.int32, (tm, tm), 0)
        c = lax.broadcasted_iota(jnp.int32, (tm, tm), 1)
        earlier = (c < r).astype(BF16)
        before = jnp.dot(earlier, onehot.astype(BF16), preferred_element_type=F32)
        slot = starts[...] * ROW_BLK + carry[...] + before
        d1 = jnp.sum(jnp.where(hit1, slot, 0.0), axis=-1, keepdims=True)
        d2 = jnp.sum(jnp.where(hit2, slot, 0.0), axis=-1, keepdims=True)
        info_ref[...] = jnp.where(lane == 0, d1, jnp.where(lane == 1, d2, 0.0)).astype(jnp.int32)
        wts_ref[...] = jnp.where(lane == 0, w1, jnp.where(lane == 1, w2, 0.0))
        carry[...] += tile_counts


def _route(logits):
    rows = logits.shape[0]
    tile = lambda p, i: (i * p, 0)
    return pl.pallas_call(
        _route_kernel,
        grid=(2, rows // ROUTE_TM),
        in_specs=[pl.BlockSpec((ROUTE_TM, LANES), lambda p, i: (i, 0))],
        out_specs=(pl.BlockSpec((ROUTE_TM, LANES), tile), pl.BlockSpec((ROUTE_TM, LANES), tile),
                   pl.BlockSpec((8, LANES), lambda p, i: (0, 0))),
        out_shape=(jax.ShapeDtypeStruct((rows, LANES), jnp.int32),
                   jax.ShapeDtypeStruct((rows, LANES), F32),
                   jax.ShapeDtypeStruct((8, LANES), jnp.int32)),
        scratch_shapes=[pltpu.VMEM((1, LANES), F32)] * 3,
        compiler_params=pltpu.CompilerParams(dimension_semantics=("arbitrary", "arbitrary"),
                                             vmem_limit_bytes=VMEM_LIMIT),
        name="route",
    )(logits)


def _work_items(blocks_e, blk_start, n_items):
    items_e = (blocks_e + ITEM_BLKS - 1) // ITEM_BLKS
    item_end = jnp.cumsum(items_e)
    item_start = item_end - items_e
    total = item_end[-1]
    idx = jnp.arange(n_items, dtype=jnp.int32)
    live = idx < total
    ref = jnp.minimum(idx, total - 1)
    e = jnp.searchsorted(item_end, ref, side='right').astype(jnp.int32)
    j = ref - item_start[e]
    r0 = blk_start[e] + ITEM_BLKS * j
    nb = jnp.where(live, jnp.clip(blocks_e[e] - ITEM_BLKS * j, 0, ITEM_BLKS), 0)
    return e, r0.astype(jnp.int32), nb.astype(jnp.int32), total.astype(jnp.int32)


def kernel(x, meta_tokens, attn_norm_g, w_in, swa_sinks, diff_lambda, diff_subln_g, w_out,
           ffn_norm_g, w_router_group, w_router_expert, w_gate, w_up, w_down, final_norm_g):
    batch, seq, _ = x.shape
    n_tok = batch * seq
    nblk = seq // Q_BLOCK
    x2 = x.reshape(n_tok, D_MODEL)

    w_in_bf = w_in[0].astype(BF16)
    g_attn = attn_norm_g[0].reshape(1, D_MODEL)
    cos_x, sin_x = _rope_tables(N_META + jnp.arange(seq))
    cos_m, sin_m = _rope_tables(jnp.maximum(jnp.arange(Q_BLOCK) - PAD_FRONT, 0))
    qs, ks, vs, qd, kd, vd = _inproj(x2, g_attn, w_in_bf, cos_x, sin_x, 512, seq)
    meta_blk = jnp.concatenate(
        [jnp.zeros((PAD_FRONT, D_MODEL), x.dtype), meta_tokens.astype(x.dtype)], axis=0)
    _, ks_m, vs_m, _, kd_m, vd_m = _inproj(meta_blk, g_attn, w_in_bf, cos_m, sin_m, Q_BLOCK, Q_BLOCK)

    o_s = _swa(swa_sinks[0].astype(F32), qs, ks, vs, ks_m, vs_m, batch, nblk)
    o_d = _diff(diff_lambda[0].astype(F32), diff_subln_g[0].astype(F32).reshape(1, LANES),
                qd, kd, vd, kd_m, vd_m, batch, seq)

    w_router = jnp.concatenate(
        [w_router_group[0], w_router_expert[0],
         jnp.zeros((D_MODEL, LANES - N_GROUPS - N_EXPERTS), F32)], axis=1)
    h2, f_packed, logits = _outproj(o_s, o_d, x2, w_out[0].astype(BF16),
                                    ffn_norm_g[0].reshape(1, D_MODEL), w_router, 512)

    info, wts, tab = _route(logits)
    dest_flat = info[:, :TOP_K].reshape(-1)
    blocks_e, blk_start, n_used = tab[0, :N_EXPERTS], tab[1, :N_EXPERTS], tab[2, 0:1]
    n_blocks = n_tok * TOP_K // ROW_BLK + N_EXPERTS
    n_items = n_blocks // ITEM_BLKS + N_EXPERTS
    blk = jnp.arange(n_blocks, dtype=jnp.int32)
    is_last = jnp.any((blk[:, None] + 1 == (blk_start + blocks_e)[None, :])
                      & (blocks_e > 0)[None, :], axis=1)
    need_zero = is_last | (blk >= n_used[0])
    zero_blk = jnp.nonzero(need_zero, size=2 * N_EXPERTS, fill_value=0)[0].astype(jnp.int32)
    n_zero = jnp.sum(need_zero).astype(jnp.int32).reshape(1)
    xs = _dispatch(zero_blk, n_zero, dest_flat, f_packed, n_blocks)
    item_e, item_r0, item_nb, n_live = _work_items(blocks_e, blk_start, n_items)
    y = _moe(item_e, item_r0, item_nb, jnp.concatenate([n_used, n_live.reshape(1)]), xs,
             w_gate[0], w_up[0], w_down[0], n_blocks)

    out = _combine(dest_flat, h2, wts, final_norm_g.reshape(1, D_MODEL), y, 256)
    return out.reshape(batch, seq, D_MODEL)
```

```python
import functools
import math

import jax
import jax.numpy as jnp
from jax import lax
from jax.experimental import pallas as pl
from jax.experimental.pallas import tpu as pltpu

D_MODEL = 2048
N_META = 16
CHUNK = 64
Q_BLOCK = 128
PAD_FRONT = Q_BLOCK - N_META
HEAD_DIM = 64
ROPE_THETA = 10000.0
SWA_Q_HEADS = 16
SWA_KV_HEADS = 2
DIFF_HEADS = 8
SWA_Q_W = SWA_Q_HEADS * HEAD_DIM
SWA_KV_W = SWA_KV_HEADS * HEAD_DIM
DIFF_W = DIFF_HEADS * 2 * HEAD_DIM
IN_WIDTH = SWA_Q_W + 2 * SWA_KV_W + 3 * DIFF_W
N_GROUPS = 8
EXPERTS_PER_GROUP = 8
N_EXPERTS = N_GROUPS * EXPERTS_PER_GROUP
TOP_K = 2
D_EXPERT = 1408
RMS_EPS = 1e-6
NEG_INF = -1e30
LAM_INIT = 0.8 - 0.6 * math.exp(-0.3 * 0)

LANES = 128
ROW_BLK = 128
VMEM_LIMIT = 56 * 1024 * 1024

BF16 = jnp.bfloat16
F32 = jnp.float32


def _nt_dot(a, b):
    return lax.dot_general(a, b, (((1,), (1,)), ((), ())), preferred_element_type=F32)


LOG2_E = math.log2(math.e)
Q_SCALE = HEAD_DIM ** -0.5 * LOG2_E
_SEGS = (
    ("qs", 0, 512, True, Q_SCALE), ("qs", 512, 512, True, Q_SCALE),
    ("ks", 1024, 128, True, 1.0), ("vs", 1152, 128, False, 1.0),
    ("qd", 1280, 512, True, Q_SCALE), ("qd", 1792, 512, True, Q_SCALE),
    ("kd", 2304, 512, True, 1.0), ("kd", 2816, 512, True, 1.0),
    ("vd", 3328, 512, False, 1.0), ("vd", 3840, 512, False, 1.0),
)
_SEG_BASE = {"qs": 0, "ks": 1024, "vs": 1152, "qd": 1280, "kd": 2304, "vd": 3328}


def _inproj_kernel(x_ref, g_ref, w_ref, cos_ref, sin_ref,
                   qs_ref, ks_ref, vs_ref, qd_ref, kd_ref, vd_ref):
    x = x_ref[...]
    ms = jnp.mean(x * x, axis=-1, keepdims=True)
    a = (x * lax.rsqrt(ms + RMS_EPS) * g_ref[...]).astype(BF16)
    cos = cos_ref[...]
    sin = sin_ref[...]
    lane = lax.broadcasted_iota(jnp.int32, cos.shape, 1)
    first_half = (lane % HEAD_DIM) < (HEAD_DIM // 2)
    low_head = lane < HEAD_DIM
    outs = {"qs": qs_ref, "ks": ks_ref, "vs": vs_ref, "qd": qd_ref, "kd": kd_ref, "vd": vd_ref}
    for name, c0, width, rotary, scale in _SEGS:
        y = jnp.dot(a, w_ref[:, c0:c0 + width], preferred_element_type=F32)
        o0 = c0 - _SEG_BASE[name]
        for t in range(width // LANES):
            yt = y[:, t * LANES:(t + 1) * LANES]
            if rotary:
                partner = jnp.where(first_half, pltpu.roll(yt, LANES - HEAD_DIM // 2, 1),
                                    pltpu.roll(yt, HEAD_DIM // 2, 1))
                yt = yt * cos + partner * sin
            if scale != 1.0:
                yt = yt * scale
            if name in ("ks", "vs"):
                other = pltpu.roll(yt, HEAD_DIM, 1)
                outs[name][:, 0:LANES] = jnp.where(low_head, yt, other).astype(BF16)
                outs[name][:, LANES:2 * LANES] = jnp.where(low_head, other, yt).astype(BF16)
            else:
                outs[name][:, o0 + t * LANES:o0 + (t + 1) * LANES] = yt.astype(BF16)


def _inproj(x2, g, w_bf, cos_t, sin_t, tm, rows_per_seq):
    rows = x2.shape[0]
    nseq = rows_per_seq // tm
    row_spec = lambda width: pl.BlockSpec((tm, width), lambda i: (i, 0))
    out_shapes = (
        jax.ShapeDtypeStruct((rows, SWA_Q_W), BF16), jax.ShapeDtypeStruct((rows, 2 * SWA_KV_W), BF16),
        jax.ShapeDtypeStruct((rows, 2 * SWA_KV_W), BF16), jax.ShapeDtypeStruct((rows, DIFF_W), BF16),
        jax.ShapeDtypeStruct((rows, DIFF_W), BF16), jax.ShapeDtypeStruct((rows, DIFF_W), BF16),
    )
    return pl.pallas_call(
        _inproj_kernel,
        grid=(rows // tm,),
        in_specs=[
            row_spec(D_MODEL),
            pl.BlockSpec((1, D_MODEL), lambda i: (0, 0)),
            pl.BlockSpec((D_MODEL, IN_WIDTH), lambda i: (0, 0), pipeline_mode=pl.Buffered(1)),
            pl.BlockSpec((tm, LANES), lambda i: (i % nseq, 0)),
            pl.BlockSpec((tm, LANES), lambda i: (i % nseq, 0)),
        ],
        out_specs=(row_spec(SWA_Q_W), row_spec(2 * SWA_KV_W), row_spec(2 * SWA_KV_W),
                   row_spec(DIFF_W), row_spec(DIFF_W), row_spec(DIFF_W)),
        out_shape=out_shapes,
        compiler_params=pltpu.CompilerParams(dimension_semantics=("arbitrary",),
                                             vmem_limit_bytes=VMEM_LIMIT),
        name="inproj",
    )(x2, g, w_bf, cos_t, sin_t)


def _rope_tables(positions):
    inv = 1.0 / (ROPE_THETA ** (jnp.arange(0, HEAD_DIM, 2, dtype=F32) / HEAD_DIM))
    ang = positions.astype(F32)[:, None] * inv[None, :]
    cos, sin = jnp.cos(ang), jnp.sin(ang)
    cos_t = jnp.tile(cos, (1, 2 * LANES // HEAD_DIM))
    sin_t = jnp.tile(jnp.concatenate([-sin, sin], axis=1), (1, LANES // HEAD_DIM))
    return cos_t, sin_t


SWA_BLKS_PER_STEP = 8
SWA_SLAB = 256
SWA_BAND = 3 * CHUNK + N_META


def _swa_kernel(sink_ref, q_ref, kp_ref, kc_ref, km_ref, vp_ref, vc_ref, vm_ref, o_ref):
    i = pl.program_id(1)
    pairs = SWA_Q_HEADS // SWA_KV_HEADS // 2
    m_rows = pairs * 2 * CHUNK
    col = lax.broadcasted_iota(jnp.int32, (m_rows, SWA_SLAB), 1)
    row = lax.broadcasted_iota(jnp.int32, (m_rows, 1), 0)
    lane = lax.broadcasted_iota(jnp.int32, (CHUNK, LANES), 1)
    low_head = lane < HEAD_DIM
    zero = jnp.zeros((CHUNK, LANES), BF16)
    pad = jnp.zeros((SWA_SLAB - SWA_BAND, LANES), BF16)
    meta_rows = slice(PAD_FRONT, Q_BLOCK)
    for g in range(SWA_KV_HEADS):
        gl = slice(g * LANES, (g + 1) * LANES)
        sink = jnp.zeros((m_rows, 1), F32)
        for h in range(2 * pairs):
            sink = jnp.where(row // CHUNK == h, sink_ref[2 * g * pairs + h], sink)
        sink = sink * LOG2_E
        k_seq = jnp.concatenate([kp_ref[:, gl], kc_ref[:, gl]], axis=0)
        v_seq = jnp.concatenate([vp_ref[:, gl], vc_ref[:, gl]], axis=0)
        for half in range(2 * SWA_BLKS_PER_STEP):
            q_rows = slice(half * CHUNK, (half + 1) * CHUNK)
            band = slice(half * CHUNK, (half + 3) * CHUNK)
            k_slab = jnp.concatenate([k_seq[band], km_ref[meta_rows, gl], pad], axis=0)
            v_slab = jnp.concatenate([v_seq[band], vm_ref[meta_rows, gl], pad], axis=0)
            n_missing = max(0, (2 - half) * CHUNK)
            valid = (col < SWA_BAND) & ((i > 0) | (col >= n_missing))
            q2 = []
            for c in range(g * pairs, (g + 1) * pairs):
                qp = q_ref[q_rows, c * LANES:(c + 1) * LANES]
                q2 += [jnp.where(low_head, qp, zero), jnp.where(low_head, zero, qp)]
            s = jnp.where(valid, _nt_dot(jnp.concatenate(q2, axis=0), k_slab), NEG_INF)
            m = jnp.maximum(jnp.max(s, axis=-1, keepdims=True), sink)
            p = jnp.exp2(s - m)
            denom = jnp.sum(p, axis=-1, keepdims=True) + jnp.exp2(sink - m)
            o2 = jnp.dot(p.astype(BF16), v_slab, preferred_element_type=F32) / denom
            for j, c in enumerate(range(g * pairs, (g + 1) * pairs)):
                lo = o2[(2 * j) * CHUNK:(2 * j + 1) * CHUNK]
                hi = o2[(2 * j + 1) * CHUNK:(2 * j + 2) * CHUNK]
                o_ref[q_rows, c * LANES:(c + 1) * LANES] = jnp.where(low_head, lo, hi).astype(BF16)


def _swa(sinks, qs, ks, vs, ks_meta, vs_meta, batch, nblk):
    rows = qs.shape[0]
    steps = nblk // SWA_BLKS_PER_STEP
    step_rows = SWA_BLKS_PER_STEP * Q_BLOCK
    cur = lambda b, i: (b * steps + i, 0)
    prev = lambda b, i: (b * nblk + jnp.maximum(SWA_BLKS_PER_STEP * i - 1, 0), 0)
    kv_w = 2 * SWA_KV_W
    return pl.pallas_call(
        _swa_kernel,
        grid=(batch, steps),
        in_specs=[
            pl.BlockSpec(memory_space=pltpu.SMEM),
            pl.BlockSpec((step_rows, SWA_Q_W), cur),
            pl.BlockSpec((Q_BLOCK, kv_w), prev), pl.BlockSpec((step_rows, kv_w), cur),
            pl.BlockSpec((Q_BLOCK, kv_w), lambda b, i: (0, 0)),
            pl.BlockSpec((Q_BLOCK, kv_w), prev), pl.BlockSpec((step_rows, kv_w), cur),
            pl.BlockSpec((Q_BLOCK, kv_w), lambda b, i: (0, 0)),
        ],
        out_specs=pl.BlockSpec((step_rows, SWA_Q_W), cur),
        out_shape=jax.ShapeDtypeStruct((rows, SWA_Q_W), BF16),
        compiler_params=pltpu.CompilerParams(dimension_semantics=("arbitrary", "arbitrary"),
                                             vmem_limit_bytes=VMEM_LIMIT),
        name="swa",
    )(sinks, qs, ks, ks, ks_meta, vs, vs, vs_meta)


DIFF_TQ = 256
DIFF_TK = 512
DIFF_HEADS_PER_STEP = 2


def _diff_kernel(lam_ref, g_ref, q_ref, k_ref, v_ref, km_ref, vm_ref, o_ref, *, seq):
    dl = lam_ref[...]
    lam = (jnp.exp(jnp.sum(dl[0:1] * dl[1:2], axis=-1, keepdims=True))
           - jnp.exp(jnp.sum(dl[2:3] * dl[3:4], axis=-1, keepdims=True)) + LAM_INIT)
    lane = lax.broadcasted_iota(jnp.int32, (DIFF_TQ, LANES), 1)
    low_map = lane < HEAD_DIM
    zero = jnp.zeros((DIFF_TQ, LANES), BF16)

    def last_mask(n_keys):
        shape = (2 * DIFF_TQ, Q_BLOCK + n_keys)
        col = lax.broadcasted_iota(jnp.int32, shape, 1)
        q_chunk = (lax.broadcasted_iota(jnp.int32, shape, 0) % DIFF_TQ) // CHUNK
        first_chunk = (DIFF_TQ - n_keys) // CHUNK
        k_chunk = (col - Q_BLOCK) // CHUNK + first_chunk
        return ((col >= PAD_FRONT) & (col < Q_BLOCK)) | ((col >= Q_BLOCK) & (k_chunk <= q_chunk))

    def step(q2, k_t, v_t, mask, state):
        s = _nt_dot(q2, k_t)
        if mask is not None:
            s = jnp.where(mask, s, NEG_INF)
        smax = jnp.max(s, axis=-1, keepdims=True)
        if state is None:
            p = jnp.exp2(s - smax)
            return smax, jnp.sum(p, axis=-1, keepdims=True), jnp.dot(
                p.astype(BF16), v_t, preferred_element_type=F32)
        m_old, l_old, acc = state
        m_new = jnp.maximum(m_old, smax)
        alpha = jnp.exp2(m_old - m_new)
        p = jnp.exp2(s - m_new)
        return (m_new, alpha * l_old + jnp.sum(p, axis=-1, keepdims=True),
                alpha * acc + jnp.dot(p.astype(BF16), v_t, preferred_element_type=F32))

    heads = [slice(h * LANES, (h + 1) * LANES) for h in range(DIFF_HEADS_PER_STEP)]

    tiles_per_k = DIFF_TK // DIFF_TQ
    masks = [last_mask((t + 1) * DIFF_TQ) for t in range(tiles_per_k)]

    def q_tile(n_full, t):
        if n_full is None:
            aligned = lambda v: v
            q0 = t * DIFF_TQ
        else:
            aligned = lambda v: pl.multiple_of(v, DIFF_TQ)
            q0 = aligned((n_full * tiles_per_k + t) * DIFF_TQ)
        q2s = []
        for hs in heads:
            q = q_ref[pl.ds(q0, DIFF_TQ), hs]
            q2s.append(jnp.concatenate(
                [jnp.where(low_map, q, zero), jnp.where(low_map, zero, q)], axis=0))

        def full_tiles(kj, states):
            rows = pl.ds(pl.multiple_of(kj * DIFF_TK, DIFF_TK), DIFF_TK)
            out = ()
            for h, hs in enumerate(heads):
                st = None if states is None else states[3 * h:3 * h + 3]
                out = out + step(q2s[h], k_ref[rows, hs], v_ref[rows, hs], None, st)
            return out

        if n_full is None:
            states = None
        else:
            states = lax.fori_loop(1, n_full, full_tiles, full_tiles(0, None))

        n_keys = (t + 1) * DIFF_TQ
        rows = pl.ds(aligned(q0 + DIFF_TQ - n_keys), n_keys)
        for h, hs in enumerate(heads):
            st = None if states is None else states[3 * h:3 * h + 3]
            k_t = jnp.concatenate([km_ref[:, hs], k_ref[rows, hs]], axis=0)
            v_t = jnp.concatenate([vm_ref[:, hs], v_ref[rows, hs]], axis=0)
            _, l, acc = step(q2s[h], k_t, v_t, masks[t], st)
            o = acc / l
            o = o[:DIFF_TQ] - lam * o[DIFF_TQ:]
            o = o * lax.rsqrt(jnp.mean(o * o, axis=-1, keepdims=True) + RMS_EPS)
            o_ref[pl.ds(q0, DIFF_TQ), hs] = (o * g_ref[...] * (1.0 - LAM_INIT)).astype(BF16)

    def q_group(n_full, carry):
        for t in range(tiles_per_k):
            q_tile(n_full, t)
        return carry

    q_group(None, 0)
    lax.fori_loop(1, seq // DIFF_TK, q_group, 0)


def _diff(diff_lambda, subln_g, qd, kd, vd, kd_meta, vd_meta, batch, seq):
    rows = qd.shape[0]
    head = lambda b, h: (b, h)
    meta = lambda b, h: (0, h)
    width = DIFF_HEADS_PER_STEP * LANES
    return pl.pallas_call(
        functools.partial(_diff_kernel, seq=seq),
        grid=(batch, DIFF_HEADS // DIFF_HEADS_PER_STEP),
        in_specs=[
            pl.BlockSpec((4, HEAD_DIM), lambda b, h: (0, 0)),
            pl.BlockSpec((1, LANES), lambda b, h: (0, 0)),
            pl.BlockSpec((seq, width), head), pl.BlockSpec((seq, width), head),
            pl.BlockSpec((seq, width), head),
            pl.BlockSpec((Q_BLOCK, width), meta), pl.BlockSpec((Q_BLOCK, width), meta),
        ],
        out_specs=pl.BlockSpec((seq, width), head),
        out_shape=jax.ShapeDtypeStruct((rows, DIFF_W), BF16),
        compiler_params=pltpu.CompilerParams(dimension_semantics=("arbitrary", "arbitrary"),
                                             vmem_limit_bytes=VMEM_LIMIT),
        name="diffattn",
    )(diff_lambda, subln_g, qd, kd, vd, kd_meta, vd_meta)


HALF = D_MODEL // 2


def _pack_bf16_pair(a, b):
    hi = pltpu.bitcast(a.astype(BF16).astype(F32), jnp.uint32)
    lo = pltpu.bitcast(b.astype(BF16).astype(F32), jnp.uint32)
    return hi | (lo >> 16)


def _unpack_bf16_pair(w):
    a = pltpu.bitcast(w & jnp.uint32(0xFFFF0000), F32)
    b = pltpu.bitcast(w << 16, F32)
    return a, b


OUT_CHUNK = 512


def _outproj_kernel(os_ref, od_ref, x_ref, w_ref, g_ref, wr_ref, h_ref, fp_ref, lg_ref):
    tm = x_ref.shape[0]
    ssq = jnp.zeros((tm, 1), F32)
    for c in range(D_MODEL // OUT_CHUNK):
        cols = slice(c * OUT_CHUNK, (c + 1) * OUT_CHUNK)
        y = jnp.dot(os_ref[...], w_ref[0:SWA_Q_W, cols], preferred_element_type=F32)
        y = y + jnp.dot(od_ref[...], w_ref[SWA_Q_W:, cols], preferred_element_type=F32)
        h = x_ref[:, cols] + y
        h_ref[:, cols] = h
        ssq = ssq + jnp.sum(h * h, axis=-1, keepdims=True)
    inv = lax.rsqrt(ssq * (1.0 / D_MODEL) + RMS_EPS)
    lg = jnp.zeros((tm, LANES), F32)
    for c in range(HALF // OUT_CHUNK):
        ca = slice(c * OUT_CHUNK, (c + 1) * OUT_CHUNK)
        cb = slice(HALF + c * OUT_CHUNK, HALF + (c + 1) * OUT_CHUNK)
        fa = h_ref[:, ca] * inv * g_ref[:, ca]
        fb = h_ref[:, cb] * inv * g_ref[:, cb]
        fp_ref[:, ca] = _pack_bf16_pair(fa, fb)
        for f, cc in ((fa, ca), (fb, cb)):
            f_hi = f.astype(BF16)
            f_lo = (f - f_hi.astype(F32)).astype(BF16)
            w = wr_ref[cc, :]
            w_hi = w.astype(BF16)
            w_lo = (w - w_hi.astype(F32)).astype(BF16)
            p = jnp.dot(f_hi, jnp.concatenate([w_hi, w_lo], axis=1), preferred_element_type=F32)
            lg = lg + p[:, 0:LANES] + p[:, LANES:] + jnp.dot(f_lo, w_hi, preferred_element_type=F32)
    lg_ref[...] = lg


def _outproj(o_s, o_d, x2, w_bf, g, w_router, tm):
    rows = x2.shape[0]
    row_spec = lambda width: pl.BlockSpec((tm, width), lambda i: (i, 0))
    return pl.pallas_call(
        _outproj_kernel,
        grid=(rows // tm,),
        in_specs=[
            row_spec(SWA_Q_W), row_spec(DIFF_W), row_spec(D_MODEL),
            pl.BlockSpec((D_MODEL, D_MODEL), lambda i: (0, 0), pipeline_mode=pl.Buffered(1)),
            pl.BlockSpec((1, D_MODEL), lambda i: (0, 0)),
            pl.BlockSpec((D_MODEL, LANES), lambda i: (0, 0)),
        ],
        out_specs=(row_spec(D_MODEL), row_spec(HALF), row_spec(LANES)),
        out_shape=(jax.ShapeDtypeStruct((rows, D_MODEL), F32),
                   jax.ShapeDtypeStruct((rows, HALF), jnp.uint32),
                   jax.ShapeDtypeStruct((rows, LANES), F32)),
        compiler_params=pltpu.CompilerParams(dimension_semantics=("arbitrary",),
                                             vmem_limit_bytes=VMEM_LIMIT),
        name="outproj",
    )(o_s, o_d, x2, w_bf, g, w_router)


DISPATCH_TM = 2048


def _dispatch_kernel(zero_blk_ref, n_zero_ref, dest_ref, f_ref, xs_ref, zeros, sem, zsem):
    i = pl.program_id(0)

    @pl.when(i == 0)
    def _():
        zeros[...] = jnp.zeros_like(zeros)

        def zero_copy(j):
            rows = pl.ds(zero_blk_ref[j] * ROW_BLK, ROW_BLK)
            return pltpu.make_async_copy(zeros, xs_ref.at[rows], zsem.at[0])

        def start(j, c):
            zero_copy(j).start()
            return c

        def wait(j, c):
            zero_copy(j).wait()
            return c
        lax.fori_loop(0, n_zero_ref[0], start, 0)
        lax.fori_loop(0, n_zero_ref[0], wait, 0)

    def issue(r, c):
        for kk in range(TOP_K):
            pltpu.make_async_copy(f_ref.at[pl.ds(r, 1)], xs_ref.at[pl.ds(dest_ref[TOP_K * r + kk], 1)],
                                  sem.at[0]).start(priority=kk % 2)
        return c
    lax.fori_loop(0, DISPATCH_TM, issue, 0, unroll=8)
    n_rows = TOP_K * DISPATCH_TM
    pltpu.make_async_copy(xs_ref.at[pl.ds(0, n_rows)], xs_ref.at[pl.ds(0, n_rows)], sem.at[0]).wait()


def _dispatch(zero_blk, n_zero, dest_flat, f_packed, n_blocks):
    rows = f_packed.shape[0]
    return pl.pallas_call(
        _dispatch_kernel,
        grid_spec=pltpu.PrefetchScalarGridSpec(
            num_scalar_prefetch=2,
            grid=(rows // DISPATCH_TM,),
            in_specs=[
                pl.BlockSpec((TOP_K * DISPATCH_TM,), lambda i, z, n: (i,), memory_space=pltpu.SMEM),
                pl.BlockSpec((DISPATCH_TM, HALF), lambda i, z, n: (i, 0)),
            ],
            out_specs=pl.BlockSpec(memory_space=pl.ANY),
            scratch_shapes=[pltpu.VMEM((ROW_BLK, HALF), jnp.uint32),
                            pltpu.SemaphoreType.DMA((1,)),
                            pltpu.SemaphoreType.DMA((1,))],
        ),
        out_shape=jax.ShapeDtypeStruct((n_blocks * ROW_BLK, HALF), jnp.uint32),
        compiler_params=pltpu.CompilerParams(dimension_semantics=("arbitrary",),
                                             vmem_limit_bytes=VMEM_LIMIT),
        name="dispatch",
    )(zero_blk, n_zero, dest_flat, f_packed)


ITEM_BLKS = 6
W_BUFS = 3
ITEM_ROWS = ITEM_BLKS * ROW_BLK
F_TILE = 256
N_FT = -(-D_EXPERT // F_TILE)
Y_TILE = 512
N_YT = D_MODEL // Y_TILE


def _f_tile_start(k):
    return pl.multiple_of(jnp.minimum(k * F_TILE, D_EXPERT - F_TILE), LANES)


def _moe_kernel(e_ref, r0_ref, nb_ref, meta_ref, xs_ref, wg_hbm, wu_hbm, wd_hbm, y_ref,
                xraw, xb, hid, ystage, wgf, wuf, wdf, wgu, wdr, xsem, ysem, wsem):
    n_used = meta_ref[0]
    n_live = meta_ref[1]

    def for_blocks(count, fn):
        def body(j, c):
            fn(j)
            return c
        lax.fori_loop(0, count, body, 0)

    def x_copy(item, buf, j):
        src = xs_ref.at[pl.ds((r0_ref[item] + j) * ROW_BLK, ROW_BLK)]
        return pltpu.make_async_copy(src, xraw.at[buf, pl.ds(j * ROW_BLK, ROW_BLK)],
                                     xsem.at[buf, j])

    def w_copies(item, k, buf):
        e = e_ref[item]
        cols = pl.ds(_f_tile_start(k), F_TILE)
        return (pltpu.make_async_copy(wg_hbm.at[e, :, cols], wgf.at[buf], wsem.at[0, buf]),
                pltpu.make_async_copy(wu_hbm.at[e, :, cols], wuf.at[buf], wsem.at[1, buf]),
                pltpu.make_async_copy(wd_hbm.at[e, cols, :], wdf.at[buf], wsem.at[2, buf]))

    def zero_copy(j):
        dst = y_ref.at[pl.ds((n_used + j) * ROW_BLK, ROW_BLK)]
        return pltpu.make_async_copy(ystage.at[pl.ds(0, ROW_BLK)], dst, ysem.at[0])
    n_unused = y_ref.shape[0] // ROW_BLK - n_used
    ystage[0:ROW_BLK, :] = jnp.zeros((ROW_BLK, HALF), jnp.uint32)
    for_blocks(n_unused, lambda j: zero_copy(j).start())
    for_blocks(n_unused, lambda j: zero_copy(j).wait())

    for_blocks(nb_ref[0], lambda j: x_copy(0, 0, j).start())

    @pl.when(nb_ref[0] > 0)
    def _():
        for s in range(W_BUFS - 1):
            for c in w_copies(0, s, s):
                c.start()

    def item_body(i, carry):
        slot = i % 2
        nb = nb_ref[i]
        prev = jnp.maximum(i - 1, 0)
        nb_prev = jnp.where(i > 0, nb_ref[prev], 0)

        def y_copy(j):
            dst = y_ref.at[pl.ds((r0_ref[prev] + j) * ROW_BLK, ROW_BLK)]
            return pltpu.make_async_copy(ystage.at[pl.ds(j * ROW_BLK, ROW_BLK)], dst, ysem.at[0])

        for_blocks(nb_ref[i + 1], lambda j: x_copy(i + 1, 1 - slot, j).start())

        def arrive(j):
            x_copy(i, slot, j).wait()
            rows = pl.ds(pl.multiple_of(j * ROW_BLK, ROW_BLK), ROW_BLK)
            lo, hi = _unpack_bf16_pair(xraw[slot, rows, :])
            xb[rows, 0:HALF] = lo.astype(BF16)
            xb[rows, HALF:] = hi.astype(BF16)
        for_blocks(nb, arrive)

        def tile_body(k, carry):
            step = i * N_FT + k
            wbuf = step % W_BUFS
            last = k == N_FT - 1
            ahead = k + (W_BUFS - 1)
            nxt_i = jnp.where(ahead >= N_FT, i + 1, i)
            nxt_k = jnp.where(ahead >= N_FT, ahead - N_FT, ahead)

            @pl.when(nb_ref[nxt_i] > 0)
            def _():
                for c in w_copies(nxt_i, nxt_k, (step + (W_BUFS - 1)) % W_BUFS):
                    c.start()

            def cast_tiles():
                wgu[:, 0:F_TILE] = wgf[wbuf].astype(BF16)
                wgu[:, F_TILE:] = wuf[wbuf].astype(BF16)
                f_rows = pl.ds(_f_tile_start(k), F_TILE)
                for n in range(N_YT):
                    wdr[slot, n, f_rows, :] = wdf[wbuf, :, n * Y_TILE:(n + 1) * Y_TILE].astype(BF16)

            def gate_up(row0, nrows):
                rows = pl.ds(pl.multiple_of(row0, ROW_BLK), nrows)
                gu = jnp.dot(xb[rows, :], wgu[...], preferred_element_type=F32)
                gate, up = gu[:, 0:F_TILE], gu[:, F_TILE:]
                h = (gate * (1.0 / (1.0 + jnp.exp(-gate))) * up).astype(BF16)
                hid[slot, rows, pl.ds(_f_tile_start(k), F_TILE)] = h

            k_y = jnp.minimum(k, N_YT - 1)

            def down(row0, nrows):
                rows = pl.ds(pl.multiple_of(row0, ROW_BLK), nrows)
                yt = jnp.dot(hid[1 - slot, rows, :], wdr[1 - slot, k_y], preferred_element_type=F32)
                packed = _pack_bf16_pair(yt[:, 0:Y_TILE // 2], yt[:, Y_TILE // 2:])
                words = pl.ds(pl.multiple_of(k_y * (Y_TILE // 2), Y_TILE // 2), Y_TILE // 2)
                ystage[rows, words] = packed

            def for_row_tiles(n_blk, fn):
                for whole in range(1, ITEM_BLKS + 1):
                    @pl.when(n_blk == whole)
                    def _():
                        fn(0, whole * ROW_BLK)

            @pl.when(nb > 0)
            def _():
                for c in w_copies(i, k, wbuf):
                    c.wait()
                cast_tiles()

            for_row_tiles(nb, gate_up)
            for_row_tiles(jnp.where(k < N_YT, nb_prev, 0), down)

            @pl.when(k == N_YT)
            def _():
                for_blocks(nb_prev, lambda j: y_copy(j).start())

            @pl.when(last)
            def _():
                for_blocks(nb_prev, lambda j: y_copy(j).wait())
            return carry

        lax.fori_loop(0, N_FT, tile_body, 0)
        return carry

    lax.fori_loop(0, n_live + 1, item_body, 0)


def _moe(item_e, item_r0, item_nb, meta, xs, w_gate, w_up, w_down, n_blocks):
    return pl.pallas_call(
        _moe_kernel,
        grid_spec=pltpu.PrefetchScalarGridSpec(
            num_scalar_prefetch=4,
            grid=(1,),
            in_specs=[pl.BlockSpec(memory_space=pl.ANY)] * 4,
            out_specs=pl.BlockSpec(memory_space=pl.ANY),
            scratch_shapes=[
                pltpu.VMEM((2, ITEM_ROWS, HALF), jnp.uint32),
                pltpu.VMEM((ITEM_ROWS, D_MODEL), BF16),
                pltpu.VMEM((2, ITEM_ROWS, D_EXPERT), BF16),
                pltpu.VMEM((ITEM_ROWS, HALF), jnp.uint32),
                pltpu.VMEM((W_BUFS, D_MODEL, F_TILE), F32),
                pltpu.VMEM((W_BUFS, D_MODEL, F_TILE), F32),
                pltpu.VMEM((W_BUFS, F_TILE, D_MODEL), F32),
                pltpu.VMEM((D_MODEL, 2 * F_TILE), BF16),
                pltpu.VMEM((2, N_YT, D_EXPERT, Y_TILE), BF16),
                pltpu.SemaphoreType.DMA((2, ITEM_BLKS)),
                pltpu.SemaphoreType.DMA((1,)),
                pltpu.SemaphoreType.DMA((3, W_BUFS)),
            ],
        ),
        out_shape=jax.ShapeDtypeStruct((n_blocks * ROW_BLK, HALF), jnp.uint32),
        compiler_params=pltpu.CompilerParams(dimension_semantics=("arbitrary",),
                                             vmem_limit_bytes=VMEM_LIMIT),
        name="moe",
    )(item_e, item_r0, item_nb, meta, xs, w_gate, w_up, w_down)


def _combine_kernel(cur_ref, nxt_ref, h_ref, w_ref, g_ref, y_ref, o_ref, ya, yb, sem, *, tm):
    i = pl.program_id(0)
    n_steps = pl.num_programs(0)
    slot = i % 2

    def issue(idx_ref, buf, r):
        pltpu.make_async_copy(y_ref.at[pl.ds(idx_ref[TOP_K * r], 1)], ya.at[buf, pl.ds(r, 1)],
                              sem.at[buf]).start(priority=0)
        pltpu.make_async_copy(y_ref.at[pl.ds(idx_ref[TOP_K * r + 1], 1)], yb.at[buf, pl.ds(r, 1)],
                              sem.at[buf]).start(priority=1)

    def wait(buf):
        pltpu.make_async_copy(ya.at[buf], ya.at[buf], sem.at[buf]).wait()
        pltpu.make_async_copy(yb.at[buf], yb.at[buf], sem.at[buf]).wait()

    @pl.when(i == 0)
    def _():
        def body(r, c):
            issue(cur_ref, 0, r)
            return c
        lax.fori_loop(0, tm, body, 0, unroll=8)

    wait(slot)
    for r in range(tm):
        issue(nxt_ref, 1 - slot, r)

    wa = w_ref[:, 0:1]
    wb = w_ref[:, 1:2]
    half_t = Y_TILE // 2
    ssq = jnp.zeros((tm, 1), F32)
    for n in range(N_YT):
        words = slice(n * half_t, (n + 1) * half_t)
        a_lo, a_hi = _unpack_bf16_pair(ya[slot, :, words])
        b_lo, b_hi = _unpack_bf16_pair(yb[slot, :, words])
        lo_cols = slice(n * Y_TILE, n * Y_TILE + half_t)
        hi_cols = slice(n * Y_TILE + half_t, (n + 1) * Y_TILE)
        lo = h_ref[:, lo_cols] + wa * a_lo + wb * b_lo
        hi = h_ref[:, hi_cols] + wa * a_hi + wb * b_hi
        ssq = ssq + jnp.sum(lo * lo, axis=-1, keepdims=True) + jnp.sum(hi * hi, axis=-1, keepdims=True)
        o_ref[:, lo_cols] = lo
        o_ref[:, hi_cols] = hi
    inv = lax.rsqrt(ssq * (1.0 / D_MODEL) + RMS_EPS)
    o_ref[...] = o_ref[...] * inv * g_ref[...]

    @pl.when(i == n_steps - 1)
    def _():
        wait(1 - slot)


def _combine(dest_flat, h2, wts, g, y, tm):
    rows = h2.shape[0]
    steps = rows // tm
    idx_spec = lambda fn: pl.BlockSpec((TOP_K * tm,), fn, memory_space=pltpu.SMEM)
    return pl.pallas_call(
        functools.partial(_combine_kernel, tm=tm),
        grid=(steps,),
        in_specs=[
            idx_spec(lambda i: (i,)),
            idx_spec(lambda i: (jnp.minimum(i + 1, steps - 1),)),
            pl.BlockSpec((tm, D_MODEL), lambda i: (i, 0)),
            pl.BlockSpec((tm, LANES), lambda i: (i, 0)),
            pl.BlockSpec((1, D_MODEL), lambda i: (0, 0)),
            pl.BlockSpec(memory_space=pl.ANY),
        ],
        out_specs=pl.BlockSpec((tm, D_MODEL), lambda i: (i, 0)),
        scratch_shapes=[
            pltpu.VMEM((2, tm, HALF), jnp.uint32),
            pltpu.VMEM((2, tm, HALF), jnp.uint32),
            pltpu.SemaphoreType.DMA((2,)),
        ],
        out_shape=jax.ShapeDtypeStruct((rows, D_MODEL), F32),
        compiler_params=pltpu.CompilerParams(dimension_semantics=("arbitrary",),
                                             vmem_limit_bytes=VMEM_LIMIT),
        name="combine",
    )(dest_flat, dest_flat, h2, wts, g, y)


ROUTE_TM = 1024


def _route_kernel(lg_ref, info_ref, wts_ref, tab_ref, counts, starts, carry):
    phase = pl.program_id(0)
    i = pl.program_id(1)
    tm = ROUTE_TM
    lane = lax.broadcasted_iota(jnp.int32, (tm, LANES), 1)
    lg = lg_ref[...]

    def first_lane_of_max(v):
        vmax = jnp.max(v, axis=-1, keepdims=True)
        return vmax, jnp.min(jnp.where(v == vmax, lane, LANES), axis=-1, keepdims=True)

    gl = jnp.where(lane < N_GROUPS, lg, NEG_INF)
    gmax, g_sel = first_lane_of_max(gl)
    gate = 1.0 / jnp.sum(jnp.exp(gl - gmax), axis=-1, keepdims=True)
    lo = N_GROUPS + EXPERTS_PER_GROUP * g_sel
    el = jnp.where((lane >= lo) & (lane < lo + EXPERTS_PER_GROUP), lg, NEG_INF)
    m1, l1 = first_lane_of_max(el)
    m2, l2 = first_lane_of_max(jnp.where(lane == l1, NEG_INF, el))
    ratio = jnp.exp(m2 - m1)
    w1 = gate / (1.0 + ratio)
    w2 = w1 * ratio
    e1 = l1 - N_GROUPS
    e2 = l2 - N_GROUPS
    hit1 = lane == e1
    hit2 = lane == e2
    onehot = (hit1 | hit2).astype(F32)
    tile_counts = jnp.sum(onehot, axis=0, keepdims=True)

    @pl.when((phase == 0) & (i == 0))
    def _():
        counts[...] = jnp.zeros_like(counts)

    @pl.when(phase == 0)
    def _():
        counts[...] += tile_counts

    @pl.when((phase == 1) & (i == 0))
    def _():
        blocks = jnp.floor((counts[...] + (ROW_BLK - 1)) * (1.0 / ROW_BLK))
        r = lax.broadcasted_iota(jnp.int32, (LANES, LANES), 0)
        c = lax.broadcasted_iota(jnp.int32, (LANES, LANES), 1)
        before = (r < c).astype(F32)
        first = jnp.dot(jnp.broadcast_to(blocks, (8, LANES)), before,
                        preferred_element_type=F32, precision=lax.Precision.HIGHEST)[0:1]
        starts[...] = first
        carry[...] = jnp.zeros_like(carry)
        used = jnp.sum(blocks, axis=-1, keepdims=True)
        row = lax.broadcasted_iota(jnp.int32, (8, LANES), 0)
        tab = jnp.where(row == 0, blocks, jnp.where(row == 1, first, jnp.where(row == 2, used, 0.0)))
        tab_ref[...] = tab.astype(jnp.int32)

    @pl.when(phase == 1)
    def _():
        r = lax.broadcasted_iota(jnp.int32, (tm, tm), 0)
        c = lax.broadcasted_iota(jnp.int32, (tm, tm), 1)
        earlier = (c < r).astype(BF16)
        before = jnp.dot(earlier, onehot.astype(BF16), preferred_element_type=F32)
        slot = starts[...] * ROW_BLK + carry[...] + before
        d1 = jnp.sum(jnp.where(hit1, slot, 0.0), axis=-1, keepdims=True)
        d2 = jnp.sum(jnp.where(hit2, slot, 0.0), axis=-1, keepdims=True)
        info_ref[...] = jnp.where(lane == 0, d1, jnp.where(lane == 1, d2, 0.0)).astype(jnp.int32)
        wts_ref[...] = jnp.where(lane == 0, w1, jnp.where(lane == 1, w2, 0.0))
        carry[...] += tile_counts


def _route(logits):
    rows = logits.shape[0]
    tile = lambda p, i: (i * p, 0)
    return pl.pallas_call(
        _route_kernel,
        grid=(2, rows // ROUTE_TM),
        in_specs=[pl.BlockSpec((ROUTE_TM, LANES), lambda p, i: (i, 0))],
        out_specs=(pl.BlockSpec((ROUTE_TM, LANES), tile), pl.BlockSpec((ROUTE_TM, LANES), tile),
                   pl.BlockSpec((8, LANES), lambda p, i: (0, 0))),
        out_shape=(jax.ShapeDtypeStruct((rows, LANES), jnp.int32),
                   jax.ShapeDtypeStruct((rows, LANES), F32),
                   jax.ShapeDtypeStruct((8, LANES), jnp.int32)),
        scratch_shapes=[pltpu.VMEM((1, LANES), F32)] * 3,
        compiler_params=pltpu.CompilerParams(dimension_semantics=("arbitrary", "arbitrary"),
                                             vmem_limit_bytes=VMEM_LIMIT),
        name="route",
    )(logits)


def _work_items(blocks_e, blk_start, n_items):
    items_e = (blocks_e + ITEM_BLKS - 1) // ITEM_BLKS
    item_end = jnp.cumsum(items_e)
    item_start = item_end - items_e
    total = item_end[-1]
    idx = jnp.arange(n_items, dtype=jnp.int32)
    live = idx < total
    ref = jnp.minimum(idx, total - 1)
    e = jnp.searchsorted(item_end, ref, side='right').astype(jnp.int32)
    j = ref - item_start[e]
    r0 = blk_start[e] + ITEM_BLKS * j
    nb = jnp.where(live, jnp.clip(blocks_e[e] - ITEM_BLKS * j, 0, ITEM_BLKS), 0)
    return e, r0.astype(jnp.int32), nb.astype(jnp.int32), total.astype(jnp.int32)


def kernel(x, meta_tokens, attn_norm_g, w_in, swa_sinks, diff_lambda, diff_subln_g, w_out,
           ffn_norm_g, w_router_group, w_router_expert, w_gate, w_up, w_down, final_norm_g):
    batch, seq, _ = x.shape
    n_tok = batch * seq
    nblk = seq // Q_BLOCK
    x2 = x.reshape(n_tok, D_MODEL)

    w_in_bf = w_in[0].astype(BF16)
    g_attn = attn_norm_g[0].reshape(1, D_MODEL)
    cos_x, sin_x = _rope_tables(N_META + jnp.arange(seq))
    cos_m, sin_m = _rope_tables(jnp.maximum(jnp.arange(Q_BLOCK) - PAD_FRONT, 0))
    qs, ks, vs, qd, kd, vd = _inproj(x2, g_attn, w_in_bf, cos_x, sin_x, 512, seq)
    meta_blk = jnp.concatenate(
        [jnp.zeros((PAD_FRONT, D_MODEL), x.dtype), meta_tokens.astype(x.dtype)], axis=0)
    _, ks_m, vs_m, _, kd_m, vd_m = _inproj(meta_blk, g_attn, w_in_bf, cos_m, sin_m, Q_BLOCK, Q_BLOCK)

    o_s = _swa(swa_sinks[0].astype(F32), qs, ks, vs, ks_m, vs_m, batch, nblk)
    o_d = _diff(diff_lambda[0].astype(F32), diff_subln_g[0].astype(F32).reshape(1, LANES),
                qd, kd, vd, kd_m, vd_m, batch, seq)

    w_router = jnp.concatenate(
        [w_router_group[0], w_router_expert[0],
         jnp.zeros((D_MODEL, LANES - N_GROUPS - N_EXPERTS), F32)], axis=1)
    h2, f_packed, logits = _outproj(o_s, o_d, x2, w_out[0].astype(BF16),
                                    ffn_norm_g[0].reshape(1, D_MODEL), w_router, 512)

    info, wts, tab = _route(logits)
    dest_flat = info[:, :TOP_K].reshape(-1)
    blocks_e, blk_start, n_used = tab[0, :N_EXPERTS], tab[1, :N_EXPERTS], tab[2, 0:1]
    n_blocks = n_tok * TOP_K // ROW_BLK + N_EXPERTS
    n_items = n_blocks // ITEM_BLKS + N_EXPERTS
    blk = jnp.arange(n_blocks, dtype=jnp.int32)
    is_last = jnp.any((blk[:, None] + 1 == (blk_start + blocks_e)[None, :])
                      & (blocks_e > 0)[None, :], axis=1)
    need_zero = is_last | (blk >= n_used[0])
    zero_blk = jnp.nonzero(need_zero, size=2 * N_EXPERTS, fill_value=0)[0].astype(jnp.int32)
    n_zero = jnp.sum(need_zero).astype(jnp.int32).reshape(1)
    xs = _dispatch(zero_blk, n_zero, dest_flat, f_packed, n_blocks)
    item_e, item_r0, item_nb, n_live = _work_items(blocks_e, blk_start, n_items)
    y = _moe(item_e, item_r0, item_nb, jnp.concatenate([n_used, n_live.reshape(1)]), xs,
             w_gate[0], w_up[0], w_down[0], n_blocks)

    out = _combine(dest_flat, h2, wts, final_norm_g.reshape(1, D_MODEL), y, 256)
    return out.reshape(batch, seq, D_MODEL)
```

```python
import functools
import math

import jax
import jax.numpy as jnp
from jax import lax
from jax.experimental import pallas as pl
from jax.experimental.pallas import tpu as pltpu

D_MODEL = 2048
N_META = 16
CHUNK = 64
Q_BLOCK = 128
PAD_FRONT = Q_BLOCK - N_META
HEAD_DIM = 64
ROPE_THETA = 10000.0
SWA_Q_HEADS = 16
SWA_KV_HEADS = 2
DIFF_HEADS = 8
SWA_Q_W = SWA_Q_HEADS * HEAD_DIM
SWA_KV_W = SWA_KV_HEADS * HEAD_DIM
DIFF_W = DIFF_HEADS * 2 * HEAD_DIM
IN_WIDTH = SWA_Q_W + 2 * SWA_KV_W + 3 * DIFF_W
N_GROUPS = 8
EXPERTS_PER_GROUP = 8
N_EXPERTS = N_GROUPS * EXPERTS_PER_GROUP
TOP_K = 2
D_EXPERT = 1408
RMS_EPS = 1e-6
NEG_INF = -1e30
LAM_INIT = 0.8 - 0.6 * math.exp(-0.3 * 0)

LANES = 128
ROW_BLK = 128
VMEM_LIMIT = 56 * 1024 * 1024

BF16 = jnp.bfloat16
F32 = jnp.float32


def _nt_dot(a, b):
    return lax.dot_general(a, b, (((1,), (1,)), ((), ())), preferred_element_type=F32)


LOG2_E = math.log2(math.e)
Q_SCALE = HEAD_DIM ** -0.5 * LOG2_E
_SEGS = (
    ("qs", 0, 512, True, Q_SCALE), ("qs", 512, 512, True, Q_SCALE),
    ("ks", 1024, 128, True, 1.0), ("vs", 1152, 128, False, 1.0),
    ("qd", 1280, 512, True, Q_SCALE), ("qd", 1792, 512, True, Q_SCALE),
    ("kd", 2304, 512, True, 1.0), ("kd", 2816, 512, True, 1.0),
    ("vd", 3328, 512, False, 1.0), ("vd", 3840, 512, False, 1.0),
)
_SEG_BASE = {"qs": 0, "ks": 1024, "vs": 1152, "qd": 1280, "kd": 2304, "vd": 3328}


def _inproj_kernel(x_ref, g_ref, w_ref, cos_ref, sin_ref,
                   qs_ref, ks_ref, vs_ref, qd_ref, kd_ref, vd_ref):
    x = x_ref[...]
    ms = jnp.mean(x * x, axis=-1, keepdims=True)
    a = (x * lax.rsqrt(ms + RMS_EPS) * g_ref[...]).astype(BF16)
    cos = cos_ref[...]
    sin = sin_ref[...]
    lane = lax.broadcasted_iota(jnp.int32, cos.shape, 1)
    first_half = (lane % HEAD_DIM) < (HEAD_DIM // 2)
    low_head = lane < HEAD_DIM
    outs = {"qs": qs_ref, "ks": ks_ref, "vs": vs_ref, "qd": qd_ref, "kd": kd_ref, "vd": vd_ref}
    for name, c0, width, rotary, scale in _SEGS:
        y = jnp.dot(a, w_ref[:, c0:c0 + width], preferred_element_type=F32)
        o0 = c0 - _SEG_BASE[name]
        for t in range(width // LANES):
            yt = y[:, t * LANES:(t + 1) * LANES]
            if rotary:
                partner = jnp.where(first_half, pltpu.roll(yt, LANES - HEAD_DIM // 2, 1),
                                    pltpu.roll(yt, HEAD_DIM // 2, 1))
                yt = yt * cos + partner * sin
            if scale != 1.0:
                yt = yt * scale
            if name in ("ks", "vs"):
                other = pltpu.roll(yt, HEAD_DIM, 1)
                outs[name][:, 0:LANES] = jnp.where(low_head, yt, other).astype(BF16)
                outs[name][:, LANES:2 * LANES] = jnp.where(low_head, other, yt).astype(BF16)
            else:
                outs[name][:, o0 + t * LANES:o0 + (t + 1) * LANES] = yt.astype(BF16)


def _inproj(x2, g, w_bf, cos_t, sin_t, tm, rows_per_seq):
    rows = x2.shape[0]
    nseq = rows_per_seq // tm
    row_spec = lambda width: pl.BlockSpec((tm, width), lambda i: (i, 0))
    out_shapes = (
        jax.ShapeDtypeStruct((rows, SWA_Q_W), BF16), jax.ShapeDtypeStruct((rows, 2 * SWA_KV_W), BF16),
        jax.ShapeDtypeStruct((rows, 2 * SWA_KV_W), BF16), jax.ShapeDtypeStruct((rows, DIFF_W), BF16),
        jax.ShapeDtypeStruct((rows, DIFF_W), BF16), jax.ShapeDtypeStruct((rows, DIFF_W), BF16),
    )
    return pl.pallas_call(
        _inproj_kernel,
        grid=(rows // tm,),
        in_specs=[
            row_spec(D_MODEL),
            pl.BlockSpec((1, D_MODEL), lambda i: (0, 0)),
            pl.BlockSpec((D_MODEL, IN_WIDTH), lambda i: (0, 0), pipeline_mode=pl.Buffered(1)),
            pl.BlockSpec((tm, LANES), lambda i: (i % nseq, 0)),
            pl.BlockSpec((tm, LANES), lambda i: (i % nseq, 0)),
        ],
        out_specs=(row_spec(SWA_Q_W), row_spec(2 * SWA_KV_W), row_spec(2 * SWA_KV_W),
                   row_spec(DIFF_W), row_spec(DIFF_W), row_spec(DIFF_W)),
        out_shape=out_shapes,
        compiler_params=pltpu.CompilerParams(dimension_semantics=("arbitrary",),
                                             vmem_limit_bytes=VMEM_LIMIT),
        name="inproj",
    )(x2, g, w_bf, cos_t, sin_t)


def _rope_tables(positions):
    inv = 1.0 / (ROPE_THETA ** (jnp.arange(0, HEAD_DIM, 2, dtype=F32) / HEAD_DIM))
    ang = positions.astype(F32)[:, None] * inv[None, :]
    cos, sin = jnp.cos(ang), jnp.sin(ang)
    cos_t = jnp.tile(cos, (1, 2 * LANES // HEAD_DIM))
    sin_t = jnp.tile(jnp.concatenate([-sin, sin], axis=1), (1, LANES // HEAD_DIM))
    return cos_t, sin_t


SWA_BLKS_PER_STEP = 8
SWA_SLAB = 256
SWA_BAND = 3 * CHUNK + N_META


def _swa_kernel(sink_ref, q_ref, kp_ref, kc_ref, km_ref, vp_ref, vc_ref, vm_ref, o_ref):
    i = pl.program_id(1)
    pairs = SWA_Q_HEADS // SWA_KV_HEADS // 2
    m_rows = pairs * 2 * CHUNK
    col = lax.broadcasted_iota(jnp.int32, (m_rows, SWA_SLAB), 1)
    row = lax.broadcasted_iota(jnp.int32, (m_rows, 1), 0)
    lane = lax.broadcasted_iota(jnp.int32, (CHUNK, LANES), 1)
    low_head = lane < HEAD_DIM
    zero = jnp.zeros((CHUNK, LANES), BF16)
    pad = jnp.zeros((SWA_SLAB - SWA_BAND, LANES), BF16)
    meta_rows = slice(PAD_FRONT, Q_BLOCK)
    for g in range(SWA_KV_HEADS):
        gl = slice(g * LANES, (g + 1) * LANES)
        sink = jnp.zeros((m_rows, 1), F32)
        for h in range(2 * pairs):
            sink = jnp.where(row // CHUNK == h, sink_ref[2 * g * pairs + h], sink)
        sink = sink * LOG2_E
        k_seq = jnp.concatenate([kp_ref[:, gl], kc_ref[:, gl]], axis=0)
        v_seq = jnp.concatenate([vp_ref[:, gl], vc_ref[:, gl]], axis=0)
        for half in range(2 * SWA_BLKS_PER_STEP):
            q_rows = slice(half * CHUNK, (half + 1) * CHUNK)
            band = slice(half * CHUNK, (half + 3) * CHUNK)
            k_slab = jnp.concatenate([k_seq[band], km_ref[meta_rows, gl], pad], axis=0)
            v_slab = jnp.concatenate([v_seq[band], vm_ref[meta_rows, gl], pad], axis=0)
            n_missing = max(0, (2 - half) * CHUNK)
            valid = (col < SWA_BAND) & ((i > 0) | (col >= n_missing))
            q2 = []
            for c in range(g * pairs, (g + 1) * pairs):
                qp = q_ref[q_rows, c * LANES:(c + 1) * LANES]
                q2 += [jnp.where(low_head, qp, zero), jnp.where(low_head, zero, qp)]
            s = jnp.where(valid, _nt_dot(jnp.concatenate(q2, axis=0), k_slab), NEG_INF)
            m = jnp.maximum(jnp.max(s, axis=-1, keepdims=True), sink)
            p = jnp.exp2(s - m)
            denom = jnp.sum(p, axis=-1, keepdims=True) + jnp.exp2(sink - m)
            o2 = jnp.dot(p.astype(BF16), v_slab, preferred_element_type=F32) / denom
            for j, c in enumerate(range(g * pairs, (g + 1) * pairs)):
                lo = o2[(2 * j) * CHUNK:(2 * j + 1) * CHUNK]
                hi = o2[(2 * j + 1) * CHUNK:(2 * j + 2) * CHUNK]
                o_ref[q_rows, c * LANES:(c + 1) * LANES] = jnp.where(low_head, lo, hi).astype(BF16)


def _swa(sinks, qs, ks, vs, ks_meta, vs_meta, batch, nblk):
    rows = qs.shape[0]
    steps = nblk // SWA_BLKS_PER_STEP
    step_rows = SWA_BLKS_PER_STEP * Q_BLOCK
    cur = lambda b, i: (b * steps + i, 0)
    prev = lambda b, i: (b * nblk + jnp.maximum(SWA_BLKS_PER_STEP * i - 1, 0), 0)
    kv_w = 2 * SWA_KV_W
    return pl.pallas_call(
        _swa_kernel,
        grid=(batch, steps),
        in_specs=[
            pl.BlockSpec(memory_space=pltpu.SMEM),
            pl.BlockSpec((step_rows, SWA_Q_W), cur),
            pl.BlockSpec((Q_BLOCK, kv_w), prev), pl.BlockSpec((step_rows, kv_w), cur),
            pl.BlockSpec((Q_BLOCK, kv_w), lambda b, i: (0, 0)),
            pl.BlockSpec((Q_BLOCK, kv_w), prev), pl.BlockSpec((step_rows, kv_w), cur),
            pl.BlockSpec((Q_BLOCK, kv_w), lambda b, i: (0, 0)),
        ],
        out_specs=pl.BlockSpec((step_rows, SWA_Q_W), cur),
        out_shape=jax.ShapeDtypeStruct((rows, SWA_Q_W), BF16),
        compiler_params=pltpu.CompilerParams(dimension_semantics=("arbitrary", "arbitrary"),
                                             vmem_limit_bytes=VMEM_LIMIT),
        name="swa",
    )(sinks, qs, ks, ks, ks_meta, vs, vs, vs_meta)


DIFF_TQ = 256
DIFF_TK = 512
DIFF_HEADS_PER_STEP = 2


def _diff_kernel(lam_ref, g_ref, q_ref, k_ref, v_ref, km_ref, vm_ref, o_ref, *, seq):
    dl = lam_ref[...]
    lam = (jnp.exp(jnp.sum(dl[0:1] * dl[1:2], axis=-1, keepdims=True))
           - jnp.exp(jnp.sum(dl[2:3] * dl[3:4], axis=-1, keepdims=True)) + LAM_INIT)
    lane = lax.broadcasted_iota(jnp.int32, (DIFF_TQ, LANES), 1)
    low_map = lane < HEAD_DIM
    zero = jnp.zeros((DIFF_TQ, LANES), BF16)

    def last_mask(n_keys):
        shape = (2 * DIFF_TQ, Q_BLOCK + n_keys)
        col = lax.broadcasted_iota(jnp.int32, shape, 1)
        q_chunk = (lax.broadcasted_iota(jnp.int32, shape, 0) % DIFF_TQ) // CHUNK
        first_chunk = (DIFF_TQ - n_keys) // CHUNK
        k_chunk = (col - Q_BLOCK) // CHUNK + first_chunk
        return ((col >= PAD_FRONT) & (col < Q_BLOCK)) | ((col >= Q_BLOCK) & (k_chunk <= q_chunk))

    def step(q2, k_t, v_t, mask, state):
        s = _nt_dot(q2, k_t)
        if mask is not None:
            s = jnp.where(mask, s, NEG_INF)
        smax = jnp.max(s, axis=-1, keepdims=True)
        if state is None:
            p = jnp.exp2(s - smax)
            return smax, jnp.sum(p, axis=-1, keepdims=True), jnp.dot(
                p.astype(BF16), v_t, preferred_element_type=F32)
        m_old, l_old, acc = state
        m_new = jnp.maximum(m_old, smax)
        alpha = jnp.exp2(m_old - m_new)
        p = jnp.exp2(s - m_new)
        return (m_new, alpha * l_old + jnp.sum(p, axis=-1, keepdims=True),
                alpha * acc + jnp.dot(p.astype(BF16), v_t, preferred_element_type=F32))

    heads = [slice(h * LANES, (h + 1) * LANES) for h in range(DIFF_HEADS_PER_STEP)]

    tiles_per_k = DIFF_TK // DIFF_TQ
    masks = [last_mask((t + 1) * DIFF_TQ) for t in range(tiles_per_k)]

    def q_tile(n_full, t):
        if n_full is None:
            aligned = lambda v: v
            q0 = t * DIFF_TQ
        else:
            aligned = lambda v: pl.multiple_of(v, DIFF_TQ)
            q0 = aligned((n_full * tiles_per_k + t) * DIFF_TQ)
        q2s = []
        for hs in heads:
            q = q_ref[pl.ds(q0, DIFF_TQ), hs]
            q2s.append(jnp.concatenate(
                [jnp.where(low_map, q, zero), jnp.where(low_map, zero, q)], axis=0))

        def full_tiles(kj, states):
            rows = pl.ds(pl.multiple_of(kj * DIFF_TK, DIFF_TK), DIFF_TK)
            out = ()
            for h, hs in enumerate(heads):
                st = None if states is None else states[3 * h:3 * h + 3]
                out = out + step(q2s[h], k_ref[rows, hs], v_ref[rows, hs], None, st)
            return out

        if n_full is None:
            states = None
        else:
            states = lax.fori_loop(1, n_full, full_tiles, full_tiles(0, None))

        n_keys = (t + 1) * DIFF_TQ
        rows = pl.ds(aligned(q0 + DIFF_TQ - n_keys), n_keys)
        for h, hs in enumerate(heads):
            st = None if states is None else states[3 * h:3 * h + 3]
            k_t = jnp.concatenate([km_ref[:, hs], k_ref[rows, hs]], axis=0)
            v_t = jnp.concatenate([vm_ref[:, hs], v_ref[rows, hs]], axis=0)
            _, l, acc = step(q2s[h], k_t, v_t, masks[t], st)
            o = acc / l
            o = o[:DIFF_TQ] - lam * o[DIFF_TQ:]
            o = o * lax.rsqrt(jnp.mean(o * o, axis=-1, keepdims=True) + RMS_EPS)
            o_ref[pl.ds(q0, DIFF_TQ), hs] = (o * g_ref[...] * (1.0 - LAM_INIT)).astype(BF16)

    def q_group(n_full, carry):
        for t in range(tiles_per_k):
            q_tile(n_full, t)
        return carry

    q_group(None, 0)
    lax.fori_loop(1, seq // DIFF_TK, q_group, 0)


def _diff(diff_lambda, subln_g, qd, kd, vd, kd_meta, vd_meta, batch, seq):
    rows = qd.shape[0]
    head = lambda b, h: (b, h)
    meta = lambda b, h: (0, h)
    width = DIFF_HEADS_PER_STEP * LANES
    return pl.pallas_call(
        functools.partial(_diff_kernel, seq=seq),
        grid=(batch, DIFF_HEADS // DIFF_HEADS_PER_STEP),
        in_specs=[
            pl.BlockSpec((4, HEAD_DIM), lambda b, h: (0, 0)),
            pl.BlockSpec((1, LANES), lambda b, h: (0, 0)),
            pl.BlockSpec((seq, width), head), pl.BlockSpec((seq, width), head),
            pl.BlockSpec((seq, width), head),
            pl.BlockSpec((Q_BLOCK, width), meta), pl.BlockSpec((Q_BLOCK, width), meta),
        ],
        out_specs=pl.BlockSpec((seq, width), head),
        out_shape=jax.ShapeDtypeStruct((rows, DIFF_W), BF16),
        compiler_params=pltpu.CompilerParams(dimension_semantics=("arbitrary", "arbitrary"),
                                             vmem_limit_bytes=VMEM_LIMIT),
        name="diffattn",
    )(diff_lambda, subln_g, qd, kd, vd, kd_meta, vd_meta)


HALF = D_MODEL // 2


def _pack_bf16_pair(a, b):
    hi = pltpu.bitcast(a.astype(BF16).astype(F32), jnp.uint32)
    lo = pltpu.bitcast(b.astype(BF16).astype(F32), jnp.uint32)
    return hi | (lo >> 16)


def _unpack_bf16_pair(w):
    a = pltpu.bitcast(w & jnp.uint32(0xFFFF0000), F32)
    b = pltpu.bitcast(w << 16, F32)
    return a, b


OUT_CHUNK = 512


def _outproj_kernel(os_ref, od_ref, x_ref, w_ref, g_ref, wr_ref, h_ref, fp_ref, lg_ref):
    tm = x_ref.shape[0]
    ssq = jnp.zeros((tm, 1), F32)
    for c in range(D_MODEL // OUT_CHUNK):
        cols = slice(c * OUT_CHUNK, (c + 1) * OUT_CHUNK)
        y = jnp.dot(os_ref[...], w_ref[0:SWA_Q_W, cols], preferred_element_type=F32)
        y = y + jnp.dot(od_ref[...], w_ref[SWA_Q_W:, cols], preferred_element_type=F32)
        h = x_ref[:, cols] + y
        h_ref[:, cols] = h
        ssq = ssq + jnp.sum(h * h, axis=-1, keepdims=True)
    inv = lax.rsqrt(ssq * (1.0 / D_MODEL) + RMS_EPS)
    lg = jnp.zeros((tm, LANES), F32)
    for c in range(HALF // OUT_CHUNK):
        ca = slice(c * OUT_CHUNK, (c + 1) * OUT_CHUNK)
        cb = slice(HALF + c * OUT_CHUNK, HALF + (c + 1) * OUT_CHUNK)
        fa = h_ref[:, ca] * inv * g_ref[:, ca]
        fb = h_ref[:, cb] * inv * g_ref[:, cb]
        fp_ref[:, ca] = _pack_bf16_pair(fa, fb)
        for f, cc in ((fa, ca), (fb, cb)):
            f_hi = f.astype(BF16)
            f_lo = (f - f_hi.astype(F32)).astype(BF16)
            w = wr_ref[cc, :]
            w_hi = w.astype(BF16)
            w_lo = (w - w_hi.astype(F32)).astype(BF16)
            p = jnp.dot(f_hi, jnp.concatenate([w_hi, w_lo], axis=1), preferred_element_type=F32)
            lg = lg + p[:, 0:LANES] + p[:, LANES:] + jnp.dot(f_lo, w_hi, preferred_element_type=F32)
    lg_ref[...] = lg


def _outproj(o_s, o_d, x2, w_bf, g, w_router, tm):
    rows = x2.shape[0]
    row_spec = lambda width: pl.BlockSpec((tm, width), lambda i: (i, 0))
    return pl.pallas_call(
        _outproj_kernel,
        grid=(rows // tm,),
        in_specs=[
            row_spec(SWA_Q_W), row_spec(DIFF_W), row_spec(D_MODEL),
            pl.BlockSpec((D_MODEL, D_MODEL), lambda i: (0, 0), pipeline_mode=pl.Buffered(1)),
            pl.BlockSpec((1, D_MODEL), lambda i: (0, 0)),
            pl.BlockSpec((D_MODEL, LANES), lambda i: (0, 0)),
        ],
        out_specs=(row_spec(D_MODEL), row_spec(HALF), row_spec(LANES)),
        out_shape=(jax.ShapeDtypeStruct((rows, D_MODEL), F32),
                   jax.ShapeDtypeStruct((rows, HALF), jnp.uint32),
                   jax.ShapeDtypeStruct((rows, LANES), F32)),
        compiler_params=pltpu.CompilerParams(dimension_semantics=("arbitrary",),
                                             vmem_limit_bytes=VMEM_LIMIT),
        name="outproj",
    )(o_s, o_d, x2, w_bf, g, w_router)


DISPATCH_TM = 2048


def _dispatch_kernel(zero_blk_ref, n_zero_ref, dest_ref, f_ref, xs_ref, zeros, sem, zsem):
    i = pl.program_id(0)

    @pl.when(i == 0)
    def _():
        zeros[...] = jnp.zeros_like(zeros)

        def zero_copy(j):
            rows = pl.ds(zero_blk_ref[j] * ROW_BLK, ROW_BLK)
            return pltpu.make_async_copy(zeros, xs_ref.at[rows], zsem.at[0])

        def start(j, c):
            zero_copy(j).start()
            return c

        def wait(j, c):
            zero_copy(j).wait()
            return c
        lax.fori_loop(0, n_zero_ref[0], start, 0)
        lax.fori_loop(0, n_zero_ref[0], wait, 0)

    def issue(r, c):
        for kk in range(TOP_K):
            pltpu.make_async_copy(f_ref.at[pl.ds(r, 1)], xs_ref.at[pl.ds(dest_ref[TOP_K * r + kk], 1)],
                                  sem.at[0]).start(priority=kk % 2)
        return c
    lax.fori_loop(0, DISPATCH_TM, issue, 0, unroll=8)
    n_rows = TOP_K * DISPATCH_TM
    pltpu.make_async_copy(xs_ref.at[pl.ds(0, n_rows)], xs_ref.at[pl.ds(0, n_rows)], sem.at[0]).wait()


def _dispatch(zero_blk, n_zero, dest_flat, f_packed, n_blocks):
    rows = f_packed.shape[0]
    return pl.pallas_call(
        _dispatch_kernel,
        grid_spec=pltpu.PrefetchScalarGridSpec(
            num_scalar_prefetch=2,
            grid=(rows // DISPATCH_TM,),
            in_specs=[
                pl.BlockSpec((TOP_K * DISPATCH_TM,), lambda i, z, n: (i,), memory_space=pltpu.SMEM),
                pl.BlockSpec((DISPATCH_TM, HALF), lambda i, z, n: (i, 0)),
            ],
            out_specs=pl.BlockSpec(memory_space=pl.ANY),
            scratch_shapes=[pltpu.VMEM((ROW_BLK, HALF), jnp.uint32),
                            pltpu.SemaphoreType.DMA((1,)),
                            pltpu.SemaphoreType.DMA((1,))],
        ),
        out_shape=jax.ShapeDtypeStruct((n_blocks * ROW_BLK, HALF), jnp.uint32),
        compiler_params=pltpu.CompilerParams(dimension_semantics=("arbitrary",),
                                             vmem_limit_bytes=VMEM_LIMIT),
        name="dispatch",
    )(zero_blk, n_zero, dest_flat, f_packed)


ITEM_BLKS = 6
W_BUFS = 4
ITEM_ROWS = ITEM_BLKS * ROW_BLK
F_TILE = 256
N_FT = -(-D_EXPERT // F_TILE)
Y_TILE = 512
N_YT = D_MODEL // Y_TILE


def _f_tile_start(k):
    return pl.multiple_of(jnp.minimum(k * F_TILE, D_EXPERT - F_TILE), LANES)


def _moe_kernel(e_ref, r0_ref, nb_ref, meta_ref, xs_ref, wg_hbm, wu_hbm, wd_hbm, y_ref,
                xraw, xb, hid, ystage, wgf, wuf, wdf, wgu, wdr, xsem, ysem, wsem):
    n_used = meta_ref[0]
    n_live = meta_ref[1]

    def for_blocks(count, fn):
        def body(j, c):
            fn(j)
            return c
        lax.fori_loop(0, count, body, 0)

    def x_copy(item, buf, j):
        src = xs_ref.at[pl.ds((r0_ref[item] + j) * ROW_BLK, ROW_BLK)]
        return pltpu.make_async_copy(src, xraw.at[buf, pl.ds(j * ROW_BLK, ROW_BLK)],
                                     xsem.at[buf, j])

    def w_copies(item, k, buf):
        e = e_ref[item]
        cols = pl.ds(_f_tile_start(k), F_TILE)
        return (pltpu.make_async_copy(wg_hbm.at[e, :, cols], wgf.at[buf], wsem.at[0, buf]),
                pltpu.make_async_copy(wu_hbm.at[e, :, cols], wuf.at[buf], wsem.at[1, buf]),
                pltpu.make_async_copy(wd_hbm.at[e, cols, :], wdf.at[buf], wsem.at[2, buf]))

    def zero_copy(j):
        dst = y_ref.at[pl.ds((n_used + j) * ROW_BLK, ROW_BLK)]
        return pltpu.make_async_copy(ystage.at[pl.ds(0, ROW_BLK)], dst, ysem.at[0])
    n_unused = y_ref.shape[0] // ROW_BLK - n_used
    ystage[0:ROW_BLK, :] = jnp.zeros((ROW_BLK, HALF), jnp.uint32)
    for_blocks(n_unused, lambda j: zero_copy(j).start())
    for_blocks(n_unused, lambda j: zero_copy(j).wait())

    for_blocks(nb_ref[0], lambda j: x_copy(0, 0, j).start())

    @pl.when(nb_ref[0] > 0)
    def _():
        for s in range(W_BUFS - 1):
            for c in w_copies(0, s, s):
                c.start()

    def item_body(i, carry):
        slot = i % 2
        nb = nb_ref[i]
        prev = jnp.maximum(i - 1, 0)
        nb_prev = jnp.where(i > 0, nb_ref[prev], 0)

        def y_copy(j):
            dst = y_ref.at[pl.ds((r0_ref[prev] + j) * ROW_BLK, ROW_BLK)]
            return pltpu.make_async_copy(ystage.at[pl.ds(j * ROW_BLK, ROW_BLK)], dst, ysem.at[0])

        for_blocks(nb_ref[i + 1], lambda j: x_copy(i + 1, 1 - slot, j).start())

        def arrive(j):
            x_copy(i, slot, j).wait()
            rows = pl.ds(pl.multiple_of(j * ROW_BLK, ROW_BLK), ROW_BLK)
            lo, hi = _unpack_bf16_pair(xraw[slot, rows, :])
            xb[rows, 0:HALF] = lo.astype(BF16)
            xb[rows, HALF:] = hi.astype(BF16)
        for_blocks(nb, arrive)

        def tile_body(k, carry):
            step = i * N_FT + k
            wbuf = step % W_BUFS
            last = k == N_FT - 1
            ahead = k + (W_BUFS - 1)
            nxt_i = jnp.where(ahead >= N_FT, i + 1, i)
            nxt_k = jnp.where(ahead >= N_FT, ahead - N_FT, ahead)

            @pl.when(nb_ref[nxt_i] > 0)
            def _():
                for c in w_copies(nxt_i, nxt_k, (step + (W_BUFS - 1)) % W_BUFS):
                    c.start()

            def cast_tiles():
                wgu[:, 0:F_TILE] = wgf[wbuf].astype(BF16)
                wgu[:, F_TILE:] = wuf[wbuf].astype(BF16)
                f_rows = pl.ds(_f_tile_start(k), F_TILE)
                for n in range(N_YT):
                    wdr[slot, n, f_rows, :] = wdf[wbuf, :, n * Y_TILE:(n + 1) * Y_TILE].astype(BF16)

            def gate_up(row0, nrows):
                rows = pl.ds(pl.multiple_of(row0, ROW_BLK), nrows)
                gu = jnp.dot(xb[rows, :], wgu[...], preferred_element_type=F32)
                gate, up = gu[:, 0:F_TILE], gu[:, F_TILE:]
                h = (gate * (1.0 / (1.0 + jnp.exp(-gate))) * up).astype(BF16)
                hid[slot, rows, pl.ds(_f_tile_start(k), F_TILE)] = h

            k_y = jnp.minimum(k, N_YT - 1)

            def down(row0, nrows):
                rows = pl.ds(pl.multiple_of(row0, ROW_BLK), nrows)
                yt = jnp.dot(hid[1 - slot, rows, :], wdr[1 - slot, k_y], preferred_element_type=F32)
                packed = _pack_bf16_pair(yt[:, 0:Y_TILE // 2], yt[:, Y_TILE // 2:])
                words = pl.ds(pl.multiple_of(k_y * (Y_TILE // 2), Y_TILE // 2), Y_TILE // 2)
                ystage[rows, words] = packed

            def for_row_tiles(n_blk, fn):
                for whole in range(1, ITEM_BLKS + 1):
                    @pl.when(n_blk == whole)
                    def _():
                        fn(0, whole * ROW_BLK)

            @pl.when(nb > 0)
            def _():
                for c in w_copies(i, k, wbuf):
                    c.wait()
                cast_tiles()

            for_row_tiles(nb, gate_up)
            for_row_tiles(jnp.where(k < N_YT, nb_prev, 0), down)

            @pl.when(k == N_YT)
            def _():
                for_blocks(nb_prev, lambda j: y_copy(j).start())

            @pl.when(last)
            def _():
                for_blocks(nb_prev, lambda j: y_copy(j).wait())
            return carry

        lax.fori_loop(0, N_FT, tile_body, 0)
        return carry

    lax.fori_loop(0, n_live + 1, item_body, 0)


def _moe(item_e, item_r0, item_nb, meta, xs, w_gate, w_up, w_down, n_blocks):
    return pl.pallas_call(
        _moe_kernel,
        grid_spec=pltpu.PrefetchScalarGridSpec(
            num_scalar_prefetch=4,
            grid=(1,),
            in_specs=[pl.BlockSpec(memory_space=pl.ANY)] * 4,
            out_specs=pl.BlockSpec(memory_space=pl.ANY),
            scratch_shapes=[
                pltpu.VMEM((2, ITEM_ROWS, HALF), jnp.uint32),
                pltpu.VMEM((ITEM_ROWS, D_MODEL), BF16),
                pltpu.VMEM((2, ITEM_ROWS, D_EXPERT), BF16),
                pltpu.VMEM((ITEM_ROWS, HALF), jnp.uint32),
                pltpu.VMEM((W_BUFS, D_MODEL, F_TILE), F32),
                pltpu.VMEM((W_BUFS, D_MODEL, F_TILE), F32),
                pltpu.VMEM((W_BUFS, F_TILE, D_MODEL), F32),
                pltpu.VMEM((D_MODEL, 2 * F_TILE), BF16),
                pltpu.VMEM((2, N_YT, D_EXPERT, Y_TILE), BF16),
                pltpu.SemaphoreType.DMA((2, ITEM_BLKS)),
                pltpu.SemaphoreType.DMA((1,)),
                pltpu.SemaphoreType.DMA((3, W_BUFS)),
            ],
        ),
        out_shape=jax.ShapeDtypeStruct((n_blocks * ROW_BLK, HALF), jnp.uint32),
        compiler_params=pltpu.CompilerParams(dimension_semantics=("arbitrary",),
                                             vmem_limit_bytes=VMEM_LIMIT),
        name="moe",
    )(item_e, item_r0, item_nb, meta, xs, w_gate, w_up, w_down)


def _combine_kernel(cur_ref, nxt_ref, h_ref, w_ref, g_ref, y_ref, o_ref, ya, yb, sem, *, tm):
    i = pl.program_id(0)
    n_steps = pl.num_programs(0)
    slot = i % 2

    def issue(idx_ref, buf, r):
        pltpu.make_async_copy(y_ref.at[pl.ds(idx_ref[TOP_K * r], 1)], ya.at[buf, pl.ds(r, 1)],
                              sem.at[buf]).start(priority=0)
        pltpu.make_async_copy(y_ref.at[pl.ds(idx_ref[TOP_K * r + 1], 1)], yb.at[buf, pl.ds(r, 1)],
                              sem.at[buf]).start(priority=1)

    def wait(buf):
        pltpu.make_async_copy(ya.at[buf], ya.at[buf], sem.at[buf]).wait()
        pltpu.make_async_copy(yb.at[buf], yb.at[buf], sem.at[buf]).wait()

    @pl.when(i == 0)
    def _():
        def body(r, c):
            issue(cur_ref, 0, r)
            return c
        lax.fori_loop(0, tm, body, 0, unroll=8)

    wait(slot)
    for r in range(tm):
        issue(nxt_ref, 1 - slot, r)

    wa = w_ref[:, 0:1]
    wb = w_ref[:, 1:2]
    half_t = Y_TILE // 2
    ssq = jnp.zeros((tm, 1), F32)
    for n in range(N_YT):
        words = slice(n * half_t, (n + 1) * half_t)
        a_lo, a_hi = _unpack_bf16_pair(ya[slot, :, words])
        b_lo, b_hi = _unpack_bf16_pair(yb[slot, :, words])
        lo_cols = slice(n * Y_TILE, n * Y_TILE + half_t)
        hi_cols = slice(n * Y_TILE + half_t, (n + 1) * Y_TILE)
        lo = h_ref[:, lo_cols] + wa * a_lo + wb * b_lo
        hi = h_ref[:, hi_cols] + wa * a_hi + wb * b_hi
        ssq = ssq + jnp.sum(lo * lo, axis=-1, keepdims=True) + jnp.sum(hi * hi, axis=-1, keepdims=True)
        o_ref[:, lo_cols] = lo
        o_ref[:, hi_cols] = hi
    inv = lax.rsqrt(ssq * (1.0 / D_MODEL) + RMS_EPS)
    o_ref[...] = o_ref[...] * inv * g_ref[...]

    @pl.when(i == n_steps - 1)
    def _():
        wait(1 - slot)


def _combine(dest_flat, h2, wts, g, y, tm):
    rows = h2.shape[0]
    steps = rows // tm
    idx_spec = lambda fn: pl.BlockSpec((TOP_K * tm,), fn, memory_space=pltpu.SMEM)
    return pl.pallas_call(
        functools.partial(_combine_kernel, tm=tm),
        grid=(steps,),
        in_specs=[
            idx_spec(lambda i: (i,)),
            idx_spec(lambda i: (jnp.minimum(i + 1, steps - 1),)),
            pl.BlockSpec((tm, D_MODEL), lambda i: (i, 0)),
            pl.BlockSpec((tm, LANES), lambda i: (i, 0)),
            pl.BlockSpec((1, D_MODEL), lambda i: (0, 0)),
            pl.BlockSpec(memory_space=pl.ANY),
        ],
        out_specs=pl.BlockSpec((tm, D_MODEL), lambda i: (i, 0)),
        scratch_shapes=[
            pltpu.VMEM((2, tm, HALF), jnp.uint32),
            pltpu.VMEM((2, tm, HALF), jnp.uint32),
            pltpu.SemaphoreType.DMA((2,)),
        ],
        out_shape=jax.ShapeDtypeStruct((rows, D_MODEL), F32),
        compiler_params=pltpu.CompilerParams(dimension_semantics=("arbitrary",),
                                             vmem_limit_bytes=VMEM_LIMIT),
        name="combine",
    )(dest_flat, dest_flat, h2, wts, g, y)


ROUTE_TM = 1024


def _route_kernel(lg_ref, info_ref, wts_ref, tab_ref, counts, starts, carry):
    phase = pl.program_id(0)
    i = pl.program_id(1)
    tm = ROUTE_TM
    lane = lax.broadcasted_iota(jnp.int32, (tm, LANES), 1)
    lg = lg_ref[...]

    def first_lane_of_max(v):
        vmax = jnp.max(v, axis=-1, keepdims=True)
        return vmax, jnp.min(jnp.where(v == vmax, lane, LANES), axis=-1, keepdims=True)

    gl = jnp.where(lane < N_GROUPS, lg, NEG_INF)
    gmax, g_sel = first_lane_of_max(gl)
    gate = 1.0 / jnp.sum(jnp.exp(gl - gmax), axis=-1, keepdims=True)
    lo = N_GROUPS + EXPERTS_PER_GROUP * g_sel
    el = jnp.where((lane >= lo) & (lane < lo + EXPERTS_PER_GROUP), lg, NEG_INF)
    m1, l1 = first_lane_of_max(el)
    m2, l2 = first_lane_of_max(jnp.where(lane == l1, NEG_INF, el))
    ratio = jnp.exp(m2 - m1)
    w1 = gate / (1.0 + ratio)
    w2 = w1 * ratio
    e1 = l1 - N_GROUPS
    e2 = l2 - N_GROUPS
    hit1 = lane == e1
    hit2 = lane == e2
    onehot = (hit1 | hit2).astype(F32)
    tile_counts = jnp.sum(onehot, axis=0, keepdims=True)

    @pl.when((phase == 0) & (i == 0))
    def _():
        counts[...] = jnp.zeros_like(counts)

    @pl.when(phase == 0)
    def _():
        counts[...] += tile_counts

    @pl.when((phase == 1) & (i == 0))
    def _():
        blocks = jnp.floor((counts[...] + (ROW_BLK - 1)) * (1.0 / ROW_BLK))
        r = lax.broadcasted_iota(jnp.int32, (LANES, LANES), 0)
        c = lax.broadcasted_iota(jnp.int32, (LANES, LANES), 1)
        before = (r < c).astype(F32)
        first = jnp.dot(jnp.broadcast_to(blocks, (8, LANES)), before,
                        preferred_element_type=F32, precision=lax.Precision.HIGHEST)[0:1]
        starts[...] = first
        carry[...] = jnp.zeros_like(carry)
        used = jnp.sum(blocks, axis=-1, keepdims=True)
        row = lax.broadcasted_iota(jnp.int32, (8, LANES), 0)
        tab = jnp.where(row == 0, blocks, jnp.where(row == 1, first, jnp.where(row == 2, used, 0.0)))
        tab_ref[...] = tab.astype(jnp.int32)

    @pl.when(phase == 1)
    def _():
        r = lax.broadcasted_iota(jnp.int32, (tm, tm), 0)
        c = lax.broadcasted_iota(jnp.int32, (tm, tm), 1)
        earlier = (c < r).astype(BF16)
        before = jnp.dot(earlier, onehot.astype(BF16), preferred_element_type=F32)
        slot = starts[...] * ROW_BLK + carry[...] + before
        d1 = jnp.sum(jnp.where(hit1, slot, 0.0), axis=-1, keepdims=True)
        d2 = jnp.sum(jnp.where(hit2, slot, 0.0), axis=-1, keepdims=True)
        info_ref[...] = jnp.where(lane == 0, d1, jnp.where(lane == 1, d2, 0.0)).astype(jnp.int32)
        wts_ref[...] = jnp.where(lane == 0, w1, jnp.where(lane == 1, w2, 0.0))
        carry[...] += tile_counts


def _route(logits):
    rows = logits.shape[0]
    tile = lambda p, i: (i * p, 0)
    return pl.pallas_call(
        _route_kernel,
        grid=(2, rows // ROUTE_TM),
        in_specs=[pl.BlockSpec((ROUTE_TM, LANES), lambda p, i: (i, 0))],
        out_specs=(pl.BlockSpec((ROUTE_TM, LANES), tile), pl.BlockSpec((ROUTE_TM, LANES), tile),
                   pl.BlockSpec((8, LANES), lambda p, i: (0, 0))),
        out_shape=(jax.ShapeDtypeStruct((rows, LANES), jnp.int32),
                   jax.ShapeDtypeStruct((rows, LANES), F32),
                   jax.ShapeDtypeStruct((8, LANES), jnp.int32)),
        scratch_shapes=[pltpu.VMEM((1, LANES), F32)] * 3,
        compiler_params=pltpu.CompilerParams(dimension_semantics=("arbitrary", "arbitrary"),
                                             vmem_limit_bytes=VMEM_LIMIT),
        name="route",
    )(logits)


def _work_items(blocks_e, blk_start, n_items):
    items_e = (blocks_e + ITEM_BLKS - 1) // ITEM_BLKS
    item_end = jnp.cumsum(items_e)
    item_start = item_end - items_e
    total = item_end[-1]
    idx = jnp.arange(n_items, dtype=jnp.int32)
    live = idx < total
    ref = jnp.minimum(idx, total - 1)
    e = jnp.searchsorted(item_end, ref, side='right').astype(jnp.int32)
    j = ref - item_start[e]
    r0 = blk_start[e] + ITEM_BLKS * j
    nb = jnp.where(live, jnp.clip(blocks_e[e] - ITEM_BLKS * j, 0, ITEM_BLKS), 0)
    return e, r0.astype(jnp.int32), nb.astype(jnp.int32), total.astype(jnp.int32)


def kernel(x, meta_tokens, attn_norm_g, w_in, swa_sinks, diff_lambda, diff_subln_g, w_out,
           ffn_norm_g, w_router_group, w_router_expert, w_gate, w_up, w_down, final_norm_g):
    batch, seq, _ = x.shape
    n_tok = batch * seq
    nblk = seq // Q_BLOCK
    x2 = x.reshape(n_tok, D_MODEL)

    w_in_bf = w_in[0].astype(BF16)
    g_attn = attn_norm_g[0].reshape(1, D_MODEL)
    cos_x, sin_x = _rope_tables(N_META + jnp.arange(seq))
    cos_m, sin_m = _rope_tables(jnp.maximum(jnp.arange(Q_BLOCK) - PAD_FRONT, 0))
    qs, ks, vs, qd, kd, vd = _inproj(x2, g_attn, w_in_bf, cos_x, sin_x, 512, seq)
    meta_blk = jnp.concatenate(
        [jnp.zeros((PAD_FRONT, D_MODEL), x.dtype), meta_tokens.astype(x.dtype)], axis=0)
    _, ks_m, vs_m, _, kd_m, vd_m = _inproj(meta_blk, g_attn, w_in_bf, cos_m, sin_m, Q_BLOCK, Q_BLOCK)

    o_s = _swa(swa_sinks[0].astype(F32), qs, ks, vs, ks_m, vs_m, batch, nblk)
    o_d = _diff(diff_lambda[0].astype(F32), diff_subln_g[0].astype(F32).reshape(1, LANES),
                qd, kd, vd, kd_m, vd_m, batch, seq)

    w_router = jnp.concatenate(
        [w_router_group[0], w_router_expert[0],
         jnp.zeros((D_MODEL, LANES - N_GROUPS - N_EXPERTS), F32)], axis=1)
    h2, f_packed, logits = _outproj(o_s, o_d, x2, w_out[0].astype(BF16),
                                    ffn_norm_g[0].reshape(1, D_MODEL), w_router, 512)

    info, wts, tab = _route(logits)
    dest_flat = info[:, :TOP_K].reshape(-1)
    blocks_e, blk_start, n_used = tab[0, :N_EXPERTS], tab[1, :N_EXPERTS], tab[2, 0:1]
    n_blocks = n_tok * TOP_K // ROW_BLK + N_EXPERTS
    n_items = n_blocks // ITEM_BLKS + N_EXPERTS
    blk = jnp.arange(n_blocks, dtype=jnp.int32)
    is_last = jnp.any((blk[:, None] + 1 == (blk_start + blocks_e)[None, :])
                      & (blocks_e > 0)[None, :], axis=1)
    need_zero = is_last | (blk >= n_used[0])
    zero_blk = jnp.nonzero(need_zero, size=2 * N_EXPERTS, fill_value=0)[0].astype(jnp.int32)
    n_zero = jnp.sum(need_zero).astype(jnp.int32).reshape(1)
    xs = _dispatch(zero_blk, n_zero, dest_flat, f_packed, n_blocks)
    item_e, item_r0, item_nb, n_live = _work_items(blocks_e, blk_start, n_items)
    y = _moe(item_e, item_r0, item_nb, jnp.concatenate([n_used, n_live.reshape(1)]), xs,
             w_gate[0], w_up[0], w_down[0], n_blocks)

    out = _combine(dest_flat, h2, wts, final_norm_g.reshape(1, D_MODEL), y, 256)
    return out.reshape(batch, seq, D_MODEL)
```
